```python
import jax, jax.numpy as jnp
from jax import lax
import numpy as np

D_MODEL = 1024
BATCH = 8
SEQ = 4096
DEPTH = 4

PLE_DIM = 256
CHUNK = 128
A_WIDTH = D_MODEL
A_GROUPS = 8
A_GROUP_DIM = A_WIDTH // A_GROUPS
B_HEAD_DIM = 64
B_HEADS = D_MODEL // B_HEAD_DIM
B_WIDTH = B_HEADS * B_HEAD_DIM
Q_BLOCK = 128
IN_SIZES = (A_WIDTH, A_WIDTH, A_WIDTH, B_WIDTH, B_WIDTH, B_WIDTH, B_WIDTH, D_MODEL, D_MODEL)
IN_WIDTH = 3 * A_WIDTH + 4 * B_WIDTH + 2 * D_MODEL
ALPHA = (2 * DEPTH) ** 0.25
BETA = (8 * DEPTH) ** -0.25
LN_EPS = 1e-5

kernel_name = "gated_gmlp_stickbreaking_deepnorm_trunk"


def _layer_norm(x, g, b):
    xf = x.astype(jnp.float32)
    mu = jnp.mean(xf, axis=-1, keepdims=True)
    var = jnp.mean(jnp.square(xf - mu), axis=-1, keepdims=True)
    return ((xf - mu) * lax.rsqrt(var + LN_EPS) * g.astype(jnp.float32) + b.astype(jnp.float32)).astype(x.dtype)


def _chunked_spatial_gating(u, v, vn_g, vn_b, w_s, b_s):
    bsz, seq, _ = v.shape
    vn = _layer_norm(v, vn_g, vn_b).reshape(bsz, seq // CHUNK, CHUNK, A_GROUPS, A_GROUP_DIM)
    causal = jnp.tril(jnp.ones((CHUNK, CHUNK), dtype=bool))
    ws = jnp.where(causal, w_s, jnp.zeros((), w_s.dtype))
    mixed = jnp.einsum('gts,bcsgd->bctgd', ws, vn) + b_s.T[None, None, :, :, None]
    return u * mixed.reshape(bsz, seq, A_WIDTH)


def _stick_breaking_attention(q, k, v):
    bsz, seq, _ = q.shape

    def heads(t):
        return t.reshape(bsz, seq, B_HEADS, B_HEAD_DIM).transpose(0, 2, 1, 3).astype(jnp.float32)

    qh, kh, vh = heads(q), heads(k), heads(v)
    scale = B_HEAD_DIM ** -0.5
    outs = []
    for start in range(0, seq, Q_BLOCK):
        end = start + Q_BLOCK
        z = jnp.einsum('bhtd,bhsd->bhts', qh[:, :, start:end], kh[:, :, :end]) * scale
        t_pos = start + jnp.arange(Q_BLOCK)[:, None]
        s_pos = jnp.arange(end)[None, :]
        strict = s_pos < t_pos
        log_keep = jnp.where(strict, jax.nn.log_sigmoid(-z), 0.0)
        between = lax.cumsum(log_keep, axis=3, reverse=True) - log_keep
        weights = jnp.where(strict, jnp.exp(jax.nn.log_sigmoid(z) + between), 0.0)
        outs.append(jnp.einsum('bhts,bhsd->bhtd', weights, vh[:, :, :end]))
    o = jnp.concatenate(outs, axis=2)
    return o.transpose(0, 2, 1, 3).reshape(bsz, seq, B_WIDTH).astype(v.dtype)


def _layer(x, p_i, w_in, vn_g, vn_b, w_s, b_s, w_pa, w_pb, w_out, w_pe, w_pg, ln_g, ln_b):
    proj = jnp.einsum('bsd,dn->bsn', x, w_in)
    split_points = [int(s) for s in np.cumsum(IN_SIZES)[:-1]]
    u, v, gate_a, q, k, v_b, gate_b, merge_a, merge_b = jnp.split(proj, split_points, axis=-1)
    y_a = _chunked_spatial_gating(u, v, vn_g, vn_b, w_s, b_s) * jax.nn.silu(gate_a)
    y_b = _stick_breaking_attention(q, k, v_b) * jax.nn.silu(gate_b)
    merged = (jax.nn.sigmoid(merge_a) * jnp.einsum('bsc,cd->bsd', y_a, w_pa)
              + jax.nn.sigmoid(merge_b) * jnp.einsum('bsc,cd->bsd', y_b, w_pb))
    y = jnp.einsum('bsd,de->bse', merged, w_out)
    h = ALPHA * x + y
    h = h + jnp.einsum('bsp,pd->bsd', p_i, w_pe) * jax.nn.sigmoid(jnp.einsum('bsd,de->bse', h, w_pg))
    return _layer_norm(h, ln_g, ln_b)


def _fwd_setup_inputs(seed: int = 0) -> dict:
    key = jax.random.key(seed)
    ks = jax.random.split(key, 15)
    f32 = jnp.float32
    d_sc = D_MODEL ** -0.5
    return {
        "x": jax.random.normal(ks[0], (BATCH, SEQ, D_MODEL), f32),
        "p": jax.random.normal(ks[1], (DEPTH, BATCH, SEQ, PLE_DIM), f32),
        "w_in": jax.random.normal(ks[2], (DEPTH, D_MODEL, IN_WIDTH), f32) * d_sc,
        "vn_g": 1.0 + 0.02 * jax.random.normal(ks[3], (DEPTH, A_WIDTH), f32),
        "vn_b": 0.02 * jax.random.normal(ks[4], (DEPTH, A_WIDTH), f32),
        "w_s": jax.random.normal(ks[5], (DEPTH, A_GROUPS, CHUNK, CHUNK), f32) * CHUNK ** -0.5,
        "b_s": 1.0 + 0.02 * jax.random.normal(ks[6], (DEPTH, A_GROUPS, CHUNK), f32),
        "w_pa": jax.random.normal(ks[7], (DEPTH, A_WIDTH, D_MODEL), f32) * A_WIDTH ** -0.5,
        "w_pb": jax.random.normal(ks[8], (DEPTH, B_WIDTH, D_MODEL), f32) * B_WIDTH ** -0.5,
        "w_out": jax.random.normal(ks[9], (DEPTH, D_MODEL, D_MODEL), f32) * d_sc * BETA,
        "w_pe": jax.random.normal(ks[10], (DEPTH, PLE_DIM, D_MODEL), f32) * PLE_DIM ** -0.5,
        "w_pg": jax.random.normal(ks[11], (DEPTH, D_MODEL, D_MODEL), f32) * d_sc,
        "ln_g": 1.0 + 0.02 * jax.random.normal(ks[12], (DEPTH, D_MODEL), f32),
        "ln_b": 0.02 * jax.random.normal(ks[13], (DEPTH, D_MODEL), f32),
    }


def _fwd_reference(x, p, w_in, vn_g, vn_b, w_s, b_s, w_pa, w_pb, w_out, w_pe, w_pg, ln_g, ln_b):
    for i in range(DEPTH):
        x = _layer(x, p[i], w_in[i], vn_g[i], vn_b[i], w_s[i], b_s[i], w_pa[i], w_pb[i],
                   w_out[i], w_pe[i], w_pg[i], ln_g[i], ln_b[i])
    return x


import jax as _jax
import jax.numpy as _jnp

TWIN_FORMAT = 'train_step'
FWD_PARAMS = ['x', 'p', 'w_in', 'vn_g', 'vn_b', 'w_s', 'b_s', 'w_pa', 'w_pb', 'w_out', 'w_pe', 'w_pg', 'ln_g', 'ln_b']
TWIN_WEIGHTS = ['w_in', 'vn_g', 'vn_b', 'w_s', 'b_s', 'w_pa', 'w_pb', 'w_out', 'w_pe', 'w_pg', 'ln_g', 'ln_b']
TWIN_DIFF_INPUT = 'x'
TWIN_INPUTS = ['x', 'p', 'w_in', 'vn_g', 'vn_b', 'w_s', 'b_s', 'w_pa', 'w_pb', 'w_out', 'w_pe', 'w_pg', 'ln_g', 'ln_b', 'loss_target', 'm_w_in', 'm_vn_g', 'm_vn_b', 'm_w_s', 'm_b_s', 'm_w_pa', 'm_w_pb', 'm_w_out', 'm_w_pe', 'm_w_pg', 'm_ln_g', 'm_ln_b', 'v_w_in', 'v_vn_g', 'v_vn_b', 'v_w_s', 'v_b_s', 'v_w_pa', 'v_w_pb', 'v_w_out', 'v_w_pe', 'v_w_pg', 'v_ln_g', 'v_ln_b']
TWIN_OUTPUTS = ['loss', 'grad_x', 'grad_w_in', 'grad_vn_g', 'grad_vn_b', 'grad_w_s', 'grad_b_s', 'grad_w_pa', 'grad_w_pb', 'grad_w_out', 'grad_w_pe', 'grad_w_pg', 'grad_ln_g', 'grad_ln_b', 'delta_w_in', 'delta_vn_g', 'delta_vn_b', 'delta_w_s', 'delta_b_s', 'delta_w_pa', 'delta_w_pb', 'delta_w_out', 'delta_w_pe', 'delta_w_pg', 'delta_ln_g', 'delta_ln_b', 'new_m_w_in', 'new_m_vn_g', 'new_m_vn_b', 'new_m_w_s', 'new_m_b_s', 'new_m_w_pa', 'new_m_w_pb', 'new_m_w_out', 'new_m_w_pe', 'new_m_w_pg', 'new_m_ln_g', 'new_m_ln_b', 'new_v_w_in', 'new_v_vn_g', 'new_v_vn_b', 'new_v_w_s', 'new_v_b_s', 'new_v_w_pa', 'new_v_w_pb', 'new_v_w_out', 'new_v_w_pe', 'new_v_w_pg', 'new_v_ln_g', 'new_v_ln_b']
TWIN_LEAF_KINDS = {'loss': 'loss', 'grad_x': 'grad_x', 'grad_w_in': 'grad_w', 'grad_vn_g': 'grad_w', 'grad_vn_b': 'grad_w', 'grad_w_s': 'grad_w', 'grad_b_s': 'grad_w', 'grad_w_pa': 'grad_w', 'grad_w_pb': 'grad_w', 'grad_w_out': 'grad_w', 'grad_w_pe': 'grad_w', 'grad_w_pg': 'grad_w', 'grad_ln_g': 'grad_w', 'grad_ln_b': 'grad_w', 'delta_w_in': 'delta_w', 'delta_vn_g': 'delta_w', 'delta_vn_b': 'delta_w', 'delta_w_s': 'delta_w', 'delta_b_s': 'delta_w', 'delta_w_pa': 'delta_w', 'delta_w_pb': 'delta_w', 'delta_w_out': 'delta_w', 'delta_w_pe': 'delta_w', 'delta_w_pg': 'delta_w', 'delta_ln_g': 'delta_w', 'delta_ln_b': 'delta_w', 'new_m_w_in': 'new_m', 'new_m_vn_g': 'new_m', 'new_m_vn_b': 'new_m', 'new_m_w_s': 'new_m', 'new_m_b_s': 'new_m', 'new_m_w_pa': 'new_m', 'new_m_w_pb': 'new_m', 'new_m_w_out': 'new_m', 'new_m_w_pe': 'new_m', 'new_m_w_pg': 'new_m', 'new_m_ln_g': 'new_m', 'new_m_ln_b': 'new_m', 'new_v_w_in': 'new_v', 'new_v_vn_g': 'new_v', 'new_v_vn_b': 'new_v', 'new_v_w_s': 'new_v', 'new_v_b_s': 'new_v', 'new_v_w_pa': 'new_v', 'new_v_w_pb': 'new_v', 'new_v_w_out': 'new_v', 'new_v_w_pe': 'new_v', 'new_v_w_pg': 'new_v', 'new_v_ln_g': 'new_v', 'new_v_ln_b': 'new_v'}


def _forward(args):
    return _fwd_reference(*[args[k] for k in FWD_PARAMS])


def _output_shape():
    out = _jax.eval_shape(lambda: _forward(_fwd_setup_inputs(0)))
    return out.shape, out.dtype

N_MICROBATCH = 1
ADAM_LR = 0.001
ADAM_B1 = 0.9
ADAM_B2 = 0.999
ADAM_EPS = 1e-08
ADAM_WD = 0.01
ADAM_STEP = 10
PER_EXAMPLE_BATCH_AXIS = {'x': 0, 'p': 1, 'loss_target': 0}
SHARED_INPUTS = []
_WEIGHT_DTYPES = {'w_in': _jnp.float32, 'vn_g': _jnp.float32, 'vn_b': _jnp.float32, 'w_s': _jnp.float32, 'b_s': _jnp.float32, 'w_pa': _jnp.float32, 'w_pb': _jnp.float32, 'w_out': _jnp.float32, 'w_pe': _jnp.float32, 'w_pg': _jnp.float32, 'ln_g': _jnp.float32, 'ln_b': _jnp.float32}
MOMENT_SCALE = {'w_in': 9.714638e-03, 'vn_g': 9.388338e-03, 'vn_b': 9.136357e-03, 'w_s': 9.327335e-03, 'b_s': 1.303952e-02, 'w_pa': 1.607743e-02, 'w_pb': 8.482027e-03, 'w_out': 4.316336e-02, 'w_pe': 5.481272e-02, 'w_pg': 2.904127e-02, 'ln_g': 1.599282e+01, 'ln_b': 5.392933e-01}


def _to_microbatches(a, axis):
    t = _jnp.moveaxis(a, axis, 0)
    t = t.reshape((N_MICROBATCH, t.shape[0] // N_MICROBATCH) + t.shape[1:])
    return _jnp.moveaxis(t, 1, axis + 1)


def setup_inputs(seed: int = 0) -> dict:
    inp = _fwd_setup_inputs(seed)
    key = _jax.random.fold_in(_jax.random.key(seed), 7919)
    shape, _ = _output_shape()
    out = dict(inp)
    out["loss_target"] = _jax.random.normal(_jax.random.fold_in(key, 0), shape, _jnp.float32)
    for i, name in enumerate(TWIN_WEIGHTS):
        w = inp[name].astype(_jnp.float32)
        if MOMENT_SCALE is None:
            s = _jnp.sqrt(_jnp.mean(_jnp.square(w)) + 1e-30)
        else:
            s = MOMENT_SCALE[name]
        km, kv = _jax.random.split(_jax.random.fold_in(key, i + 1))
        out[name] = w
        out["m_" + name] = s * _jax.random.normal(km, w.shape, _jnp.float32)
        out["v_" + name] = (s * s) * _jax.random.uniform(kv, w.shape, _jnp.float32, 0.5, 1.5)
    if N_MICROBATCH > 1:
        for name, axis in PER_EXAMPLE_BATCH_AXIS.items():
            out[name] = _to_microbatches(out[name], axis)
    return {'x': out['x'], 'p': out['p'], 'w_in': out['w_in'], 'vn_g': out['vn_g'], 'vn_b': out['vn_b'], 'w_s': out['w_s'], 'b_s': out['b_s'], 'w_pa': out['w_pa'], 'w_pb': out['w_pb'], 'w_out': out['w_out'], 'w_pe': out['w_pe'], 'w_pg': out['w_pg'], 'ln_g': out['ln_g'], 'ln_b': out['ln_b'], 'loss_target': out['loss_target'], 'm_w_in': out['m_w_in'], 'm_vn_g': out['m_vn_g'], 'm_vn_b': out['m_vn_b'], 'm_w_s': out['m_w_s'], 'm_b_s': out['m_b_s'], 'm_w_pa': out['m_w_pa'], 'm_w_pb': out['m_w_pb'], 'm_w_out': out['m_w_out'], 'm_w_pe': out['m_w_pe'], 'm_w_pg': out['m_w_pg'], 'm_ln_g': out['m_ln_g'], 'm_ln_b': out['m_ln_b'], 'v_w_in': out['v_w_in'], 'v_vn_g': out['v_vn_g'], 'v_vn_b': out['v_vn_b'], 'v_w_s': out['v_w_s'], 'v_b_s': out['v_b_s'], 'v_w_pa': out['v_w_pa'], 'v_w_pb': out['v_w_pb'], 'v_w_out': out['v_w_out'], 'v_w_pe': out['v_w_pe'], 'v_w_pg': out['v_w_pg'], 'v_ln_g': out['v_ln_g'], 'v_ln_b': out['v_ln_b']}


def _loss(weights, diff, rest, loss_target):
    with _jax.named_scope("forward"):
        args = {**rest, TWIN_DIFF_INPUT: diff, **{k: w.astype(_WEIGHT_DTYPES[k]) for k, w in weights.items()}}
        y = _forward(args)
    with _jax.named_scope("loss_head"):
        err = _jnp.square(y.astype(_jnp.float32) - loss_target)
        return 0.5 * _jnp.sum(_jnp.mean(err, axis=-1)) if err.ndim else 0.5 * err


def _adamw(w, g, m, v):
    m = ADAM_B1 * m + (1.0 - ADAM_B1) * g
    v = ADAM_B2 * v + (1.0 - ADAM_B2) * _jnp.square(g)
    m_hat = m / (1.0 - ADAM_B1 ** ADAM_STEP)
    v_hat = v / (1.0 - ADAM_B2 ** ADAM_STEP)
    delta = -ADAM_LR * (m_hat / (_jnp.sqrt(v_hat) + ADAM_EPS) + ADAM_WD * w)
    return delta, m, v


def reference(x, p, w_in, vn_g, vn_b, w_s, b_s, w_pa, w_pb, w_out, w_pe, w_pg, ln_g, ln_b, loss_target, m_w_in, m_vn_g, m_vn_b, m_w_s, m_b_s, m_w_pa, m_w_pb, m_w_out, m_w_pe, m_w_pg, m_ln_g, m_ln_b, v_w_in, v_vn_g, v_vn_b, v_w_s, v_b_s, v_w_pa, v_w_pb, v_w_out, v_w_pe, v_w_pg, v_ln_g, v_ln_b):
    given = dict(x=x, p=p, w_in=w_in, vn_g=vn_g, vn_b=vn_b, w_s=w_s, b_s=b_s, w_pa=w_pa, w_pb=w_pb, w_out=w_out, w_pe=w_pe, w_pg=w_pg, ln_g=ln_g, ln_b=ln_b, loss_target=loss_target, m_w_in=m_w_in, m_vn_g=m_vn_g, m_vn_b=m_vn_b, m_w_s=m_w_s, m_b_s=m_b_s, m_w_pa=m_w_pa, m_w_pb=m_w_pb, m_w_out=m_w_out, m_w_pe=m_w_pe, m_w_pg=m_w_pg, m_ln_g=m_ln_g, m_ln_b=m_ln_b, v_w_in=v_w_in, v_vn_g=v_vn_g, v_vn_b=v_vn_b, v_w_s=v_w_s, v_b_s=v_b_s, v_w_pa=v_w_pa, v_w_pb=v_w_pb, v_w_out=v_w_out, v_w_pe=v_w_pe, v_w_pg=v_w_pg, v_ln_g=v_ln_g, v_ln_b=v_ln_b)
    weights = {n: given[n] for n in TWIN_WEIGHTS}
    shared = {n: given[n] for n in SHARED_INPUTS}
    per_example = {n: given[n] for n in ['x', 'p']}
    grad_fn = _jax.value_and_grad(_loss, argnums=(0, 1))

    def one_microbatch(ex, loss_target):
        ex = dict(ex)
        diff = ex.pop(TWIN_DIFF_INPUT)
        return grad_fn(weights, diff, {**shared, **ex}, loss_target)

    if N_MICROBATCH == 1:
        loss, (grad_w, grad_x) = one_microbatch(per_example, given["loss_target"])
    else:
        def body(carry, xs):
            loss_sum, grad_sum = carry
            l_k, (gw_k, gx_k) = one_microbatch(xs[0], xs[1])
            with _jax.named_scope("update"):
                return (loss_sum + l_k, _jax.tree.map(_jnp.add, grad_sum, gw_k)), gx_k

        init = (_jnp.zeros((), _jnp.float32), _jax.tree.map(_jnp.zeros_like, weights))
        (loss, grad_w), grad_x = _jax.lax.scan(body, init, (per_example, given["loss_target"]))
    with _jax.named_scope("update"):
        delta_w, new_m, new_v = {}, {}, {}
        for n in TWIN_WEIGHTS:
            delta_w[n], new_m[n], new_v[n] = _adamw(weights[n], grad_w[n], given["m_" + n], given["v_" + n])
    return (loss, grad_x, *[grad_w[n] for n in TWIN_WEIGHTS], *[delta_w[n] for n in TWIN_WEIGHTS],
            *[new_m[n] for n in TWIN_WEIGHTS], *[new_v[n] for n in TWIN_WEIGHTS])
```

```python
import functools

import jax
import jax.numpy as jnp
from jax import lax
from jax.experimental import pallas as pl
from jax.experimental.pallas import tpu as pltpu

F32 = jnp.float32
BF16 = jnp.bfloat16
LANES = 128
SUBLANES = 8
HEAD_DIM = 64
HEADS_PER_BLOCK = LANES // HEAD_DIM
LN_EPS = 1e-5
N_IN = 9
N_CHIPS = 4
N_DEV = 8
ADAM_LR = 0.001
ADAM_B1 = 0.9
ADAM_B2 = 0.999
ADAM_EPS = 1e-08
ADAM_WD = 0.01
ADAM_STEP = 10
MESH = pl.DeviceIdType.MESH
ANY = pl.BlockSpec(memory_space=pl.ANY)
BIG = ("w_in", "w_pa", "w_pb", "w_out", "w_pe", "w_pg")
COL_SHARDED = {"w_in": True, "w_pa": False, "w_pb": False, "w_out": False, "w_pe": True, "w_pg": False}
SMALL = ("vn_g", "vn_b", "w_s", "b_s", "ln_g", "ln_b")
WEIGHTS = ("w_in", "vn_g", "vn_b", "w_s", "b_s", "w_pa", "w_pb", "w_out", "w_pe", "w_pg", "ln_g", "ln_b")


def _params(*sem):
    return pltpu.CompilerParams(dimension_semantics=sem)


def _dot(a, b):
    return jnp.dot(a, b, preferred_element_type=F32)


def _dot_nt(a, b):
    return lax.dot_general(a, b, (((1,), (1,)), ((), ())), preferred_element_type=F32)


def _dot_tn(a, b):
    return lax.dot_general(a, b, (((0,), (0,)), ((), ())), preferred_element_type=F32)


def _sigmoid(a):
    return 1.0 / (1.0 + jnp.exp(-a))


def _row_tile(rows, cols, cap_bytes):
    best = None
    for t in range(16, rows + 1, 16):
        if rows % t == 0 and t * cols * 4 <= cap_bytes:
            best = t
    return best or rows


def _col_tile(cols, cap):
    best = LANES
    for t in range(LANES, min(cols, cap) + 1, LANES):
        if cols % t == 0:
            best = t
    return best


def _ln_stats(h):
    mu = jnp.mean(h, axis=-1, keepdims=True)
    hc = h - mu
    var = jnp.mean(hc * hc, axis=-1, keepdims=True)
    rstd = lax.rsqrt(var + LN_EPS)
    return hc * rstd, rstd


def _ln_bwd(dxhat, xhat, rstd):
    m1 = jnp.mean(dxhat, axis=-1, keepdims=True)
    m2 = jnp.mean(dxhat * xhat, axis=-1, keepdims=True)
    return rstd * (dxhat - m1 - xhat * m2)


def _sum_rows8(a):
    t, d = a.shape
    return jnp.sum(a.reshape(t // SUBLANES, SUBLANES, d), axis=0)


def _cast_bf16(a):
    nl, r, c = a.shape
    tr = _row_tile(r, c, 2 << 20)

    def body(a_ref, o_ref):
        o_ref[...] = a_ref[...].astype(BF16)

    spec = pl.BlockSpec((1, tr, c), lambda l, i: (l, i, 0))
    return pl.pallas_call(body, name="cast_bf16", grid=(nl, r // tr), in_specs=[spec], out_specs=spec,
                          out_shape=jax.ShapeDtypeStruct(a.shape, BF16),
                          compiler_params=_params("parallel", "parallel"))(a)


def _add_own_half(own, recv, core, col_sharded):
    r, c = recv.shape
    tr = _row_tile(r, c, 2 << 20)
    nb = r // tr

    def body(core_ref, own_ref, recv_ref, o_ref):
        o_ref[...] = own_ref[...] + recv_ref[...]

    if col_sharded:
        own_spec = pl.BlockSpec((tr, c), lambda i, core_ref: (core_ref[0] * nb + i, 0))
    else:
        own_spec = pl.BlockSpec((tr, c), lambda i, core_ref: (i, core_ref[0]))
    spec = pl.BlockSpec((tr, c), lambda i, core_ref: (i, 0))
    grid_spec = pltpu.PrefetchScalarGridSpec(num_scalar_prefetch=1, grid=(nb,), in_specs=[own_spec, spec],
                                             out_specs=spec)
    return pl.pallas_call(body, name="add_own_half", grid_spec=grid_spec,
                          out_shape=jax.ShapeDtypeStruct(recv.shape, F32),
                          compiler_params=_params("parallel"))(core, own, recv)


def _sum_slots(a):
    n, r, c = a.shape
    tr = _row_tile(r, c * n, 4 << 20)

    def body(a_ref, o_ref):
        acc = a_ref[0]
        for s in range(1, n):
            acc = acc + a_ref[s]
        o_ref[...] = acc

    return pl.pallas_call(body, name="sum_slots", grid=(r // tr,),
                          in_specs=[pl.BlockSpec((n, tr, c), lambda i: (0, i, 0))],
                          out_specs=pl.BlockSpec((tr, c), lambda i: (i, 0)),
                          out_shape=jax.ShapeDtypeStruct((r, c), F32),
                          compiler_params=_params("parallel"))(a)


def _adamw(w, g, m, v):
    r, c = w.shape
    tr = _row_tile(r, c, 1 << 20)

    def body(w_ref, g_ref, m_ref, v_ref, d_ref, nm_ref, nv_ref):
        gg = g_ref[...]
        nm = ADAM_B1 * m_ref[...] + (1.0 - ADAM_B1) * gg
        nv = ADAM_B2 * v_ref[...] + (1.0 - ADAM_B2) * (gg * gg)
        m_hat = nm / (1.0 - ADAM_B1 ** ADAM_STEP)
        v_hat = nv / (1.0 - ADAM_B2 ** ADAM_STEP)
        d_ref[...] = -ADAM_LR * (m_hat / (jnp.sqrt(v_hat) + ADAM_EPS) + ADAM_WD * w_ref[...])
        nm_ref[...] = nm
        nv_ref[...] = nv

    spec = pl.BlockSpec((tr, c), lambda i: (i, 0))
    sh = jax.ShapeDtypeStruct((r, c), F32)
    return pl.pallas_call(body, name="adamw", grid=(r // tr,), in_specs=[spec] * 4, out_specs=[spec] * 3,
                          out_shape=[sh, sh, sh], compiler_params=_params("parallel"))(w, g, m, v)


def _proj_fwd(x, w):
    s, d = x.shape
    n = w.shape[1]
    tm, tn = min(s, 512), _col_tile(n, 1024)

    def body(x_ref, w_ref, o_ref):
        o_ref[...] = _dot(x_ref[...].astype(BF16), w_ref[...]).astype(BF16)

    return pl.pallas_call(body, name="proj_fwd", grid=(s // tm, n // tn),
                          in_specs=[pl.BlockSpec((tm, d), lambda i, j: (i, 0)),
                                    pl.BlockSpec((d, tn), lambda i, j: (0, j))],
                          out_specs=pl.BlockSpec((tm, tn), lambda i, j: (i, j)),
                          out_shape=jax.ShapeDtypeStruct((s, n), BF16),
                          compiler_params=_params("parallel", "parallel"))(x, w)


def _matmul_tn(a, b):
    s, m = a.shape
    n = b.shape[1]
    tk, tn = min(s, 512), _col_tile(n, 1024)
    nk = s // tk

    def body(a_ref, b_ref, o_ref):
        @pl.when(pl.program_id(1) == 0)
        def _():
            o_ref[...] = jnp.zeros_like(o_ref)

        o_ref[...] += _dot_tn(a_ref[...].astype(BF16), b_ref[...].astype(BF16))

    return pl.pallas_call(body, name="matmul_tn", grid=(n // tn, nk),
                          in_specs=[pl.BlockSpec((tk, m), lambda j, k: (k, 0)),
                                    pl.BlockSpec((tk, tn), lambda j, k: (k, j))],
                          out_specs=pl.BlockSpec((m, tn), lambda j, k: (0, j)),
                          out_shape=jax.ShapeDtypeStruct((m, n), F32),
                          compiler_params=_params("parallel", "arbitrary"))(a, b)


def _dx_matmul(dxr, dproj, w):
    s, d = dxr.shape
    n = w.shape[1]
    tm, tk = min(s, 512), _col_tile(n, 1024)

    def body(r_ref, g_ref, w_ref, o_ref):
        @pl.when(pl.program_id(1) == 0)
        def _():
            o_ref[...] = r_ref[...]

        o_ref[...] += _dot_nt(g_ref[...], w_ref[...])

    return pl.pallas_call(body, name="dx_matmul", grid=(s // tm, n // tk),
                          in_specs=[pl.BlockSpec((tm, d), lambda i, k: (i, 0)),
                                    pl.BlockSpec((tm, tk), lambda i, k: (i, k)),
                                    pl.BlockSpec((d, tk), lambda i, k: (0, k))],
                          out_specs=pl.BlockSpec((tm, d), lambda i, k: (i, 0)),
                          out_shape=jax.ShapeDtypeStruct((s, d), F32),
                          compiler_params=_params("parallel", "arbitrary"))(dxr, dproj, w)


def _mix_chunks(ws_ref, src_ref, dst_ref, bias_ref, t, groups, chunk):
    for c in range(t // chunk):
        rows = slice(c * chunk, (c + 1) * chunk)
        for g in range(groups):
            cols = slice(g * LANES, (g + 1) * LANES)
            val = _dot(ws_ref[g], src_ref[rows, cols])
            if bias_ref is not None:
                val = val + bias_ref[:, g:g + 1]
            dst_ref[rows, cols] = val


def _gmlp_fwd(proj, vn_g, vn_b, ws_m, bs_t):
    s = proj.shape[0]
    d = proj.shape[1] // N_IN
    groups, chunk = ws_m.shape[0], ws_m.shape[1]
    t = min(s, 512)

    def body(u_ref, v_ref, ga_ref, g_ref, b_ref, ws_ref, bs_ref, o_ref, vn_ref, mix_ref):
        xhat, _ = _ln_stats(v_ref[...].astype(F32))
        vn_ref[...] = (xhat * g_ref[...] + b_ref[...]).astype(BF16)
        _mix_chunks(ws_ref, vn_ref, mix_ref, bs_ref, t, groups, chunk)
        ga = ga_ref[...].astype(F32)
        o_ref[...] = (u_ref[...].astype(F32) * mix_ref[...] * (ga * _sigmoid(ga))).astype(BF16)

    col = lambda j: pl.BlockSpec((t, d), lambda i: (i, j))
    full = lambda a: pl.BlockSpec(a.shape, lambda i: (0,) * a.ndim)
    return pl.pallas_call(body, name="gmlp_fwd", grid=(s // t,),
                          in_specs=[col(0), col(1), col(2), full(vn_g), full(vn_b), full(ws_m), full(bs_t)],
                          out_specs=pl.BlockSpec((t, d), lambda i: (i, 0)),
                          out_shape=jax.ShapeDtypeStruct((s, d), BF16),
                          scratch_shapes=[pltpu.VMEM((t, d), BF16), pltpu.VMEM((t, d), F32)],
                          compiler_params=_params("parallel"))(proj, proj, proj, vn_g, vn_b, ws_m, bs_t)


def _gmlp_bwd(dproj, proj, dya, dq, dk, dv, vn_g, vn_b, ws_m, ws_mt, bs_t):
    s = proj.shape[0]
    d = proj.shape[1] // N_IN
    groups, chunk = ws_m.shape[0], ws_m.shape[1]
    t = min(s, 256)
    nsteps = s // t

    def body(dproj_hbm, u_ref, v_ref, ga_ref, dya_ref, dq_ref, dk_ref, dv_ref, g_ref, b_ref, ws_ref, wst_ref, bs_ref,
             o_ref, dg_ref, db_ref, dws_ref, dbs_ref,
             vn_ref, mix_ref, dm_ref, dvn_ref, dbs_acc, dg_acc, db_acc):
        del dproj_hbm
        i = pl.program_id(0)

        @pl.when(i == 0)
        def _():
            dws_ref[...] = jnp.zeros_like(dws_ref)
            dbs_acc[...] = jnp.zeros_like(dbs_acc)
            dg_acc[...] = jnp.zeros_like(dg_acc)
            db_acc[...] = jnp.zeros_like(db_acc)

        xhat, rstd = _ln_stats(v_ref[...].astype(F32))
        vn_ref[...] = (xhat * g_ref[...] + b_ref[...]).astype(BF16)
        _mix_chunks(ws_ref, vn_ref, mix_ref, bs_ref, t, groups, chunk)
        ga = ga_ref[...].astype(F32)
        sg = _sigmoid(ga)
        silu = ga * sg
        dsilu = sg * (1.0 + ga * (1.0 - sg))
        u = u_ref[...].astype(F32)
        dya_f = dya_ref[...].astype(F32)
        mix = mix_ref[...]
        o_ref[:, 0:d] = (dya_f * mix * silu).astype(BF16)
        o_ref[:, 2 * d:3 * d] = (dya_f * u * mix * dsilu).astype(BF16)
        dmix = dya_f * u * silu
        dm_ref[...] = dmix.astype(BF16)
        for c in range(t // chunk):
            dbs_acc[...] += dmix[c * chunk:(c + 1) * chunk, :]
        _mix_chunks(wst_ref, dm_ref, dvn_ref, None, t, groups, chunk)
        for c in range(t // chunk):
            rows = slice(c * chunk, (c + 1) * chunk)
            for g in range(groups):
                cols = slice(g * LANES, (g + 1) * LANES)
                dws_ref[g] += _dot_nt(dm_ref[rows, cols], vn_ref[rows, cols])
        dvn = dvn_ref[...]
        dg_acc[...] += _sum_rows8(dvn * xhat)
        db_acc[...] += _sum_rows8(dvn)
        o_ref[:, d:2 * d] = _ln_bwd(dvn * g_ref[...], xhat, rstd).astype(BF16)
        o_ref[:, 3 * d:4 * d] = dq_ref[...]
        o_ref[:, 4 * d:5 * d] = dk_ref[...]
        o_ref[:, 5 * d:6 * d] = dv_ref[...]

        @pl.when(i == nsteps - 1)
        def _():
            row = lax.broadcasted_iota(jnp.int32, (chunk, chunk), 0)
            col = lax.broadcasted_iota(jnp.int32, (chunk, chunk), 1)
            for g in range(groups):
                dws_ref[g] = jnp.where(col <= row, dws_ref[g], 0.0)
            lane = lax.broadcasted_iota(jnp.int32, (chunk, LANES), 1)
            res = jnp.zeros((chunk, LANES), F32)
            for g in range(groups):
                tot = jnp.sum(dbs_acc[:, g * LANES:(g + 1) * LANES], axis=1, keepdims=True)
                res = jnp.where(lane == g, tot, res)
            dbs_ref[...] = res
            dg_ref[...] = jnp.sum(dg_acc[...], axis=0, keepdims=True)
            db_ref[...] = jnp.sum(db_acc[...], axis=0, keepdims=True)

    col = lambda j: pl.BlockSpec((t, d), lambda i: (i, j))
    tok = pl.BlockSpec((t, d), lambda i: (i, 0))
    full = lambda a: pl.BlockSpec(a.shape, lambda i: (0,) * a.ndim)
    vec = jax.ShapeDtypeStruct((1, d), F32)
    outs = pl.pallas_call(
        body, name="gmlp_bwd", grid=(nsteps,),
        in_specs=[ANY, col(0), col(1), col(2), tok, tok, tok, tok,
                  full(vn_g), full(vn_b), full(ws_m), full(ws_mt), full(bs_t)],
        out_specs=[pl.BlockSpec((t, 6 * d), lambda i: (i, 0)),
                   pl.BlockSpec((1, d), lambda i: (0, 0)), pl.BlockSpec((1, d), lambda i: (0, 0)),
                   pl.BlockSpec((groups, chunk, chunk), lambda i: (0, 0, 0)),
                   pl.BlockSpec((chunk, LANES), lambda i: (0, 0))],
        out_shape=[jax.ShapeDtypeStruct(dproj.shape, BF16), vec, vec,
                   jax.ShapeDtypeStruct((groups, chunk, chunk), F32),
                   jax.ShapeDtypeStruct((chunk, LANES), F32)],
        scratch_shapes=[pltpu.VMEM((t, d), BF16), pltpu.VMEM((t, d), F32), pltpu.VMEM((t, d), BF16),
                        pltpu.VMEM((t, d), F32), pltpu.VMEM((chunk, d), F32),
                        pltpu.VMEM((SUBLANES, d), F32), pltpu.VMEM((SUBLANES, d), F32)],
        input_output_aliases={0: 0},
        compiler_params=_params("arbitrary"))(dproj, proj, proj, proj, dya, dq, dk, dv,
                                              vn_g, vn_b, ws_m, ws_mt, bs_t)
    return outs


def _log_sigmoid(z):
    return jnp.minimum(z, 0.0) - jnp.log1p(jnp.exp(-jnp.abs(z)))


def _rev_excl_cumsum(a, upper):
    hi = a.astype(BF16)
    lo = (a - hi.astype(F32)).astype(BF16)
    return _dot(hi, upper) + _dot(lo, upper)


def _attn_masks(tq):
    lane = lax.broadcasted_iota(jnp.int32, (tq, LANES), 1)
    row = lax.broadcasted_iota(jnp.int32, (tq, tq), 0)
    col = lax.broadcasted_iota(jnp.int32, (tq, tq), 1)
    return lane < HEAD_DIM, col < row, (row > col).astype(BF16)


def _split_heads(a, head0):
    zero = jnp.zeros_like(a)
    return [jnp.where(head0, a, zero), jnp.where(head0, zero, a)]


def _attn_fwd(proj):
    s = proj.shape[0]
    d = proj.shape[1] // N_IN
    hp = d // LANES
    tq = LANES
    nq = s // tq
    scale = HEAD_DIM ** -0.5

    def body(q_ref, k_ref, v_ref, o_ref, acc_ref, r_ref):
        i = pl.program_id(1)
        head0, strict, upper = _attn_masks(tq)
        qm = _split_heads((q_ref[...].astype(F32) * scale).astype(BF16), head0)
        acc_ref[...] = jnp.zeros_like(acc_ref)
        r_ref[...] = jnp.zeros_like(r_ref)

        def block(j, diag):
            off = pl.multiple_of(j * tq, tq)
            k2 = k_ref[pl.ds(off, tq), :]
            v2 = v_ref[pl.ds(off, tq), :]
            for h in range(HEADS_PER_BLOCK):
                z = _dot_nt(qm[h], k2)
                lsz = _log_sigmoid(z)
                keep = lsz - z
                if diag:
                    keep = jnp.where(strict, keep, 0.0)
                between = _rev_excl_cumsum(keep, upper)
                w = jnp.exp(lsz + between + r_ref[h])
                if diag:
                    w = jnp.where(strict, w, 0.0)
                acc_ref[h] += _dot(w.astype(BF16), v2)
                r_ref[h] += jnp.sum(keep, axis=1, keepdims=True)

        block(i, True)

        def step(n, carry):
            block(i - 1 - n, False)
            return carry

        lax.fori_loop(0, i, step, 0)
        o_ref[...] = jnp.where(head0, acc_ref[0], acc_ref[1])

    return pl.pallas_call(
        body, name="attn_fwd", grid=(hp, nq),
        in_specs=[pl.BlockSpec((tq, LANES), lambda h, i: (i, 3 * hp + h)),
                  pl.BlockSpec((s, LANES), lambda h, i: (0, 4 * hp + h)),
                  pl.BlockSpec((s, LANES), lambda h, i: (0, 5 * hp + h))],
        out_specs=pl.BlockSpec((tq, LANES), lambda h, i: (i, h)),
        out_shape=jax.ShapeDtypeStruct((s, d), F32),
        scratch_shapes=[pltpu.VMEM((HEADS_PER_BLOCK, tq, LANES), F32),
                        pltpu.VMEM((HEADS_PER_BLOCK, tq, LANES), F32)],
        compiler_params=_params("parallel", "parallel"))(proj, proj, proj)


def _attn_bwd(proj, o, do):
    s = proj.shape[0]
    d = proj.shape[1] // N_IN
    hp = d // LANES
    tq = LANES
    nq = s // tq
    scale = HEAD_DIM ** -0.5

    def body(q_ref, k_ref, v_ref, o_ref, do_ref, dq_ref, dk_ref, dv_ref,
             dq_acc, dk_acc, dv_acc, rk_ref, rg_ref):
        i = pl.program_id(1)
        head0, strict, upper = _attn_masks(tq)
        qm = _split_heads((q_ref[...].astype(F32) * scale).astype(BF16), head0)
        dom = _split_heads(do_ref[...], head0)
        prod = do_ref[...].astype(F32) * o_ref[...]
        delta = [jnp.sum(jnp.where(head0, prod, 0.0), axis=1, keepdims=True),
                 jnp.sum(jnp.where(head0, 0.0, prod), axis=1, keepdims=True)]

        @pl.when(i == 0)
        def _():
            dk_acc[...] = jnp.zeros_like(dk_acc)
            dv_acc[...] = jnp.zeros_like(dv_acc)

        dq_acc[...] = jnp.zeros_like(dq_acc)
        rk_ref[...] = jnp.zeros_like(rk_ref)
        rg_ref[...] = jnp.zeros_like(rg_ref)

        def block(j, diag):
            off = pl.multiple_of(j * tq, tq)
            k2 = k_ref[pl.ds(off, tq), :]
            v2 = v_ref[pl.ds(off, tq), :]
            for h in range(HEADS_PER_BLOCK):
                z = _dot_nt(qm[h], k2)
                lsz = _log_sigmoid(z)
                keep = lsz - z
                if diag:
                    keep = jnp.where(strict, keep, 0.0)
                between = _rev_excl_cumsum(keep, upper)
                w = jnp.exp(lsz + between + rk_ref[h])
                if diag:
                    w = jnp.where(strict, w, 0.0)
                wb = w.astype(BF16)
                g = _dot_nt(dom[h], v2) * wb.astype(F32)
                after = _rev_excl_cumsum(g, upper) + g + rg_ref[h]
                sig = jnp.exp(lsz)
                dz = g * (1.0 - sig) - sig * (delta[h] - after)
                if diag:
                    dz = jnp.where(strict, dz, 0.0)
                dzb = dz.astype(BF16)
                dq_acc[h] += _dot(dzb, k2)
                dk_acc[pl.ds(off, tq), :] += _dot_tn(dzb, qm[h])
                dv_acc[pl.ds(off, tq), :] += _dot_tn(wb, dom[h])
                rk_ref[h] += jnp.sum(keep, axis=1, keepdims=True)
                rg_ref[h] += jnp.sum(g, axis=1, keepdims=True)

        block(i, True)

        def step(n, carry):
            block(i - 1 - n, False)
            return carry

        lax.fori_loop(0, i, step, 0)
        dq_ref[...] = (jnp.where(head0, dq_acc[0], dq_acc[1]) * scale).astype(BF16)

        @pl.when(i == nq - 1)
        def _():
            dk_ref[...] = dk_acc[...].astype(BF16)
            dv_ref[...] = dv_acc[...].astype(BF16)

    blk = pl.BlockSpec((tq, LANES), lambda h, i: (i, h))
    seq = pl.BlockSpec((s, LANES), lambda h, i: (0, h))
    sh = jax.ShapeDtypeStruct((s, d), BF16)
    return pl.pallas_call(
        body, name="attn_bwd", grid=(hp, nq),
        in_specs=[pl.BlockSpec((tq, LANES), lambda h, i: (i, 3 * hp + h)),
                  pl.BlockSpec((s, LANES), lambda h, i: (0, 4 * hp + h)),
                  pl.BlockSpec((s, LANES), lambda h, i: (0, 5 * hp + h)),
                  blk, blk],
        out_specs=[blk, seq, seq], out_shape=[sh, sh, sh],
        scratch_shapes=[pltpu.VMEM((HEADS_PER_BLOCK, tq, LANES), F32),
                        pltpu.VMEM((s, LANES), F32), pltpu.VMEM((s, LANES), F32),
                        pltpu.VMEM((HEADS_PER_BLOCK, tq, LANES), F32),
                        pltpu.VMEM((HEADS_PER_BLOCK, tq, LANES), F32)],
        compiler_params=_params("parallel", "arbitrary"))(proj, proj, proj, o, do)


def _post_math(ya_ref, o_ref, gb_ref, ma_ref, mb_ref, x_ref, p_ref, wpa_ref, wpb_ref, wout_ref, wpe_ref, wpg_ref,
               alpha):
    gb = gb_ref[...].astype(F32)
    sgb = _sigmoid(gb)
    o = o_ref[...]
    yb = (o * (gb * sgb)).astype(BF16)
    pa = _dot(ya_ref[...], wpa_ref[...])
    pb = _dot(yb, wpb_ref[...])
    sa = _sigmoid(ma_ref[...].astype(F32))
    sb = _sigmoid(mb_ref[...].astype(F32))
    merged = (sa * pa + sb * pb).astype(BF16)
    h1 = alpha * x_ref[...] + _dot(merged, wout_ref[...])
    h1b = h1.astype(BF16)
    e = _dot(p_ref[...].astype(BF16), wpe_ref[...])
    sg = _sigmoid(_dot(h1b, wpg_ref[...]))
    h2 = h1 + e * sg
    return dict(gb=gb, sgb=sgb, o=o, yb=yb, pa=pa, pb=pb, sa=sa, sb=sb, merged=merged, h1b=h1b, e=e, sg=sg, h2=h2)


def _post_specs(tm, d, ple, weights):
    tok = pl.BlockSpec((tm, d), lambda i: (i, 0))
    col = lambda j: pl.BlockSpec((tm, d), lambda i: (i, j))
    full = lambda a: pl.BlockSpec(a.shape, lambda i: (0,) * a.ndim, pipeline_mode=pl.Buffered(1))
    return tok, [tok, tok, col(6), col(7), col(8), tok, pl.BlockSpec((tm, ple), lambda i: (i, 0))] + [
        full(w) for w in weights]


def _post_fwd(ya, o, proj, x, p, w_pa, w_pb, w_out, w_pe, w_pg, ln_g, ln_b, alpha):
    s, d = x.shape
    ple = p.shape[1]
    tm = min(s, 256)
    weights = (w_pa, w_pb, w_out, w_pe, w_pg, ln_g, ln_b)

    def body(ya_ref, o_ref, gb_ref, ma_ref, mb_ref, x_ref, p_ref, wpa_ref, wpb_ref, wout_ref, wpe_ref, wpg_ref,
             g_ref, b_ref, out_ref):
        f = _post_math(ya_ref, o_ref, gb_ref, ma_ref, mb_ref, x_ref, p_ref, wpa_ref, wpb_ref, wout_ref, wpe_ref,
                       wpg_ref, alpha)
        xhat, _ = _ln_stats(f["h2"])
        out_ref[...] = xhat * g_ref[...] + b_ref[...]

    tok, in_specs = _post_specs(tm, d, ple, weights)
    return pl.pallas_call(body, name="post_fwd", grid=(s // tm,), in_specs=in_specs, out_specs=tok,
                          out_shape=jax.ShapeDtypeStruct((s, d), F32),
                          compiler_params=_params("parallel"))(ya, o, proj, proj, proj, x, p, *weights)


def _post_bwd(dxo, ya, o, proj, x, p, w_pa, w_pb, w_out, w_pe, w_pg, ln_g, ln_b, alpha):
    s, d = x.shape
    ple = p.shape[1]
    tm = min(s, 256)
    nsteps = s // tm
    weights = (w_pa, w_pb, w_out, w_pe, w_pg, ln_g, ln_b)

    def body(dxo_ref, ya_ref, o_ref, gb_ref, ma_ref, mb_ref, x_ref, p_ref, wpa_ref, wpb_ref, wout_ref, wpe_ref,
             wpg_ref, g_ref, b_ref,
             dproj_ref, dxr_ref, dya_ref, do_ref, de_ref, h1_ref, dzg_ref, mrg_ref, dh1_ref, dpa_ref, yb_ref, dpb_ref,
             dg_ref, db_ref, dg_acc, db_acc):
        i = pl.program_id(0)

        @pl.when(i == 0)
        def _():
            dg_acc[...] = jnp.zeros_like(dg_acc)
            db_acc[...] = jnp.zeros_like(db_acc)

        f = _post_math(ya_ref, o_ref, gb_ref, ma_ref, mb_ref, x_ref, p_ref, wpa_ref, wpb_ref, wout_ref, wpe_ref,
                       wpg_ref, alpha)
        xhat, rstd = _ln_stats(f["h2"])
        dxo = dxo_ref[...]
        dg_acc[...] += _sum_rows8(dxo * xhat)
        db_acc[...] += _sum_rows8(dxo)
        dh2 = _ln_bwd(dxo * g_ref[...], xhat, rstd)
        sg, e = f["sg"], f["e"]
        de_ref[...] = (dh2 * sg).astype(BF16)
        dzg = (dh2 * e * sg * (1.0 - sg)).astype(BF16)
        dzg_ref[...] = dzg
        dh1 = dh2 + _dot_nt(dzg, wpg_ref[...])
        dh1b = dh1.astype(BF16)
        dxr_ref[...] = alpha * dh1
        dh1_ref[...] = dh1b
        h1_ref[...] = f["h1b"]
        mrg_ref[...] = f["merged"]
        yb_ref[...] = f["yb"]
        dmerged = _dot_nt(dh1b, wout_ref[...])
        sa, sb = f["sa"], f["sb"]
        dpa = (dmerged * sa).astype(BF16)
        dpb = (dmerged * sb).astype(BF16)
        dpa_ref[...] = dpa
        dpb_ref[...] = dpb
        dproj_ref[:, d:2 * d] = (dmerged * f["pa"] * sa * (1.0 - sa)).astype(BF16)
        dproj_ref[:, 2 * d:3 * d] = (dmerged * f["pb"] * sb * (1.0 - sb)).astype(BF16)
        dya_ref[...] = _dot_nt(dpa, wpa_ref[...]).astype(BF16)
        dyb = _dot_nt(dpb, wpb_ref[...])
        gb, sgb = f["gb"], f["sgb"]
        do_ref[...] = (dyb * (gb * sgb)).astype(BF16)
        dproj_ref[:, 0:d] = (dyb * f["o"] * (sgb * (1.0 + gb * (1.0 - sgb)))).astype(BF16)

        @pl.when(i == nsteps - 1)
        def _():
            dg_ref[...] = jnp.sum(dg_acc[...], axis=0, keepdims=True)
            db_ref[...] = jnp.sum(db_acc[...], axis=0, keepdims=True)

    tok, in_specs = _post_specs(tm, d, ple, weights)
    vec_spec = pl.BlockSpec((1, d), lambda i: (0, 0))
    vec = jax.ShapeDtypeStruct((1, d), F32)
    act = jax.ShapeDtypeStruct((s, d), BF16)
    return pl.pallas_call(
        body, name="post_bwd", grid=(nsteps,), in_specs=[tok] + in_specs,
        out_specs=[pl.BlockSpec((tm, 3 * d), lambda i: (i, 2)), tok] + [tok] * 10 + [vec_spec, vec_spec],
        out_shape=[jax.ShapeDtypeStruct((s, N_IN * d), BF16), jax.ShapeDtypeStruct((s, d), F32)] + [act] * 10 + [vec, vec],
        scratch_shapes=[pltpu.VMEM((SUBLANES, d), F32), pltpu.VMEM((SUBLANES, d), F32)],
        compiler_params=_params("arbitrary"))(dxo, ya, o, proj, proj, proj, x, p, *weights)


def _loss_head(y, target):
    s, d = y.shape
    tm = min(s, 512)

    def body(y_ref, t_ref, dy_ref, l_ref):
        @pl.when(pl.program_id(0) == 0)
        def _():
            l_ref[...] = jnp.zeros_like(l_ref)

        err = y_ref[...] - t_ref[...]
        dy_ref[...] = err / d
        row = jnp.sum(err * err, axis=1, keepdims=True) / d
        l_ref[...] += 0.5 * jnp.sum(row, axis=0, keepdims=True)

    tok = pl.BlockSpec((tm, d), lambda i: (i, 0))
    return pl.pallas_call(body, name="loss_head", grid=(s // tm,), in_specs=[tok, tok],
                          out_specs=[tok, pl.BlockSpec((SUBLANES, LANES), lambda i: (0, 0))],
                          out_shape=[jax.ShapeDtypeStruct((s, d), F32),
                                     jax.ShapeDtypeStruct((SUBLANES, LANES), F32)],
                          compiler_params=_params("arbitrary"))(y, target)


def _position():
    x, y, c = lax.axis_index("x"), lax.axis_index("y"), lax.axis_index("c")
    chips = [(1 - x, y), (x, 1 - y), (1 - x, 1 - y)]
    return x, y, c, chips


def _shard_of(ref, col_sharded, j, n):
    off = pl.multiple_of(j * n, n)
    return ref.at[:, pl.ds(off, n)] if col_sharded else ref.at[pl.ds(off, n), :]


def _half_of(ref, col_sharded, h, n):
    off = pl.multiple_of(h * n, n)
    return ref.at[pl.ds(off, n), :] if col_sharded else ref.at[:, pl.ds(off, n)]


def _gather_weights(shards):
    nl = shards["w_in"].shape[0]
    n_w = len(BIG)
    full_shapes = []
    for name in BIG:
        _, r, c = shards[name].shape
        full_shapes.append((r, c * N_CHIPS) if COL_SHARDED[name] else (r * N_CHIPS, c))

    def body(*refs):
        ins, outs = refs[:n_w], refs[n_w:n_w + nl * n_w]
        send_sems, recv_sems, loc_sems = refs[n_w + nl * n_w:]
        x, y, c, chips = _position()
        my_chip = 2 * x + y
        local, sends, recvs = [], [], []
        for l in range(nl):
            for w, name in enumerate(BIG):
                cs = COL_SHARDED[name]
                n = ins[w].shape[2] if cs else ins[w].shape[1]
                out = outs[l * n_w + w]
                src = ins[w].at[l]
                local.append(pltpu.make_async_copy(src, _shard_of(out, cs, my_chip, n), loc_sems.at[l * n_w + w]))
                for j, chip in enumerate(chips):
                    k = (l * n_w + w) * 3 + j
                    sends.append(pltpu.make_async_remote_copy(
                        src_ref=src, dst_ref=_shard_of(out, cs, my_chip, n), send_sem=send_sems.at[k],
                        recv_sem=recv_sems.at[k], device_id=(chip[0], chip[1], c), device_id_type=MESH))
                    recvs.append(pltpu.make_async_remote_copy(
                        src_ref=src, dst_ref=_shard_of(out, cs, 2 * chip[0] + chip[1], n), send_sem=send_sems.at[k],
                        recv_sem=recv_sems.at[k], device_id=(chip[0], chip[1], c), device_id_type=MESH))
        for cp in local + sends:
            cp.start()
        for cp in recvs:
            cp.wait_recv()
        for cp in sends:
            cp.wait_send()
        for cp in local:
            cp.wait()

    outs = pl.pallas_call(
        body, name="gather_weights", in_specs=[ANY] * n_w, out_specs=[ANY] * (nl * n_w),
        out_shape=[jax.ShapeDtypeStruct(full_shapes[w], BF16) for _ in range(nl) for w in range(n_w)],
        scratch_shapes=[pltpu.SemaphoreType.DMA((nl * n_w * 3,)), pltpu.SemaphoreType.DMA((nl * n_w * 3,)),
                        pltpu.SemaphoreType.DMA((nl * n_w,))],
    )(*[shards[name] for name in BIG])
    return [{name: outs[l * n_w + w] for w, name in enumerate(BIG)} for l in range(nl)]


def _half_shape(shape, col_sharded):
    r, c = shape
    return (r // 2, c) if col_sharded else (r, c // 2)


def _exchange_halves(grads):
    nl, n_w = len(grads), len(BIG)
    flat = [grads[l][name] for l in range(nl) for name in BIG]
    n_arr = len(flat)

    def body(*refs):
        ins, outs = refs[:n_arr], refs[n_arr:2 * n_arr]
        send_sems, recv_sems = refs[2 * n_arr:]
        x, y, c, _ = _position()
        copies = []
        for a in range(n_arr):
            cs = COL_SHARDED[BIG[a % n_w]]
            n = outs[a].shape[0] if cs else outs[a].shape[1]
            copies.append(pltpu.make_async_remote_copy(
                src_ref=_half_of(ins[a], cs, 1 - c, n), dst_ref=outs[a], send_sem=send_sems.at[a],
                recv_sem=recv_sems.at[a], device_id=(x, y, 1 - c), device_id_type=MESH))
        for cp in copies:
            cp.start()
        for cp in copies:
            cp.wait_recv()
        for cp in copies:
            cp.wait_send()

    outs = pl.pallas_call(
        body, name="exchange_halves", in_specs=[ANY] * n_arr, out_specs=[ANY] * n_arr,
        out_shape=[jax.ShapeDtypeStruct(_half_shape(a.shape, COL_SHARDED[BIG[i % n_w]]), F32)
                   for i, a in enumerate(flat)],
        scratch_shapes=[pltpu.SemaphoreType.DMA((n_arr,)), pltpu.SemaphoreType.DMA((n_arr,))],
    )(*flat)
    return [{name: outs[l * n_w + w] for w, name in enumerate(BIG)} for l in range(nl)]


def _scatter_to_owners(halves):
    nl, n_w = len(halves), len(BIG)
    flat = [halves[l][name] for l in range(nl) for name in BIG]
    n_arr = len(flat)

    def piece_shape(a, cs):
        r, c = a.shape
        return (r, c // N_CHIPS) if cs else (r // N_CHIPS, c)

    def body(*refs):
        ins, outs = refs[:n_arr], refs[n_arr:2 * n_arr]
        send_sems, recv_sems, loc_sems = refs[2 * n_arr:]
        x, y, c, chips = _position()
        my_chip = 2 * x + y
        local, sends, recvs = [], [], []
        for a in range(n_arr):
            cs = COL_SHARDED[BIG[a % n_w]]
            n = outs[a].shape[2] if cs else outs[a].shape[1]
            local.append(pltpu.make_async_copy(_shard_of(ins[a], cs, my_chip, n), outs[a].at[my_chip],
                                               loc_sems.at[a]))
            for j, chip in enumerate(chips):
                k = a * 3 + j
                their = 2 * chip[0] + chip[1]
                sends.append(pltpu.make_async_remote_copy(
                    src_ref=_shard_of(ins[a], cs, their, n), dst_ref=outs[a].at[my_chip], send_sem=send_sems.at[k],
                    recv_sem=recv_sems.at[k], device_id=(chip[0], chip[1], c), device_id_type=MESH))
                recvs.append(pltpu.make_async_remote_copy(
                    src_ref=_shard_of(ins[a], cs, their, n), dst_ref=outs[a].at[their], send_sem=send_sems.at[k],
                    recv_sem=recv_sems.at[k], device_id=(chip[0], chip[1], c), device_id_type=MESH))
        for cp in local + sends:
            cp.start()
        for cp in recvs:
            cp.wait_recv()
        for cp in sends:
            cp.wait_send()
        for cp in local:
            cp.wait()

    outs = pl.pallas_call(
        body, name="scatter_to_owners", in_specs=[ANY] * n_arr, out_specs=[ANY] * n_arr,
        out_shape=[jax.ShapeDtypeStruct((N_CHIPS,) + piece_shape(a, COL_SHARDED[BIG[i % n_w]]), F32)
                   for i, a in enumerate(flat)],
        scratch_shapes=[pltpu.SemaphoreType.DMA((n_arr * 3,)), pltpu.SemaphoreType.DMA((n_arr * 3,)),
                        pltpu.SemaphoreType.DMA((n_arr,))],
    )(*flat)
    return [{name: outs[l * n_w + w] for w, name in enumerate(BIG)} for l in range(nl)]


def _join_halves(halves):
    nl, n_w = len(halves), len(BIG)
    flat = [halves[l][name] for l in range(nl) for name in BIG]
    n_arr = len(flat)
    out_shapes = []
    for name in BIG:
        r, c = halves[0][name].shape
        out_shapes.append((nl, 2 * r, c) if COL_SHARDED[name] else (nl, r, 2 * c))

    def body(*refs):
        ins, outs = refs[:n_arr], refs[n_arr:n_arr + n_w]
        send_sems, recv_sems, loc_sems = refs[n_arr + n_w:]
        x, y, c, _ = _position()
        local, sends, recvs = [], [], []
        for a in range(n_arr):
            l, w = divmod(a, n_w)
            cs = COL_SHARDED[BIG[w]]
            n = ins[a].shape[0] if cs else ins[a].shape[1]
            out = outs[w].at[l]
            local.append(pltpu.make_async_copy(ins[a], _half_of(out, cs, c, n), loc_sems.at[a]))
            sends.append(pltpu.make_async_remote_copy(
                src_ref=ins[a], dst_ref=_half_of(out, cs, c, n), send_sem=send_sems.at[a], recv_sem=recv_sems.at[a],
                device_id=(x, y, 1 - c), device_id_type=MESH))
            recvs.append(pltpu.make_async_remote_copy(
                src_ref=ins[a], dst_ref=_half_of(out, cs, 1 - c, n), send_sem=send_sems.at[a],
                recv_sem=recv_sems.at[a], device_id=(x, y, 1 - c), device_id_type=MESH))
        for cp in local + sends:
            cp.start()
        for cp in recvs:
            cp.wait_recv()
        for cp in sends:
            cp.wait_send()
        for cp in local:
            cp.wait()

    outs = pl.pallas_call(
        body, name="join_halves", in_specs=[ANY] * n_arr, out_specs=[ANY] * n_w,
        out_shape=[jax.ShapeDtypeStruct(sh, F32) for sh in out_shapes],
        scratch_shapes=[pltpu.SemaphoreType.DMA((n_arr,)), pltpu.SemaphoreType.DMA((n_arr,)),
                        pltpu.SemaphoreType.DMA((n_arr,))],
    )(*flat)
    return dict(zip(BIG, outs))


def _gather_small(packed):
    r, lanes = packed.shape

    def body(in_ref, out_ref, send_sems, recv_sems, loc_sem):
        x, y, c, _ = _position()
        me = 4 * x + 2 * y + c
        local = pltpu.make_async_copy(in_ref, out_ref.at[me], loc_sem)
        local.start()
        sends, recvs = [], []
        for k in range(1, N_DEV):
            px, py, pc = x ^ (k >> 2), y ^ ((k >> 1) & 1), c ^ (k & 1)
            sends.append(pltpu.make_async_remote_copy(
                src_ref=in_ref, dst_ref=out_ref.at[me], send_sem=send_sems.at[k - 1], recv_sem=recv_sems.at[k - 1],
                device_id=(px, py, pc), device_id_type=MESH))
            recvs.append(pltpu.make_async_remote_copy(
                src_ref=in_ref, dst_ref=out_ref.at[4 * px + 2 * py + pc], send_sem=send_sems.at[k - 1],
                recv_sem=recv_sems.at[k - 1], device_id=(px, py, pc), device_id_type=MESH))
        for cp in sends:
            cp.start()
        for cp in recvs:
            cp.wait_recv()
        for cp in sends:
            cp.wait_send()
        local.wait()

    return pl.pallas_call(
        body, name="gather_small", in_specs=[ANY], out_specs=ANY,
        out_shape=jax.ShapeDtypeStruct((N_DEV, r, lanes), F32),
        scratch_shapes=[pltpu.SemaphoreType.DMA((N_DEV - 1,)), pltpu.SemaphoreType.DMA((N_DEV - 1,)),
                        pltpu.SemaphoreType.DMA],
    )(packed)


def _pack_small(t):
    return jnp.concatenate([t[name].reshape(-1, LANES) for name in SMALL], axis=0)


def _unpack_small(packed, like):
    out, row = {}, 0
    for name in SMALL:
        n = like[name].size // LANES
        out[name] = packed[row:row + n].reshape(like[name].shape)
        row += n
    return out


def kernel(x, p, w_in, vn_g, vn_b, w_s, b_s, w_pa, w_pb, w_out, w_pe, w_pg, ln_g, ln_b, loss_target, m_w_in, m_vn_g, m_vn_b, m_w_s, m_b_s, m_w_pa, m_w_pb, m_w_out, m_w_pe, m_w_pg, m_ln_g, m_ln_b, v_w_in, v_vn_g, v_vn_b, v_w_s, v_b_s, v_w_pa, v_w_pb, v_w_out, v_w_pe, v_w_pg, v_ln_g, v_ln_b):
    weights = dict(w_in=w_in, vn_g=vn_g, vn_b=vn_b, w_s=w_s, b_s=b_s, w_pa=w_pa, w_pb=w_pb, w_out=w_out, w_pe=w_pe,
                   w_pg=w_pg, ln_g=ln_g, ln_b=ln_b)
    mom1 = dict(w_in=m_w_in, vn_g=m_vn_g, vn_b=m_vn_b, w_s=m_w_s, b_s=m_b_s, w_pa=m_w_pa, w_pb=m_w_pb, w_out=m_w_out,
                w_pe=m_w_pe, w_pg=m_w_pg, ln_g=m_ln_g, ln_b=m_ln_b)
    mom2 = dict(w_in=v_w_in, vn_g=v_vn_g, vn_b=v_vn_b, w_s=v_w_s, b_s=v_b_s, w_pa=v_w_pa, w_pb=v_w_pb, w_out=v_w_out,
                w_pe=v_w_pe, w_pg=v_w_pg, ln_g=v_ln_g, ln_b=v_ln_b)
    nl, d = vn_g.shape
    chunk = w_s.shape[2]
    assert chunk == LANES and w_s.shape[3] == LANES and d % LANES == 0
    alpha = (2 * nl) ** 0.25
    core = lax.axis_index("c").astype(jnp.int32).reshape(1)

    full = _gather_weights({name: _cast_bf16(weights[name]) for name in BIG})
    causal = jnp.tril(jnp.ones((chunk, chunk), dtype=bool))
    ws_m = jnp.where(causal, w_s, 0.0).astype(BF16)
    ws_mt = jnp.swapaxes(ws_m, 2, 3)
    bs_t = jnp.swapaxes(b_s, 1, 2)

    xs, projs, yas, os_ = [x[0]], [], [], []
    for l in range(nl):
        proj = _proj_fwd(xs[l], full[l]["w_in"])
        ya = _gmlp_fwd(proj, vn_g[l:l + 1], vn_b[l:l + 1], ws_m[l], bs_t[l])
        o = _attn_fwd(proj)
        xs.append(_post_fwd(ya, o, proj, xs[l], p[l, 0], full[l]["w_pa"], full[l]["w_pb"], full[l]["w_out"],
                            full[l]["w_pe"], full[l]["w_pg"], ln_g[l:l + 1], ln_b[l:l + 1], alpha))
        projs.append(proj)
        yas.append(ya)
        os_.append(o)

    dx, loss_tile = _loss_head(xs[nl], loss_target[0])
    loss = lax.psum(loss_tile[0, 0], ("x", "y", "c"))

    big_grads = [None] * nl
    small_grads = {name: [None] * nl for name in SMALL}
    for l in reversed(range(nl)):
        w = full[l]
        (dproj, dxr, dya, do, de, h1b, dzg, merged, dh1, dpa, yb, dpb, dln_g, dln_b) = _post_bwd(
            dx, yas[l], os_[l], projs[l], xs[l], p[l, 0], w["w_pa"], w["w_pb"], w["w_out"], w["w_pe"], w["w_pg"],
            ln_g[l:l + 1], ln_b[l:l + 1], alpha)
        dq, dk, dv = _attn_bwd(projs[l], os_[l], do)
        dproj, dvn_g, dvn_b, dw_s, dbs_cols = _gmlp_bwd(dproj, projs[l], dya, dq, dk, dv, vn_g[l:l + 1],
                                                         vn_b[l:l + 1], ws_m[l], ws_mt[l], bs_t[l])
        big_grads[l] = dict(w_in=_matmul_tn(xs[l], dproj), w_pa=_matmul_tn(yas[l], dpa), w_pb=_matmul_tn(yb, dpb),
                            w_out=_matmul_tn(merged, dh1), w_pe=_matmul_tn(p[l, 0], de), w_pg=_matmul_tn(h1b, dzg))
        dx = _dx_matmul(dxr, dproj, w["w_in"])
        small_grads["vn_g"][l], small_grads["vn_b"][l] = dvn_g[0], dvn_b[0]
        small_grads["ln_g"][l], small_grads["ln_b"][l] = dln_g[0], dln_b[0]
        small_grads["w_s"][l] = dw_s
        small_grads["b_s"][l] = dbs_cols[:, :b_s.shape[1]].T

    received = _exchange_halves(big_grads)
    chip_sums = [{name: _add_own_half(big_grads[l][name], received[l][name], core, COL_SHARDED[name])
                  for name in BIG} for l in range(nl)]
    slots = _scatter_to_owners(chip_sums)
    grads = _join_halves([{name: _sum_slots(slots[l][name]) for name in BIG} for l in range(nl)])

    small_like = {name: weights[name] for name in SMALL}
    packed = _pack_small({name: jnp.stack(small_grads[name]) for name in SMALL})
    grads.update(_unpack_small(_sum_slots(_gather_small(packed)), small_like))

    delta, new_m, new_v = {}, {}, {}
    for name in BIG:
        sh = weights[name].shape
        flat = lambda a: a.reshape(sh[0] * sh[1], sh[2])
        dl, nm, nv = _adamw(flat(weights[name]), flat(grads[name]), flat(mom1[name]), flat(mom2[name]))
        delta[name], new_m[name], new_v[name] = dl.reshape(sh), nm.reshape(sh), nv.reshape(sh)
    dl, nm, nv = _adamw(_pack_small(small_like), _pack_small({n: grads[n] for n in SMALL}),
                        _pack_small({n: mom1[n] for n in SMALL}), _pack_small({n: mom2[n] for n in SMALL}))
    delta.update(_unpack_small(dl, small_like))
    new_m.update(_unpack_small(nm, small_like))
    new_v.update(_unpack_small(nv, small_like))

    return (loss, dx[None], *[grads[n] for n in WEIGHTS], *[delta[n] for n in WEIGHTS],
            *[new_m[n] for n in WEIGHTS], *[new_v[n] for n in WEIGHTS])
```

```python
import math

import jax
import jax.numpy as jnp
from jax import lax
from jax.experimental import pallas as pl
from jax.experimental.pallas import tpu as pltpu

F32 = jnp.float32
BF16 = jnp.bfloat16
LANES = 128
SUBLANES = 8
HEAD_DIM = 64
HEADS_PER_BLOCK = LANES // HEAD_DIM
LN_EPS = 1e-5
N_IN = 9
N_CHIPS = 4
N_DEV = 8
ADAM_LR = 0.001
ADAM_B1 = 0.9
ADAM_B2 = 0.999
ADAM_EPS = 1e-08
ADAM_WD = 0.01
ADAM_STEP = 10
MESH = pl.DeviceIdType.MESH
ANY = pl.BlockSpec(memory_space=pl.ANY)
BIG = ("w_in", "w_pa", "w_pb", "w_out", "w_pe", "w_pg")
COL_SHARDED = {"w_in": True, "w_pa": False, "w_pb": False, "w_out": False, "w_pe": True, "w_pg": False}
SMALL = ("vn_g", "vn_b", "w_s", "b_s", "ln_g", "ln_b")
WEIGHTS = ("w_in", "vn_g", "vn_b", "w_s", "b_s", "w_pa", "w_pb", "w_out", "w_pe", "w_pg", "ln_g", "ln_b")


def _params(*sem):
    return pltpu.CompilerParams(dimension_semantics=sem)


def _dot(a, b):
    return jnp.dot(a, b, preferred_element_type=F32)


def _dot_nt(a, b):
    return lax.dot_general(a, b, (((1,), (1,)), ((), ())), preferred_element_type=F32)


def _dot_tn(a, b):
    return lax.dot_general(a, b, (((0,), (0,)), ((), ())), preferred_element_type=F32)


def _sigmoid(a):
    return 1.0 / (1.0 + jnp.exp(-a))


def _row_tile(rows, cols, cap_bytes):
    best = None
    for t in range(16, rows + 1, 16):
        if rows % t == 0 and t * cols * 4 <= cap_bytes:
            best = t
    return best or rows


def _col_tile(cols, cap):
    best = LANES
    for t in range(LANES, min(cols, cap) + 1, LANES):
        if cols % t == 0:
            best = t
    return best


def _ln_stats(h):
    mu = jnp.mean(h, axis=-1, keepdims=True)
    hc = h - mu
    var = jnp.mean(hc * hc, axis=-1, keepdims=True)
    rstd = lax.rsqrt(var + LN_EPS)
    return hc * rstd, rstd


def _ln_bwd(dxhat, xhat, rstd):
    m1 = jnp.mean(dxhat, axis=-1, keepdims=True)
    m2 = jnp.mean(dxhat * xhat, axis=-1, keepdims=True)
    return rstd * (dxhat - m1 - xhat * m2)


def _sum_rows8(a):
    t, d = a.shape
    return jnp.sum(a.reshape(t // SUBLANES, SUBLANES, d), axis=0)


def _chip(x_ref, y_ref):
    return 2 * x_ref[0] + y_ref[0]


def _cast_into_place(shards, l, pos, col_sharded):
    _, r, c = shards.shape
    tr = _row_tile(r, c, 2 << 20)
    nb = r // tr

    def body(c_ref, x_ref, y_ref, a_ref, o_ref):
        o_ref[...] = a_ref[...].astype(BF16)

    if col_sharded:
        out_spec = pl.BlockSpec((tr, c), lambda i, c_ref, x_ref, y_ref: (i, _chip(x_ref, y_ref)))
        full = (r, c * N_CHIPS)
    else:
        out_spec = pl.BlockSpec((tr, c), lambda i, c_ref, x_ref, y_ref: (_chip(x_ref, y_ref) * nb + i, 0))
        full = (r * N_CHIPS, c)
    grid_spec = pltpu.PrefetchScalarGridSpec(
        num_scalar_prefetch=3, grid=(nb,),
        in_specs=[pl.BlockSpec((None, tr, c), lambda i, c_ref, x_ref, y_ref: (l, i, 0))], out_specs=out_spec)
    return pl.pallas_call(body, name="cast_into_place", grid_spec=grid_spec,
                          out_shape=jax.ShapeDtypeStruct(full, BF16),
                          compiler_params=_params("parallel"))(*pos, shards)


def _add_own_half(own, recv, pos, col_sharded):
    r, c = recv.shape
    tr = _row_tile(r, c, 2 << 20)
    nb = r // tr

    def body(c_ref, x_ref, y_ref, own_ref, recv_ref, o_ref):
        o_ref[...] = (own_ref[...] + recv_ref[...]).astype(BF16)

    if col_sharded:
        own_spec = pl.BlockSpec((tr, c), lambda i, c_ref, x_ref, y_ref: (c_ref[0] * nb + i, 0))
    else:
        own_spec = pl.BlockSpec((tr, c), lambda i, c_ref, x_ref, y_ref: (i, c_ref[0]))
    spec = pl.BlockSpec((tr, c), lambda i, c_ref, x_ref, y_ref: (i, 0))
    grid_spec = pltpu.PrefetchScalarGridSpec(num_scalar_prefetch=3, grid=(nb,), in_specs=[own_spec, spec],
                                             out_specs=spec)
    return pl.pallas_call(body, name="add_own_half", grid_spec=grid_spec,
                          out_shape=jax.ShapeDtypeStruct(recv.shape, BF16),
                          compiler_params=_params("parallel"))(*pos, own, recv)


def _reduce_block(buf, own, recv, slots, l, nl, pos, col_sharded):
    _, r, c = slots.shape
    tr = _row_tile(r, c, 1 << 20)
    nb = r // tr

    def body(c_ref, x_ref, y_ref, own_ref, recv_ref, slots_ref, *rest):
        acc = own_ref[...] + recv_ref[...]
        for j in range(N_CHIPS - 1):
            acc = acc + slots_ref[j].astype(F32)
        rest[-1][...] = acc

    if col_sharded:
        own_spec = pl.BlockSpec((tr, c), lambda i, c_ref, x_ref, y_ref: (c_ref[0] * nb + i, _chip(x_ref, y_ref)))
        recv_spec = pl.BlockSpec((tr, c), lambda i, c_ref, x_ref, y_ref: (i, _chip(x_ref, y_ref)))
        out_spec = pl.BlockSpec((None, tr, c), lambda i, c_ref, x_ref, y_ref: (l, c_ref[0] * nb + i, 0))
        out_shape = (nl, 2 * r, c)
    else:
        own_spec = pl.BlockSpec((tr, c), lambda i, c_ref, x_ref, y_ref: (_chip(x_ref, y_ref) * nb + i, c_ref[0]))
        recv_spec = pl.BlockSpec((tr, c), lambda i, c_ref, x_ref, y_ref: (_chip(x_ref, y_ref) * nb + i, 0))
        out_spec = pl.BlockSpec((None, tr, c), lambda i, c_ref, x_ref, y_ref: (l, i, c_ref[0]))
        out_shape = (nl, r, 2 * c)
    in_specs = [own_spec, recv_spec,
                pl.BlockSpec((N_CHIPS - 1, tr, c), lambda i, c_ref, x_ref, y_ref: (0, i, 0))]
    args = [*pos, own, recv, slots]
    aliases = {}
    if buf is not None:
        in_specs.append(ANY)
        args.append(buf)
        aliases = {len(args) - 1: 0}
    grid_spec = pltpu.PrefetchScalarGridSpec(num_scalar_prefetch=3, grid=(nb,), in_specs=in_specs,
                                             out_specs=out_spec)
    return pl.pallas_call(body, name="reduce_block", grid_spec=grid_spec,
                          out_shape=jax.ShapeDtypeStruct(out_shape, F32), input_output_aliases=aliases,
                          compiler_params=_params("parallel"))(*args)


def _sum_slots(a):
    n, r, c = a.shape
    tr = _row_tile(r, c * n, 4 << 20)

    def body(a_ref, o_ref):
        acc = a_ref[0]
        for s in range(1, n):
            acc = acc + a_ref[s]
        o_ref[...] = acc

    return pl.pallas_call(body, name="sum_slots", grid=(r // tr,),
                          in_specs=[pl.BlockSpec((n, tr, c), lambda i: (0, i, 0))],
                          out_specs=pl.BlockSpec((tr, c), lambda i: (i, 0)),
                          out_shape=jax.ShapeDtypeStruct((r, c), F32),
                          compiler_params=_params("parallel"))(a)


def _adamw(w, g, m, v):
    r, c = w.shape
    tr = _row_tile(r, c, 1 << 20)

    def body(w_ref, g_ref, m_ref, v_ref, d_ref, nm_ref, nv_ref):
        gg = g_ref[...]
        nm = ADAM_B1 * m_ref[...] + (1.0 - ADAM_B1) * gg
        nv = ADAM_B2 * v_ref[...] + (1.0 - ADAM_B2) * (gg * gg)
        m_hat = nm / (1.0 - ADAM_B1 ** ADAM_STEP)
        v_hat = nv / (1.0 - ADAM_B2 ** ADAM_STEP)
        d_ref[...] = -ADAM_LR * (m_hat / (jnp.sqrt(v_hat) + ADAM_EPS) + ADAM_WD * w_ref[...])
        nm_ref[...] = nm
        nv_ref[...] = nv

    spec = pl.BlockSpec((tr, c), lambda i: (i, 0))
    sh = jax.ShapeDtypeStruct((r, c), F32)
    return pl.pallas_call(body, name="adamw", grid=(r // tr,), in_specs=[spec] * 4, out_specs=[spec] * 3,
                          out_shape=[sh, sh, sh], compiler_params=_params("parallel"))(w, g, m, v)


def _proj_fwd(x, w):
    s, d = x.shape
    n = w.shape[1]
    tm, tn = min(s, 512), _col_tile(n, 1024)

    def body(x_ref, w_ref, o_ref):
        o_ref[...] = _dot(x_ref[...].astype(BF16), w_ref[...]).astype(BF16)

    return pl.pallas_call(body, name="proj_fwd", grid=(s // tm, n // tn),
                          in_specs=[pl.BlockSpec((tm, d), lambda i, j: (i, 0)),
                                    pl.BlockSpec((d, tn), lambda i, j: (0, j))],
                          out_specs=pl.BlockSpec((tm, tn), lambda i, j: (i, j)),
                          out_shape=jax.ShapeDtypeStruct((s, n), BF16),
                          compiler_params=_params("parallel", "parallel"))(x, w)


def _matmul_tn(a, b):
    s, m = a.shape
    n = b.shape[1]
    tk, tn = min(s, 512), _col_tile(n, 1024)
    nk = s // tk

    def body(a_ref, b_ref, o_ref):
        @pl.when(pl.program_id(1) == 0)
        def _():
            o_ref[...] = jnp.zeros_like(o_ref)

        o_ref[...] += _dot_tn(a_ref[...].astype(BF16), b_ref[...].astype(BF16))

    return pl.pallas_call(body, name="matmul_tn", grid=(n // tn, nk),
                          in_specs=[pl.BlockSpec((tk, m), lambda j, k: (k, 0)),
                                    pl.BlockSpec((tk, tn), lambda j, k: (k, j))],
                          out_specs=pl.BlockSpec((m, tn), lambda j, k: (0, j)),
                          out_shape=jax.ShapeDtypeStruct((m, n), F32),
                          compiler_params=_params("parallel", "arbitrary"))(a, b)


def _dx_matmul(dxr, dproj, w):
    s, d = dxr.shape
    n = w.shape[1]
    tm, tk = min(s, 512), _col_tile(n, 1024)

    def body(r_ref, g_ref, w_ref, o_ref):
        @pl.when(pl.program_id(1) == 0)
        def _():
            o_ref[...] = r_ref[...]

        o_ref[...] += _dot_nt(g_ref[...], w_ref[...])

    return pl.pallas_call(body, name="dx_matmul", grid=(s // tm, n // tk),
                          in_specs=[pl.BlockSpec((tm, d), lambda i, k: (i, 0)),
                                    pl.BlockSpec((tm, tk), lambda i, k: (i, k)),
                                    pl.BlockSpec((d, tk), lambda i, k: (0, k))],
                          out_specs=pl.BlockSpec((tm, d), lambda i, k: (i, 0)),
                          out_shape=jax.ShapeDtypeStruct((s, d), F32),
                          compiler_params=_params("parallel", "arbitrary"))(dxr, dproj, w)


def _mix_chunks(ws_ref, src_ref, dst_ref, bias_ref, t, groups, chunk):
    for c in range(t // chunk):
        rows = slice(c * chunk, (c + 1) * chunk)
        for g in range(groups):
            cols = slice(g * LANES, (g + 1) * LANES)
            val = _dot(ws_ref[g], src_ref[rows, cols])
            if bias_ref is not None:
                val = val + bias_ref[:, g:g + 1]
            dst_ref[rows, cols] = val


def _gmlp_fwd(proj, vn_g, vn_b, ws_m, bs_t):
    s = proj.shape[0]
    d = proj.shape[1] // N_IN
    groups, chunk = ws_m.shape[0], ws_m.shape[1]
    t = min(s, 512)

    def body(u_ref, v_ref, ga_ref, g_ref, b_ref, ws_ref, bs_ref, o_ref, vn_ref, mix_ref):
        xhat, _ = _ln_stats(v_ref[...].astype(F32))
        vn_ref[...] = (xhat * g_ref[...] + b_ref[...]).astype(BF16)
        _mix_chunks(ws_ref, vn_ref, mix_ref, bs_ref, t, groups, chunk)
        ga = ga_ref[...].astype(F32)
        o_ref[...] = (u_ref[...].astype(F32) * mix_ref[...] * (ga * _sigmoid(ga))).astype(BF16)

    col = lambda j: pl.BlockSpec((t, d), lambda i: (i, j))
    full = lambda a: pl.BlockSpec(a.shape, lambda i: (0,) * a.ndim)
    return pl.pallas_call(body, name="gmlp_fwd", grid=(s // t,),
                          in_specs=[col(0), col(1), col(2), full(vn_g), full(vn_b), full(ws_m), full(bs_t)],
                          out_specs=pl.BlockSpec((t, d), lambda i: (i, 0)),
                          out_shape=jax.ShapeDtypeStruct((s, d), BF16),
                          scratch_shapes=[pltpu.VMEM((t, d), BF16), pltpu.VMEM((t, d), F32)],
                          compiler_params=_params("parallel"))(proj, proj, proj, vn_g, vn_b, ws_m, bs_t)


def _gmlp_bwd(dproj, proj, dya, dq, dk, dv, vn_g, vn_b, ws_m, ws_mt, bs_t):
    s = proj.shape[0]
    d = proj.shape[1] // N_IN
    groups, chunk = ws_m.shape[0], ws_m.shape[1]
    t = min(s, 256)
    nsteps = s // t

    def body(dproj_hbm, u_ref, v_ref, ga_ref, dya_ref, dq_ref, dk_ref, dv_ref, g_ref, b_ref, ws_ref, wst_ref, bs_ref,
             o_ref, dg_ref, db_ref, dws_ref, dbs_ref,
             vn_ref, mix_ref, dm_ref, dvn_ref, dbs_acc, dg_acc, db_acc):
        del dproj_hbm
        i = pl.program_id(0)

        @pl.when(i == 0)
        def _():
            dws_ref[...] = jnp.zeros_like(dws_ref)
            dbs_acc[...] = jnp.zeros_like(dbs_acc)
            dg_acc[...] = jnp.zeros_like(dg_acc)
            db_acc[...] = jnp.zeros_like(db_acc)

        xhat, rstd = _ln_stats(v_ref[...].astype(F32))
        vn_ref[...] = (xhat * g_ref[...] + b_ref[...]).astype(BF16)
        _mix_chunks(ws_ref, vn_ref, mix_ref, bs_ref, t, groups, chunk)
        ga = ga_ref[...].astype(F32)
        sg = _sigmoid(ga)
        silu = ga * sg
        dsilu = sg * (1.0 + ga * (1.0 - sg))
        u = u_ref[...].astype(F32)
        dya_f = dya_ref[...].astype(F32)
        mix = mix_ref[...]
        o_ref[:, 0:d] = (dya_f * mix * silu).astype(BF16)
        o_ref[:, 2 * d:3 * d] = (dya_f * u * mix * dsilu).astype(BF16)
        dmix = dya_f * u * silu
        dm_ref[...] = dmix.astype(BF16)
        for c in range(t // chunk):
            dbs_acc[...] += dmix[c * chunk:(c + 1) * chunk, :]
        _mix_chunks(wst_ref, dm_ref, dvn_ref, None, t, groups, chunk)
        for c in range(t // chunk):
            rows = slice(c * chunk, (c + 1) * chunk)
            for g in range(groups):
                cols = slice(g * LANES, (g + 1) * LANES)
                dws_ref[g] += _dot_nt(dm_ref[rows, cols], vn_ref[rows, cols])
        dvn = dvn_ref[...]
        dg_acc[...] += _sum_rows8(dvn * xhat)
        db_acc[...] += _sum_rows8(dvn)
        o_ref[:, d:2 * d] = _ln_bwd(dvn * g_ref[...], xhat, rstd).astype(BF16)
        o_ref[:, 3 * d:4 * d] = dq_ref[...]
        o_ref[:, 4 * d:5 * d] = dk_ref[...]
        o_ref[:, 5 * d:6 * d] = dv_ref[...]

        @pl.when(i == nsteps - 1)
        def _():
            row = lax.broadcasted_iota(jnp.int32, (chunk, chunk), 0)
            col = lax.broadcasted_iota(jnp.int32, (chunk, chunk), 1)
            for g in range(groups):
                dws_ref[g] = jnp.where(col <= row, dws_ref[g], 0.0)
            lane = lax.broadcasted_iota(jnp.int32, (chunk, LANES), 1)
            res = jnp.zeros((chunk, LANES), F32)
            for g in range(groups):
                tot = jnp.sum(dbs_acc[:, g * LANES:(g + 1) * LANES], axis=1, keepdims=True)
                res = jnp.where(lane == g, tot, res)
            dbs_ref[...] = res
            dg_ref[...] = jnp.sum(dg_acc[...], axis=0, keepdims=True)
            db_ref[...] = jnp.sum(db_acc[...], axis=0, keepdims=True)

    col = lambda j: pl.BlockSpec((t, d), lambda i: (i, j))
    tok = pl.BlockSpec((t, d), lambda i: (i, 0))
    full = lambda a: pl.BlockSpec(a.shape, lambda i: (0,) * a.ndim)
    vec = jax.ShapeDtypeStruct((1, d), F32)
    outs = pl.pallas_call(
        body, name="gmlp_bwd", grid=(nsteps,),
        in_specs=[ANY, col(0), col(1), col(2), tok, tok, tok, tok,
                  full(vn_g), full(vn_b), full(ws_m), full(ws_mt), full(bs_t)],
        out_specs=[pl.BlockSpec((t, 6 * d), lambda i: (i, 0)),
                   pl.BlockSpec((1, d), lambda i: (0, 0)), pl.BlockSpec((1, d), lambda i: (0, 0)),
                   pl.BlockSpec((groups, chunk, chunk), lambda i: (0, 0, 0)),
                   pl.BlockSpec((chunk, LANES), lambda i: (0, 0))],
        out_shape=[jax.ShapeDtypeStruct(dproj.shape, BF16), vec, vec,
                   jax.ShapeDtypeStruct((groups, chunk, chunk), F32),
                   jax.ShapeDtypeStruct((chunk, LANES), F32)],
        scratch_shapes=[pltpu.VMEM((t, d), BF16), pltpu.VMEM((t, d), F32), pltpu.VMEM((t, d), BF16),
                        pltpu.VMEM((t, d), F32), pltpu.VMEM((chunk, d), F32),
                        pltpu.VMEM((SUBLANES, d), F32), pltpu.VMEM((SUBLANES, d), F32)],
        input_output_aliases={0: 0},
        compiler_params=_params("arbitrary"))(dproj, proj, proj, proj, dya, dq, dk, dv,
                                              vn_g, vn_b, ws_m, ws_mt, bs_t)
    return outs


ATTN_TILE = 512


def _log_sigmoid(z):
    return jnp.minimum(z, 0.0) - jnp.log(1.0 + jnp.exp(-jnp.abs(z)))


def _suffix_rhs():
    row = lax.broadcasted_iota(jnp.int32, (LANES, LANES), 0)
    col = lax.broadcasted_iota(jnp.int32, (LANES, LANES), 1)
    rhs = jnp.concatenate([(row > col).astype(BF16), jnp.ones((LANES, LANES), BF16)], axis=1)
    return jnp.concatenate([rhs, rhs], axis=0)


def _suffix_sums(a, rhs_ref, t):
    hi = a.astype(BF16)
    lo = (a - hi.astype(F32)).astype(BF16)
    n = t // LANES
    inside, totals = [], []
    for c in range(n):
        cols = slice(c * LANES, (c + 1) * LANES)
        res = _dot(jnp.concatenate([hi[:, cols], lo[:, cols]], axis=1), rhs_ref[...])
        inside.append(res[:, :LANES])
        totals.append(res[:, LANES:])
    later = totals[n - 1]
    for c in reversed(range(n - 1)):
        inside[c] = inside[c] + later
        later = later + totals[c]
    return jnp.concatenate(inside, axis=1), later


def _lanes_to_tile(a, t):
    return jnp.concatenate([a] * (t // LANES), axis=1)


def _attn_masks(t):
    lane = lax.broadcasted_iota(jnp.int32, (t, LANES), 1)
    row = lax.broadcasted_iota(jnp.int32, (t, t), 0)
    col = lax.broadcasted_iota(jnp.int32, (t, t), 1)
    return lane < HEAD_DIM, col < row


def _split_heads(a, head0):
    zero = jnp.zeros_like(a)
    return [jnp.where(head0, a, zero), jnp.where(head0, zero, a)]


def _attn_fwd(proj):
    s = proj.shape[0]
    d = proj.shape[1] // N_IN
    hp = d // LANES
    tq = min(s, ATTN_TILE)
    nq = s // tq
    scale = HEAD_DIM ** -0.5
    assert math.log2(scale).is_integer()

    def body(q_ref, k_ref, v_ref, rhs_ref, o_ref, acc_ref, r_ref):
        i = pl.program_id(1)
        head0, strict = _attn_masks(tq)
        qm = _split_heads((q_ref[...].astype(F32) * scale).astype(BF16), head0)
        acc_ref[...] = jnp.zeros_like(acc_ref)
        r_ref[...] = jnp.zeros_like(r_ref)

        def block(j, diag):
            off = pl.multiple_of(j * tq, tq)
            k2 = k_ref[pl.ds(off, tq), :]
            v2 = v_ref[pl.ds(off, tq), :]
            heads = range(HEADS_PER_BLOCK)
            z = [_dot_nt(qm[h], k2) for h in heads]
            lsz = [_log_sigmoid(z[h]) for h in heads]
            keep = [lsz[h] - z[h] for h in heads]
            if diag:
                keep = [jnp.where(strict, keep[h], 0.0) for h in heads]
            sums = [_suffix_sums(keep[h], rhs_ref, tq) for h in heads]
            w = [jnp.exp(lsz[h] + sums[h][0] + _lanes_to_tile(r_ref[h], tq)) for h in heads]
            if diag:
                w = [jnp.where(strict, w[h], 0.0) for h in heads]
            for h in heads:
                acc_ref[h] += _dot(w[h].astype(BF16), v2)
                r_ref[h] += sums[h][1]

        block(i, True)

        def step(n, carry):
            block(i - 1 - n, False)
            return carry

        lax.fori_loop(0, i, step, 0)
        o_ref[...] = jnp.where(head0, acc_ref[0], acc_ref[1])

    rhs = _suffix_rhs()
    return pl.pallas_call(
        body, name="attn_fwd", grid=(hp, nq),
        in_specs=[pl.BlockSpec((tq, LANES), lambda h, i: (i, 3 * hp + h)),
                  pl.BlockSpec((s, LANES), lambda h, i: (0, 4 * hp + h)),
                  pl.BlockSpec((s, LANES), lambda h, i: (0, 5 * hp + h)),
                  pl.BlockSpec(rhs.shape, lambda h, i: (0, 0))],
        out_specs=pl.BlockSpec((tq, LANES), lambda h, i: (i, h)),
        out_shape=jax.ShapeDtypeStruct((s, d), F32),
        scratch_shapes=[pltpu.VMEM((HEADS_PER_BLOCK, tq, LANES), F32),
                        pltpu.VMEM((HEADS_PER_BLOCK, tq, LANES), F32)],
        compiler_params=_params("parallel", "parallel"))(proj, proj, proj, rhs)


def _attn_bwd(proj, o, do):
    s = proj.shape[0]
    d = proj.shape[1] // N_IN
    hp = d // LANES
    tq = min(s, ATTN_TILE)
    nq = s // tq
    scale = HEAD_DIM ** -0.5

    def body(q_ref, k_ref, v_ref, o_ref, do_ref, rhs_ref, dq_ref, dk_ref, dv_ref,
             dq_acc, dk_acc, dv_acc, rk_ref, rg_ref):
        i = pl.program_id(1)
        head0, strict = _attn_masks(tq)
        qm = _split_heads((q_ref[...].astype(F32) * scale).astype(BF16), head0)
        dom = _split_heads(do_ref[...], head0)
        prod = do_ref[...].astype(F32) * o_ref[...]
        delta = [jnp.sum(jnp.where(head0, prod, 0.0), axis=1, keepdims=True),
                 jnp.sum(jnp.where(head0, 0.0, prod), axis=1, keepdims=True)]

        @pl.when(i == 0)
        def _():
            dk_acc[...] = jnp.zeros_like(dk_acc)
            dv_acc[...] = jnp.zeros_like(dv_acc)

        dq_acc[...] = jnp.zeros_like(dq_acc)
        rk_ref[...] = jnp.zeros_like(rk_ref)
        for h in range(HEADS_PER_BLOCK):
            rg_ref[h] = jnp.broadcast_to(delta[h], (tq, LANES))

        def block(j, diag):
            off = pl.multiple_of(j * tq, tq)
            k2 = k_ref[pl.ds(off, tq), :]
            v2 = v_ref[pl.ds(off, tq), :]
            heads = range(HEADS_PER_BLOCK)
            z = [_dot_nt(qm[h], k2) for h in heads]
            dw = [_dot_nt(dom[h], v2) for h in heads]
            lsz = [_log_sigmoid(z[h]) for h in heads]
            keep = [lsz[h] - z[h] for h in heads]
            if diag:
                keep = [jnp.where(strict, keep[h], 0.0) for h in heads]
            ksum = [_suffix_sums(keep[h], rhs_ref, tq) for h in heads]
            w = [jnp.exp(lsz[h] + ksum[h][0] + _lanes_to_tile(rk_ref[h], tq)) for h in heads]
            if diag:
                w = [jnp.where(strict, w[h], 0.0) for h in heads]
            wb = [w[h].astype(BF16) for h in heads]
            g = [dw[h] * wb[h].astype(F32) for h in heads]
            gsum = [_suffix_sums(g[h], rhs_ref, tq) for h in heads]
            dz = [g[h] - jnp.exp(lsz[h]) * (_lanes_to_tile(rg_ref[h], tq) - gsum[h][0]) for h in heads]
            if diag:
                dz = [jnp.where(strict, dz[h], 0.0) for h in heads]
            dzb = [dz[h].astype(BF16) for h in heads]
            for h in heads:
                dq_acc[h] += _dot(dzb[h], k2)
                rk_ref[h] += ksum[h][1]
                rg_ref[h] -= gsum[h][1]
            dk_acc[pl.ds(off, tq), :] += _dot_tn(dzb[0], qm[0]) + _dot_tn(dzb[1], qm[1])
            dv_acc[pl.ds(off, tq), :] += _dot_tn(wb[0], dom[0]) + _dot_tn(wb[1], dom[1])

        block(i, True)

        def step(n, carry):
            block(i - 1 - n, False)
            return carry

        lax.fori_loop(0, i, step, 0)
        dq_ref[...] = (jnp.where(head0, dq_acc[0], dq_acc[1]) * scale).astype(BF16)

        @pl.when(i == nq - 1)
        def _():
            dk_ref[...] = dk_acc[...].astype(BF16)
            dv_ref[...] = dv_acc[...].astype(BF16)

    blk = pl.BlockSpec((tq, LANES), lambda h, i: (i, h))
    seq = pl.BlockSpec((s, LANES), lambda h, i: (0, h))
    sh = jax.ShapeDtypeStruct((s, d), BF16)
    rhs = _suffix_rhs()
    return pl.pallas_call(
        body, name="attn_bwd", grid=(hp, nq),
        in_specs=[pl.BlockSpec((tq, LANES), lambda h, i: (i, 3 * hp + h)),
                  pl.BlockSpec((s, LANES), lambda h, i: (0, 4 * hp + h)),
                  pl.BlockSpec((s, LANES), lambda h, i: (0, 5 * hp + h)),
                  blk, blk, pl.BlockSpec(rhs.shape, lambda h, i: (0, 0))],
        out_specs=[blk, seq, seq], out_shape=[sh, sh, sh],
        scratch_shapes=[pltpu.VMEM((HEADS_PER_BLOCK, tq, LANES), F32),
                        pltpu.VMEM((s, LANES), F32), pltpu.VMEM((s, LANES), F32),
                        pltpu.VMEM((HEADS_PER_BLOCK, tq, LANES), F32),
                        pltpu.VMEM((HEADS_PER_BLOCK, tq, LANES), F32)],
        compiler_params=_params("parallel", "arbitrary"))(proj, proj, proj, o, do, rhs)


def _post_math(ya_ref, o_ref, gb_ref, ma_ref, mb_ref, x_ref, p_ref, wpa_ref, wpb_ref, wout_ref, wpe_ref, wpg_ref,
               alpha):
    gb = gb_ref[...].astype(F32)
    sgb = _sigmoid(gb)
    o = o_ref[...]
    yb = (o * (gb * sgb)).astype(BF16)
    pa = _dot(ya_ref[...], wpa_ref[...])
    pb = _dot(yb, wpb_ref[...])
    sa = _sigmoid(ma_ref[...].astype(F32))
    sb = _sigmoid(mb_ref[...].astype(F32))
    merged = (sa * pa + sb * pb).astype(BF16)
    h1 = alpha * x_ref[...] + _dot(merged, wout_ref[...])
    h1b = h1.astype(BF16)
    e = _dot(p_ref[...].astype(BF16), wpe_ref[...])
    sg = _sigmoid(_dot(h1b, wpg_ref[...]))
    h2 = h1 + e * sg
    return dict(gb=gb, sgb=sgb, o=o, yb=yb, pa=pa, pb=pb, sa=sa, sb=sb, merged=merged, h1b=h1b, e=e, sg=sg, h2=h2)


def _post_specs(tm, d, ple, weights):
    tok = pl.BlockSpec((tm, d), lambda i: (i, 0))
    col = lambda j: pl.BlockSpec((tm, d), lambda i: (i, j))
    full = lambda a: pl.BlockSpec(a.shape, lambda i: (0,) * a.ndim, pipeline_mode=pl.Buffered(1))
    return tok, [tok, tok, col(6), col(7), col(8), tok, pl.BlockSpec((tm, ple), lambda i: (i, 0))] + [
        full(w) for w in weights]


def _post_fwd(ya, o, proj, x, p, w_pa, w_pb, w_out, w_pe, w_pg, ln_g, ln_b, alpha):
    s, d = x.shape
    ple = p.shape[1]
    tm = min(s, 256)
    weights = (w_pa, w_pb, w_out, w_pe, w_pg, ln_g, ln_b)

    def body(ya_ref, o_ref, gb_ref, ma_ref, mb_ref, x_ref, p_ref, wpa_ref, wpb_ref, wout_ref, wpe_ref, wpg_ref,
             g_ref, b_ref, out_ref):
        f = _post_math(ya_ref, o_ref, gb_ref, ma_ref, mb_ref, x_ref, p_ref, wpa_ref, wpb_ref, wout_ref, wpe_ref,
                       wpg_ref, alpha)
        xhat, _ = _ln_stats(f["h2"])
        out_ref[...] = xhat * g_ref[...] + b_ref[...]

    tok, in_specs = _post_specs(tm, d, ple, weights)
    return pl.pallas_call(body, name="post_fwd", grid=(s // tm,), in_specs=in_specs, out_specs=tok,
                          out_shape=jax.ShapeDtypeStruct((s, d), F32),
                          compiler_params=_params("parallel"))(ya, o, proj, proj, proj, x, p, *weights)


def _post_bwd(dxo, ya, o, proj, x, p, w_pa, w_pb, w_out, w_pe, w_pg, ln_g, ln_b, alpha):
    s, d = x.shape
    ple = p.shape[1]
    tm = min(s, 256)
    nsteps = s // tm
    weights = (w_pa, w_pb, w_out, w_pe, w_pg, ln_g, ln_b)

    def body(dxo_ref, ya_ref, o_ref, gb_ref, ma_ref, mb_ref, x_ref, p_ref, wpa_ref, wpb_ref, wout_ref, wpe_ref,
             wpg_ref, g_ref, b_ref,
             dproj_ref, dxr_ref, dya_ref, do_ref, de_ref, h1_ref, dzg_ref, mrg_ref, dh1_ref, dpa_ref, yb_ref, dpb_ref,
             dg_ref, db_ref, dg_acc, db_acc):
        i = pl.program_id(0)

        @pl.when(i == 0)
        def _():
            dg_acc[...] = jnp.zeros_like(dg_acc)
            db_acc[...] = jnp.zeros_like(db_acc)

        f = _post_math(ya_ref, o_ref, gb_ref, ma_ref, mb_ref, x_ref, p_ref, wpa_ref, wpb_ref, wout_ref, wpe_ref,
                       wpg_ref, alpha)
        xhat, rstd = _ln_stats(f["h2"])
        dxo = dxo_ref[...]
        dg_acc[...] += _sum_rows8(dxo * xhat)
        db_acc[...] += _sum_rows8(dxo)
        dh2 = _ln_bwd(dxo * g_ref[...], xhat, rstd)
        sg, e = f["sg"], f["e"]
        de_ref[...] = (dh2 * sg).astype(BF16)
        dzg = (dh2 * e * sg * (1.0 - sg)).astype(BF16)
        dzg_ref[...] = dzg
        dh1 = dh2 + _dot_nt(dzg, wpg_ref[...])
        dh1b = dh1.astype(BF16)
        dxr_ref[...] = alpha * dh1
        dh1_ref[...] = dh1b
        h1_ref[...] = f["h1b"]
        mrg_ref[...] = f["merged"]
        yb_ref[...] = f["yb"]
        dmerged = _dot_nt(dh1b, wout_ref[...])
        sa, sb = f["sa"], f["sb"]
        dpa = (dmerged * sa).astype(BF16)
        dpb = (dmerged * sb).astype(BF16)
        dpa_ref[...] = dpa
        dpb_ref[...] = dpb
        dproj_ref[:, d:2 * d] = (dmerged * f["pa"] * sa * (1.0 - sa)).astype(BF16)
        dproj_ref[:, 2 * d:3 * d] = (dmerged * f["pb"] * sb * (1.0 - sb)).astype(BF16)
        dya_ref[...] = _dot_nt(dpa, wpa_ref[...]).astype(BF16)
        dyb = _dot_nt(dpb, wpb_ref[...])
        gb, sgb = f["gb"], f["sgb"]
        do_ref[...] = (dyb * (gb * sgb)).astype(BF16)
        dproj_ref[:, 0:d] = (dyb * f["o"] * (sgb * (1.0 + gb * (1.0 - sgb)))).astype(BF16)

        @pl.when(i == nsteps - 1)
        def _():
            dg_ref[...] = jnp.sum(dg_acc[...], axis=0, keepdims=True)
            db_ref[...] = jnp.sum(db_acc[...], axis=0, keepdims=True)

    tok, in_specs = _post_specs(tm, d, ple, weights)
    vec_spec = pl.BlockSpec((1, d), lambda i: (0, 0))
    vec = jax.ShapeDtypeStruct((1, d), F32)
    act = jax.ShapeDtypeStruct((s, d), BF16)
    return pl.pallas_call(
        body, name="post_bwd", grid=(nsteps,), in_specs=[tok] + in_specs,
        out_specs=[pl.BlockSpec((tm, 3 * d), lambda i: (i, 2)), tok] + [tok] * 10 + [vec_spec, vec_spec],
        out_shape=[jax.ShapeDtypeStruct((s, N_IN * d), BF16), jax.ShapeDtypeStruct((s, d), F32)] + [act] * 10 + [vec, vec],
        scratch_shapes=[pltpu.VMEM((SUBLANES, d), F32), pltpu.VMEM((SUBLANES, d), F32)],
        compiler_params=_params("arbitrary"))(dxo, ya, o, proj, proj, proj, x, p, *weights)


def _loss_head(y, target):
    s, d = y.shape
    tm = min(s, 512)

    def body(y_ref, t_ref, dy_ref, l_ref):
        @pl.when(pl.program_id(0) == 0)
        def _():
            l_ref[...] = jnp.zeros_like(l_ref)

        err = y_ref[...] - t_ref[...]
        dy_ref[...] = err / d
        row = jnp.sum(err * err, axis=1, keepdims=True) / d
        l_ref[...] += 0.5 * jnp.sum(row, axis=0, keepdims=True)

    tok = pl.BlockSpec((tm, d), lambda i: (i, 0))
    return pl.pallas_call(body, name="loss_head", grid=(s // tm,), in_specs=[tok, tok],
                          out_specs=[tok, pl.BlockSpec((SUBLANES, LANES), lambda i: (0, 0))],
                          out_shape=[jax.ShapeDtypeStruct((s, d), F32),
                                     jax.ShapeDtypeStruct((SUBLANES, LANES), F32)],
                          compiler_params=_params("arbitrary"))(y, target)


def _position():
    x, y, c = lax.axis_index("x"), lax.axis_index("y"), lax.axis_index("c")
    chips = [(1 - x, y), (x, 1 - y), (1 - x, 1 - y)]
    return x, y, c, chips


def _shard_of(ref, col_sharded, j, n):
    off = pl.multiple_of(j * n, n)
    return ref.at[:, pl.ds(off, n)] if col_sharded else ref.at[pl.ds(off, n), :]


def _half_of(ref, col_sharded, h, n):
    off = pl.multiple_of(h * n, n)
    return ref.at[pl.ds(off, n), :] if col_sharded else ref.at[:, pl.ds(off, n)]


def _piece_of(ref, col_sharded, chip, n_block, half, n_half):
    block = pl.ds(pl.multiple_of(chip * n_block, n_block), n_block)
    part = pl.ds(pl.multiple_of(half * n_half, n_half), n_half)
    return ref.at[part, block] if col_sharded else ref.at[block, part]


def _gather_weights(bufs):
    nl, n_w = len(bufs), len(BIG)
    flat = [bufs[l][name] for l in range(nl) for name in BIG]
    n_arr = len(flat)

    def body(*refs):
        outs = refs[n_arr:2 * n_arr]
        send_sems, recv_sems, pass_send_sems, pass_recv_sems = refs[2 * n_arr:]
        x, y, c, chips = _position()
        my_chip = 2 * x + y
        sends, arrivals, passes, passed = [], [], [], []
        for a in range(n_arr):
            cs = COL_SHARDED[BIG[a % n_w]]
            rows, cols = outs[a].shape
            n_block = (cols if cs else rows) // N_CHIPS
            n_half = (rows if cs else cols) // 2
            piece = lambda chip, half: _piece_of(outs[a], cs, chip, n_block, half, n_half)
            for j, chip in enumerate(chips):
                k = a * 3 + j
                their = 2 * chip[0] + chip[1]
                sends.append(pltpu.make_async_remote_copy(
                    src_ref=piece(my_chip, c), dst_ref=piece(my_chip, c), send_sem=send_sems.at[k],
                    recv_sem=recv_sems.at[k], device_id=(chip[0], chip[1], c), device_id_type=MESH))
                arrivals.append(pltpu.make_async_remote_copy(
                    src_ref=piece(their, c), dst_ref=piece(their, c), send_sem=send_sems.at[k],
                    recv_sem=recv_sems.at[k], device_id=(chip[0], chip[1], c), device_id_type=MESH))
                passes.append(pltpu.make_async_remote_copy(
                    src_ref=piece(their, c), dst_ref=piece(their, c), send_sem=pass_send_sems.at[k],
                    recv_sem=pass_recv_sems.at[k], device_id=(x, y, 1 - c), device_id_type=MESH))
                passed.append(pltpu.make_async_remote_copy(
                    src_ref=piece(their, 1 - c), dst_ref=piece(their, 1 - c), send_sem=pass_send_sems.at[k],
                    recv_sem=pass_recv_sems.at[k], device_id=(x, y, 1 - c), device_id_type=MESH))
        for cp in sends:
            cp.start()
        for arrival, onward in zip(arrivals, passes):
            arrival.wait_recv()
            onward.start()
        for cp in passed:
            cp.wait_recv()
        for cp in sends + passes:
            cp.wait_send()

    outs = pl.pallas_call(
        body, name="gather_weights", in_specs=[ANY] * n_arr, out_specs=[ANY] * n_arr,
        out_shape=[jax.ShapeDtypeStruct(a.shape, BF16) for a in flat],
        input_output_aliases={a: a for a in range(n_arr)},
        scratch_shapes=[pltpu.SemaphoreType.DMA((n_arr * 3,))] * 4,
    )(*flat)
    return [{name: outs[l * n_w + w] for w, name in enumerate(BIG)} for l in range(nl)]


def _half_shape(shape, col_sharded):
    r, c = shape
    return (r // 2, c) if col_sharded else (r, c // 2)


def _exchange_halves(grads):
    nl, n_w = len(grads), len(BIG)
    flat = [grads[l][name] for l in range(nl) for name in BIG]
    n_arr = len(flat)

    def body(*refs):
        ins, outs = refs[:n_arr], refs[n_arr:2 * n_arr]
        send_sems, recv_sems = refs[2 * n_arr:]
        x, y, c, _ = _position()
        copies = []
        for a in range(n_arr):
            cs = COL_SHARDED[BIG[a % n_w]]
            n = outs[a].shape[0] if cs else outs[a].shape[1]
            copies.append(pltpu.make_async_remote_copy(
                src_ref=_half_of(ins[a], cs, 1 - c, n), dst_ref=outs[a], send_sem=send_sems.at[a],
                recv_sem=recv_sems.at[a], device_id=(x, y, 1 - c), device_id_type=MESH))
        for cp in copies:
            cp.start()
        for cp in copies:
            cp.wait_recv()
        for cp in copies:
            cp.wait_send()

    outs = pl.pallas_call(
        body, name="exchange_halves", in_specs=[ANY] * n_arr, out_specs=[ANY] * n_arr,
        out_shape=[jax.ShapeDtypeStruct(_half_shape(a.shape, COL_SHARDED[BIG[i % n_w]]), F32)
                   for i, a in enumerate(flat)],
        scratch_shapes=[pltpu.SemaphoreType.DMA((n_arr,)), pltpu.SemaphoreType.DMA((n_arr,))],
    )(*flat)
    return [{name: outs[l * n_w + w] for w, name in enumerate(BIG)} for l in range(nl)]


def _scatter_to_owners(halves):
    nl, n_w = len(halves), len(BIG)
    flat = [halves[l][name] for l in range(nl) for name in BIG]
    n_arr = len(flat)

    def piece_shape(a, cs):
        r, c = a.shape
        return (r, c // N_CHIPS) if cs else (r // N_CHIPS, c)

    def body(*refs):
        ins, outs = refs[:n_arr], refs[n_arr:2 * n_arr]
        send_sems, recv_sems = refs[2 * n_arr:]
        x, y, c, chips = _position()
        copies = []
        for a in range(n_arr):
            cs = COL_SHARDED[BIG[a % n_w]]
            n = outs[a].shape[2] if cs else outs[a].shape[1]
            for j, chip in enumerate(chips):
                k = a * 3 + j
                copies.append(pltpu.make_async_remote_copy(
                    src_ref=_shard_of(ins[a], cs, 2 * chip[0] + chip[1], n), dst_ref=outs[a].at[j],
                    send_sem=send_sems.at[k], recv_sem=recv_sems.at[k], device_id=(chip[0], chip[1], c),
                    device_id_type=MESH))
        for cp in copies:
            cp.start()
        for cp in copies:
            cp.wait_recv()
        for cp in copies:
            cp.wait_send()

    outs = pl.pallas_call(
        body, name="scatter_to_owners", in_specs=[ANY] * n_arr, out_specs=[ANY] * n_arr,
        out_shape=[jax.ShapeDtypeStruct((N_CHIPS - 1,) + piece_shape(a, COL_SHARDED[BIG[i % n_w]]), a.dtype)
                   for i, a in enumerate(flat)],
        scratch_shapes=[pltpu.SemaphoreType.DMA((n_arr * 3,)), pltpu.SemaphoreType.DMA((n_arr * 3,))],
    )(*flat)
    return [{name: outs[l * n_w + w] for w, name in enumerate(BIG)} for l in range(nl)]


def _join_halves(halves):
    flat = [halves[name] for name in BIG]
    n_w = len(flat)

    def body(*refs):
        outs = refs[n_w:2 * n_w]
        send_sems, recv_sems = refs[2 * n_w:]
        x, y, c, _ = _position()
        sends, recvs = [], []
        for w, name in enumerate(BIG):
            cs = COL_SHARDED[name]
            n = (outs[w].shape[1] if cs else outs[w].shape[2]) // 2

            def half(h):
                part = pl.ds(pl.multiple_of(h * n, n), n)
                return outs[w].at[:, part, :] if cs else outs[w].at[:, :, part]

            sends.append(pltpu.make_async_remote_copy(
                src_ref=half(c), dst_ref=half(c), send_sem=send_sems.at[w], recv_sem=recv_sems.at[w],
                device_id=(x, y, 1 - c), device_id_type=MESH))
            recvs.append(pltpu.make_async_remote_copy(
                src_ref=half(1 - c), dst_ref=half(1 - c), send_sem=send_sems.at[w], recv_sem=recv_sems.at[w],
                device_id=(x, y, 1 - c), device_id_type=MESH))
        for cp in sends:
            cp.start()
        for cp in recvs:
            cp.wait_recv()
        for cp in sends:
            cp.wait_send()

    outs = pl.pallas_call(
        body, name="join_halves", in_specs=[ANY] * n_w, out_specs=[ANY] * n_w,
        out_shape=[jax.ShapeDtypeStruct(a.shape, F32) for a in flat],
        input_output_aliases={w: w for w in range(n_w)},
        scratch_shapes=[pltpu.SemaphoreType.DMA((n_w,)), pltpu.SemaphoreType.DMA((n_w,))],
    )(*flat)
    return dict(zip(BIG, outs))


def _gather_small(packed):
    r, lanes = packed.shape

    def body(in_ref, out_ref, send_sems, recv_sems, loc_sem):
        x, y, c, _ = _position()
        me = 4 * x + 2 * y + c
        local = pltpu.make_async_copy(in_ref, out_ref.at[me], loc_sem)
        local.start()
        sends, recvs = [], []
        for k in range(1, N_DEV):
            px, py, pc = x ^ (k >> 2), y ^ ((k >> 1) & 1), c ^ (k & 1)
            sends.append(pltpu.make_async_remote_copy(
                src_ref=in_ref, dst_ref=out_ref.at[me], send_sem=send_sems.at[k - 1], recv_sem=recv_sems.at[k - 1],
                device_id=(px, py, pc), device_id_type=MESH))
            recvs.append(pltpu.make_async_remote_copy(
                src_ref=in_ref, dst_ref=out_ref.at[4 * px + 2 * py + pc], send_sem=send_sems.at[k - 1],
                recv_sem=recv_sems.at[k - 1], device_id=(px, py, pc), device_id_type=MESH))
        for cp in sends:
            cp.start()
        for cp in recvs:
            cp.wait_recv()
        for cp in sends:
            cp.wait_send()
        local.wait()

    return pl.pallas_call(
        body, name="gather_small", in_specs=[ANY], out_specs=ANY,
        out_shape=jax.ShapeDtypeStruct((N_DEV, r, lanes), F32),
        scratch_shapes=[pltpu.SemaphoreType.DMA((N_DEV - 1,)), pltpu.SemaphoreType.DMA((N_DEV - 1,)),
                        pltpu.SemaphoreType.DMA],
    )(packed)


def _pack_small(t):
    return jnp.concatenate([t[name].reshape(-1, LANES) for name in SMALL], axis=0)


def _unpack_small(packed, like):
    out, row = {}, 0
    for name in SMALL:
        n = like[name].size // LANES
        out[name] = packed[row:row + n].reshape(like[name].shape)
        row += n
    return out


def kernel(x, p, w_in, vn_g, vn_b, w_s, b_s, w_pa, w_pb, w_out, w_pe, w_pg, ln_g, ln_b, loss_target, m_w_in, m_vn_g, m_vn_b, m_w_s, m_b_s, m_w_pa, m_w_pb, m_w_out, m_w_pe, m_w_pg, m_ln_g, m_ln_b, v_w_in, v_vn_g, v_vn_b, v_w_s, v_b_s, v_w_pa, v_w_pb, v_w_out, v_w_pe, v_w_pg, v_ln_g, v_ln_b):
    weights = dict(w_in=w_in, vn_g=vn_g, vn_b=vn_b, w_s=w_s, b_s=b_s, w_pa=w_pa, w_pb=w_pb, w_out=w_out, w_pe=w_pe,
                   w_pg=w_pg, ln_g=ln_g, ln_b=ln_b)
    mom1 = dict(w_in=m_w_in, vn_g=m_vn_g, vn_b=m_vn_b, w_s=m_w_s, b_s=m_b_s, w_pa=m_w_pa, w_pb=m_w_pb, w_out=m_w_out,
                w_pe=m_w_pe, w_pg=m_w_pg, ln_g=m_ln_g, ln_b=m_ln_b)
    mom2 = dict(w_in=v_w_in, vn_g=v_vn_g, vn_b=v_vn_b, w_s=v_w_s, b_s=v_b_s, w_pa=v_w_pa, w_pb=v_w_pb, w_out=v_w_out,
                w_pe=v_w_pe, w_pg=v_w_pg, ln_g=v_ln_g, ln_b=v_ln_b)
    nl, d = vn_g.shape
    chunk = w_s.shape[2]
    assert chunk == LANES and w_s.shape[3] == LANES and d % LANES == 0
    alpha = (2 * nl) ** 0.25
    pos = tuple(lax.axis_index(a).astype(jnp.int32).reshape(1) for a in ("c", "x", "y"))

    full = _gather_weights([{name: _cast_into_place(weights[name], l, pos, COL_SHARDED[name]) for name in BIG}
                            for l in range(nl)])
    causal = jnp.tril(jnp.ones((chunk, chunk), dtype=bool))
    ws_m = jnp.where(causal, w_s, 0.0).astype(BF16)
    ws_mt = jnp.swapaxes(ws_m, 2, 3)
    bs_t = jnp.swapaxes(b_s, 1, 2)

    xs, projs, yas, os_ = [x[0]], [], [], []
    for l in range(nl):
        proj = _proj_fwd(xs[l], full[l]["w_in"])
        ya = _gmlp_fwd(proj, vn_g[l:l + 1], vn_b[l:l + 1], ws_m[l], bs_t[l])
        o = _attn_fwd(proj)
        xs.append(_post_fwd(ya, o, proj, xs[l], p[l, 0], full[l]["w_pa"], full[l]["w_pb"], full[l]["w_out"],
                            full[l]["w_pe"], full[l]["w_pg"], ln_g[l:l + 1], ln_b[l:l + 1], alpha))
        projs.append(proj)
        yas.append(ya)
        os_.append(o)

    dx, loss_tile = _loss_head(xs[nl], loss_target[0])
    loss = lax.psum(loss_tile[0, 0], ("x", "y", "c"))

    big_grads = [None] * nl
    small_grads = {name: [None] * nl for name in SMALL}
    for l in reversed(range(nl)):
        w = full[l]
        (dproj, dxr, dya, do, de, h1b, dzg, merged, dh1, dpa, yb, dpb, dln_g, dln_b) = _post_bwd(
            dx, yas[l], os_[l], projs[l], xs[l], p[l, 0], w["w_pa"], w["w_pb"], w["w_out"], w["w_pe"], w["w_pg"],
            ln_g[l:l + 1], ln_b[l:l + 1], alpha)
        dq, dk, dv = _attn_bwd(projs[l], os_[l], do)
        dproj, dvn_g, dvn_b, dw_s, dbs_cols = _gmlp_bwd(dproj, projs[l], dya, dq, dk, dv, vn_g[l:l + 1],
                                                         vn_b[l:l + 1], ws_m[l], ws_mt[l], bs_t[l])
        big_grads[l] = dict(w_in=_matmul_tn(xs[l], dproj), w_pa=_matmul_tn(yas[l], dpa), w_pb=_matmul_tn(yb, dpb),
                            w_out=_matmul_tn(merged, dh1), w_pe=_matmul_tn(p[l, 0], de), w_pg=_matmul_tn(h1b, dzg))
        dx = _dx_matmul(dxr, dproj, w["w_in"])
        small_grads["vn_g"][l], small_grads["vn_b"][l] = dvn_g[0], dvn_b[0]
        small_grads["ln_g"][l], small_grads["ln_b"][l] = dln_g[0], dln_b[0]
        small_grads["w_s"][l] = dw_s
        small_grads["b_s"][l] = dbs_cols[:, :b_s.shape[1]].T

    received = _exchange_halves(big_grads)
    chip_sums = [{name: _add_own_half(big_grads[l][name], received[l][name], pos, COL_SHARDED[name])
                  for name in BIG} for l in range(nl)]
    slots = _scatter_to_owners(chip_sums)
    reduced = {}
    for name in BIG:
        buf = None
        for l in range(nl):
            buf = _reduce_block(buf, big_grads[l][name], received[l][name], slots[l][name], l, nl, pos,
                                COL_SHARDED[name])
        reduced[name] = buf
    grads = _join_halves(reduced)

    small_like = {name: weights[name] for name in SMALL}
    packed = _pack_small({name: jnp.stack(small_grads[name]) for name in SMALL})
    grads.update(_unpack_small(_sum_slots(_gather_small(packed)), small_like))

    delta, new_m, new_v = {}, {}, {}
    for name in BIG:
        sh = weights[name].shape
        flat = lambda a: a.reshape(sh[0] * sh[1], sh[2])
        dl, nm, nv = _adamw(flat(weights[name]), flat(grads[name]), flat(mom1[name]), flat(mom2[name]))
        delta[name], new_m[name], new_v[name] = dl.reshape(sh), nm.reshape(sh), nv.reshape(sh)
    dl, nm, nv = _adamw(_pack_small(small_like), _pack_small({n: grads[n] for n in SMALL}),
                        _pack_small({n: mom1[n] for n in SMALL}), _pack_small({n: mom2[n] for n in SMALL}))
    delta.update(_unpack_small(dl, small_like))
    new_m.update(_unpack_small(nm, small_like))
    new_v.update(_unpack_small(nv, small_like))

    return (loss, dx[None], *[grads[n] for n in WEIGHTS], *[delta[n] for n in WEIGHTS],
            *[new_m[n] for n in WEIGHTS], *[new_v[n] for n in WEIGHTS])
```

```python
import math

import jax
import jax.numpy as jnp
from jax import lax
from jax.experimental import pallas as pl
from jax.experimental.pallas import tpu as pltpu

F32 = jnp.float32
BF16 = jnp.bfloat16
LANES = 128
SUBLANES = 8
HEAD_DIM = 64
HEADS_PER_BLOCK = LANES // HEAD_DIM
LN_EPS = 1e-5
N_IN = 9
N_CHIPS = 4
N_DEV = 8
ADAM_LR = 0.001
ADAM_B1 = 0.9
ADAM_B2 = 0.999
ADAM_EPS = 1e-08
ADAM_WD = 0.01
ADAM_STEP = 10
MESH = pl.DeviceIdType.MESH
ANY = pl.BlockSpec(memory_space=pl.ANY)
BIG = ("w_in", "w_pa", "w_pb", "w_out", "w_pe", "w_pg")
COL_SHARDED = {"w_in": True, "w_pa": False, "w_pb": False, "w_out": False, "w_pe": True, "w_pg": False}
SMALL = ("vn_g", "vn_b", "w_s", "b_s", "ln_g", "ln_b")
WEIGHTS = ("w_in", "vn_g", "vn_b", "w_s", "b_s", "w_pa", "w_pb", "w_out", "w_pe", "w_pg", "ln_g", "ln_b")


def _params(*sem):
    return pltpu.CompilerParams(dimension_semantics=sem)


def _dot(a, b):
    return jnp.dot(a, b, preferred_element_type=F32)


def _dot_nt(a, b):
    return lax.dot_general(a, b, (((1,), (1,)), ((), ())), preferred_element_type=F32)


def _dot_tn(a, b):
    return lax.dot_general(a, b, (((0,), (0,)), ((), ())), preferred_element_type=F32)


def _sigmoid(a):
    return 1.0 / (1.0 + jnp.exp(-a))


def _row_tile(rows, cols, cap_bytes):
    best = None
    for t in range(16, rows + 1, 16):
        if rows % t == 0 and t * cols * 4 <= cap_bytes:
            best = t
    return best or rows


def _col_tile(cols, cap):
    best = LANES
    for t in range(LANES, min(cols, cap) + 1, LANES):
        if cols % t == 0:
            best = t
    return best


def _ln_stats(h):
    mu = jnp.mean(h, axis=-1, keepdims=True)
    hc = h - mu
    var = jnp.mean(hc * hc, axis=-1, keepdims=True)
    rstd = lax.rsqrt(var + LN_EPS)
    return hc * rstd, rstd


def _ln_bwd(dxhat, xhat, rstd):
    m1 = jnp.mean(dxhat, axis=-1, keepdims=True)
    m2 = jnp.mean(dxhat * xhat, axis=-1, keepdims=True)
    return rstd * (dxhat - m1 - xhat * m2)


def _sum_rows8(a):
    t, d = a.shape
    return jnp.sum(a.reshape(t // SUBLANES, SUBLANES, d), axis=0)


def _chip(x_ref, y_ref):
    return 2 * x_ref[0] + y_ref[0]


def _cast_into_place(shards, l, pos, col_sharded):
    _, r, c = shards.shape
    tr = _row_tile(r, c, 2 << 20)
    nb = r // tr

    def body(c_ref, x_ref, y_ref, a_ref, o_ref):
        o_ref[...] = a_ref[...].astype(BF16)

    if col_sharded:
        out_spec = pl.BlockSpec((tr, c), lambda i, c_ref, x_ref, y_ref: (i, _chip(x_ref, y_ref)))
        full = (r, c * N_CHIPS)
    else:
        out_spec = pl.BlockSpec((tr, c), lambda i, c_ref, x_ref, y_ref: (_chip(x_ref, y_ref) * nb + i, 0))
        full = (r * N_CHIPS, c)
    grid_spec = pltpu.PrefetchScalarGridSpec(
        num_scalar_prefetch=3, grid=(nb,),
        in_specs=[pl.BlockSpec((None, tr, c), lambda i, c_ref, x_ref, y_ref: (l, i, 0))], out_specs=out_spec)
    return pl.pallas_call(body, name="cast_into_place", grid_spec=grid_spec,
                          out_shape=jax.ShapeDtypeStruct(full, BF16),
                          compiler_params=_params("parallel"))(*pos, shards)


def _add_own_half(own, recv, pos, col_sharded):
    r, c = recv.shape
    tr = _row_tile(r, c, 2 << 20)
    nb = r // tr

    def body(c_ref, x_ref, y_ref, own_ref, recv_ref, o_ref):
        o_ref[...] = (own_ref[...] + recv_ref[...]).astype(BF16)

    if col_sharded:
        own_spec = pl.BlockSpec((tr, c), lambda i, c_ref, x_ref, y_ref: (c_ref[0] * nb + i, 0))
    else:
        own_spec = pl.BlockSpec((tr, c), lambda i, c_ref, x_ref, y_ref: (i, c_ref[0]))
    spec = pl.BlockSpec((tr, c), lambda i, c_ref, x_ref, y_ref: (i, 0))
    grid_spec = pltpu.PrefetchScalarGridSpec(num_scalar_prefetch=3, grid=(nb,), in_specs=[own_spec, spec],
                                             out_specs=spec)
    return pl.pallas_call(body, name="add_own_half", grid_spec=grid_spec,
                          out_shape=jax.ShapeDtypeStruct(recv.shape, BF16),
                          compiler_params=_params("parallel"))(*pos, own, recv)


def _reduce_block(buf, own, recv, slots, l, nl, pos, col_sharded):
    _, r, c = slots.shape
    tr = _row_tile(r, c, 1 << 20)
    nb = r // tr

    def body(c_ref, x_ref, y_ref, own_ref, recv_ref, slots_ref, *rest):
        acc = own_ref[...] + recv_ref[...]
        for j in range(N_CHIPS - 1):
            acc = acc + slots_ref[j].astype(F32)
        rest[-1][...] = acc

    if col_sharded:
        own_spec = pl.BlockSpec((tr, c), lambda i, c_ref, x_ref, y_ref: (c_ref[0] * nb + i, _chip(x_ref, y_ref)))
        recv_spec = pl.BlockSpec((tr, c), lambda i, c_ref, x_ref, y_ref: (i, _chip(x_ref, y_ref)))
        out_spec = pl.BlockSpec((None, tr, c), lambda i, c_ref, x_ref, y_ref: (l, c_ref[0] * nb + i, 0))
        out_shape = (nl, 2 * r, c)
    else:
        own_spec = pl.BlockSpec((tr, c), lambda i, c_ref, x_ref, y_ref: (_chip(x_ref, y_ref) * nb + i, c_ref[0]))
        recv_spec = pl.BlockSpec((tr, c), lambda i, c_ref, x_ref, y_ref: (_chip(x_ref, y_ref) * nb + i, 0))
        out_spec = pl.BlockSpec((None, tr, c), lambda i, c_ref, x_ref, y_ref: (l, i, c_ref[0]))
        out_shape = (nl, r, 2 * c)
    in_specs = [own_spec, recv_spec,
                pl.BlockSpec((N_CHIPS - 1, tr, c), lambda i, c_ref, x_ref, y_ref: (0, i, 0))]
    args = [*pos, own, recv, slots]
    aliases = {}
    if buf is not None:
        in_specs.append(ANY)
        args.append(buf)
        aliases = {len(args) - 1: 0}
    grid_spec = pltpu.PrefetchScalarGridSpec(num_scalar_prefetch=3, grid=(nb,), in_specs=in_specs,
                                             out_specs=out_spec)
    return pl.pallas_call(body, name="reduce_block", grid_spec=grid_spec,
                          out_shape=jax.ShapeDtypeStruct(out_shape, F32), input_output_aliases=aliases,
                          compiler_params=_params("parallel"))(*args)


def _sum_slots(a):
    n, r, c = a.shape
    tr = _row_tile(r, c * n, 4 << 20)

    def body(a_ref, o_ref):
        acc = a_ref[0]
        for s in range(1, n):
            acc = acc + a_ref[s]
        o_ref[...] = acc

    return pl.pallas_call(body, name="sum_slots", grid=(r // tr,),
                          in_specs=[pl.BlockSpec((n, tr, c), lambda i: (0, i, 0))],
                          out_specs=pl.BlockSpec((tr, c), lambda i: (i, 0)),
                          out_shape=jax.ShapeDtypeStruct((r, c), F32),
                          compiler_params=_params("parallel"))(a)


def _adamw(w, g, m, v):
    r, c = w.shape
    tr = _row_tile(r, c, 1 << 20)

    def body(w_ref, g_ref, m_ref, v_ref, d_ref, nm_ref, nv_ref):
        gg = g_ref[...]
        nm = ADAM_B1 * m_ref[...] + (1.0 - ADAM_B1) * gg
        nv = ADAM_B2 * v_ref[...] + (1.0 - ADAM_B2) * (gg * gg)
        m_hat = nm / (1.0 - ADAM_B1 ** ADAM_STEP)
        v_hat = nv / (1.0 - ADAM_B2 ** ADAM_STEP)
        d_ref[...] = -ADAM_LR * (m_hat / (jnp.sqrt(v_hat) + ADAM_EPS) + ADAM_WD * w_ref[...])
        nm_ref[...] = nm
        nv_ref[...] = nv

    spec = pl.BlockSpec((tr, c), lambda i: (i, 0))
    sh = jax.ShapeDtypeStruct((r, c), F32)
    return pl.pallas_call(body, name="adamw", grid=(r // tr,), in_specs=[spec] * 4, out_specs=[spec] * 3,
                          out_shape=[sh, sh, sh], compiler_params=_params("parallel"))(w, g, m, v)


def _proj_fwd(x, w):
    s, d = x.shape
    n = w.shape[1]
    tm, tn = min(s, 1024), _col_tile(n, 1024)

    def body(x_ref, w_ref, o_ref):
        o_ref[...] = _dot(x_ref[...].astype(BF16), w_ref[...]).astype(BF16)

    return pl.pallas_call(body, name="proj_fwd", grid=(s // tm, n // tn),
                          in_specs=[pl.BlockSpec((tm, d), lambda i, j: (i, 0)),
                                    pl.BlockSpec((d, tn), lambda i, j: (0, j))],
                          out_specs=pl.BlockSpec((tm, tn), lambda i, j: (i, j)),
                          out_shape=jax.ShapeDtypeStruct((s, n), BF16),
                          compiler_params=_params("parallel", "parallel"))(x, w)


def _matmul_tn(a, b):
    s, m = a.shape
    n = b.shape[1]
    tk, tn = min(s, 1024), _col_tile(n, 1024)
    nk = s // tk

    def body(a_ref, b_ref, o_ref):
        @pl.when(pl.program_id(1) == 0)
        def _():
            o_ref[...] = jnp.zeros_like(o_ref)

        o_ref[...] += _dot_tn(a_ref[...].astype(BF16), b_ref[...].astype(BF16))

    return pl.pallas_call(body, name="matmul_tn", grid=(n // tn, nk),
                          in_specs=[pl.BlockSpec((tk, m), lambda j, k: (k, 0)),
                                    pl.BlockSpec((tk, tn), lambda j, k: (k, j))],
                          out_specs=pl.BlockSpec((m, tn), lambda j, k: (0, j)),
                          out_shape=jax.ShapeDtypeStruct((m, n), F32),
                          compiler_params=_params("parallel", "arbitrary"))(a, b)


def _dx_matmul(dxr, dproj, w):
    s, d = dxr.shape
    n = w.shape[1]
    tm, tk = min(s, 1024), _col_tile(n, 1024)

    def body(r_ref, g_ref, w_ref, o_ref):
        @pl.when(pl.program_id(1) == 0)
        def _():
            o_ref[...] = r_ref[...]

        o_ref[...] += _dot_nt(g_ref[...], w_ref[...])

    return pl.pallas_call(body, name="dx_matmul", grid=(s // tm, n // tk),
                          in_specs=[pl.BlockSpec((tm, d), lambda i, k: (i, 0)),
                                    pl.BlockSpec((tm, tk), lambda i, k: (i, k)),
                                    pl.BlockSpec((d, tk), lambda i, k: (0, k))],
                          out_specs=pl.BlockSpec((tm, d), lambda i, k: (i, 0)),
                          out_shape=jax.ShapeDtypeStruct((s, d), F32),
                          compiler_params=_params("parallel", "arbitrary"))(dxr, dproj, w)


def _mix_chunks(ws_ref, src_ref, dst_ref, bias_ref, t, groups, chunk):
    for c in range(t // chunk):
        rows = slice(c * chunk, (c + 1) * chunk)
        for g in range(groups):
            cols = slice(g * LANES, (g + 1) * LANES)
            val = _dot(ws_ref[g], src_ref[rows, cols])
            if bias_ref is not None:
                val = val + bias_ref[:, g:g + 1]
            dst_ref[rows, cols] = val


def _gmlp_fwd(proj, vn_g, vn_b, ws_m, bs_t):
    s = proj.shape[0]
    d = proj.shape[1] // N_IN
    groups, chunk = ws_m.shape[0], ws_m.shape[1]
    t = min(s, 512)

    def body(u_ref, v_ref, ga_ref, g_ref, b_ref, ws_ref, bs_ref, o_ref, vn_ref, mix_ref):
        xhat, _ = _ln_stats(v_ref[...].astype(F32))
        vn_ref[...] = (xhat * g_ref[...] + b_ref[...]).astype(BF16)
        _mix_chunks(ws_ref, vn_ref, mix_ref, bs_ref, t, groups, chunk)
        ga = ga_ref[...].astype(F32)
        o_ref[...] = (u_ref[...].astype(F32) * mix_ref[...] * (ga * _sigmoid(ga))).astype(BF16)

    col = lambda j: pl.BlockSpec((t, d), lambda i: (i, j))
    full = lambda a: pl.BlockSpec(a.shape, lambda i: (0,) * a.ndim)
    return pl.pallas_call(body, name="gmlp_fwd", grid=(s // t,),
                          in_specs=[col(0), col(1), col(2), full(vn_g), full(vn_b), full(ws_m), full(bs_t)],
                          out_specs=pl.BlockSpec((t, d), lambda i: (i, 0)),
                          out_shape=jax.ShapeDtypeStruct((s, d), BF16),
                          scratch_shapes=[pltpu.VMEM((t, d), BF16), pltpu.VMEM((t, d), F32)],
                          compiler_params=_params("parallel"))(proj, proj, proj, vn_g, vn_b, ws_m, bs_t)


def _gmlp_bwd(dproj, proj, dya, dq, dk, dv, vn_g, vn_b, ws_m, ws_mt, bs_t):
    s = proj.shape[0]
    d = proj.shape[1] // N_IN
    groups, chunk = ws_m.shape[0], ws_m.shape[1]
    t = min(s, 256)
    nsteps = s // t

    def body(dproj_hbm, u_ref, v_ref, ga_ref, dya_ref, dq_ref, dk_ref, dv_ref, g_ref, b_ref, ws_ref, wst_ref, bs_ref,
             o_ref, dg_ref, db_ref, dws_ref, dbs_ref,
             vn_ref, mix_ref, dm_ref, dvn_ref, dbs_acc, dg_acc, db_acc):
        del dproj_hbm
        i = pl.program_id(0)

        @pl.when(i == 0)
        def _():
            dws_ref[...] = jnp.zeros_like(dws_ref)
            dbs_acc[...] = jnp.zeros_like(dbs_acc)
            dg_acc[...] = jnp.zeros_like(dg_acc)
            db_acc[...] = jnp.zeros_like(db_acc)

        xhat, rstd = _ln_stats(v_ref[...].astype(F32))
        vn_ref[...] = (xhat * g_ref[...] + b_ref[...]).astype(BF16)
        _mix_chunks(ws_ref, vn_ref, mix_ref, bs_ref, t, groups, chunk)
        ga = ga_ref[...].astype(F32)
        sg = _sigmoid(ga)
        silu = ga * sg
        dsilu = sg * (1.0 + ga * (1.0 - sg))
        u = u_ref[...].astype(F32)
        dya_f = dya_ref[...].astype(F32)
        mix = mix_ref[...]
        o_ref[:, 0:d] = (dya_f * mix * silu).astype(BF16)
        o_ref[:, 2 * d:3 * d] = (dya_f * u * mix * dsilu).astype(BF16)
        dmix = dya_f * u * silu
        dm_ref[...] = dmix.astype(BF16)
        for c in range(t // chunk):
            dbs_acc[...] += dmix[c * chunk:(c + 1) * chunk, :]
        _mix_chunks(wst_ref, dm_ref, dvn_ref, None, t, groups, chunk)
        for c in range(t // chunk):
            rows = slice(c * chunk, (c + 1) * chunk)
            for g in range(groups):
                cols = slice(g * LANES, (g + 1) * LANES)
                dws_ref[g] += _dot_nt(dm_ref[rows, cols], vn_ref[rows, cols])
        dvn = dvn_ref[...]
        dg_acc[...] += _sum_rows8(dvn * xhat)
        db_acc[...] += _sum_rows8(dvn)
        o_ref[:, d:2 * d] = _ln_bwd(dvn * g_ref[...], xhat, rstd).astype(BF16)
        o_ref[:, 3 * d:4 * d] = dq_ref[...]
        o_ref[:, 4 * d:5 * d] = dk_ref[...]
        o_ref[:, 5 * d:6 * d] = dv_ref[...]

        @pl.when(i == nsteps - 1)
        def _():
            row = lax.broadcasted_iota(jnp.int32, (chunk, chunk), 0)
            col = lax.broadcasted_iota(jnp.int32, (chunk, chunk), 1)
            for g in range(groups):
                dws_ref[g] = jnp.where(col <= row, dws_ref[g], 0.0)
            lane = lax.broadcasted_iota(jnp.int32, (chunk, LANES), 1)
            res = jnp.zeros((chunk, LANES), F32)
            for g in range(groups):
                tot = jnp.sum(dbs_acc[:, g * LANES:(g + 1) * LANES], axis=1, keepdims=True)
                res = jnp.where(lane == g, tot, res)
            dbs_ref[...] = res
            dg_ref[...] = jnp.sum(dg_acc[...], axis=0, keepdims=True)
            db_ref[...] = jnp.sum(db_acc[...], axis=0, keepdims=True)

    col = lambda j: pl.BlockSpec((t, d), lambda i: (i, j))
    tok = pl.BlockSpec((t, d), lambda i: (i, 0))
    full = lambda a: pl.BlockSpec(a.shape, lambda i: (0,) * a.ndim)
    vec = jax.ShapeDtypeStruct((1, d), F32)
    outs = pl.pallas_call(
        body, name="gmlp_bwd", grid=(nsteps,),
        in_specs=[ANY, col(0), col(1), col(2), tok, tok, tok, tok,
                  full(vn_g), full(vn_b), full(ws_m), full(ws_mt), full(bs_t)],
        out_specs=[pl.BlockSpec((t, 6 * d), lambda i: (i, 0)),
                   pl.BlockSpec((1, d), lambda i: (0, 0)), pl.BlockSpec((1, d), lambda i: (0, 0)),
                   pl.BlockSpec((groups, chunk, chunk), lambda i: (0, 0, 0)),
                   pl.BlockSpec((chunk, LANES), lambda i: (0, 0))],
        out_shape=[jax.ShapeDtypeStruct(dproj.shape, BF16), vec, vec,
                   jax.ShapeDtypeStruct((groups, chunk, chunk), F32),
                   jax.ShapeDtypeStruct((chunk, LANES), F32)],
        scratch_shapes=[pltpu.VMEM((t, d), BF16), pltpu.VMEM((t, d), F32), pltpu.VMEM((t, d), BF16),
                        pltpu.VMEM((t, d), F32), pltpu.VMEM((chunk, d), F32),
                        pltpu.VMEM((SUBLANES, d), F32), pltpu.VMEM((SUBLANES, d), F32)],
        input_output_aliases={0: 0},
        compiler_params=_params("arbitrary"))(dproj, proj, proj, proj, dya, dq, dk, dv,
                                              vn_g, vn_b, ws_m, ws_mt, bs_t)
    return outs


ATTN_TILE = 256
EXP_UNDERFLOW = -104.0


def _log_sigmoid(z):
    return jnp.minimum(z, 0.0) - jnp.log(1.0 + jnp.exp(-jnp.abs(z)))


def _suffix_rhs():
    row = lax.broadcasted_iota(jnp.int32, (LANES, LANES), 0)
    col = lax.broadcasted_iota(jnp.int32, (LANES, LANES), 1)
    rhs = jnp.concatenate([(row > col).astype(BF16), jnp.ones((LANES, LANES), BF16)], axis=1)
    return jnp.concatenate([rhs, rhs], axis=0)


def _suffix_sums(a, rhs_ref, t):
    hi = a.astype(BF16)
    lo = (a - hi.astype(F32)).astype(BF16)
    n = t // LANES
    inside, totals = [], []
    for c in range(n):
        cols = slice(c * LANES, (c + 1) * LANES)
        res = _dot(jnp.concatenate([hi[:, cols], lo[:, cols]], axis=1), rhs_ref[...])
        inside.append(res[:, :LANES])
        totals.append(res[:, LANES:])
    later = totals[n - 1]
    for c in reversed(range(n - 1)):
        inside[c] = inside[c] + later
        later = later + totals[c]
    return jnp.concatenate(inside, axis=1), later


def _lanes_to_tile(a, t):
    return jnp.concatenate([a] * (t // LANES), axis=1)


def _sweep_earlier_tiles(block, i, keep_sum_ref):
    def live():
        return jnp.max(jnp.maximum(keep_sum_ref[0], keep_sum_ref[1])) >= EXP_UNDERFLOW

    def cond(carry):
        n, alive = carry
        return jnp.logical_and(n < i, alive)

    def step(carry):
        n, _ = carry
        block(i - 1 - n, False)
        return n + 1, live()

    lax.while_loop(cond, step, (jnp.int32(0), live()))


def _attn_masks(t):
    lane = lax.broadcasted_iota(jnp.int32, (t, LANES), 1)
    row = lax.broadcasted_iota(jnp.int32, (t, t), 0)
    col = lax.broadcasted_iota(jnp.int32, (t, t), 1)
    return lane < HEAD_DIM, col < row


def _split_heads(a, head0):
    zero = jnp.zeros_like(a)
    return [jnp.where(head0, a, zero), jnp.where(head0, zero, a)]


def _attn_fwd(proj):
    s = proj.shape[0]
    d = proj.shape[1] // N_IN
    hp = d // LANES
    tq = min(s, ATTN_TILE)
    nq = s // tq
    scale = HEAD_DIM ** -0.5
    assert math.log2(scale).is_integer()

    def body(q_ref, k_ref, v_ref, rhs_ref, o_ref, acc_ref, r_ref):
        i = pl.program_id(1)
        head0, strict = _attn_masks(tq)
        qm = _split_heads((q_ref[...].astype(F32) * scale).astype(BF16), head0)
        acc_ref[...] = jnp.zeros_like(acc_ref)
        r_ref[...] = jnp.zeros_like(r_ref)

        def block(j, diag):
            off = pl.multiple_of(j * tq, tq)
            k2 = k_ref[pl.ds(off, tq), :]
            v2 = v_ref[pl.ds(off, tq), :]
            heads = range(HEADS_PER_BLOCK)
            z = [_dot_nt(qm[h], k2) for h in heads]
            lsz = [_log_sigmoid(z[h]) for h in heads]
            keep = [lsz[h] - z[h] for h in heads]
            if diag:
                keep = [jnp.where(strict, keep[h], 0.0) for h in heads]
            sums = [_suffix_sums(keep[h], rhs_ref, tq) for h in heads]
            w = [jnp.exp(lsz[h] + sums[h][0] + _lanes_to_tile(r_ref[h], tq)) for h in heads]
            if diag:
                w = [jnp.where(strict, w[h], 0.0) for h in heads]
            for h in heads:
                acc_ref[h] += _dot(w[h].astype(BF16), v2)
                r_ref[h] += sums[h][1]

        block(i, True)
        _sweep_earlier_tiles(block, i, r_ref)
        o_ref[...] = jnp.where(head0, acc_ref[0], acc_ref[1])

    rhs = _suffix_rhs()
    return pl.pallas_call(
        body, name="attn_fwd", grid=(hp, nq),
        in_specs=[pl.BlockSpec((tq, LANES), lambda h, i: (i, 3 * hp + h)),
                  pl.BlockSpec((s, LANES), lambda h, i: (0, 4 * hp + h)),
                  pl.BlockSpec((s, LANES), lambda h, i: (0, 5 * hp + h)),
                  pl.BlockSpec(rhs.shape, lambda h, i: (0, 0))],
        out_specs=pl.BlockSpec((tq, LANES), lambda h, i: (i, h)),
        out_shape=jax.ShapeDtypeStruct((s, d), F32),
        scratch_shapes=[pltpu.VMEM((HEADS_PER_BLOCK, tq, LANES), F32),
                        pltpu.VMEM((HEADS_PER_BLOCK, tq, LANES), F32)],
        compiler_params=_params("parallel", "parallel"))(proj, proj, proj, rhs)


def _attn_bwd(proj, o, do):
    s = proj.shape[0]
    d = proj.shape[1] // N_IN
    hp = d // LANES
    tq = min(s, ATTN_TILE)
    nq = s // tq
    scale = HEAD_DIM ** -0.5

    def body(q_ref, k_ref, v_ref, o_ref, do_ref, rhs_ref, dq_ref, dk_ref, dv_ref,
             dq_acc, dk_acc, dv_acc, rk_ref, rg_ref):
        i = pl.program_id(1)
        head0, strict = _attn_masks(tq)
        qm = _split_heads((q_ref[...].astype(F32) * scale).astype(BF16), head0)
        dom = _split_heads(do_ref[...], head0)
        prod = do_ref[...].astype(F32) * o_ref[...]
        delta = [jnp.sum(jnp.where(head0, prod, 0.0), axis=1, keepdims=True),
                 jnp.sum(jnp.where(head0, 0.0, prod), axis=1, keepdims=True)]

        @pl.when(i == 0)
        def _():
            dk_acc[...] = jnp.zeros_like(dk_acc)
            dv_acc[...] = jnp.zeros_like(dv_acc)

        dq_acc[...] = jnp.zeros_like(dq_acc)
        rk_ref[...] = jnp.zeros_like(rk_ref)
        for h in range(HEADS_PER_BLOCK):
            rg_ref[h] = jnp.broadcast_to(delta[h], (tq, LANES))

        def block(j, diag):
            off = pl.multiple_of(j * tq, tq)
            k2 = k_ref[pl.ds(off, tq), :]
            v2 = v_ref[pl.ds(off, tq), :]
            heads = range(HEADS_PER_BLOCK)
            z = [_dot_nt(qm[h], k2) for h in heads]
            dw = [_dot_nt(dom[h], v2) for h in heads]
            lsz = [_log_sigmoid(z[h]) for h in heads]
            keep = [lsz[h] - z[h] for h in heads]
            if diag:
                keep = [jnp.where(strict, keep[h], 0.0) for h in heads]
            ksum = [_suffix_sums(keep[h], rhs_ref, tq) for h in heads]
            w = [jnp.exp(lsz[h] + ksum[h][0] + _lanes_to_tile(rk_ref[h], tq)) for h in heads]
            if diag:
                w = [jnp.where(strict, w[h], 0.0) for h in heads]
            wb = [w[h].astype(BF16) for h in heads]
            g = [dw[h] * wb[h].astype(F32) for h in heads]
            gsum = [_suffix_sums(g[h], rhs_ref, tq) for h in heads]
            dz = [g[h] - jnp.exp(lsz[h]) * (_lanes_to_tile(rg_ref[h], tq) - gsum[h][0]) for h in heads]
            if diag:
                dz = [jnp.where(strict, dz[h], 0.0) for h in heads]
            dzb = [dz[h].astype(BF16) for h in heads]
            for h in heads:
                dq_acc[h] += _dot(dzb[h], k2)
                rk_ref[h] += ksum[h][1]
                rg_ref[h] -= gsum[h][1]
            dk_acc[pl.ds(off, tq), :] += _dot_tn(dzb[0], qm[0]) + _dot_tn(dzb[1], qm[1])
            dv_acc[pl.ds(off, tq), :] += _dot_tn(wb[0], dom[0]) + _dot_tn(wb[1], dom[1])

        block(i, True)
        _sweep_earlier_tiles(block, i, rk_ref)
        dq_ref[...] = (jnp.where(head0, dq_acc[0], dq_acc[1]) * scale).astype(BF16)

        @pl.when(i == nq - 1)
        def _():
            dk_ref[...] = dk_acc[...].astype(BF16)
            dv_ref[...] = dv_acc[...].astype(BF16)

    blk = pl.BlockSpec((tq, LANES), lambda h, i: (i, h))
    seq = pl.BlockSpec((s, LANES), lambda h, i: (0, h))
    sh = jax.ShapeDtypeStruct((s, d), BF16)
    rhs = _suffix_rhs()
    return pl.pallas_call(
        body, name="attn_bwd", grid=(hp, nq),
        in_specs=[pl.BlockSpec((tq, LANES), lambda h, i: (i, 3 * hp + h)),
                  pl.BlockSpec((s, LANES), lambda h, i: (0, 4 * hp + h)),
                  pl.BlockSpec((s, LANES), lambda h, i: (0, 5 * hp + h)),
                  blk, blk, pl.BlockSpec(rhs.shape, lambda h, i: (0, 0))],
        out_specs=[blk, seq, seq], out_shape=[sh, sh, sh],
        scratch_shapes=[pltpu.VMEM((HEADS_PER_BLOCK, tq, LANES), F32),
                        pltpu.VMEM((s, LANES), F32), pltpu.VMEM((s, LANES), F32),
                        pltpu.VMEM((HEADS_PER_BLOCK, tq, LANES), F32),
                        pltpu.VMEM((HEADS_PER_BLOCK, tq, LANES), F32)],
        compiler_params=_params("parallel", "arbitrary"))(proj, proj, proj, o, do, rhs)


def _post_math(ya_ref, o_ref, gb_ref, ma_ref, mb_ref, x_ref, p_ref, wpa_ref, wpb_ref, wout_ref, wpe_ref, wpg_ref,
               alpha):
    gb = gb_ref[...].astype(F32)
    sgb = _sigmoid(gb)
    o = o_ref[...]
    yb = (o * (gb * sgb)).astype(BF16)
    pa = _dot(ya_ref[...], wpa_ref[...])
    pb = _dot(yb, wpb_ref[...])
    sa = _sigmoid(ma_ref[...].astype(F32))
    sb = _sigmoid(mb_ref[...].astype(F32))
    merged = (sa * pa + sb * pb).astype(BF16)
    h1 = alpha * x_ref[...] + _dot(merged, wout_ref[...])
    h1b = h1.astype(BF16)
    e = _dot(p_ref[...].astype(BF16), wpe_ref[...])
    sg = _sigmoid(_dot(h1b, wpg_ref[...]))
    h2 = h1 + e * sg
    return dict(gb=gb, sgb=sgb, o=o, yb=yb, pa=pa, pb=pb, sa=sa, sb=sb, merged=merged, h1b=h1b, e=e, sg=sg, h2=h2)


def _post_specs(tm, d, ple, weights):
    tok = pl.BlockSpec((tm, d), lambda i: (i, 0))
    col = lambda j: pl.BlockSpec((tm, d), lambda i: (i, j))
    full = lambda a: pl.BlockSpec(a.shape, lambda i: (0,) * a.ndim, pipeline_mode=pl.Buffered(1))
    return tok, [tok, tok, col(6), col(7), col(8), tok, pl.BlockSpec((tm, ple), lambda i: (i, 0))] + [
        full(w) for w in weights]


def _post_fwd(ya, o, proj, x, p, w_pa, w_pb, w_out, w_pe, w_pg, ln_g, ln_b, alpha):
    s, d = x.shape
    ple = p.shape[1]
    tm = min(s, 256)
    weights = (w_pa, w_pb, w_out, w_pe, w_pg, ln_g, ln_b)

    def body(ya_ref, o_ref, gb_ref, ma_ref, mb_ref, x_ref, p_ref, wpa_ref, wpb_ref, wout_ref, wpe_ref, wpg_ref,
             g_ref, b_ref, out_ref):
        f = _post_math(ya_ref, o_ref, gb_ref, ma_ref, mb_ref, x_ref, p_ref, wpa_ref, wpb_ref, wout_ref, wpe_ref,
                       wpg_ref, alpha)
        xhat, _ = _ln_stats(f["h2"])
        out_ref[...] = xhat * g_ref[...] + b_ref[...]

    tok, in_specs = _post_specs(tm, d, ple, weights)
    return pl.pallas_call(body, name="post_fwd", grid=(s // tm,), in_specs=in_specs, out_specs=tok,
                          out_shape=jax.ShapeDtypeStruct((s, d), F32),
                          compiler_params=_params("parallel"))(ya, o, proj, proj, proj, x, p, *weights)


def _post_bwd(dxo, ya, o, proj, x, p, w_pa, w_pb, w_out, w_pe, w_pg, ln_g, ln_b, alpha):
    s, d = x.shape
    ple = p.shape[1]
    tm = min(s, 256)
    nsteps = s // tm
    weights = (w_pa, w_pb, w_out, w_pe, w_pg, ln_g, ln_b)

    def body(dxo_ref, ya_ref, o_ref, gb_ref, ma_ref, mb_ref, x_ref, p_ref, wpa_ref, wpb_ref, wout_ref, wpe_ref,
             wpg_ref, g_ref, b_ref,
             dproj_ref, dxr_ref, dya_ref, do_ref, de_ref, h1_ref, dzg_ref, mrg_ref, dh1_ref, dpa_ref, yb_ref, dpb_ref,
             dg_ref, db_ref, dg_acc, db_acc):
        i = pl.program_id(0)

        @pl.when(i == 0)
        def _():
            dg_acc[...] = jnp.zeros_like(dg_acc)
            db_acc[...] = jnp.zeros_like(db_acc)

        f = _post_math(ya_ref, o_ref, gb_ref, ma_ref, mb_ref, x_ref, p_ref, wpa_ref, wpb_ref, wout_ref, wpe_ref,
                       wpg_ref, alpha)
        xhat, rstd = _ln_stats(f["h2"])
        dxo = dxo_ref[...]
        dg_acc[...] += _sum_rows8(dxo * xhat)
        db_acc[...] += _sum_rows8(dxo)
        dh2 = _ln_bwd(dxo * g_ref[...], xhat, rstd)
        sg, e = f["sg"], f["e"]
        de_ref[...] = (dh2 * sg).astype(BF16)
        dzg = (dh2 * e * sg * (1.0 - sg)).astype(BF16)
        dzg_ref[...] = dzg
        dh1 = dh2 + _dot_nt(dzg, wpg_ref[...])
        dh1b = dh1.astype(BF16)
        dxr_ref[...] = alpha * dh1
        dh1_ref[...] = dh1b
        h1_ref[...] = f["h1b"]
        mrg_ref[...] = f["merged"]
        yb_ref[...] = f["yb"]
        dmerged = _dot_nt(dh1b, wout_ref[...])
        sa, sb = f["sa"], f["sb"]
        dpa = (dmerged * sa).astype(BF16)
        dpb = (dmerged * sb).astype(BF16)
        dpa_ref[...] = dpa
        dpb_ref[...] = dpb
        dproj_ref[:, d:2 * d] = (dmerged * f["pa"] * sa * (1.0 - sa)).astype(BF16)
        dproj_ref[:, 2 * d:3 * d] = (dmerged * f["pb"] * sb * (1.0 - sb)).astype(BF16)
        dya_ref[...] = _dot_nt(dpa, wpa_ref[...]).astype(BF16)
        dyb = _dot_nt(dpb, wpb_ref[...])
        gb, sgb = f["gb"], f["sgb"]
        do_ref[...] = (dyb * (gb * sgb)).astype(BF16)
        dproj_ref[:, 0:d] = (dyb * f["o"] * (sgb * (1.0 + gb * (1.0 - sgb)))).astype(BF16)

        @pl.when(i == nsteps - 1)
        def _():
            dg_ref[...] = jnp.sum(dg_acc[...], axis=0, keepdims=True)
            db_ref[...] = jnp.sum(db_acc[...], axis=0, keepdims=True)

    tok, in_specs = _post_specs(tm, d, ple, weights)
    vec_spec = pl.BlockSpec((1, d), lambda i: (0, 0))
    vec = jax.ShapeDtypeStruct((1, d), F32)
    act = jax.ShapeDtypeStruct((s, d), BF16)
    return pl.pallas_call(
        body, name="post_bwd", grid=(nsteps,), in_specs=[tok] + in_specs,
        out_specs=[pl.BlockSpec((tm, 3 * d), lambda i: (i, 2)), tok] + [tok] * 10 + [vec_spec, vec_spec],
        out_shape=[jax.ShapeDtypeStruct((s, N_IN * d), BF16), jax.ShapeDtypeStruct((s, d), F32)] + [act] * 10 + [vec, vec],
        scratch_shapes=[pltpu.VMEM((SUBLANES, d), F32), pltpu.VMEM((SUBLANES, d), F32)],
        compiler_params=_params("arbitrary"))(dxo, ya, o, proj, proj, proj, x, p, *weights)


def _loss_head(y, target):
    s, d = y.shape
    tm = min(s, 512)

    def body(y_ref, t_ref, dy_ref, l_ref):
        @pl.when(pl.program_id(0) == 0)
        def _():
            l_ref[...] = jnp.zeros_like(l_ref)

        err = y_ref[...] - t_ref[...]
        dy_ref[...] = err / d
        row = jnp.sum(err * err, axis=1, keepdims=True) / d
        l_ref[...] += 0.5 * jnp.sum(row, axis=0, keepdims=True)

    tok = pl.BlockSpec((tm, d), lambda i: (i, 0))
    return pl.pallas_call(body, name="loss_head", grid=(s // tm,), in_specs=[tok, tok],
                          out_specs=[tok, pl.BlockSpec((SUBLANES, LANES), lambda i: (0, 0))],
                          out_shape=[jax.ShapeDtypeStruct((s, d), F32),
                                     jax.ShapeDtypeStruct((SUBLANES, LANES), F32)],
                          compiler_params=_params("arbitrary"))(y, target)


def _position():
    x, y, c = lax.axis_index("x"), lax.axis_index("y"), lax.axis_index("c")
    chips = [(1 - x, y), (x, 1 - y), (1 - x, 1 - y)]
    return x, y, c, chips


def _shard_of(ref, col_sharded, j, n):
    off = pl.multiple_of(j * n, n)
    return ref.at[:, pl.ds(off, n)] if col_sharded else ref.at[pl.ds(off, n), :]


def _half_of(ref, col_sharded, h, n):
    off = pl.multiple_of(h * n, n)
    return ref.at[pl.ds(off, n), :] if col_sharded else ref.at[:, pl.ds(off, n)]


def _piece_of(ref, col_sharded, chip, n_block, half, n_half):
    block = pl.ds(pl.multiple_of(chip * n_block, n_block), n_block)
    part = pl.ds(pl.multiple_of(half * n_half, n_half), n_half)
    return ref.at[part, block] if col_sharded else ref.at[block, part]


def _gather_weights(bufs):
    nl, n_w = len(bufs), len(BIG)
    flat = [bufs[l][name] for l in range(nl) for name in BIG]
    n_arr = len(flat)

    def body(*refs):
        outs = refs[n_arr:2 * n_arr]
        send_sems, recv_sems, pass_send_sems, pass_recv_sems = refs[2 * n_arr:]
        x, y, c, chips = _position()
        my_chip = 2 * x + y
        sends, arrivals, passes, passed = [], [], [], []
        for a in range(n_arr):
            cs = COL_SHARDED[BIG[a % n_w]]
            rows, cols = outs[a].shape
            n_block = (cols if cs else rows) // N_CHIPS
            n_half = (rows if cs else cols) // 2
            piece = lambda chip, half: _piece_of(outs[a], cs, chip, n_block, half, n_half)
            for j, chip in enumerate(chips):
                k = a * 3 + j
                their = 2 * chip[0] + chip[1]
                sends.append(pltpu.make_async_remote_copy(
                    src_ref=piece(my_chip, c), dst_ref=piece(my_chip, c), send_sem=send_sems.at[k],
                    recv_sem=recv_sems.at[k], device_id=(chip[0], chip[1], c), device_id_type=MESH))
                arrivals.append(pltpu.make_async_remote_copy(
                    src_ref=piece(their, c), dst_ref=piece(their, c), send_sem=send_sems.at[k],
                    recv_sem=recv_sems.at[k], device_id=(chip[0], chip[1], c), device_id_type=MESH))
                passes.append(pltpu.make_async_remote_copy(
                    src_ref=piece(their, c), dst_ref=piece(their, c), send_sem=pass_send_sems.at[k],
                    recv_sem=pass_recv_sems.at[k], device_id=(x, y, 1 - c), device_id_type=MESH))
                passed.append(pltpu.make_async_remote_copy(
                    src_ref=piece(their, 1 - c), dst_ref=piece(their, 1 - c), send_sem=pass_send_sems.at[k],
                    recv_sem=pass_recv_sems.at[k], device_id=(x, y, 1 - c), device_id_type=MESH))
        for cp in sends:
            cp.start()
        for arrival, onward in zip(arrivals, passes):
            arrival.wait_recv()
            onward.start()
        for cp in passed:
            cp.wait_recv()
        for cp in sends + passes:
            cp.wait_send()

    outs = pl.pallas_call(
        body, name="gather_weights", in_specs=[ANY] * n_arr, out_specs=[ANY] * n_arr,
        out_shape=[jax.ShapeDtypeStruct(a.shape, BF16) for a in flat],
        input_output_aliases={a: a for a in range(n_arr)},
        scratch_shapes=[pltpu.SemaphoreType.DMA((n_arr * 3,))] * 4,
    )(*flat)
    return [{name: outs[l * n_w + w] for w, name in enumerate(BIG)} for l in range(nl)]


def _half_shape(shape, col_sharded):
    r, c = shape
    return (r // 2, c) if col_sharded else (r, c // 2)


def _exchange_halves(grads):
    nl, n_w = len(grads), len(BIG)
    flat = [grads[l][name] for l in range(nl) for name in BIG]
    n_arr = len(flat)

    def body(*refs):
        ins, outs = refs[:n_arr], refs[n_arr:2 * n_arr]
        send_sems, recv_sems = refs[2 * n_arr:]
        x, y, c, _ = _position()
        copies = []
        for a in range(n_arr):
            cs = COL_SHARDED[BIG[a % n_w]]
            n = outs[a].shape[0] if cs else outs[a].shape[1]
            copies.append(pltpu.make_async_remote_copy(
                src_ref=_half_of(ins[a], cs, 1 - c, n), dst_ref=outs[a], send_sem=send_sems.at[a],
                recv_sem=recv_sems.at[a], device_id=(x, y, 1 - c), device_id_type=MESH))
        for cp in copies:
            cp.start()
        for cp in copies:
            cp.wait_recv()
        for cp in copies:
            cp.wait_send()

    outs = pl.pallas_call(
        body, name="exchange_halves", in_specs=[ANY] * n_arr, out_specs=[ANY] * n_arr,
        out_shape=[jax.ShapeDtypeStruct(_half_shape(a.shape, COL_SHARDED[BIG[i % n_w]]), F32)
                   for i, a in enumerate(flat)],
        scratch_shapes=[pltpu.SemaphoreType.DMA((n_arr,)), pltpu.SemaphoreType.DMA((n_arr,))],
    )(*flat)
    return [{name: outs[l * n_w + w] for w, name in enumerate(BIG)} for l in range(nl)]


def _scatter_to_owners(halves):
    nl, n_w = len(halves), len(BIG)
    flat = [halves[l][name] for l in range(nl) for name in BIG]
    n_arr = len(flat)

    def piece_shape(a, cs):
        r, c = a.shape
        return (r, c // N_CHIPS) if cs else (r // N_CHIPS, c)

    def body(*refs):
        ins, outs = refs[:n_arr], refs[n_arr:2 * n_arr]
        send_sems, recv_sems = refs[2 * n_arr:]
        x, y, c, chips = _position()
        copies = []
        for a in range(n_arr):
            cs = COL_SHARDED[BIG[a % n_w]]
            n = outs[a].shape[2] if cs else outs[a].shape[1]
            for j, chip in enumerate(chips):
                k = a * 3 + j
                copies.append(pltpu.make_async_remote_copy(
                    src_ref=_shard_of(ins[a], cs, 2 * chip[0] + chip[1], n), dst_ref=outs[a].at[j],
                    send_sem=send_sems.at[k], recv_sem=recv_sems.at[k], device_id=(chip[0], chip[1], c),
                    device_id_type=MESH))
        for cp in copies:
            cp.start()
        for cp in copies:
            cp.wait_recv()
        for cp in copies:
            cp.wait_send()

    outs = pl.pallas_call(
        body, name="scatter_to_owners", in_specs=[ANY] * n_arr, out_specs=[ANY] * n_arr,
        out_shape=[jax.ShapeDtypeStruct((N_CHIPS - 1,) + piece_shape(a, COL_SHARDED[BIG[i % n_w]]), a.dtype)
                   for i, a in enumerate(flat)],
        scratch_shapes=[pltpu.SemaphoreType.DMA((n_arr * 3,)), pltpu.SemaphoreType.DMA((n_arr * 3,))],
    )(*flat)
    return [{name: outs[l * n_w + w] for w, name in enumerate(BIG)} for l in range(nl)]


def _join_halves(halves):
    flat = [halves[name] for name in BIG]
    n_w = len(flat)

    def body(*refs):
        outs = refs[n_w:2 * n_w]
        send_sems, recv_sems = refs[2 * n_w:]
        x, y, c, _ = _position()
        sends, recvs = [], []
        for w, name in enumerate(BIG):
            cs = COL_SHARDED[name]
            n = (outs[w].shape[1] if cs else outs[w].shape[2]) // 2

            def half(h):
                part = pl.ds(pl.multiple_of(h * n, n), n)
                return outs[w].at[:, part, :] if cs else outs[w].at[:, :, part]

            sends.append(pltpu.make_async_remote_copy(
                src_ref=half(c), dst_ref=half(c), send_sem=send_sems.at[w], recv_sem=recv_sems.at[w],
                device_id=(x, y, 1 - c), device_id_type=MESH))
            recvs.append(pltpu.make_async_remote_copy(
                src_ref=half(1 - c), dst_ref=half(1 - c), send_sem=send_sems.at[w], recv_sem=recv_sems.at[w],
                device_id=(x, y, 1 - c), device_id_type=MESH))
        for cp in sends:
            cp.start()
        for cp in recvs:
            cp.wait_recv()
        for cp in sends:
            cp.wait_send()

    outs = pl.pallas_call(
        body, name="join_halves", in_specs=[ANY] * n_w, out_specs=[ANY] * n_w,
        out_shape=[jax.ShapeDtypeStruct(a.shape, F32) for a in flat],
        input_output_aliases={w: w for w in range(n_w)},
        scratch_shapes=[pltpu.SemaphoreType.DMA((n_w,)), pltpu.SemaphoreType.DMA((n_w,))],
    )(*flat)
    return dict(zip(BIG, outs))


def _gather_small(packed):
    r, lanes = packed.shape

    def body(in_ref, out_ref, send_sems, recv_sems, loc_sem):
        x, y, c, _ = _position()
        me = 4 * x + 2 * y + c
        local = pltpu.make_async_copy(in_ref, out_ref.at[me], loc_sem)
        local.start()
        sends, recvs = [], []
        for k in range(1, N_DEV):
            px, py, pc = x ^ (k >> 2), y ^ ((k >> 1) & 1), c ^ (k & 1)
            sends.append(pltpu.make_async_remote_copy(
                src_ref=in_ref, dst_ref=out_ref.at[me], send_sem=send_sems.at[k - 1], recv_sem=recv_sems.at[k - 1],
                device_id=(px, py, pc), device_id_type=MESH))
            recvs.append(pltpu.make_async_remote_copy(
                src_ref=in_ref, dst_ref=out_ref.at[4 * px + 2 * py + pc], send_sem=send_sems.at[k - 1],
                recv_sem=recv_sems.at[k - 1], device_id=(px, py, pc), device_id_type=MESH))
        for cp in sends:
            cp.start()
        for cp in recvs:
            cp.wait_recv()
        for cp in sends:
            cp.wait_send()
        local.wait()

    return pl.pallas_call(
        body, name="gather_small", in_specs=[ANY], out_specs=ANY,
        out_shape=jax.ShapeDtypeStruct((N_DEV, r, lanes), F32),
        scratch_shapes=[pltpu.SemaphoreType.DMA((N_DEV - 1,)), pltpu.SemaphoreType.DMA((N_DEV - 1,)),
                        pltpu.SemaphoreType.DMA],
    )(packed)


def _pack_small(t):
    return jnp.concatenate([t[name].reshape(-1, LANES) for name in SMALL], axis=0)


def _unpack_small(packed, like):
    out, row = {}, 0
    for name in SMALL:
        n = like[name].size // LANES
        out[name] = packed[row:row + n].reshape(like[name].shape)
        row += n
    return out


def kernel(x, p, w_in, vn_g, vn_b, w_s, b_s, w_pa, w_pb, w_out, w_pe, w_pg, ln_g, ln_b, loss_target, m_w_in, m_vn_g, m_vn_b, m_w_s, m_b_s, m_w_pa, m_w_pb, m_w_out, m_w_pe, m_w_pg, m_ln_g, m_ln_b, v_w_in, v_vn_g, v_vn_b, v_w_s, v_b_s, v_w_pa, v_w_pb, v_w_out, v_w_pe, v_w_pg, v_ln_g, v_ln_b):
    weights = dict(w_in=w_in, vn_g=vn_g, vn_b=vn_b, w_s=w_s, b_s=b_s, w_pa=w_pa, w_pb=w_pb, w_out=w_out, w_pe=w_pe,
                   w_pg=w_pg, ln_g=ln_g, ln_b=ln_b)
    mom1 = dict(w_in=m_w_in, vn_g=m_vn_g, vn_b=m_vn_b, w_s=m_w_s, b_s=m_b_s, w_pa=m_w_pa, w_pb=m_w_pb, w_out=m_w_out,
                w_pe=m_w_pe, w_pg=m_w_pg, ln_g=m_ln_g, ln_b=m_ln_b)
    mom2 = dict(w_in=v_w_in, vn_g=v_vn_g, vn_b=v_vn_b, w_s=v_w_s, b_s=v_b_s, w_pa=v_w_pa, w_pb=v_w_pb, w_out=v_w_out,
                w_pe=v_w_pe, w_pg=v_w_pg, ln_g=v_ln_g, ln_b=v_ln_b)
    nl, d = vn_g.shape
    chunk = w_s.shape[2]
    assert chunk == LANES and w_s.shape[3] == LANES and d % LANES == 0
    alpha = (2 * nl) ** 0.25
    pos = tuple(lax.axis_index(a).astype(jnp.int32).reshape(1) for a in ("c", "x", "y"))

    full = _gather_weights([{name: _cast_into_place(weights[name], l, pos, COL_SHARDED[name]) for name in BIG}
                            for l in range(nl)])
    causal = jnp.tril(jnp.ones((chunk, chunk), dtype=bool))
    ws_m = jnp.where(causal, w_s, 0.0).astype(BF16)
    ws_mt = jnp.swapaxes(ws_m, 2, 3)
    bs_t = jnp.swapaxes(b_s, 1, 2)

    xs, projs, yas, os_ = [x[0]], [], [], []
    for l in range(nl):
        proj = _proj_fwd(xs[l], full[l]["w_in"])
        ya = _gmlp_fwd(proj, vn_g[l:l + 1], vn_b[l:l + 1], ws_m[l], bs_t[l])
        o = _attn_fwd(proj)
        xs.append(_post_fwd(ya, o, proj, xs[l], p[l, 0], full[l]["w_pa"], full[l]["w_pb"], full[l]["w_out"],
                            full[l]["w_pe"], full[l]["w_pg"], ln_g[l:l + 1], ln_b[l:l + 1], alpha))
        projs.append(proj)
        yas.append(ya)
        os_.append(o)

    dx, loss_tile = _loss_head(xs[nl], loss_target[0])
    loss = lax.psum(loss_tile[0, 0], ("x", "y", "c"))

    big_grads = [None] * nl
    small_grads = {name: [None] * nl for name in SMALL}
    for l in reversed(range(nl)):
        w = full[l]
        (dproj, dxr, dya, do, de, h1b, dzg, merged, dh1, dpa, yb, dpb, dln_g, dln_b) = _post_bwd(
            dx, yas[l], os_[l], projs[l], xs[l], p[l, 0], w["w_pa"], w["w_pb"], w["w_out"], w["w_pe"], w["w_pg"],
            ln_g[l:l + 1], ln_b[l:l + 1], alpha)
        dq, dk, dv = _attn_bwd(projs[l], os_[l], do)
        dproj, dvn_g, dvn_b, dw_s, dbs_cols = _gmlp_bwd(dproj, projs[l], dya, dq, dk, dv, vn_g[l:l + 1],
                                                         vn_b[l:l + 1], ws_m[l], ws_mt[l], bs_t[l])
        big_grads[l] = dict(w_in=_matmul_tn(xs[l], dproj), w_pa=_matmul_tn(yas[l], dpa), w_pb=_matmul_tn(yb, dpb),
                            w_out=_matmul_tn(merged, dh1), w_pe=_matmul_tn(p[l, 0], de), w_pg=_matmul_tn(h1b, dzg))
        dx = _dx_matmul(dxr, dproj, w["w_in"])
        small_grads["vn_g"][l], small_grads["vn_b"][l] = dvn_g[0], dvn_b[0]
        small_grads["ln_g"][l], small_grads["ln_b"][l] = dln_g[0], dln_b[0]
        small_grads["w_s"][l] = dw_s
        small_grads["b_s"][l] = dbs_cols[:, :b_s.shape[1]].T

    received = _exchange_halves(big_grads)
    chip_sums = [{name: _add_own_half(big_grads[l][name], received[l][name], pos, COL_SHARDED[name])
                  for name in BIG} for l in range(nl)]
    slots = _scatter_to_owners(chip_sums)
    reduced = {}
    for name in BIG:
        buf = None
        for l in range(nl):
            buf = _reduce_block(buf, big_grads[l][name], received[l][name], slots[l][name], l, nl, pos,
                                COL_SHARDED[name])
        reduced[name] = buf
    grads = _join_halves(reduced)

    small_like = {name: weights[name] for name in SMALL}
    packed = _pack_small({name: jnp.stack(small_grads[name]) for name in SMALL})
    grads.update(_unpack_small(_sum_slots(_gather_small(packed)), small_like))

    delta, new_m, new_v = {}, {}, {}
    for name in BIG:
        sh = weights[name].shape
        flat = lambda a: a.reshape(sh[0] * sh[1], sh[2])
        dl, nm, nv = _adamw(flat(weights[name]), flat(grads[name]), flat(mom1[name]), flat(mom2[name]))
        delta[name], new_m[name], new_v[name] = dl.reshape(sh), nm.reshape(sh), nv.reshape(sh)
    dl, nm, nv = _adamw(_pack_small(small_like), _pack_small({n: grads[n] for n in SMALL}),
                        _pack_small({n: mom1[n] for n in SMALL}), _pack_small({n: mom2[n] for n in SMALL}))
    delta.update(_unpack_small(dl, small_like))
    new_m.update(_unpack_small(nm, small_like))
    new_v.update(_unpack_small(nv, small_like))

    return (loss, dx[None], *[grads[n] for n in WEIGHTS], *[delta[n] for n in WEIGHTS],
            *[new_m[n] for n in WEIGHTS], *[new_v[n] for n in WEIGHTS])
```

```python
import math
from typing import Callable, NamedTuple

import jax
import jax.numpy as jnp
from jax import lax
from jax.experimental import pallas as pl
from jax.experimental.pallas import tpu as pltpu

F32 = jnp.float32
BF16 = jnp.bfloat16
LANES = 128
SUBLANES = 8
HEAD_DIM = 64
HEADS_PER_BLOCK = LANES // HEAD_DIM
LN_EPS = 1e-5
N_IN = 9
N_CHIPS = 4
N_DEV = 8
ADAM_LR = 0.001
ADAM_B1 = 0.9
ADAM_B2 = 0.999
ADAM_EPS = 1e-08
ADAM_WD = 0.01
ADAM_STEP = 10
MESH = pl.DeviceIdType.MESH
ANY = pl.BlockSpec(memory_space=pl.ANY)
BIG = ("w_in", "w_pa", "w_pb", "w_out", "w_pe", "w_pg")
COL_SHARDED = {"w_in": True, "w_pa": False, "w_pb": False, "w_out": False, "w_pe": True, "w_pg": False}
SMALL = ("vn_g", "vn_b", "w_s", "b_s", "ln_g", "ln_b")
WEIGHTS = ("w_in", "vn_g", "vn_b", "w_s", "b_s", "w_pa", "w_pb", "w_out", "w_pe", "w_pg", "ln_g", "ln_b")


def _params(*sem):
    return pltpu.CompilerParams(dimension_semantics=sem)


def _dot(a, b):
    return jnp.dot(a, b, preferred_element_type=F32)


def _dot_nt(a, b):
    return lax.dot_general(a, b, (((1,), (1,)), ((), ())), preferred_element_type=F32)


def _dot_tn(a, b):
    return lax.dot_general(a, b, (((0,), (0,)), ((), ())), preferred_element_type=F32)


def _sigmoid(a):
    return 1.0 / (1.0 + jnp.exp(-a))


def _row_tile(rows, cols, cap_bytes):
    best = None
    for t in range(16, rows + 1, 16):
        if rows % t == 0 and t * cols * 4 <= cap_bytes:
            best = t
    return best or rows


def _col_tile(cols, cap):
    best = LANES
    for t in range(LANES, min(cols, cap) + 1, LANES):
        if cols % t == 0:
            best = t
    return best


def _ln_stats(h):
    mu = jnp.mean(h, axis=-1, keepdims=True)
    hc = h - mu
    var = jnp.mean(hc * hc, axis=-1, keepdims=True)
    rstd = lax.rsqrt(var + LN_EPS)
    return hc * rstd, rstd


def _ln_bwd(dxhat, xhat, rstd):
    m1 = jnp.mean(dxhat, axis=-1, keepdims=True)
    m2 = jnp.mean(dxhat * xhat, axis=-1, keepdims=True)
    return rstd * (dxhat - m1 - xhat * m2)


def _sum_rows8(a):
    t, d = a.shape
    return jnp.sum(a.reshape(t // SUBLANES, SUBLANES, d), axis=0)


def _chip(x_ref, y_ref):
    return 2 * x_ref[0] + y_ref[0]


def _cast_into_place(shards, l, pos, col_sharded):
    _, r, c = shards.shape
    tr = _row_tile(r, c, 2 << 20)
    nb = r // tr

    def body(c_ref, x_ref, y_ref, a_ref, o_ref):
        o_ref[...] = a_ref[...].astype(BF16)

    if col_sharded:
        out_spec = pl.BlockSpec((tr, c), lambda i, c_ref, x_ref, y_ref: (i, _chip(x_ref, y_ref)))
        full = (r, c * N_CHIPS)
    else:
        out_spec = pl.BlockSpec((tr, c), lambda i, c_ref, x_ref, y_ref: (_chip(x_ref, y_ref) * nb + i, 0))
        full = (r * N_CHIPS, c)
    grid_spec = pltpu.PrefetchScalarGridSpec(
        num_scalar_prefetch=3, grid=(nb,),
        in_specs=[pl.BlockSpec((None, tr, c), lambda i, c_ref, x_ref, y_ref: (l, i, 0))], out_specs=out_spec)
    return pl.pallas_call(body, name="cast_into_place", grid_spec=grid_spec,
                          out_shape=jax.ShapeDtypeStruct(full, BF16),
                          compiler_params=_params("parallel"))(*pos, shards)


def _add_own_half(own, recv, pos, col_sharded):
    r, c = recv.shape
    tr = _row_tile(r, c, 2 << 20)
    nb = r // tr

    def body(c_ref, x_ref, y_ref, own_ref, recv_ref, o_ref):
        o_ref[...] = (own_ref[...] + recv_ref[...]).astype(BF16)

    if col_sharded:
        own_spec = pl.BlockSpec((tr, c), lambda i, c_ref, x_ref, y_ref: (c_ref[0] * nb + i, 0))
    else:
        own_spec = pl.BlockSpec((tr, c), lambda i, c_ref, x_ref, y_ref: (i, c_ref[0]))
    spec = pl.BlockSpec((tr, c), lambda i, c_ref, x_ref, y_ref: (i, 0))
    grid_spec = pltpu.PrefetchScalarGridSpec(num_scalar_prefetch=3, grid=(nb,), in_specs=[own_spec, spec],
                                             out_specs=spec)
    return pl.pallas_call(body, name="add_own_half", grid_spec=grid_spec,
                          out_shape=jax.ShapeDtypeStruct(recv.shape, BF16),
                          compiler_params=_params("parallel"))(*pos, own, recv)


def _reduce_block(buf, own, recv, slots, l, nl, pos, col_sharded):
    _, r, c = slots.shape
    tr = _row_tile(r, c, 1 << 20)
    nb = r // tr

    def body(c_ref, x_ref, y_ref, own_ref, recv_ref, slots_ref, *rest):
        acc = own_ref[...] + recv_ref[...]
        for j in range(N_CHIPS - 1):
            acc = acc + slots_ref[j].astype(F32)
        rest[-1][...] = acc

    if col_sharded:
        own_spec = pl.BlockSpec((tr, c), lambda i, c_ref, x_ref, y_ref: (c_ref[0] * nb + i, _chip(x_ref, y_ref)))
        recv_spec = pl.BlockSpec((tr, c), lambda i, c_ref, x_ref, y_ref: (i, _chip(x_ref, y_ref)))
        out_spec = pl.BlockSpec((None, tr, c), lambda i, c_ref, x_ref, y_ref: (l, c_ref[0] * nb + i, 0))
        out_shape = (nl, 2 * r, c)
    else:
        own_spec = pl.BlockSpec((tr, c), lambda i, c_ref, x_ref, y_ref: (_chip(x_ref, y_ref) * nb + i, c_ref[0]))
        recv_spec = pl.BlockSpec((tr, c), lambda i, c_ref, x_ref, y_ref: (_chip(x_ref, y_ref) * nb + i, 0))
        out_spec = pl.BlockSpec((None, tr, c), lambda i, c_ref, x_ref, y_ref: (l, i, c_ref[0]))
        out_shape = (nl, r, 2 * c)
    in_specs = [own_spec, recv_spec,
                pl.BlockSpec((N_CHIPS - 1, tr, c), lambda i, c_ref, x_ref, y_ref: (0, i, 0))]
    args = [*pos, own, recv, slots]
    aliases = {}
    if buf is not None:
        in_specs.append(ANY)
        args.append(buf)
        aliases = {len(args) - 1: 0}
    grid_spec = pltpu.PrefetchScalarGridSpec(num_scalar_prefetch=3, grid=(nb,), in_specs=in_specs,
                                             out_specs=out_spec)
    return pl.pallas_call(body, name="reduce_block", grid_spec=grid_spec,
                          out_shape=jax.ShapeDtypeStruct(out_shape, F32), input_output_aliases=aliases,
                          compiler_params=_params("parallel"))(*args)


def _sum_slots(a):
    n, r, c = a.shape
    tr = _row_tile(r, c * n, 4 << 20)

    def body(a_ref, o_ref):
        acc = a_ref[0]
        for s in range(1, n):
            acc = acc + a_ref[s]
        o_ref[...] = acc

    return pl.pallas_call(body, name="sum_slots", grid=(r // tr,),
                          in_specs=[pl.BlockSpec((n, tr, c), lambda i: (0, i, 0))],
                          out_specs=pl.BlockSpec((tr, c), lambda i: (i, 0)),
                          out_shape=jax.ShapeDtypeStruct((r, c), F32),
                          compiler_params=_params("parallel"))(a)


def _adamw(w, g, m, v):
    r, c = w.shape
    tr = _row_tile(r, c, 1 << 20)

    def body(w_ref, g_ref, m_ref, v_ref, d_ref, nm_ref, nv_ref):
        gg = g_ref[...]
        nm = ADAM_B1 * m_ref[...] + (1.0 - ADAM_B1) * gg
        nv = ADAM_B2 * v_ref[...] + (1.0 - ADAM_B2) * (gg * gg)
        m_hat = nm / (1.0 - ADAM_B1 ** ADAM_STEP)
        v_hat = nv / (1.0 - ADAM_B2 ** ADAM_STEP)
        d_ref[...] = -ADAM_LR * (m_hat / (jnp.sqrt(v_hat) + ADAM_EPS) + ADAM_WD * w_ref[...])
        nm_ref[...] = nm
        nv_ref[...] = nv

    spec = pl.BlockSpec((tr, c), lambda i: (i, 0))
    sh = jax.ShapeDtypeStruct((r, c), F32)
    return pl.pallas_call(body, name="adamw", grid=(r // tr,), in_specs=[spec] * 4, out_specs=[spec] * 3,
                          out_shape=[sh, sh, sh], compiler_params=_params("parallel"))(w, g, m, v)


def _proj_fwd(x, w):
    s, d = x.shape
    n = w.shape[1]
    tm, tn = min(s, 1024), _col_tile(n, 1024)

    def body(x_ref, w_ref, o_ref):
        o_ref[...] = _dot(x_ref[...].astype(BF16), w_ref[...]).astype(BF16)

    return pl.pallas_call(body, name="proj_fwd", grid=(s // tm, n // tn),
                          in_specs=[pl.BlockSpec((tm, d), lambda i, j: (i, 0)),
                                    pl.BlockSpec((d, tn), lambda i, j: (0, j))],
                          out_specs=pl.BlockSpec((tm, tn), lambda i, j: (i, j)),
                          out_shape=jax.ShapeDtypeStruct((s, n), BF16),
                          compiler_params=_params("parallel", "parallel"))(x, w)


def _matmul_tn(a, b):
    s, m = a.shape
    n = b.shape[1]
    tk, tn = min(s, 1024), _col_tile(n, 1024)
    nk = s // tk

    def body(a_ref, b_ref, o_ref):
        @pl.when(pl.program_id(1) == 0)
        def _():
            o_ref[...] = jnp.zeros_like(o_ref)

        o_ref[...] += _dot_tn(a_ref[...].astype(BF16), b_ref[...].astype(BF16))

    return pl.pallas_call(body, name="matmul_tn", grid=(n // tn, nk),
                          in_specs=[pl.BlockSpec((tk, m), lambda j, k: (k, 0)),
                                    pl.BlockSpec((tk, tn), lambda j, k: (k, j))],
                          out_specs=pl.BlockSpec((m, tn), lambda j, k: (0, j)),
                          out_shape=jax.ShapeDtypeStruct((m, n), F32),
                          compiler_params=_params("parallel", "arbitrary"))(a, b)


def _dx_matmul(dxr, dproj, w):
    s, d = dxr.shape
    n = w.shape[1]
    tm, tk = min(s, 1024), _col_tile(n, 1024)

    def body(r_ref, g_ref, w_ref, o_ref):
        @pl.when(pl.program_id(1) == 0)
        def _():
            o_ref[...] = r_ref[...]

        o_ref[...] += _dot_nt(g_ref[...], w_ref[...])

    return pl.pallas_call(body, name="dx_matmul", grid=(s // tm, n // tk),
                          in_specs=[pl.BlockSpec((tm, d), lambda i, k: (i, 0)),
                                    pl.BlockSpec((tm, tk), lambda i, k: (i, k)),
                                    pl.BlockSpec((d, tk), lambda i, k: (0, k))],
                          out_specs=pl.BlockSpec((tm, d), lambda i, k: (i, 0)),
                          out_shape=jax.ShapeDtypeStruct((s, d), F32),
                          compiler_params=_params("parallel", "arbitrary"))(dxr, dproj, w)


def _mix_chunks(ws_ref, src_ref, dst_ref, bias_ref, t, groups, chunk):
    for c in range(t // chunk):
        rows = slice(c * chunk, (c + 1) * chunk)
        for g in range(groups):
            cols = slice(g * LANES, (g + 1) * LANES)
            val = _dot(ws_ref[g], src_ref[rows, cols])
            if bias_ref is not None:
                val = val + bias_ref[:, g:g + 1]
            dst_ref[rows, cols] = val


def _gmlp_fwd(proj, vn_g, vn_b, ws_m, bs_t):
    s = proj.shape[0]
    d = proj.shape[1] // N_IN
    groups, chunk = ws_m.shape[0], ws_m.shape[1]
    t = min(s, 512)

    def body(u_ref, v_ref, ga_ref, g_ref, b_ref, ws_ref, bs_ref, o_ref, vn_ref, mix_ref):
        xhat, _ = _ln_stats(v_ref[...].astype(F32))
        vn_ref[...] = (xhat * g_ref[...] + b_ref[...]).astype(BF16)
        _mix_chunks(ws_ref, vn_ref, mix_ref, bs_ref, t, groups, chunk)
        ga = ga_ref[...].astype(F32)
        o_ref[...] = (u_ref[...].astype(F32) * mix_ref[...] * (ga * _sigmoid(ga))).astype(BF16)

    col = lambda j: pl.BlockSpec((t, d), lambda i: (i, j))
    full = lambda a: pl.BlockSpec(a.shape, lambda i: (0,) * a.ndim)
    return pl.pallas_call(body, name="gmlp_fwd", grid=(s // t,),
                          in_specs=[col(0), col(1), col(2), full(vn_g), full(vn_b), full(ws_m), full(bs_t)],
                          out_specs=pl.BlockSpec((t, d), lambda i: (i, 0)),
                          out_shape=jax.ShapeDtypeStruct((s, d), BF16),
                          scratch_shapes=[pltpu.VMEM((t, d), BF16), pltpu.VMEM((t, d), F32)],
                          compiler_params=_params("parallel"))(proj, proj, proj, vn_g, vn_b, ws_m, bs_t)


def _gmlp_bwd(dproj, proj, dya, dq, dk, dv, vn_g, vn_b, ws_m, ws_mt, bs_t):
    s = proj.shape[0]
    d = proj.shape[1] // N_IN
    groups, chunk = ws_m.shape[0], ws_m.shape[1]
    t = min(s, 256)
    nsteps = s // t

    def body(dproj_hbm, u_ref, v_ref, ga_ref, dya_ref, dq_ref, dk_ref, dv_ref, g_ref, b_ref, ws_ref, wst_ref, bs_ref,
             o_ref, dg_ref, db_ref, dws_ref, dbs_ref,
             vn_ref, mix_ref, dm_ref, dvn_ref, dbs_acc, dg_acc, db_acc):
        del dproj_hbm
        i = pl.program_id(0)

        @pl.when(i == 0)
        def _():
            dws_ref[...] = jnp.zeros_like(dws_ref)
            dbs_acc[...] = jnp.zeros_like(dbs_acc)
            dg_acc[...] = jnp.zeros_like(dg_acc)
            db_acc[...] = jnp.zeros_like(db_acc)

        xhat, rstd = _ln_stats(v_ref[...].astype(F32))
        vn_ref[...] = (xhat * g_ref[...] + b_ref[...]).astype(BF16)
        _mix_chunks(ws_ref, vn_ref, mix_ref, bs_ref, t, groups, chunk)
        ga = ga_ref[...].astype(F32)
        sg = _sigmoid(ga)
        silu = ga * sg
        dsilu = sg * (1.0 + ga * (1.0 - sg))
        u = u_ref[...].astype(F32)
        dya_f = dya_ref[...].astype(F32)
        mix = mix_ref[...]
        o_ref[:, 0:d] = (dya_f * mix * silu).astype(BF16)
        o_ref[:, 2 * d:3 * d] = (dya_f * u * mix * dsilu).astype(BF16)
        dmix = dya_f * u * silu
        dm_ref[...] = dmix.astype(BF16)
        for c in range(t // chunk):
            dbs_acc[...] += dmix[c * chunk:(c + 1) * chunk, :]
        _mix_chunks(wst_ref, dm_ref, dvn_ref, None, t, groups, chunk)
        for c in range(t // chunk):
            rows = slice(c * chunk, (c + 1) * chunk)
            for g in range(groups):
                cols = slice(g * LANES, (g + 1) * LANES)
                dws_ref[g] += _dot_nt(dm_ref[rows, cols], vn_ref[rows, cols])
        dvn = dvn_ref[...]
        dg_acc[...] += _sum_rows8(dvn * xhat)
        db_acc[...] += _sum_rows8(dvn)
        o_ref[:, d:2 * d] = _ln_bwd(dvn * g_ref[...], xhat, rstd).astype(BF16)
        o_ref[:, 3 * d:4 * d] = dq_ref[...]
        o_ref[:, 4 * d:5 * d] = dk_ref[...]
        o_ref[:, 5 * d:6 * d] = dv_ref[...]

        @pl.when(i == nsteps - 1)
        def _():
            row = lax.broadcasted_iota(jnp.int32, (chunk, chunk), 0)
            col = lax.broadcasted_iota(jnp.int32, (chunk, chunk), 1)
            for g in range(groups):
                dws_ref[g] = jnp.where(col <= row, dws_ref[g], 0.0)
            lane = lax.broadcasted_iota(jnp.int32, (chunk, LANES), 1)
            res = jnp.zeros((chunk, LANES), F32)
            for g in range(groups):
                tot = jnp.sum(dbs_acc[:, g * LANES:(g + 1) * LANES], axis=1, keepdims=True)
                res = jnp.where(lane == g, tot, res)
            dbs_ref[...] = res
            dg_ref[...] = jnp.sum(dg_acc[...], axis=0, keepdims=True)
            db_ref[...] = jnp.sum(db_acc[...], axis=0, keepdims=True)

    col = lambda j: pl.BlockSpec((t, d), lambda i: (i, j))
    tok = pl.BlockSpec((t, d), lambda i: (i, 0))
    full = lambda a: pl.BlockSpec(a.shape, lambda i: (0,) * a.ndim)
    vec = jax.ShapeDtypeStruct((1, d), F32)
    outs = pl.pallas_call(
        body, name="gmlp_bwd", grid=(nsteps,),
        in_specs=[ANY, col(0), col(1), col(2), tok, tok, tok, tok,
                  full(vn_g), full(vn_b), full(ws_m), full(ws_mt), full(bs_t)],
        out_specs=[pl.BlockSpec((t, 6 * d), lambda i: (i, 0)),
                   pl.BlockSpec((1, d), lambda i: (0, 0)), pl.BlockSpec((1, d), lambda i: (0, 0)),
                   pl.BlockSpec((groups, chunk, chunk), lambda i: (0, 0, 0)),
                   pl.BlockSpec((chunk, LANES), lambda i: (0, 0))],
        out_shape=[jax.ShapeDtypeStruct(dproj.shape, BF16), vec, vec,
                   jax.ShapeDtypeStruct((groups, chunk, chunk), F32),
                   jax.ShapeDtypeStruct((chunk, LANES), F32)],
        scratch_shapes=[pltpu.VMEM((t, d), BF16), pltpu.VMEM((t, d), F32), pltpu.VMEM((t, d), BF16),
                        pltpu.VMEM((t, d), F32), pltpu.VMEM((chunk, d), F32),
                        pltpu.VMEM((SUBLANES, d), F32), pltpu.VMEM((SUBLANES, d), F32)],
        input_output_aliases={0: 0},
        compiler_params=_params("arbitrary"))(dproj, proj, proj, proj, dya, dq, dk, dv,
                                              vn_g, vn_b, ws_m, ws_mt, bs_t)
    return outs


ATTN_TILE = 256
EXP_UNDERFLOW = -104.0


def _log_sigmoid(z):
    return jnp.minimum(z, 0.0) - jnp.log(1.0 + jnp.exp(-jnp.abs(z)))


def _suffix_rhs():
    row = lax.broadcasted_iota(jnp.int32, (LANES, LANES), 0)
    col = lax.broadcasted_iota(jnp.int32, (LANES, LANES), 1)
    rhs = jnp.concatenate([(row > col).astype(BF16), jnp.ones((LANES, LANES), BF16)], axis=1)
    return jnp.concatenate([rhs, rhs], axis=0)


def _suffix_sums(a, rhs_ref, t):
    hi = a.astype(BF16)
    lo = (a - hi.astype(F32)).astype(BF16)
    n = t // LANES
    inside, totals = [], []
    for c in range(n):
        cols = slice(c * LANES, (c + 1) * LANES)
        res = _dot(jnp.concatenate([hi[:, cols], lo[:, cols]], axis=1), rhs_ref[...])
        inside.append(res[:, :LANES])
        totals.append(res[:, LANES:])
    later = totals[n - 1]
    for c in reversed(range(n - 1)):
        inside[c] = inside[c] + later
        later = later + totals[c]
    return jnp.concatenate(inside, axis=1), later


def _lanes_to_tile(a, t):
    return jnp.concatenate([a] * (t // LANES), axis=1)


def _sweep_earlier_tiles(block, i, keep_sum_ref):
    def live():
        return jnp.max(jnp.maximum(keep_sum_ref[0], keep_sum_ref[1])) >= EXP_UNDERFLOW

    def cond(carry):
        n, alive = carry
        return jnp.logical_and(n < i, alive)

    def step(carry):
        n, _ = carry
        block(i - 1 - n, False)
        return n + 1, live()

    lax.while_loop(cond, step, (jnp.int32(0), live()))


def _attn_masks(t):
    lane = lax.broadcasted_iota(jnp.int32, (t, LANES), 1)
    row = lax.broadcasted_iota(jnp.int32, (t, t), 0)
    col = lax.broadcasted_iota(jnp.int32, (t, t), 1)
    return lane < HEAD_DIM, col < row


def _split_heads(a, head0):
    zero = jnp.zeros_like(a)
    return [jnp.where(head0, a, zero), jnp.where(head0, zero, a)]


def _attn_fwd(proj, rider=None):
    s = proj.shape[0]
    d = proj.shape[1] // N_IN
    hp = d // LANES
    tq = min(s, ATTN_TILE)
    nq = s // tq
    scale = HEAD_DIM ** -0.5
    assert math.log2(scale).is_integer()

    def body(*refs):
        h_id, i = pl.program_id(0), pl.program_id(1)
        first = jnp.logical_and(h_id == 0, i == 0)
        last = jnp.logical_and(h_id == hp - 1, i == nq - 1)
        (q_ref, k_ref, v_ref, rhs_ref), (o_ref,), (acc_ref, r_ref), start, finish = _ride(rider, refs, 4, 1, first, last)
        start()
        head0, strict = _attn_masks(tq)
        qm = _split_heads((q_ref[...].astype(F32) * scale).astype(BF16), head0)
        acc_ref[...] = jnp.zeros_like(acc_ref)
        r_ref[...] = jnp.zeros_like(r_ref)

        def block(j, diag):
            off = pl.multiple_of(j * tq, tq)
            k2 = k_ref[pl.ds(off, tq), :]
            v2 = v_ref[pl.ds(off, tq), :]
            heads = range(HEADS_PER_BLOCK)
            z = [_dot_nt(qm[h], k2) for h in heads]
            lsz = [_log_sigmoid(z[h]) for h in heads]
            keep = [lsz[h] - z[h] for h in heads]
            if diag:
                keep = [jnp.where(strict, keep[h], 0.0) for h in heads]
            sums = [_suffix_sums(keep[h], rhs_ref, tq) for h in heads]
            w = [jnp.exp(lsz[h] + sums[h][0] + _lanes_to_tile(r_ref[h], tq)) for h in heads]
            if diag:
                w = [jnp.where(strict, w[h], 0.0) for h in heads]
            for h in heads:
                acc_ref[h] += _dot(w[h].astype(BF16), v2)
                r_ref[h] += sums[h][1]

        block(i, True)
        _sweep_earlier_tiles(block, i, r_ref)
        o_ref[...] = jnp.where(head0, acc_ref[0], acc_ref[1])
        finish()

    rhs = _suffix_rhs()
    r_in, r_out, r_shapes, r_sems, r_alias = _rider_call_args(rider, 4, 1)
    outs = pl.pallas_call(
        body, name="attn_fwd", grid=(hp, nq),
        in_specs=[pl.BlockSpec((tq, LANES), lambda h, i: (i, 3 * hp + h)),
                  pl.BlockSpec((s, LANES), lambda h, i: (0, 4 * hp + h)),
                  pl.BlockSpec((s, LANES), lambda h, i: (0, 5 * hp + h)),
                  pl.BlockSpec(rhs.shape, lambda h, i: (0, 0))] + r_in,
        out_specs=[pl.BlockSpec((tq, LANES), lambda h, i: (i, h))] + r_out,
        out_shape=[jax.ShapeDtypeStruct((s, d), F32)] + r_shapes,
        scratch_shapes=[pltpu.VMEM((HEADS_PER_BLOCK, tq, LANES), F32),
                        pltpu.VMEM((HEADS_PER_BLOCK, tq, LANES), F32)] + r_sems,
        input_output_aliases=r_alias,
        compiler_params=_params("arbitrary", "arbitrary"))(proj, proj, proj, rhs, *(rider.operands if rider else []))
    return outs[0], outs[1:]


def _attn_bwd(proj, o, do, rider=None):
    s = proj.shape[0]
    d = proj.shape[1] // N_IN
    hp = d // LANES
    tq = min(s, ATTN_TILE)
    nq = s // tq
    scale = HEAD_DIM ** -0.5

    def body(*refs):
        h_id, i = pl.program_id(0), pl.program_id(1)
        first = jnp.logical_and(h_id == 0, i == 0)
        last = jnp.logical_and(h_id == hp - 1, i == nq - 1)
        ((q_ref, k_ref, v_ref, o_ref, do_ref, rhs_ref), (dq_ref, dk_ref, dv_ref),
         (dq_acc, dk_acc, dv_acc, rk_ref, rg_ref), start, finish) = _ride(rider, refs, 6, 3, first, last)
        start()
        head0, strict = _attn_masks(tq)
        qm = _split_heads((q_ref[...].astype(F32) * scale).astype(BF16), head0)
        dom = _split_heads(do_ref[...], head0)
        prod = do_ref[...].astype(F32) * o_ref[...]
        delta = [jnp.sum(jnp.where(head0, prod, 0.0), axis=1, keepdims=True),
                 jnp.sum(jnp.where(head0, 0.0, prod), axis=1, keepdims=True)]

        @pl.when(i == 0)
        def _():
            dk_acc[...] = jnp.zeros_like(dk_acc)
            dv_acc[...] = jnp.zeros_like(dv_acc)

        dq_acc[...] = jnp.zeros_like(dq_acc)
        rk_ref[...] = jnp.zeros_like(rk_ref)
        for h in range(HEADS_PER_BLOCK):
            rg_ref[h] = jnp.broadcast_to(delta[h], (tq, LANES))

        def block(j, diag):
            off = pl.multiple_of(j * tq, tq)
            k2 = k_ref[pl.ds(off, tq), :]
            v2 = v_ref[pl.ds(off, tq), :]
            heads = range(HEADS_PER_BLOCK)
            z = [_dot_nt(qm[h], k2) for h in heads]
            dw = [_dot_nt(dom[h], v2) for h in heads]
            lsz = [_log_sigmoid(z[h]) for h in heads]
            keep = [lsz[h] - z[h] for h in heads]
            if diag:
                keep = [jnp.where(strict, keep[h], 0.0) for h in heads]
            ksum = [_suffix_sums(keep[h], rhs_ref, tq) for h in heads]
            w = [jnp.exp(lsz[h] + ksum[h][0] + _lanes_to_tile(rk_ref[h], tq)) for h in heads]
            if diag:
                w = [jnp.where(strict, w[h], 0.0) for h in heads]
            wb = [w[h].astype(BF16) for h in heads]
            g = [dw[h] * wb[h].astype(F32) for h in heads]
            gsum = [_suffix_sums(g[h], rhs_ref, tq) for h in heads]
            dz = [g[h] - jnp.exp(lsz[h]) * (_lanes_to_tile(rg_ref[h], tq) - gsum[h][0]) for h in heads]
            if diag:
                dz = [jnp.where(strict, dz[h], 0.0) for h in heads]
            dzb = [dz[h].astype(BF16) for h in heads]
            for h in heads:
                dq_acc[h] += _dot(dzb[h], k2)
                rk_ref[h] += ksum[h][1]
                rg_ref[h] -= gsum[h][1]
            dk_acc[pl.ds(off, tq), :] += _dot_tn(dzb[0], qm[0]) + _dot_tn(dzb[1], qm[1])
            dv_acc[pl.ds(off, tq), :] += _dot_tn(wb[0], dom[0]) + _dot_tn(wb[1], dom[1])

        block(i, True)
        _sweep_earlier_tiles(block, i, rk_ref)
        dq_ref[...] = (jnp.where(head0, dq_acc[0], dq_acc[1]) * scale).astype(BF16)

        @pl.when(i == nq - 1)
        def _():
            dk_ref[...] = dk_acc[...].astype(BF16)
            dv_ref[...] = dv_acc[...].astype(BF16)

        finish()

    blk = pl.BlockSpec((tq, LANES), lambda h, i: (i, h))
    seq = pl.BlockSpec((s, LANES), lambda h, i: (0, h))
    sh = jax.ShapeDtypeStruct((s, d), BF16)
    rhs = _suffix_rhs()
    r_in, r_out, r_shapes, r_sems, r_alias = _rider_call_args(rider, 6, 3)
    outs = pl.pallas_call(
        body, name="attn_bwd", grid=(hp, nq),
        in_specs=[pl.BlockSpec((tq, LANES), lambda h, i: (i, 3 * hp + h)),
                  pl.BlockSpec((s, LANES), lambda h, i: (0, 4 * hp + h)),
                  pl.BlockSpec((s, LANES), lambda h, i: (0, 5 * hp + h)),
                  blk, blk, pl.BlockSpec(rhs.shape, lambda h, i: (0, 0))] + r_in,
        out_specs=[blk, seq, seq] + r_out, out_shape=[sh, sh, sh] + r_shapes,
        scratch_shapes=[pltpu.VMEM((HEADS_PER_BLOCK, tq, LANES), F32),
                        pltpu.VMEM((s, LANES), F32), pltpu.VMEM((s, LANES), F32),
                        pltpu.VMEM((HEADS_PER_BLOCK, tq, LANES), F32),
                        pltpu.VMEM((HEADS_PER_BLOCK, tq, LANES), F32)] + r_sems,
        input_output_aliases=r_alias,
        compiler_params=_params("arbitrary", "arbitrary"))(proj, proj, proj, o, do, rhs,
                                                           *(rider.operands if rider else []))
    return outs[:3], outs[3:]


def _post_math(ya_ref, o_ref, gb_ref, ma_ref, mb_ref, x_ref, p_ref, wpa_ref, wpb_ref, wout_ref, wpe_ref, wpg_ref,
               alpha):
    gb = gb_ref[...].astype(F32)
    sgb = _sigmoid(gb)
    o = o_ref[...]
    yb = (o * (gb * sgb)).astype(BF16)
    pa = _dot(ya_ref[...], wpa_ref[...])
    pb = _dot(yb, wpb_ref[...])
    sa = _sigmoid(ma_ref[...].astype(F32))
    sb = _sigmoid(mb_ref[...].astype(F32))
    merged = (sa * pa + sb * pb).astype(BF16)
    h1 = alpha * x_ref[...] + _dot(merged, wout_ref[...])
    h1b = h1.astype(BF16)
    e = _dot(p_ref[...].astype(BF16), wpe_ref[...])
    sg = _sigmoid(_dot(h1b, wpg_ref[...]))
    h2 = h1 + e * sg
    return dict(gb=gb, sgb=sgb, o=o, yb=yb, pa=pa, pb=pb, sa=sa, sb=sb, merged=merged, h1b=h1b, e=e, sg=sg, h2=h2)


def _post_specs(tm, d, ple, weights):
    tok = pl.BlockSpec((tm, d), lambda i: (i, 0))
    col = lambda j: pl.BlockSpec((tm, d), lambda i: (i, j))
    full = lambda a: pl.BlockSpec(a.shape, lambda i: (0,) * a.ndim, pipeline_mode=pl.Buffered(1))
    return tok, [tok, tok, col(6), col(7), col(8), tok, pl.BlockSpec((tm, ple), lambda i: (i, 0))] + [
        full(w) for w in weights]


def _post_fwd(ya, o, proj, x, p, w_pa, w_pb, w_out, w_pe, w_pg, ln_g, ln_b, alpha, rider=None):
    s, d = x.shape
    ple = p.shape[1]
    tm = min(s, 256)
    nsteps = s // tm
    weights = (w_pa, w_pb, w_out, w_pe, w_pg, ln_g, ln_b)

    def body(*refs):
        i = pl.program_id(0)
        ins, (out_ref,), _, start, finish = _ride(rider, refs, 14, 1, i == 0, i == nsteps - 1)
        start()
        f = _post_math(*ins[:12], alpha)
        xhat, _ = _ln_stats(f["h2"])
        out_ref[...] = xhat * ins[12][...] + ins[13][...]
        finish()

    tok, in_specs = _post_specs(tm, d, ple, weights)
    r_in, r_out, r_shapes, r_sems, r_alias = _rider_call_args(rider, 14, 1)
    outs = pl.pallas_call(body, name="post_fwd", grid=(nsteps,), in_specs=in_specs + r_in, out_specs=[tok] + r_out,
                          out_shape=[jax.ShapeDtypeStruct((s, d), F32)] + r_shapes, scratch_shapes=r_sems,
                          input_output_aliases=r_alias,
                          compiler_params=_params("arbitrary"))(ya, o, proj, proj, proj, x, p, *weights,
                                                                *(rider.operands if rider else []))
    return outs[0], outs[1:]


def _post_bwd(dxo, ya, o, proj, x, p, w_pa, w_pb, w_out, w_pe, w_pg, ln_g, ln_b, alpha):
    s, d = x.shape
    ple = p.shape[1]
    tm = min(s, 256)
    nsteps = s // tm
    weights = (w_pa, w_pb, w_out, w_pe, w_pg, ln_g, ln_b)

    def body(dxo_ref, ya_ref, o_ref, gb_ref, ma_ref, mb_ref, x_ref, p_ref, wpa_ref, wpb_ref, wout_ref, wpe_ref,
             wpg_ref, g_ref, b_ref,
             dproj_ref, dxr_ref, dya_ref, do_ref, de_ref, h1_ref, dzg_ref, mrg_ref, dh1_ref, dpa_ref, yb_ref, dpb_ref,
             dg_ref, db_ref, dg_acc, db_acc):
        i = pl.program_id(0)

        @pl.when(i == 0)
        def _():
            dg_acc[...] = jnp.zeros_like(dg_acc)
            db_acc[...] = jnp.zeros_like(db_acc)

        f = _post_math(ya_ref, o_ref, gb_ref, ma_ref, mb_ref, x_ref, p_ref, wpa_ref, wpb_ref, wout_ref, wpe_ref,
                       wpg_ref, alpha)
        xhat, rstd = _ln_stats(f["h2"])
        dxo = dxo_ref[...]
        dg_acc[...] += _sum_rows8(dxo * xhat)
        db_acc[...] += _sum_rows8(dxo)
        dh2 = _ln_bwd(dxo * g_ref[...], xhat, rstd)
        sg, e = f["sg"], f["e"]
        de_ref[...] = (dh2 * sg).astype(BF16)
        dzg = (dh2 * e * sg * (1.0 - sg)).astype(BF16)
        dzg_ref[...] = dzg
        dh1 = dh2 + _dot_nt(dzg, wpg_ref[...])
        dh1b = dh1.astype(BF16)
        dxr_ref[...] = alpha * dh1
        dh1_ref[...] = dh1b
        h1_ref[...] = f["h1b"]
        mrg_ref[...] = f["merged"]
        yb_ref[...] = f["yb"]
        dmerged = _dot_nt(dh1b, wout_ref[...])
        sa, sb = f["sa"], f["sb"]
        dpa = (dmerged * sa).astype(BF16)
        dpb = (dmerged * sb).astype(BF16)
        dpa_ref[...] = dpa
        dpb_ref[...] = dpb
        dproj_ref[:, d:2 * d] = (dmerged * f["pa"] * sa * (1.0 - sa)).astype(BF16)
        dproj_ref[:, 2 * d:3 * d] = (dmerged * f["pb"] * sb * (1.0 - sb)).astype(BF16)
        dya_ref[...] = _dot_nt(dpa, wpa_ref[...]).astype(BF16)
        dyb = _dot_nt(dpb, wpb_ref[...])
        gb, sgb = f["gb"], f["sgb"]
        do_ref[...] = (dyb * (gb * sgb)).astype(BF16)
        dproj_ref[:, 0:d] = (dyb * f["o"] * (sgb * (1.0 + gb * (1.0 - sgb)))).astype(BF16)

        @pl.when(i == nsteps - 1)
        def _():
            dg_ref[...] = jnp.sum(dg_acc[...], axis=0, keepdims=True)
            db_ref[...] = jnp.sum(db_acc[...], axis=0, keepdims=True)

    tok, in_specs = _post_specs(tm, d, ple, weights)
    vec_spec = pl.BlockSpec((1, d), lambda i: (0, 0))
    vec = jax.ShapeDtypeStruct((1, d), F32)
    act = jax.ShapeDtypeStruct((s, d), BF16)
    return pl.pallas_call(
        body, name="post_bwd", grid=(nsteps,), in_specs=[tok] + in_specs,
        out_specs=[pl.BlockSpec((tm, 3 * d), lambda i: (i, 2)), tok] + [tok] * 10 + [vec_spec, vec_spec],
        out_shape=[jax.ShapeDtypeStruct((s, N_IN * d), BF16), jax.ShapeDtypeStruct((s, d), F32)] + [act] * 10 + [vec, vec],
        scratch_shapes=[pltpu.VMEM((SUBLANES, d), F32), pltpu.VMEM((SUBLANES, d), F32)],
        compiler_params=_params("arbitrary"))(dxo, ya, o, proj, proj, proj, x, p, *weights)


def _loss_head(y, target):
    s, d = y.shape
    tm = min(s, 512)

    def body(y_ref, t_ref, dy_ref, l_ref):
        @pl.when(pl.program_id(0) == 0)
        def _():
            l_ref[...] = jnp.zeros_like(l_ref)

        err = y_ref[...] - t_ref[...]
        dy_ref[...] = err / d
        row = jnp.sum(err * err, axis=1, keepdims=True) / d
        l_ref[...] += 0.5 * jnp.sum(row, axis=0, keepdims=True)

    tok = pl.BlockSpec((tm, d), lambda i: (i, 0))
    return pl.pallas_call(body, name="loss_head", grid=(s // tm,), in_specs=[tok, tok],
                          out_specs=[tok, pl.BlockSpec((SUBLANES, LANES), lambda i: (0, 0))],
                          out_shape=[jax.ShapeDtypeStruct((s, d), F32),
                                     jax.ShapeDtypeStruct((SUBLANES, LANES), F32)],
                          compiler_params=_params("arbitrary"))(y, target)


def _position():
    x, y, c = lax.axis_index("x"), lax.axis_index("y"), lax.axis_index("c")
    chips = [(1 - x, y), (x, 1 - y), (1 - x, 1 - y)]
    return x, y, c, chips


def _shard_of(ref, col_sharded, j, n):
    off = pl.multiple_of(j * n, n)
    return ref.at[:, pl.ds(off, n)] if col_sharded else ref.at[pl.ds(off, n), :]


def _half_of(ref, col_sharded, h, n):
    off = pl.multiple_of(h * n, n)
    return ref.at[pl.ds(off, n), :] if col_sharded else ref.at[:, pl.ds(off, n)]


def _piece_of(ref, col_sharded, chip, n_block, half, n_half):
    block = pl.ds(pl.multiple_of(chip * n_block, n_block), n_block)
    part = pl.ds(pl.multiple_of(half * n_half, n_half), n_half)
    return ref.at[part, block] if col_sharded else ref.at[block, part]


class _Rider(NamedTuple):
    operands: list
    out_shapes: list
    aliases: dict
    n_sems: int
    start: Callable
    finish: Callable


def _rider_call_args(rider, n_in, n_out):
    if rider is None:
        return [], [], [], [], {}
    sems = [pltpu.SemaphoreType.DMA((rider.n_sems,))] * 2
    aliases = {n_in + i: n_out + o for i, o in rider.aliases.items()}
    return [ANY] * len(rider.operands), [ANY] * len(rider.out_shapes), list(rider.out_shapes), sems, aliases


def _ride(rider, refs, n_in, n_out, first, last):
    if rider is None:
        return refs[:n_in], refs[n_in:n_in + n_out], refs[n_in + n_out:], lambda: None, lambda: None
    r_in, r_out = len(rider.operands), len(rider.out_shapes)
    ins, rins = refs[:n_in], refs[n_in:n_in + r_in]
    outs = refs[n_in + r_in:n_in + r_in + n_out]
    routs = refs[n_in + r_in + n_out:n_in + r_in + n_out + r_out]
    scratch = refs[n_in + r_in + n_out + r_out:-2]
    send_sems, recv_sems = refs[-2:]

    def start():
        pl.when(first)(lambda: rider.start(rins, routs, send_sems, recv_sems))

    def finish():
        pl.when(last)(lambda: rider.finish(rins, routs, send_sems, recv_sems))

    return ins, outs, scratch, start, finish


def _gather_copies(outs, ici_sems, d2d_sems):
    x, y, c, chips = _position()
    my_chip = 2 * x + y
    sends, arrivals, passes, passed = [], [], [], []
    for a, out in enumerate(outs):
        cs = COL_SHARDED[BIG[a % len(BIG)]]
        rows, cols = out.shape
        n_block = (cols if cs else rows) // N_CHIPS
        n_half = (rows if cs else cols) // 2
        piece = lambda chip, half: _piece_of(out, cs, chip, n_block, half, n_half)
        for j, chip in enumerate(chips):
            k = a * 3 + j
            their = 2 * chip[0] + chip[1]
            if ici_sems is not None:
                send_sems, recv_sems = ici_sems
                sends.append(pltpu.make_async_remote_copy(
                    src_ref=piece(my_chip, c), dst_ref=piece(my_chip, c), send_sem=send_sems.at[k],
                    recv_sem=recv_sems.at[k], device_id=(chip[0], chip[1], c), device_id_type=MESH))
                arrivals.append(pltpu.make_async_remote_copy(
                    src_ref=piece(their, c), dst_ref=piece(their, c), send_sem=send_sems.at[k],
                    recv_sem=recv_sems.at[k], device_id=(chip[0], chip[1], c), device_id_type=MESH))
            if d2d_sems is not None:
                send_sems, recv_sems = d2d_sems
                passes.append(pltpu.make_async_remote_copy(
                    src_ref=piece(their, c), dst_ref=piece(their, c), send_sem=send_sems.at[k],
                    recv_sem=recv_sems.at[k], device_id=(x, y, 1 - c), device_id_type=MESH))
                passed.append(pltpu.make_async_remote_copy(
                    src_ref=piece(their, 1 - c), dst_ref=piece(their, 1 - c), send_sem=send_sems.at[k],
                    recv_sem=recv_sems.at[k], device_id=(x, y, 1 - c), device_id_type=MESH))
    return sends, arrivals, passes, passed


def _gather_weights(bufs):
    flat = [bufs[name] for name in BIG]
    n_arr = len(flat)

    def body(*refs):
        outs = refs[n_arr:2 * n_arr]
        sems = refs[2 * n_arr:]
        sends, arrivals, passes, passed = _gather_copies(outs, sems[:2], sems[2:])
        for cp in sends:
            cp.start()
        for arrival, onward in zip(arrivals, passes):
            arrival.wait_recv()
            onward.start()
        for cp in passed:
            cp.wait_recv()
        for cp in sends + passes:
            cp.wait_send()

    outs = pl.pallas_call(
        body, name="gather_weights", in_specs=[ANY] * n_arr, out_specs=[ANY] * n_arr,
        out_shape=[jax.ShapeDtypeStruct(a.shape, BF16) for a in flat],
        input_output_aliases={a: a for a in range(n_arr)},
        scratch_shapes=[pltpu.SemaphoreType.DMA((n_arr * 3,))] * 4,
    )(*flat)
    return dict(zip(BIG, outs))


def _gather_rider(bufs, over_ici):
    flat = [bufs[name] for name in BIG]

    def copies(routs, send_sems, recv_sems):
        sems = (send_sems, recv_sems)
        sends, arrivals, passes, passed = _gather_copies(routs, sems if over_ici else None, None if over_ici else sems)
        return (sends, arrivals) if over_ici else (passes, passed)

    def start(rins, routs, send_sems, recv_sems):
        for cp in copies(routs, send_sems, recv_sems)[0]:
            cp.start()

    def finish(rins, routs, send_sems, recv_sems):
        out, due = copies(routs, send_sems, recv_sems)
        for cp in due:
            cp.wait_recv()
        for cp in out:
            cp.wait_send()

    return _Rider(flat, [jax.ShapeDtypeStruct(a.shape, a.dtype) for a in flat], {i: i for i in range(len(flat))},
                  len(flat) * 3, start, finish)


def _half_shape(shape, col_sharded):
    r, c = shape
    return (r // 2, c) if col_sharded else (r, c // 2)


def _exchange_halves(grads):
    nl, n_w = len(grads), len(BIG)
    flat = [grads[l][name] for l in range(nl) for name in BIG]
    n_arr = len(flat)

    def body(*refs):
        ins, outs = refs[:n_arr], refs[n_arr:2 * n_arr]
        send_sems, recv_sems = refs[2 * n_arr:]
        x, y, c, _ = _position()
        copies = []
        for a in range(n_arr):
            cs = COL_SHARDED[BIG[a % n_w]]
            n = outs[a].shape[0] if cs else outs[a].shape[1]
            copies.append(pltpu.make_async_remote_copy(
                src_ref=_half_of(ins[a], cs, 1 - c, n), dst_ref=outs[a], send_sem=send_sems.at[a],
                recv_sem=recv_sems.at[a], device_id=(x, y, 1 - c), device_id_type=MESH))
        for cp in copies:
            cp.start()
        for cp in copies:
            cp.wait_recv()
        for cp in copies:
            cp.wait_send()

    outs = pl.pallas_call(
        body, name="exchange_halves", in_specs=[ANY] * n_arr, out_specs=[ANY] * n_arr,
        out_shape=[jax.ShapeDtypeStruct(_half_shape(a.shape, COL_SHARDED[BIG[i % n_w]]), F32)
                   for i, a in enumerate(flat)],
        scratch_shapes=[pltpu.SemaphoreType.DMA((n_arr,)), pltpu.SemaphoreType.DMA((n_arr,))],
    )(*flat)
    return [{name: outs[l * n_w + w] for w, name in enumerate(BIG)} for l in range(nl)]


def _scatter_to_owners(halves):
    rider = _scatter_rider(halves)

    def body(*refs):
        n = len(rider.operands)
        args = (refs[:n], refs[n:2 * n], *refs[2 * n:])
        rider.start(*args)
        rider.finish(*args)

    outs = pl.pallas_call(
        body, name="scatter_to_owners", in_specs=[ANY] * len(rider.operands), out_specs=[ANY] * len(rider.out_shapes),
        out_shape=rider.out_shapes, scratch_shapes=[pltpu.SemaphoreType.DMA((rider.n_sems,))] * 2,
    )(*rider.operands)
    return dict(zip(BIG, outs))


def _scatter_rider(halves):
    flat = [halves[name] for name in BIG]

    def slots_shape(a, cs):
        r, c = a.shape
        return (N_CHIPS - 1,) + ((r, c // N_CHIPS) if cs else (r // N_CHIPS, c))

    def copies(rins, routs, send_sems, recv_sems):
        x, y, c, chips = _position()
        out = []
        for a, name in enumerate(BIG):
            cs = COL_SHARDED[name]
            n = routs[a].shape[2] if cs else routs[a].shape[1]
            for j, chip in enumerate(chips):
                k = a * 3 + j
                out.append(pltpu.make_async_remote_copy(
                    src_ref=_shard_of(rins[a], cs, 2 * chip[0] + chip[1], n), dst_ref=routs[a].at[j],
                    send_sem=send_sems.at[k], recv_sem=recv_sems.at[k], device_id=(chip[0], chip[1], c),
                    device_id_type=MESH))
        return out

    def start(rins, routs, send_sems, recv_sems):
        for cp in copies(rins, routs, send_sems, recv_sems):
            cp.start()

    def finish(rins, routs, send_sems, recv_sems):
        cps = copies(rins, routs, send_sems, recv_sems)
        for cp in cps:
            cp.wait_recv()
        for cp in cps:
            cp.wait_send()

    return _Rider(flat, [jax.ShapeDtypeStruct(slots_shape(a, COL_SHARDED[name]), a.dtype) for a, name in zip(flat, BIG)],
                  {}, len(flat) * 3, start, finish)


def _join_halves(halves):
    flat = [halves[name] for name in BIG]
    n_w = len(flat)

    def body(*refs):
        outs = refs[n_w:2 * n_w]
        send_sems, recv_sems = refs[2 * n_w:]
        x, y, c, _ = _position()
        sends, recvs = [], []
        for w, name in enumerate(BIG):
            cs = COL_SHARDED[name]
            n = (outs[w].shape[1] if cs else outs[w].shape[2]) // 2

            def half(h):
                part = pl.ds(pl.multiple_of(h * n, n), n)
                return outs[w].at[:, part, :] if cs else outs[w].at[:, :, part]

            sends.append(pltpu.make_async_remote_copy(
                src_ref=half(c), dst_ref=half(c), send_sem=send_sems.at[w], recv_sem=recv_sems.at[w],
                device_id=(x, y, 1 - c), device_id_type=MESH))
            recvs.append(pltpu.make_async_remote_copy(
                src_ref=half(1 - c), dst_ref=half(1 - c), send_sem=send_sems.at[w], recv_sem=recv_sems.at[w],
                device_id=(x, y, 1 - c), device_id_type=MESH))
        for cp in sends:
            cp.start()
        for cp in recvs:
            cp.wait_recv()
        for cp in sends:
            cp.wait_send()

    outs = pl.pallas_call(
        body, name="join_halves", in_specs=[ANY] * n_w, out_specs=[ANY] * n_w,
        out_shape=[jax.ShapeDtypeStruct(a.shape, F32) for a in flat],
        input_output_aliases={w: w for w in range(n_w)},
        scratch_shapes=[pltpu.SemaphoreType.DMA((n_w,)), pltpu.SemaphoreType.DMA((n_w,))],
    )(*flat)
    return dict(zip(BIG, outs))


def _gather_small(packed):
    r, lanes = packed.shape

    def body(in_ref, out_ref, send_sems, recv_sems, loc_sem):
        x, y, c, _ = _position()
        me = 4 * x + 2 * y + c
        local = pltpu.make_async_copy(in_ref, out_ref.at[me], loc_sem)
        local.start()
        sends, recvs = [], []
        for k in range(1, N_DEV):
            px, py, pc = x ^ (k >> 2), y ^ ((k >> 1) & 1), c ^ (k & 1)
            sends.append(pltpu.make_async_remote_copy(
                src_ref=in_ref, dst_ref=out_ref.at[me], send_sem=send_sems.at[k - 1], recv_sem=recv_sems.at[k - 1],
                device_id=(px, py, pc), device_id_type=MESH))
            recvs.append(pltpu.make_async_remote_copy(
                src_ref=in_ref, dst_ref=out_ref.at[4 * px + 2 * py + pc], send_sem=send_sems.at[k - 1],
                recv_sem=recv_sems.at[k - 1], device_id=(px, py, pc), device_id_type=MESH))
        for cp in sends:
            cp.start()
        for cp in recvs:
            cp.wait_recv()
        for cp in sends:
            cp.wait_send()
        local.wait()

    return pl.pallas_call(
        body, name="gather_small", in_specs=[ANY], out_specs=ANY,
        out_shape=jax.ShapeDtypeStruct((N_DEV, r, lanes), F32),
        scratch_shapes=[pltpu.SemaphoreType.DMA((N_DEV - 1,)), pltpu.SemaphoreType.DMA((N_DEV - 1,)),
                        pltpu.SemaphoreType.DMA],
    )(packed)


def _pack_small(t):
    return jnp.concatenate([t[name].reshape(-1, LANES) for name in SMALL], axis=0)


def _unpack_small(packed, like):
    out, row = {}, 0
    for name in SMALL:
        n = like[name].size // LANES
        out[name] = packed[row:row + n].reshape(like[name].shape)
        row += n
    return out


def kernel(x, p, w_in, vn_g, vn_b, w_s, b_s, w_pa, w_pb, w_out, w_pe, w_pg, ln_g, ln_b, loss_target, m_w_in, m_vn_g, m_vn_b, m_w_s, m_b_s, m_w_pa, m_w_pb, m_w_out, m_w_pe, m_w_pg, m_ln_g, m_ln_b, v_w_in, v_vn_g, v_vn_b, v_w_s, v_b_s, v_w_pa, v_w_pb, v_w_out, v_w_pe, v_w_pg, v_ln_g, v_ln_b):
    weights = dict(w_in=w_in, vn_g=vn_g, vn_b=vn_b, w_s=w_s, b_s=b_s, w_pa=w_pa, w_pb=w_pb, w_out=w_out, w_pe=w_pe,
                   w_pg=w_pg, ln_g=ln_g, ln_b=ln_b)
    mom1 = dict(w_in=m_w_in, vn_g=m_vn_g, vn_b=m_vn_b, w_s=m_w_s, b_s=m_b_s, w_pa=m_w_pa, w_pb=m_w_pb, w_out=m_w_out,
                w_pe=m_w_pe, w_pg=m_w_pg, ln_g=m_ln_g, ln_b=m_ln_b)
    mom2 = dict(w_in=v_w_in, vn_g=v_vn_g, vn_b=v_vn_b, w_s=v_w_s, b_s=v_b_s, w_pa=v_w_pa, w_pb=v_w_pb, w_out=v_w_out,
                w_pe=v_w_pe, w_pg=v_w_pg, ln_g=v_ln_g, ln_b=v_ln_b)
    nl, d = vn_g.shape
    chunk = w_s.shape[2]
    assert chunk == LANES and w_s.shape[3] == LANES and d % LANES == 0
    alpha = (2 * nl) ** 0.25
    pos = tuple(lax.axis_index(a).astype(jnp.int32).reshape(1) for a in ("c", "x", "y"))

    placed = [{name: _cast_into_place(weights[name], l, pos, COL_SHARDED[name]) for name in BIG} for l in range(nl)]
    full = [_gather_weights(placed[0])] + [None] * (nl - 1)
    causal = jnp.tril(jnp.ones((chunk, chunk), dtype=bool))
    ws_m = jnp.where(causal, w_s, 0.0).astype(BF16)
    ws_mt = jnp.swapaxes(ws_m, 2, 3)
    bs_t = jnp.swapaxes(b_s, 1, 2)

    xs, projs, yas, os_ = [x[0]], [], [], []
    for l in range(nl):
        proj = _proj_fwd(xs[l], full[l]["w_in"])
        ya = _gmlp_fwd(proj, vn_g[l:l + 1], vn_b[l:l + 1], ws_m[l], bs_t[l])
        more = l + 1 < nl
        o, arrived = _attn_fwd(proj, _gather_rider(placed[l + 1], over_ici=True) if more else None)
        x_next, handed = _post_fwd(ya, o, proj, xs[l], p[l, 0], full[l]["w_pa"], full[l]["w_pb"], full[l]["w_out"],
                                   full[l]["w_pe"], full[l]["w_pg"], ln_g[l:l + 1], ln_b[l:l + 1], alpha,
                                   _gather_rider(dict(zip(BIG, arrived)), over_ici=False) if more else None)
        if more:
            full[l + 1] = dict(zip(BIG, handed))
        xs.append(x_next)
        projs.append(proj)
        yas.append(ya)
        os_.append(o)

    dx, loss_tile = _loss_head(xs[nl], loss_target[0])
    loss = lax.psum(loss_tile[0, 0], ("x", "y", "c"))

    big_grads, received, slots = [None] * nl, [None] * nl, [None] * nl
    chip_sums = None
    small_grads = {name: [None] * nl for name in SMALL}
    for l in reversed(range(nl)):
        w = full[l]
        (dproj, dxr, dya, do, de, h1b, dzg, merged, dh1, dpa, yb, dpb, dln_g, dln_b) = _post_bwd(
            dx, yas[l], os_[l], projs[l], xs[l], p[l, 0], w["w_pa"], w["w_pb"], w["w_out"], w["w_pe"], w["w_pg"],
            ln_g[l:l + 1], ln_b[l:l + 1], alpha)
        (dq, dk, dv), scattered = _attn_bwd(projs[l], os_[l], do,
                                            _scatter_rider(chip_sums) if chip_sums is not None else None)
        if chip_sums is not None:
            slots[l + 1] = dict(zip(BIG, scattered))
        dproj, dvn_g, dvn_b, dw_s, dbs_cols = _gmlp_bwd(dproj, projs[l], dya, dq, dk, dv, vn_g[l:l + 1],
                                                         vn_b[l:l + 1], ws_m[l], ws_mt[l], bs_t[l])
        big_grads[l] = dict(w_in=_matmul_tn(xs[l], dproj), w_pa=_matmul_tn(yas[l], dpa), w_pb=_matmul_tn(yb, dpb),
                            w_out=_matmul_tn(merged, dh1), w_pe=_matmul_tn(p[l, 0], de), w_pg=_matmul_tn(h1b, dzg))
        dx = _dx_matmul(dxr, dproj, w["w_in"])
        small_grads["vn_g"][l], small_grads["vn_b"][l] = dvn_g[0], dvn_b[0]
        small_grads["ln_g"][l], small_grads["ln_b"][l] = dln_g[0], dln_b[0]
        small_grads["w_s"][l] = dw_s
        small_grads["b_s"][l] = dbs_cols[:, :b_s.shape[1]].T
        received[l] = _exchange_halves([big_grads[l]])[0]
        chip_sums = {name: _add_own_half(big_grads[l][name], received[l][name], pos, COL_SHARDED[name])
                     for name in BIG}
    slots[0] = _scatter_to_owners(chip_sums)
    reduced = {}
    for name in BIG:
        buf = None
        for l in range(nl):
            buf = _reduce_block(buf, big_grads[l][name], received[l][name], slots[l][name], l, nl, pos,
                                COL_SHARDED[name])
        reduced[name] = buf
    grads = _join_halves(reduced)

    small_like = {name: weights[name] for name in SMALL}
    packed = _pack_small({name: jnp.stack(small_grads[name]) for name in SMALL})
    grads.update(_unpack_small(_sum_slots(_gather_small(packed)), small_like))

    delta, new_m, new_v = {}, {}, {}
    for name in BIG:
        sh = weights[name].shape
        flat = lambda a: a.reshape(sh[0] * sh[1], sh[2])
        dl, nm, nv = _adamw(flat(weights[name]), flat(grads[name]), flat(mom1[name]), flat(mom2[name]))
        delta[name], new_m[name], new_v[name] = dl.reshape(sh), nm.reshape(sh), nv.reshape(sh)
    dl, nm, nv = _adamw(_pack_small(small_like), _pack_small({n: grads[n] for n in SMALL}),
                        _pack_small({n: mom1[n] for n in SMALL}), _pack_small({n: mom2[n] for n in SMALL}))
    delta.update(_unpack_small(dl, small_like))
    new_m.update(_unpack_small(nm, small_like))
    new_v.update(_unpack_small(nv, small_like))

    return (loss, dx[None], *[grads[n] for n in WEIGHTS], *[delta[n] for n in WEIGHTS],
            *[new_m[n] for n in WEIGHTS], *[new_v[n] for n in WEIGHTS])
```

```python
import math
from typing import Callable, NamedTuple

import jax
import jax.numpy as jnp
from jax import lax
from jax.experimental import pallas as pl
from jax.experimental.pallas import tpu as pltpu

F32 = jnp.float32
BF16 = jnp.bfloat16
LANES = 128
SUBLANES = 8
HEAD_DIM = 64
HEADS_PER_BLOCK = LANES // HEAD_DIM
LN_EPS = 1e-5
N_IN = 9
N_CHIPS = 4
N_DEV = 8
ADAM_LR = 0.001
ADAM_B1 = 0.9
ADAM_B2 = 0.999
ADAM_EPS = 1e-08
ADAM_WD = 0.01
ADAM_STEP = 10
MESH = pl.DeviceIdType.MESH
ANY = pl.BlockSpec(memory_space=pl.ANY)
BIG = ("w_in", "w_pa", "w_pb", "w_out", "w_pe", "w_pg")
COL_SHARDED = {"w_in": True, "w_pa": False, "w_pb": False, "w_out": False, "w_pe": True, "w_pg": False}
SMALL = ("vn_g", "vn_b", "w_s", "b_s", "ln_g", "ln_b")
WEIGHTS = ("w_in", "vn_g", "vn_b", "w_s", "b_s", "w_pa", "w_pb", "w_out", "w_pe", "w_pg", "ln_g", "ln_b")


def _params(*sem):
    return pltpu.CompilerParams(dimension_semantics=sem)


def _dot(a, b):
    return jnp.dot(a, b, preferred_element_type=F32)


def _dot_nt(a, b):
    return lax.dot_general(a, b, (((1,), (1,)), ((), ())), preferred_element_type=F32)


def _dot_tn(a, b):
    return lax.dot_general(a, b, (((0,), (0,)), ((), ())), preferred_element_type=F32)


def _sigmoid(a):
    return 1.0 / (1.0 + jnp.exp(-a))


def _row_tile(rows, cols, cap_bytes):
    best = None
    for t in range(16, rows + 1, 16):
        if rows % t == 0 and t * cols * 4 <= cap_bytes:
            best = t
    return best or rows


def _col_tile(cols, cap):
    best = LANES
    for t in range(LANES, min(cols, cap) + 1, LANES):
        if cols % t == 0:
            best = t
    return best


def _ln_stats(h):
    mu = jnp.mean(h, axis=-1, keepdims=True)
    hc = h - mu
    var = jnp.mean(hc * hc, axis=-1, keepdims=True)
    rstd = lax.rsqrt(var + LN_EPS)
    return hc * rstd, rstd


def _ln_bwd(dxhat, xhat, rstd):
    m1 = jnp.mean(dxhat, axis=-1, keepdims=True)
    m2 = jnp.mean(dxhat * xhat, axis=-1, keepdims=True)
    return rstd * (dxhat - m1 - xhat * m2)


def _sum_rows8(a):
    t, d = a.shape
    return jnp.sum(a.reshape(t // SUBLANES, SUBLANES, d), axis=0)


def _chip(x_ref, y_ref):
    return 2 * x_ref[0] + y_ref[0]


def _cast_into_place(shards, l, pos, col_sharded):
    _, r, c = shards.shape
    tr = _row_tile(r, c, 2 << 20)
    nb = r // tr

    def body(c_ref, x_ref, y_ref, a_ref, o_ref):
        o_ref[...] = a_ref[...].astype(BF16)

    if col_sharded:
        out_spec = pl.BlockSpec((tr, c), lambda i, c_ref, x_ref, y_ref: (i, _chip(x_ref, y_ref)))
        full = (r, c * N_CHIPS)
    else:
        out_spec = pl.BlockSpec((tr, c), lambda i, c_ref, x_ref, y_ref: (_chip(x_ref, y_ref) * nb + i, 0))
        full = (r * N_CHIPS, c)
    grid_spec = pltpu.PrefetchScalarGridSpec(
        num_scalar_prefetch=3, grid=(nb,),
        in_specs=[pl.BlockSpec((None, tr, c), lambda i, c_ref, x_ref, y_ref: (l, i, 0))], out_specs=out_spec)
    return pl.pallas_call(body, name="cast_into_place", grid_spec=grid_spec,
                          out_shape=jax.ShapeDtypeStruct(full, BF16),
                          compiler_params=_params("parallel"))(*pos, shards)


def _add_own_half(own, recv, pos, col_sharded):
    r, c = recv.shape
    tr = _row_tile(r, c, 2 << 20)
    nb = r // tr

    def body(c_ref, x_ref, y_ref, own_ref, recv_ref, o_ref):
        o_ref[...] = (own_ref[...] + recv_ref[...]).astype(BF16)

    if col_sharded:
        own_spec = pl.BlockSpec((tr, c), lambda i, c_ref, x_ref, y_ref: (c_ref[0] * nb + i, 0))
    else:
        own_spec = pl.BlockSpec((tr, c), lambda i, c_ref, x_ref, y_ref: (i, c_ref[0]))
    spec = pl.BlockSpec((tr, c), lambda i, c_ref, x_ref, y_ref: (i, 0))
    grid_spec = pltpu.PrefetchScalarGridSpec(num_scalar_prefetch=3, grid=(nb,), in_specs=[own_spec, spec],
                                             out_specs=spec)
    return pl.pallas_call(body, name="add_own_half", grid_spec=grid_spec,
                          out_shape=jax.ShapeDtypeStruct(recv.shape, BF16),
                          compiler_params=_params("parallel"))(*pos, own, recv)


def _reduce_block(buf, own, recv, slots, l, nl, pos, col_sharded):
    _, r, c = slots.shape
    tr = _row_tile(r, c, 1 << 20)
    nb = r // tr

    def body(c_ref, x_ref, y_ref, own_ref, recv_ref, slots_ref, *rest):
        acc = own_ref[...] + recv_ref[...]
        for j in range(N_CHIPS - 1):
            acc = acc + slots_ref[j].astype(F32)
        rest[-1][...] = acc

    if col_sharded:
        own_spec = pl.BlockSpec((tr, c), lambda i, c_ref, x_ref, y_ref: (c_ref[0] * nb + i, _chip(x_ref, y_ref)))
        recv_spec = pl.BlockSpec((tr, c), lambda i, c_ref, x_ref, y_ref: (i, _chip(x_ref, y_ref)))
        out_spec = pl.BlockSpec((None, tr, c), lambda i, c_ref, x_ref, y_ref: (l, c_ref[0] * nb + i, 0))
        out_shape = (nl, 2 * r, c)
    else:
        own_spec = pl.BlockSpec((tr, c), lambda i, c_ref, x_ref, y_ref: (_chip(x_ref, y_ref) * nb + i, c_ref[0]))
        recv_spec = pl.BlockSpec((tr, c), lambda i, c_ref, x_ref, y_ref: (_chip(x_ref, y_ref) * nb + i, 0))
        out_spec = pl.BlockSpec((None, tr, c), lambda i, c_ref, x_ref, y_ref: (l, i, c_ref[0]))
        out_shape = (nl, r, 2 * c)
    in_specs = [own_spec, recv_spec,
                pl.BlockSpec((N_CHIPS - 1, tr, c), lambda i, c_ref, x_ref, y_ref: (0, i, 0))]
    args = [*pos, own, recv, slots]
    aliases = {}
    if buf is not None:
        in_specs.append(ANY)
        args.append(buf)
        aliases = {len(args) - 1: 0}
    grid_spec = pltpu.PrefetchScalarGridSpec(num_scalar_prefetch=3, grid=(nb,), in_specs=in_specs,
                                             out_specs=out_spec)
    return pl.pallas_call(body, name="reduce_block", grid_spec=grid_spec,
                          out_shape=jax.ShapeDtypeStruct(out_shape, F32), input_output_aliases=aliases,
                          compiler_params=_params("parallel"))(*args)


def _sum_slots(a):
    n, r, c = a.shape
    tr = _row_tile(r, c * n, 4 << 20)

    def body(a_ref, o_ref):
        acc = a_ref[0]
        for s in range(1, n):
            acc = acc + a_ref[s]
        o_ref[...] = acc

    return pl.pallas_call(body, name="sum_slots", grid=(r // tr,),
                          in_specs=[pl.BlockSpec((n, tr, c), lambda i: (0, i, 0))],
                          out_specs=pl.BlockSpec((tr, c), lambda i: (i, 0)),
                          out_shape=jax.ShapeDtypeStruct((r, c), F32),
                          compiler_params=_params("parallel"))(a)


def _adamw(w, g, m, v):
    r, c = w.shape
    tr = _row_tile(r, c, 1 << 20)

    def body(w_ref, g_ref, m_ref, v_ref, d_ref, nm_ref, nv_ref):
        gg = g_ref[...]
        nm = ADAM_B1 * m_ref[...] + (1.0 - ADAM_B1) * gg
        nv = ADAM_B2 * v_ref[...] + (1.0 - ADAM_B2) * (gg * gg)
        m_hat = nm / (1.0 - ADAM_B1 ** ADAM_STEP)
        v_hat = nv / (1.0 - ADAM_B2 ** ADAM_STEP)
        d_ref[...] = -ADAM_LR * (m_hat / (jnp.sqrt(v_hat) + ADAM_EPS) + ADAM_WD * w_ref[...])
        nm_ref[...] = nm
        nv_ref[...] = nv

    spec = pl.BlockSpec((tr, c), lambda i: (i, 0))
    sh = jax.ShapeDtypeStruct((r, c), F32)
    return pl.pallas_call(body, name="adamw", grid=(r // tr,), in_specs=[spec] * 4, out_specs=[spec] * 3,
                          out_shape=[sh, sh, sh], compiler_params=_params("parallel"))(w, g, m, v)


def _proj_fwd(x, w):
    s, d = x.shape
    n = w.shape[1]
    tm, tn = min(s, 1024), _col_tile(n, 1024)

    def body(x_ref, w_ref, o_ref):
        o_ref[...] = _dot(x_ref[...].astype(BF16), w_ref[...]).astype(BF16)

    return pl.pallas_call(body, name="proj_fwd", grid=(s // tm, n // tn),
                          in_specs=[pl.BlockSpec((tm, d), lambda i, j: (i, 0)),
                                    pl.BlockSpec((d, tn), lambda i, j: (0, j))],
                          out_specs=pl.BlockSpec((tm, tn), lambda i, j: (i, j)),
                          out_shape=jax.ShapeDtypeStruct((s, n), BF16),
                          compiler_params=_params("parallel", "parallel"))(x, w)


def _matmul_tn(a, b):
    s, m = a.shape
    n = b.shape[1]
    tk, tn = min(s, 1024), _col_tile(n, 1024)
    nk = s // tk

    def body(a_ref, b_ref, o_ref):
        @pl.when(pl.program_id(1) == 0)
        def _():
            o_ref[...] = jnp.zeros_like(o_ref)

        o_ref[...] += _dot_tn(a_ref[...].astype(BF16), b_ref[...].astype(BF16))

    return pl.pallas_call(body, name="matmul_tn", grid=(n // tn, nk),
                          in_specs=[pl.BlockSpec((tk, m), lambda j, k: (k, 0)),
                                    pl.BlockSpec((tk, tn), lambda j, k: (k, j))],
                          out_specs=pl.BlockSpec((m, tn), lambda j, k: (0, j)),
                          out_shape=jax.ShapeDtypeStruct((m, n), F32),
                          compiler_params=_params("parallel", "arbitrary"))(a, b)


def _dx_matmul(dxr, dproj, w, rider=None):
    s, d = dxr.shape
    n = w.shape[1]
    tm, tk = min(s, 1024), _col_tile(n, 1024)
    nm, nk = s // tm, n // tk

    def body(*refs):
        i, k = pl.program_id(0), pl.program_id(1)
        first = jnp.logical_and(i == 0, k == 0)
        last = jnp.logical_and(i == nm - 1, k == nk - 1)
        (r_ref, g_ref, w_ref), (o_ref,), _, start, finish = _ride(rider, refs, 3, 1, first, last)
        start()

        @pl.when(k == 0)
        def _():
            o_ref[...] = r_ref[...]

        o_ref[...] += _dot_nt(g_ref[...], w_ref[...])
        finish()

    r_in, r_out, r_shapes, r_sems, r_alias = _rider_call_args(rider, 3, 1)
    outs = pl.pallas_call(body, name="dx_matmul", grid=(nm, nk),
                          in_specs=[pl.BlockSpec((tm, d), lambda i, k: (i, 0)),
                                    pl.BlockSpec((tm, tk), lambda i, k: (i, k)),
                                    pl.BlockSpec((d, tk), lambda i, k: (0, k))] + r_in,
                          out_specs=[pl.BlockSpec((tm, d), lambda i, k: (i, 0))] + r_out,
                          out_shape=[jax.ShapeDtypeStruct((s, d), F32)] + r_shapes, scratch_shapes=r_sems,
                          input_output_aliases=r_alias,
                          compiler_params=_params("arbitrary", "arbitrary"))(dxr, dproj, w,
                                                                             *(rider.operands if rider else []))
    return outs[0], outs[1:]


def _mix_chunks(ws_ref, src_ref, dst_ref, bias_ref, t, groups, chunk):
    for c in range(t // chunk):
        rows = slice(c * chunk, (c + 1) * chunk)
        for g in range(groups):
            cols = slice(g * LANES, (g + 1) * LANES)
            val = _dot(ws_ref[g], src_ref[rows, cols])
            if bias_ref is not None:
                val = val + bias_ref[:, g:g + 1]
            dst_ref[rows, cols] = val


def _gmlp_fwd(proj, vn_g, vn_b, ws_m, bs_t):
    s = proj.shape[0]
    d = proj.shape[1] // N_IN
    groups, chunk = ws_m.shape[0], ws_m.shape[1]
    t = min(s, 512)

    def body(u_ref, v_ref, ga_ref, g_ref, b_ref, ws_ref, bs_ref, o_ref, vn_ref, mix_ref):
        xhat, _ = _ln_stats(v_ref[...].astype(F32))
        vn_ref[...] = (xhat * g_ref[...] + b_ref[...]).astype(BF16)
        _mix_chunks(ws_ref, vn_ref, mix_ref, bs_ref, t, groups, chunk)
        ga = ga_ref[...].astype(F32)
        o_ref[...] = (u_ref[...].astype(F32) * mix_ref[...] * (ga * _sigmoid(ga))).astype(BF16)

    col = lambda j: pl.BlockSpec((t, d), lambda i: (i, j))
    full = lambda a: pl.BlockSpec(a.shape, lambda i: (0,) * a.ndim)
    return pl.pallas_call(body, name="gmlp_fwd", grid=(s // t,),
                          in_specs=[col(0), col(1), col(2), full(vn_g), full(vn_b), full(ws_m), full(bs_t)],
                          out_specs=pl.BlockSpec((t, d), lambda i: (i, 0)),
                          out_shape=jax.ShapeDtypeStruct((s, d), BF16),
                          scratch_shapes=[pltpu.VMEM((t, d), BF16), pltpu.VMEM((t, d), F32)],
                          compiler_params=_params("parallel"))(proj, proj, proj, vn_g, vn_b, ws_m, bs_t)


def _gmlp_bwd(dproj, proj, dya, dq, dk, dv, vn_g, vn_b, ws_m, ws_mt, bs_t):
    s = proj.shape[0]
    d = proj.shape[1] // N_IN
    groups, chunk = ws_m.shape[0], ws_m.shape[1]
    t = min(s, 256)
    nsteps = s // t

    def body(dproj_hbm, u_ref, v_ref, ga_ref, dya_ref, dq_ref, dk_ref, dv_ref, g_ref, b_ref, ws_ref, wst_ref, bs_ref,
             o_ref, dg_ref, db_ref, dws_ref, dbs_ref,
             vn_ref, mix_ref, dm_ref, dvn_ref, dbs_acc, dg_acc, db_acc):
        del dproj_hbm
        i = pl.program_id(0)

        @pl.when(i == 0)
        def _():
            dws_ref[...] = jnp.zeros_like(dws_ref)
            dbs_acc[...] = jnp.zeros_like(dbs_acc)
            dg_acc[...] = jnp.zeros_like(dg_acc)
            db_acc[...] = jnp.zeros_like(db_acc)

        xhat, rstd = _ln_stats(v_ref[...].astype(F32))
        vn_ref[...] = (xhat * g_ref[...] + b_ref[...]).astype(BF16)
        _mix_chunks(ws_ref, vn_ref, mix_ref, bs_ref, t, groups, chunk)
        ga = ga_ref[...].astype(F32)
        sg = _sigmoid(ga)
        silu = ga * sg
        dsilu = sg * (1.0 + ga * (1.0 - sg))
        u = u_ref[...].astype(F32)
        dya_f = dya_ref[...].astype(F32)
        mix = mix_ref[...]
        o_ref[:, 0:d] = (dya_f * mix * silu).astype(BF16)
        o_ref[:, 2 * d:3 * d] = (dya_f * u * mix * dsilu).astype(BF16)
        dmix = dya_f * u * silu
        dm_ref[...] = dmix.astype(BF16)
        for c in range(t // chunk):
            dbs_acc[...] += dmix[c * chunk:(c + 1) * chunk, :]
        _mix_chunks(wst_ref, dm_ref, dvn_ref, None, t, groups, chunk)
        for c in range(t // chunk):
            rows = slice(c * chunk, (c + 1) * chunk)
            for g in range(groups):
                cols = slice(g * LANES, (g + 1) * LANES)
                dws_ref[g] += _dot_nt(dm_ref[rows, cols], vn_ref[rows, cols])
        dvn = dvn_ref[...]
        dg_acc[...] += _sum_rows8(dvn * xhat)
        db_acc[...] += _sum_rows8(dvn)
        o_ref[:, d:2 * d] = _ln_bwd(dvn * g_ref[...], xhat, rstd).astype(BF16)
        o_ref[:, 3 * d:4 * d] = dq_ref[...]
        o_ref[:, 4 * d:5 * d] = dk_ref[...]
        o_ref[:, 5 * d:6 * d] = dv_ref[...]

        @pl.when(i == nsteps - 1)
        def _():
            row = lax.broadcasted_iota(jnp.int32, (chunk, chunk), 0)
            col = lax.broadcasted_iota(jnp.int32, (chunk, chunk), 1)
            for g in range(groups):
                dws_ref[g] = jnp.where(col <= row, dws_ref[g], 0.0)
            lane = lax.broadcasted_iota(jnp.int32, (chunk, LANES), 1)
            res = jnp.zeros((chunk, LANES), F32)
            for g in range(groups):
                tot = jnp.sum(dbs_acc[:, g * LANES:(g + 1) * LANES], axis=1, keepdims=True)
                res = jnp.where(lane == g, tot, res)
            dbs_ref[...] = res
            dg_ref[...] = jnp.sum(dg_acc[...], axis=0, keepdims=True)
            db_ref[...] = jnp.sum(db_acc[...], axis=0, keepdims=True)

    col = lambda j: pl.BlockSpec((t, d), lambda i: (i, j))
    tok = pl.BlockSpec((t, d), lambda i: (i, 0))
    full = lambda a: pl.BlockSpec(a.shape, lambda i: (0,) * a.ndim)
    vec = jax.ShapeDtypeStruct((1, d), F32)
    outs = pl.pallas_call(
        body, name="gmlp_bwd", grid=(nsteps,),
        in_specs=[ANY, col(0), col(1), col(2), tok, tok, tok, tok,
                  full(vn_g), full(vn_b), full(ws_m), full(ws_mt), full(bs_t)],
        out_specs=[pl.BlockSpec((t, 6 * d), lambda i: (i, 0)),
                   pl.BlockSpec((1, d), lambda i: (0, 0)), pl.BlockSpec((1, d), lambda i: (0, 0)),
                   pl.BlockSpec((groups, chunk, chunk), lambda i: (0, 0, 0)),
                   pl.BlockSpec((chunk, LANES), lambda i: (0, 0))],
        out_shape=[jax.ShapeDtypeStruct(dproj.shape, BF16), vec, vec,
                   jax.ShapeDtypeStruct((groups, chunk, chunk), F32),
                   jax.ShapeDtypeStruct((chunk, LANES), F32)],
        scratch_shapes=[pltpu.VMEM((t, d), BF16), pltpu.VMEM((t, d), F32), pltpu.VMEM((t, d), BF16),
                        pltpu.VMEM((t, d), F32), pltpu.VMEM((chunk, d), F32),
                        pltpu.VMEM((SUBLANES, d), F32), pltpu.VMEM((SUBLANES, d), F32)],
        input_output_aliases={0: 0},
        compiler_params=_params("arbitrary"))(dproj, proj, proj, proj, dya, dq, dk, dv,
                                              vn_g, vn_b, ws_m, ws_mt, bs_t)
    return outs


ATTN_TILE = 256
EXP_UNDERFLOW = -104.0


def _log_sigmoid(z):
    return jnp.minimum(z, 0.0) - jnp.log(1.0 + jnp.exp(-jnp.abs(z)))


def _suffix_rhs():
    row = lax.broadcasted_iota(jnp.int32, (LANES, LANES), 0)
    col = lax.broadcasted_iota(jnp.int32, (LANES, LANES), 1)
    rhs = jnp.concatenate([(row > col).astype(BF16), jnp.ones((LANES, LANES), BF16)], axis=1)
    return jnp.concatenate([rhs, rhs], axis=0)


def _suffix_sums(a, rhs_ref, t):
    hi = a.astype(BF16)
    lo = (a - hi.astype(F32)).astype(BF16)
    n = t // LANES
    inside, totals = [], []
    for c in range(n):
        cols = slice(c * LANES, (c + 1) * LANES)
        res = _dot(jnp.concatenate([hi[:, cols], lo[:, cols]], axis=1), rhs_ref[...])
        inside.append(res[:, :LANES])
        totals.append(res[:, LANES:])
    later = totals[n - 1]
    for c in reversed(range(n - 1)):
        inside[c] = inside[c] + later
        later = later + totals[c]
    return jnp.concatenate(inside, axis=1), later


def _lanes_to_tile(a, t):
    return jnp.concatenate([a] * (t // LANES), axis=1)


def _sweep_earlier_tiles(block, i, keep_sum_ref):
    def live():
        return jnp.max(jnp.maximum(keep_sum_ref[0], keep_sum_ref[1])) >= EXP_UNDERFLOW

    def cond(carry):
        n, alive = carry
        return jnp.logical_and(n < i, alive)

    def step(carry):
        n, _ = carry
        block(i - 1 - n, False)
        return n + 1, live()

    lax.while_loop(cond, step, (jnp.int32(0), live()))


def _attn_masks(t):
    lane = lax.broadcasted_iota(jnp.int32, (t, LANES), 1)
    row = lax.broadcasted_iota(jnp.int32, (t, t), 0)
    col = lax.broadcasted_iota(jnp.int32, (t, t), 1)
    return lane < HEAD_DIM, col < row


def _split_heads(a, head0):
    zero = jnp.zeros_like(a)
    return [jnp.where(head0, a, zero), jnp.where(head0, zero, a)]


def _attn_fwd(proj, rider=None):
    s = proj.shape[0]
    d = proj.shape[1] // N_IN
    hp = d // LANES
    tq = min(s, ATTN_TILE)
    nq = s // tq
    scale = HEAD_DIM ** -0.5
    assert math.log2(scale).is_integer()

    def body(*refs):
        h_id, i = pl.program_id(0), pl.program_id(1)
        first = jnp.logical_and(h_id == 0, i == 0)
        last = jnp.logical_and(h_id == hp - 1, i == nq - 1)
        (q_ref, k_ref, v_ref, rhs_ref), (o_ref,), (acc_ref, r_ref), start, finish = _ride(rider, refs, 4, 1, first, last)
        start()
        head0, strict = _attn_masks(tq)
        qm = _split_heads((q_ref[...].astype(F32) * scale).astype(BF16), head0)
        acc_ref[...] = jnp.zeros_like(acc_ref)
        r_ref[...] = jnp.zeros_like(r_ref)

        def tiles(js, diag):
            offs = [pl.multiple_of(j * tq, tq) for j in js]
            k2 = [k_ref[pl.ds(off, tq), :] for off in offs]
            v2 = [v_ref[pl.ds(off, tq), :] for off in offs]
            chains = [(t, h) for t in range(len(js)) for h in range(HEADS_PER_BLOCK)]
            masked = lambda c: diag and c[0] == 0
            z = {c: _dot_nt(qm[c[1]], k2[c[0]]) for c in chains}
            lsz = {c: _log_sigmoid(z[c]) for c in chains}
            keep = {c: lsz[c] - z[c] for c in chains}
            keep = {c: jnp.where(strict, keep[c], 0.0) if masked(c) else keep[c] for c in chains}
            sums = {c: _suffix_sums(keep[c], rhs_ref, tq) for c in chains}
            for h in range(HEADS_PER_BLOCK):
                keep_sum = r_ref[h]
                for t in range(len(js)):
                    w = jnp.exp(lsz[t, h] + sums[t, h][0] + _lanes_to_tile(keep_sum, tq))
                    if masked((t, h)):
                        w = jnp.where(strict, w, 0.0)
                    acc_ref[h] += _dot(w.astype(BF16), v2[t])
                    keep_sum = keep_sum + sums[t, h][1]
                r_ref[h] = keep_sum

        @pl.when(i == 0)
        def _():
            tiles([i], True)

        @pl.when(i > 0)
        def _():
            tiles([i, i - 1], True)
            _sweep_earlier_tiles(lambda j, diag: tiles([j], diag), i - 1, r_ref)

        o_ref[...] = jnp.where(head0, acc_ref[0], acc_ref[1])
        finish()

    rhs = _suffix_rhs()
    r_in, r_out, r_shapes, r_sems, r_alias = _rider_call_args(rider, 4, 1)
    outs = pl.pallas_call(
        body, name="attn_fwd", grid=(hp, nq),
        in_specs=[pl.BlockSpec((tq, LANES), lambda h, i: (i, 3 * hp + h)),
                  pl.BlockSpec((s, LANES), lambda h, i: (0, 4 * hp + h)),
                  pl.BlockSpec((s, LANES), lambda h, i: (0, 5 * hp + h)),
                  pl.BlockSpec(rhs.shape, lambda h, i: (0, 0))] + r_in,
        out_specs=[pl.BlockSpec((tq, LANES), lambda h, i: (i, h))] + r_out,
        out_shape=[jax.ShapeDtypeStruct((s, d), F32)] + r_shapes,
        scratch_shapes=[pltpu.VMEM((HEADS_PER_BLOCK, tq, LANES), F32),
                        pltpu.VMEM((HEADS_PER_BLOCK, tq, LANES), F32)] + r_sems,
        input_output_aliases=r_alias,
        compiler_params=_params("arbitrary", "arbitrary"))(proj, proj, proj, rhs, *(rider.operands if rider else []))
    return outs[0], outs[1:]


def _attn_bwd(proj, o, do, rider=None):
    s = proj.shape[0]
    d = proj.shape[1] // N_IN
    hp = d // LANES
    tq = min(s, ATTN_TILE)
    nq = s // tq
    scale = HEAD_DIM ** -0.5

    def body(*refs):
        h_id, i = pl.program_id(0), pl.program_id(1)
        first = jnp.logical_and(h_id == 0, i == 0)
        last = jnp.logical_and(h_id == hp - 1, i == nq - 1)
        ((q_ref, k_ref, v_ref, o_ref, do_ref, rhs_ref), (dq_ref, dk_ref, dv_ref),
         (dq_acc, dk_acc, dv_acc, rk_ref, rg_ref), start, finish) = _ride(rider, refs, 6, 3, first, last)
        start()
        head0, strict = _attn_masks(tq)
        qm = _split_heads((q_ref[...].astype(F32) * scale).astype(BF16), head0)
        dom = _split_heads(do_ref[...], head0)
        prod = do_ref[...].astype(F32) * o_ref[...]
        delta = [jnp.sum(jnp.where(head0, prod, 0.0), axis=1, keepdims=True),
                 jnp.sum(jnp.where(head0, 0.0, prod), axis=1, keepdims=True)]

        @pl.when(i == 0)
        def _():
            dk_acc[...] = jnp.zeros_like(dk_acc)
            dv_acc[...] = jnp.zeros_like(dv_acc)

        dq_acc[...] = jnp.zeros_like(dq_acc)
        rk_ref[...] = jnp.zeros_like(rk_ref)
        for h in range(HEADS_PER_BLOCK):
            rg_ref[h] = jnp.broadcast_to(delta[h], (tq, LANES))

        def tiles(js, diag):
            n_t = len(js)
            offs = [pl.multiple_of(j * tq, tq) for j in js]
            k2 = [k_ref[pl.ds(off, tq), :] for off in offs]
            v2 = [v_ref[pl.ds(off, tq), :] for off in offs]
            chains = [(t, h) for t in range(n_t) for h in range(HEADS_PER_BLOCK)]
            masked = lambda c: diag and c[0] == 0
            z = {c: _dot_nt(qm[c[1]], k2[c[0]]) for c in chains}
            dw = {c: _dot_nt(dom[c[1]], v2[c[0]]) for c in chains}
            lsz = {c: _log_sigmoid(z[c]) for c in chains}
            keep = {c: lsz[c] - z[c] for c in chains}
            keep = {c: jnp.where(strict, keep[c], 0.0) if masked(c) else keep[c] for c in chains}
            ksum = {c: _suffix_sums(keep[c], rhs_ref, tq) for c in chains}
            wb, g = {}, {}
            for h in range(HEADS_PER_BLOCK):
                keep_sum = rk_ref[h]
                for t in range(n_t):
                    w = jnp.exp(lsz[t, h] + ksum[t, h][0] + _lanes_to_tile(keep_sum, tq))
                    if masked((t, h)):
                        w = jnp.where(strict, w, 0.0)
                    wb[t, h] = w.astype(BF16)
                    g[t, h] = dw[t, h] * wb[t, h].astype(F32)
                    keep_sum = keep_sum + ksum[t, h][1]
                rk_ref[h] = keep_sum
            gsum = {c: _suffix_sums(g[c], rhs_ref, tq) for c in chains}
            dzb = {}
            for h in range(HEADS_PER_BLOCK):
                left = rg_ref[h]
                for t in range(n_t):
                    dz = g[t, h] - jnp.exp(lsz[t, h]) * (_lanes_to_tile(left, tq) - gsum[t, h][0])
                    if masked((t, h)):
                        dz = jnp.where(strict, dz, 0.0)
                    dzb[t, h] = dz.astype(BF16)
                    dq_acc[h] += _dot(dzb[t, h], k2[t])
                    left = left - gsum[t, h][1]
                rg_ref[h] = left
            for t in range(n_t):
                dk_acc[pl.ds(offs[t], tq), :] += _dot_tn(dzb[t, 0], qm[0]) + _dot_tn(dzb[t, 1], qm[1])
                dv_acc[pl.ds(offs[t], tq), :] += _dot_tn(wb[t, 0], dom[0]) + _dot_tn(wb[t, 1], dom[1])

        @pl.when(i == 0)
        def _():
            tiles([i], True)

        @pl.when(i > 0)
        def _():
            tiles([i, i - 1], True)
            _sweep_earlier_tiles(lambda j, diag: tiles([j], diag), i - 1, rk_ref)

        dq_ref[...] = (jnp.where(head0, dq_acc[0], dq_acc[1]) * scale).astype(BF16)

        @pl.when(i == nq - 1)
        def _():
            dk_ref[...] = dk_acc[...].astype(BF16)
            dv_ref[...] = dv_acc[...].astype(BF16)

        finish()

    blk = pl.BlockSpec((tq, LANES), lambda h, i: (i, h))
    seq = pl.BlockSpec((s, LANES), lambda h, i: (0, h))
    sh = jax.ShapeDtypeStruct((s, d), BF16)
    rhs = _suffix_rhs()
    r_in, r_out, r_shapes, r_sems, r_alias = _rider_call_args(rider, 6, 3)
    outs = pl.pallas_call(
        body, name="attn_bwd", grid=(hp, nq),
        in_specs=[pl.BlockSpec((tq, LANES), lambda h, i: (i, 3 * hp + h)),
                  pl.BlockSpec((s, LANES), lambda h, i: (0, 4 * hp + h)),
                  pl.BlockSpec((s, LANES), lambda h, i: (0, 5 * hp + h)),
                  blk, blk, pl.BlockSpec(rhs.shape, lambda h, i: (0, 0))] + r_in,
        out_specs=[blk, seq, seq] + r_out, out_shape=[sh, sh, sh] + r_shapes,
        scratch_shapes=[pltpu.VMEM((HEADS_PER_BLOCK, tq, LANES), F32),
                        pltpu.VMEM((s, LANES), F32), pltpu.VMEM((s, LANES), F32),
                        pltpu.VMEM((HEADS_PER_BLOCK, tq, LANES), F32),
                        pltpu.VMEM((HEADS_PER_BLOCK, tq, LANES), F32)] + r_sems,
        input_output_aliases=r_alias,
        compiler_params=_params("arbitrary", "arbitrary"))(proj, proj, proj, o, do, rhs,
                                                           *(rider.operands if rider else []))
    return outs[:3], outs[3:]


def _post_math(ya_ref, o_ref, gb_ref, ma_ref, mb_ref, x_ref, p_ref, wpa_ref, wpb_ref, wout_ref, wpe_ref, wpg_ref,
               alpha):
    gb = gb_ref[...].astype(F32)
    sgb = _sigmoid(gb)
    o = o_ref[...]
    yb = (o * (gb * sgb)).astype(BF16)
    pa = _dot(ya_ref[...], wpa_ref[...])
    pb = _dot(yb, wpb_ref[...])
    sa = _sigmoid(ma_ref[...].astype(F32))
    sb = _sigmoid(mb_ref[...].astype(F32))
    merged = (sa * pa + sb * pb).astype(BF16)
    h1 = alpha * x_ref[...] + _dot(merged, wout_ref[...])
    h1b = h1.astype(BF16)
    e = _dot(p_ref[...].astype(BF16), wpe_ref[...])
    sg = _sigmoid(_dot(h1b, wpg_ref[...]))
    h2 = h1 + e * sg
    return dict(gb=gb, sgb=sgb, o=o, yb=yb, pa=pa, pb=pb, sa=sa, sb=sb, merged=merged, h1b=h1b, e=e, sg=sg, h2=h2)


def _post_specs(tm, d, ple, weights):
    tok = pl.BlockSpec((tm, d), lambda i: (i, 0))
    col = lambda j: pl.BlockSpec((tm, d), lambda i: (i, j))
    full = lambda a: pl.BlockSpec(a.shape, lambda i: (0,) * a.ndim, pipeline_mode=pl.Buffered(1))
    return tok, [tok, tok, col(6), col(7), col(8), tok, pl.BlockSpec((tm, ple), lambda i: (i, 0))] + [
        full(w) for w in weights]


def _post_fwd(ya, o, proj, x, p, w_pa, w_pb, w_out, w_pe, w_pg, ln_g, ln_b, alpha, rider=None):
    s, d = x.shape
    ple = p.shape[1]
    tm = min(s, 256)
    nsteps = s // tm
    weights = (w_pa, w_pb, w_out, w_pe, w_pg, ln_g, ln_b)

    def body(*refs):
        i = pl.program_id(0)
        ins, (out_ref,), _, start, finish = _ride(rider, refs, 14, 1, i == 0, i == nsteps - 1)
        start()
        f = _post_math(*ins[:12], alpha)
        xhat, _ = _ln_stats(f["h2"])
        out_ref[...] = xhat * ins[12][...] + ins[13][...]
        finish()

    tok, in_specs = _post_specs(tm, d, ple, weights)
    r_in, r_out, r_shapes, r_sems, r_alias = _rider_call_args(rider, 14, 1)
    outs = pl.pallas_call(body, name="post_fwd", grid=(nsteps,), in_specs=in_specs + r_in, out_specs=[tok] + r_out,
                          out_shape=[jax.ShapeDtypeStruct((s, d), F32)] + r_shapes, scratch_shapes=r_sems,
                          input_output_aliases=r_alias,
                          compiler_params=_params("arbitrary"))(ya, o, proj, proj, proj, x, p, *weights,
                                                                *(rider.operands if rider else []))
    return outs[0], outs[1:]


def _post_bwd(dxo, ya, o, proj, x, p, w_pa, w_pb, w_out, w_pe, w_pg, ln_g, ln_b, alpha):
    s, d = x.shape
    ple = p.shape[1]
    tm = min(s, 256)
    nsteps = s // tm
    weights = (w_pa, w_pb, w_out, w_pe, w_pg, ln_g, ln_b)

    def body(dxo_ref, ya_ref, o_ref, gb_ref, ma_ref, mb_ref, x_ref, p_ref, wpa_ref, wpb_ref, wout_ref, wpe_ref,
             wpg_ref, g_ref, b_ref,
             dproj_ref, dxr_ref, dya_ref, do_ref, de_ref, h1_ref, dzg_ref, mrg_ref, dh1_ref, dpa_ref, yb_ref, dpb_ref,
             dg_ref, db_ref, dg_acc, db_acc):
        i = pl.program_id(0)

        @pl.when(i == 0)
        def _():
            dg_acc[...] = jnp.zeros_like(dg_acc)
            db_acc[...] = jnp.zeros_like(db_acc)

        f = _post_math(ya_ref, o_ref, gb_ref, ma_ref, mb_ref, x_ref, p_ref, wpa_ref, wpb_ref, wout_ref, wpe_ref,
                       wpg_ref, alpha)
        xhat, rstd = _ln_stats(f["h2"])
        dxo = dxo_ref[...]
        dg_acc[...] += _sum_rows8(dxo * xhat)
        db_acc[...] += _sum_rows8(dxo)
        dh2 = _ln_bwd(dxo * g_ref[...], xhat, rstd)
        sg, e = f["sg"], f["e"]
        de_ref[...] = (dh2 * sg).astype(BF16)
        dzg = (dh2 * e * sg * (1.0 - sg)).astype(BF16)
        dzg_ref[...] = dzg
        dh1 = dh2 + _dot_nt(dzg, wpg_ref[...])
        dh1b = dh1.astype(BF16)
        dxr_ref[...] = alpha * dh1
        dh1_ref[...] = dh1b
        h1_ref[...] = f["h1b"]
        mrg_ref[...] = f["merged"]
        yb_ref[...] = f["yb"]
        dmerged = _dot_nt(dh1b, wout_ref[...])
        sa, sb = f["sa"], f["sb"]
        dpa = (dmerged * sa).astype(BF16)
        dpb = (dmerged * sb).astype(BF16)
        dpa_ref[...] = dpa
        dpb_ref[...] = dpb
        dproj_ref[:, d:2 * d] = (dmerged * f["pa"] * sa * (1.0 - sa)).astype(BF16)
        dproj_ref[:, 2 * d:3 * d] = (dmerged * f["pb"] * sb * (1.0 - sb)).astype(BF16)
        dya_ref[...] = _dot_nt(dpa, wpa_ref[...]).astype(BF16)
        dyb = _dot_nt(dpb, wpb_ref[...])
        gb, sgb = f["gb"], f["sgb"]
        do_ref[...] = (dyb * (gb * sgb)).astype(BF16)
        dproj_ref[:, 0:d] = (dyb * f["o"] * (sgb * (1.0 + gb * (1.0 - sgb)))).astype(BF16)

        @pl.when(i == nsteps - 1)
        def _():
            dg_ref[...] = jnp.sum(dg_acc[...], axis=0, keepdims=True)
            db_ref[...] = jnp.sum(db_acc[...], axis=0, keepdims=True)

    tok, in_specs = _post_specs(tm, d, ple, weights)
    vec_spec = pl.BlockSpec((1, d), lambda i: (0, 0))
    vec = jax.ShapeDtypeStruct((1, d), F32)
    act = jax.ShapeDtypeStruct((s, d), BF16)
    return pl.pallas_call(
        body, name="post_bwd", grid=(nsteps,), in_specs=[tok] + in_specs,
        out_specs=[pl.BlockSpec((tm, 3 * d), lambda i: (i, 2)), tok] + [tok] * 10 + [vec_spec, vec_spec],
        out_shape=[jax.ShapeDtypeStruct((s, N_IN * d), BF16), jax.ShapeDtypeStruct((s, d), F32)] + [act] * 10 + [vec, vec],
        scratch_shapes=[pltpu.VMEM((SUBLANES, d), F32), pltpu.VMEM((SUBLANES, d), F32)],
        compiler_params=_params("arbitrary"))(dxo, ya, o, proj, proj, proj, x, p, *weights)


def _loss_head(y, target):
    s, d = y.shape
    tm = min(s, 512)

    def body(y_ref, t_ref, dy_ref, l_ref):
        @pl.when(pl.program_id(0) == 0)
        def _():
            l_ref[...] = jnp.zeros_like(l_ref)

        err = y_ref[...] - t_ref[...]
        dy_ref[...] = err / d
        row = jnp.sum(err * err, axis=1, keepdims=True) / d
        l_ref[...] += 0.5 * jnp.sum(row, axis=0, keepdims=True)

    tok = pl.BlockSpec((tm, d), lambda i: (i, 0))
    return pl.pallas_call(body, name="loss_head", grid=(s // tm,), in_specs=[tok, tok],
                          out_specs=[tok, pl.BlockSpec((SUBLANES, LANES), lambda i: (0, 0))],
                          out_shape=[jax.ShapeDtypeStruct((s, d), F32),
                                     jax.ShapeDtypeStruct((SUBLANES, LANES), F32)],
                          compiler_params=_params("arbitrary"))(y, target)


def _position():
    x, y, c = lax.axis_index("x"), lax.axis_index("y"), lax.axis_index("c")
    chips = [(1 - x, y), (x, 1 - y), (1 - x, 1 - y)]
    return x, y, c, chips


def _shard_of(ref, col_sharded, j, n):
    off = pl.multiple_of(j * n, n)
    return ref.at[:, pl.ds(off, n)] if col_sharded else ref.at[pl.ds(off, n), :]


def _half_of(ref, col_sharded, h, n):
    off = pl.multiple_of(h * n, n)
    return ref.at[pl.ds(off, n), :] if col_sharded else ref.at[:, pl.ds(off, n)]


def _piece_of(ref, col_sharded, chip, n_block, half, n_half):
    block = pl.ds(pl.multiple_of(chip * n_block, n_block), n_block)
    part = pl.ds(pl.multiple_of(half * n_half, n_half), n_half)
    return ref.at[part, block] if col_sharded else ref.at[block, part]


class _Rider(NamedTuple):
    operands: list
    out_shapes: list
    aliases: dict
    n_sems: int
    start: Callable
    finish: Callable


def _rider_call_args(rider, n_in, n_out):
    if rider is None:
        return [], [], [], [], {}
    sems = [pltpu.SemaphoreType.DMA((rider.n_sems,))] * 2
    aliases = {n_in + i: n_out + o for i, o in rider.aliases.items()}
    return [ANY] * len(rider.operands), [ANY] * len(rider.out_shapes), list(rider.out_shapes), sems, aliases


def _ride(rider, refs, n_in, n_out, first, last):
    if rider is None:
        return refs[:n_in], refs[n_in:n_in + n_out], refs[n_in + n_out:], lambda: None, lambda: None
    r_in, r_out = len(rider.operands), len(rider.out_shapes)
    ins, rins = refs[:n_in], refs[n_in:n_in + r_in]
    outs = refs[n_in + r_in:n_in + r_in + n_out]
    routs = refs[n_in + r_in + n_out:n_in + r_in + n_out + r_out]
    scratch = refs[n_in + r_in + n_out + r_out:-2]
    send_sems, recv_sems = refs[-2:]

    def start():
        pl.when(first)(lambda: rider.start(rins, routs, send_sems, recv_sems))

    def finish():
        pl.when(last)(lambda: rider.finish(rins, routs, send_sems, recv_sems))

    return ins, outs, scratch, start, finish


def _gather_copies(outs, ici_sems, d2d_sems):
    x, y, c, chips = _position()
    my_chip = 2 * x + y
    sends, arrivals, passes, passed = [], [], [], []
    for a, out in enumerate(outs):
        cs = COL_SHARDED[BIG[a % len(BIG)]]
        rows, cols = out.shape
        n_block = (cols if cs else rows) // N_CHIPS
        n_half = (rows if cs else cols) // 2
        piece = lambda chip, half: _piece_of(out, cs, chip, n_block, half, n_half)
        for j, chip in enumerate(chips):
            k = a * 3 + j
            their = 2 * chip[0] + chip[1]
            if ici_sems is not None:
                send_sems, recv_sems = ici_sems
                sends.append(pltpu.make_async_remote_copy(
                    src_ref=piece(my_chip, c), dst_ref=piece(my_chip, c), send_sem=send_sems.at[k],
                    recv_sem=recv_sems.at[k], device_id=(chip[0], chip[1], c), device_id_type=MESH))
                arrivals.append(pltpu.make_async_remote_copy(
                    src_ref=piece(their, c), dst_ref=piece(their, c), send_sem=send_sems.at[k],
                    recv_sem=recv_sems.at[k], device_id=(chip[0], chip[1], c), device_id_type=MESH))
            if d2d_sems is not None:
                send_sems, recv_sems = d2d_sems
                passes.append(pltpu.make_async_remote_copy(
                    src_ref=piece(their, c), dst_ref=piece(their, c), send_sem=send_sems.at[k],
                    recv_sem=recv_sems.at[k], device_id=(x, y, 1 - c), device_id_type=MESH))
                passed.append(pltpu.make_async_remote_copy(
                    src_ref=piece(their, 1 - c), dst_ref=piece(their, 1 - c), send_sem=send_sems.at[k],
                    recv_sem=recv_sems.at[k], device_id=(x, y, 1 - c), device_id_type=MESH))
    return sends, arrivals, passes, passed


def _gather_weights(bufs):
    flat = [bufs[name] for name in BIG]
    n_arr = len(flat)

    def body(*refs):
        outs = refs[n_arr:2 * n_arr]
        sems = refs[2 * n_arr:]
        sends, arrivals, passes, passed = _gather_copies(outs, sems[:2], sems[2:])
        for cp in sends:
            cp.start()
        for arrival, onward in zip(arrivals, passes):
            arrival.wait_recv()
            onward.start()
        for cp in passed:
            cp.wait_recv()
        for cp in sends + passes:
            cp.wait_send()

    outs = pl.pallas_call(
        body, name="gather_weights", in_specs=[ANY] * n_arr, out_specs=[ANY] * n_arr,
        out_shape=[jax.ShapeDtypeStruct(a.shape, BF16) for a in flat],
        input_output_aliases={a: a for a in range(n_arr)},
        scratch_shapes=[pltpu.SemaphoreType.DMA((n_arr * 3,))] * 4,
    )(*flat)
    return dict(zip(BIG, outs))


def _gather_rider(bufs, over_ici):
    flat = [bufs[name] for name in BIG]

    def copies(routs, send_sems, recv_sems):
        sems = (send_sems, recv_sems)
        sends, arrivals, passes, passed = _gather_copies(routs, sems if over_ici else None, None if over_ici else sems)
        return (sends, arrivals) if over_ici else (passes, passed)

    def start(rins, routs, send_sems, recv_sems):
        for cp in copies(routs, send_sems, recv_sems)[0]:
            cp.start()

    def finish(rins, routs, send_sems, recv_sems):
        out, due = copies(routs, send_sems, recv_sems)
        for cp in due:
            cp.wait_recv()
        for cp in out:
            cp.wait_send()

    return _Rider(flat, [jax.ShapeDtypeStruct(a.shape, a.dtype) for a in flat], {i: i for i in range(len(flat))},
                  len(flat) * 3, start, finish)


def _half_shape(shape, col_sharded):
    r, c = shape
    return (r // 2, c) if col_sharded else (r, c // 2)


def _exchange_rider(grads):
    flat = [grads[name] for name in BIG]

    def copies(rins, routs, send_sems, recv_sems):
        x, y, c, _ = _position()
        out = []
        for a, name in enumerate(BIG):
            cs = COL_SHARDED[name]
            n = routs[a].shape[0] if cs else routs[a].shape[1]
            out.append(pltpu.make_async_remote_copy(
                src_ref=_half_of(rins[a], cs, 1 - c, n), dst_ref=routs[a], send_sem=send_sems.at[a],
                recv_sem=recv_sems.at[a], device_id=(x, y, 1 - c), device_id_type=MESH))
        return out

    def start(rins, routs, send_sems, recv_sems):
        for cp in copies(rins, routs, send_sems, recv_sems):
            cp.start()

    def finish(rins, routs, send_sems, recv_sems):
        cps = copies(rins, routs, send_sems, recv_sems)
        for cp in cps:
            cp.wait_recv()
        for cp in cps:
            cp.wait_send()

    return _Rider(flat, [jax.ShapeDtypeStruct(_half_shape(a.shape, COL_SHARDED[name]), F32)
                         for a, name in zip(flat, BIG)], {}, len(flat), start, finish)


def _scatter_to_owners(halves):
    rider = _scatter_rider(halves)

    def body(*refs):
        n = len(rider.operands)
        args = (refs[:n], refs[n:2 * n], *refs[2 * n:])
        rider.start(*args)
        rider.finish(*args)

    outs = pl.pallas_call(
        body, name="scatter_to_owners", in_specs=[ANY] * len(rider.operands), out_specs=[ANY] * len(rider.out_shapes),
        out_shape=rider.out_shapes, scratch_shapes=[pltpu.SemaphoreType.DMA((rider.n_sems,))] * 2,
    )(*rider.operands)
    return dict(zip(BIG, outs))


def _scatter_rider(halves):
    flat = [halves[name] for name in BIG]

    def slots_shape(a, cs):
        r, c = a.shape
        return (N_CHIPS - 1,) + ((r, c // N_CHIPS) if cs else (r // N_CHIPS, c))

    def copies(rins, routs, send_sems, recv_sems):
        x, y, c, chips = _position()
        out = []
        for a, name in enumerate(BIG):
            cs = COL_SHARDED[name]
            n = routs[a].shape[2] if cs else routs[a].shape[1]
            for j, chip in enumerate(chips):
                k = a * 3 + j
                out.append(pltpu.make_async_remote_copy(
                    src_ref=_shard_of(rins[a], cs, 2 * chip[0] + chip[1], n), dst_ref=routs[a].at[j],
                    send_sem=send_sems.at[k], recv_sem=recv_sems.at[k], device_id=(chip[0], chip[1], c),
                    device_id_type=MESH))
        return out

    def start(rins, routs, send_sems, recv_sems):
        for cp in copies(rins, routs, send_sems, recv_sems):
            cp.start()

    def finish(rins, routs, send_sems, recv_sems):
        cps = copies(rins, routs, send_sems, recv_sems)
        for cp in cps:
            cp.wait_recv()
        for cp in cps:
            cp.wait_send()

    return _Rider(flat, [jax.ShapeDtypeStruct(slots_shape(a, COL_SHARDED[name]), a.dtype) for a, name in zip(flat, BIG)],
                  {}, len(flat) * 3, start, finish)


def _join_halves(halves):
    flat = [halves[name] for name in BIG]
    n_w = len(flat)

    def body(*refs):
        outs = refs[n_w:2 * n_w]
        send_sems, recv_sems = refs[2 * n_w:]
        x, y, c, _ = _position()
        sends, recvs = [], []
        for w, name in enumerate(BIG):
            cs = COL_SHARDED[name]
            n = (outs[w].shape[1] if cs else outs[w].shape[2]) // 2

            def half(h):
                part = pl.ds(pl.multiple_of(h * n, n), n)
                return outs[w].at[:, part, :] if cs else outs[w].at[:, :, part]

            sends.append(pltpu.make_async_remote_copy(
                src_ref=half(c), dst_ref=half(c), send_sem=send_sems.at[w], recv_sem=recv_sems.at[w],
                device_id=(x, y, 1 - c), device_id_type=MESH))
            recvs.append(pltpu.make_async_remote_copy(
                src_ref=half(1 - c), dst_ref=half(1 - c), send_sem=send_sems.at[w], recv_sem=recv_sems.at[w],
                device_id=(x, y, 1 - c), device_id_type=MESH))
        for cp in sends:
            cp.start()
        for cp in recvs:
            cp.wait_recv()
        for cp in sends:
            cp.wait_send()

    outs = pl.pallas_call(
        body, name="join_halves", in_specs=[ANY] * n_w, out_specs=[ANY] * n_w,
        out_shape=[jax.ShapeDtypeStruct(a.shape, F32) for a in flat],
        input_output_aliases={w: w for w in range(n_w)},
        scratch_shapes=[pltpu.SemaphoreType.DMA((n_w,)), pltpu.SemaphoreType.DMA((n_w,))],
    )(*flat)
    return dict(zip(BIG, outs))


def _gather_small(packed):
    r, lanes = packed.shape

    def body(in_ref, out_ref, send_sems, recv_sems, loc_sem):
        x, y, c, _ = _position()
        me = 4 * x + 2 * y + c
        local = pltpu.make_async_copy(in_ref, out_ref.at[me], loc_sem)
        local.start()
        sends, recvs = [], []
        for k in range(1, N_DEV):
            px, py, pc = x ^ (k >> 2), y ^ ((k >> 1) & 1), c ^ (k & 1)
            sends.append(pltpu.make_async_remote_copy(
                src_ref=in_ref, dst_ref=out_ref.at[me], send_sem=send_sems.at[k - 1], recv_sem=recv_sems.at[k - 1],
                device_id=(px, py, pc), device_id_type=MESH))
            recvs.append(pltpu.make_async_remote_copy(
                src_ref=in_ref, dst_ref=out_ref.at[4 * px + 2 * py + pc], send_sem=send_sems.at[k - 1],
                recv_sem=recv_sems.at[k - 1], device_id=(px, py, pc), device_id_type=MESH))
        for cp in sends:
            cp.start()
        for cp in recvs:
            cp.wait_recv()
        for cp in sends:
            cp.wait_send()
        local.wait()

    return pl.pallas_call(
        body, name="gather_small", in_specs=[ANY], out_specs=ANY,
        out_shape=jax.ShapeDtypeStruct((N_DEV, r, lanes), F32),
        scratch_shapes=[pltpu.SemaphoreType.DMA((N_DEV - 1,)), pltpu.SemaphoreType.DMA((N_DEV - 1,)),
                        pltpu.SemaphoreType.DMA],
    )(packed)


def _pack_small(t):
    return jnp.concatenate([t[name].reshape(-1, LANES) for name in SMALL], axis=0)


def _unpack_small(packed, like):
    out, row = {}, 0
    for name in SMALL:
        n = like[name].size // LANES
        out[name] = packed[row:row + n].reshape(like[name].shape)
        row += n
    return out


def kernel(x, p, w_in, vn_g, vn_b, w_s, b_s, w_pa, w_pb, w_out, w_pe, w_pg, ln_g, ln_b, loss_target, m_w_in, m_vn_g, m_vn_b, m_w_s, m_b_s, m_w_pa, m_w_pb, m_w_out, m_w_pe, m_w_pg, m_ln_g, m_ln_b, v_w_in, v_vn_g, v_vn_b, v_w_s, v_b_s, v_w_pa, v_w_pb, v_w_out, v_w_pe, v_w_pg, v_ln_g, v_ln_b):
    weights = dict(w_in=w_in, vn_g=vn_g, vn_b=vn_b, w_s=w_s, b_s=b_s, w_pa=w_pa, w_pb=w_pb, w_out=w_out, w_pe=w_pe,
                   w_pg=w_pg, ln_g=ln_g, ln_b=ln_b)
    mom1 = dict(w_in=m_w_in, vn_g=m_vn_g, vn_b=m_vn_b, w_s=m_w_s, b_s=m_b_s, w_pa=m_w_pa, w_pb=m_w_pb, w_out=m_w_out,
                w_pe=m_w_pe, w_pg=m_w_pg, ln_g=m_ln_g, ln_b=m_ln_b)
    mom2 = dict(w_in=v_w_in, vn_g=v_vn_g, vn_b=v_vn_b, w_s=v_w_s, b_s=v_b_s, w_pa=v_w_pa, w_pb=v_w_pb, w_out=v_w_out,
                w_pe=v_w_pe, w_pg=v_w_pg, ln_g=v_ln_g, ln_b=v_ln_b)
    nl, d = vn_g.shape
    chunk = w_s.shape[2]
    assert chunk == LANES and w_s.shape[3] == LANES and d % LANES == 0
    alpha = (2 * nl) ** 0.25
    pos = tuple(lax.axis_index(a).astype(jnp.int32).reshape(1) for a in ("c", "x", "y"))

    placed = [{name: _cast_into_place(weights[name], l, pos, COL_SHARDED[name]) for name in BIG} for l in range(nl)]
    full = [_gather_weights(placed[0])] + [None] * (nl - 1)
    causal = jnp.tril(jnp.ones((chunk, chunk), dtype=bool))
    ws_m = jnp.where(causal, w_s, 0.0).astype(BF16)
    ws_mt = jnp.swapaxes(ws_m, 2, 3)
    bs_t = jnp.swapaxes(b_s, 1, 2)

    xs, projs, yas, os_ = [x[0]], [], [], []
    for l in range(nl):
        proj = _proj_fwd(xs[l], full[l]["w_in"])
        ya = _gmlp_fwd(proj, vn_g[l:l + 1], vn_b[l:l + 1], ws_m[l], bs_t[l])
        more = l + 1 < nl
        o, arrived = _attn_fwd(proj, _gather_rider(placed[l + 1], over_ici=True) if more else None)
        x_next, handed = _post_fwd(ya, o, proj, xs[l], p[l, 0], full[l]["w_pa"], full[l]["w_pb"], full[l]["w_out"],
                                   full[l]["w_pe"], full[l]["w_pg"], ln_g[l:l + 1], ln_b[l:l + 1], alpha,
                                   _gather_rider(dict(zip(BIG, arrived)), over_ici=False) if more else None)
        if more:
            full[l + 1] = dict(zip(BIG, handed))
        xs.append(x_next)
        projs.append(proj)
        yas.append(ya)
        os_.append(o)

    dx, loss_tile = _loss_head(xs[nl], loss_target[0])
    loss = lax.psum(loss_tile[0, 0], ("x", "y", "c"))

    big_grads, received, slots = [None] * nl, [None] * nl, [None] * nl
    chip_sums = None
    small_grads = {name: [None] * nl for name in SMALL}
    for l in reversed(range(nl)):
        w = full[l]
        (dproj, dxr, dya, do, de, h1b, dzg, merged, dh1, dpa, yb, dpb, dln_g, dln_b) = _post_bwd(
            dx, yas[l], os_[l], projs[l], xs[l], p[l, 0], w["w_pa"], w["w_pb"], w["w_out"], w["w_pe"], w["w_pg"],
            ln_g[l:l + 1], ln_b[l:l + 1], alpha)
        (dq, dk, dv), scattered = _attn_bwd(projs[l], os_[l], do,
                                            _scatter_rider(chip_sums) if chip_sums is not None else None)
        if chip_sums is not None:
            slots[l + 1] = dict(zip(BIG, scattered))
        dproj, dvn_g, dvn_b, dw_s, dbs_cols = _gmlp_bwd(dproj, projs[l], dya, dq, dk, dv, vn_g[l:l + 1],
                                                         vn_b[l:l + 1], ws_m[l], ws_mt[l], bs_t[l])
        big_grads[l] = dict(w_in=_matmul_tn(xs[l], dproj), w_pa=_matmul_tn(yas[l], dpa), w_pb=_matmul_tn(yb, dpb),
                            w_out=_matmul_tn(merged, dh1), w_pe=_matmul_tn(p[l, 0], de), w_pg=_matmul_tn(h1b, dzg))
        dx, from_sibling = _dx_matmul(dxr, dproj, w["w_in"], _exchange_rider(big_grads[l]))
        received[l] = dict(zip(BIG, from_sibling))
        small_grads["vn_g"][l], small_grads["vn_b"][l] = dvn_g[0], dvn_b[0]
        small_grads["ln_g"][l], small_grads["ln_b"][l] = dln_g[0], dln_b[0]
        small_grads["w_s"][l] = dw_s
        small_grads["b_s"][l] = dbs_cols[:, :b_s.shape[1]].T
        chip_sums ={name: _add_own_half(big_grads[l][name], received[l][name], pos, COL_SHARDED[name])
                     for name in BIG}
    slots[0] = _scatter_to_owners(chip_sums)
    reduced = {}
    for name in BIG:
        buf = None
        for l in range(nl):
            buf = _reduce_block(buf, big_grads[l][name], received[l][name], slots[l][name], l, nl, pos,
                                COL_SHARDED[name])
        reduced[name] = buf
    grads = _join_halves(reduced)

    small_like = {name: weights[name] for name in SMALL}
    packed = _pack_small({name: jnp.stack(small_grads[name]) for name in SMALL})
    grads.update(_unpack_small(_sum_slots(_gather_small(packed)), small_like))

    delta, new_m, new_v = {}, {}, {}
    for name in BIG:
        sh = weights[name].shape
        flat = lambda a: a.reshape(sh[0] * sh[1], sh[2])
        dl, nm, nv = _adamw(flat(weights[name]), flat(grads[name]), flat(mom1[name]), flat(mom2[name]))
        delta[name], new_m[name], new_v[name] = dl.reshape(sh), nm.reshape(sh), nv.reshape(sh)
    dl, nm, nv = _adamw(_pack_small(small_like), _pack_small({n: grads[n] for n in SMALL}),
                        _pack_small({n: mom1[n] for n in SMALL}), _pack_small({n: mom2[n] for n in SMALL}))
    delta.update(_unpack_small(dl, small_like))
    new_m.update(_unpack_small(nm, small_like))
    new_v.update(_unpack_small(nv, small_like))

    return (loss, dx[None], *[grads[n] for n in WEIGHTS], *[delta[n] for n in WEIGHTS],
            *[new_m[n] for n in WEIGHTS], *[new_v[n] for n in WEIGHTS])
```

```python
import math
from typing import Callable, NamedTuple

import jax
import jax.numpy as jnp
from jax import lax
from jax.experimental import pallas as pl
from jax.experimental.pallas import tpu as pltpu

F32 = jnp.float32
BF16 = jnp.bfloat16
LANES = 128
SUBLANES = 8
HEAD_DIM = 64
HEADS_PER_BLOCK = LANES // HEAD_DIM
LN_EPS = 1e-5
N_IN = 9
N_CHIPS = 4
N_DEV = 8
ADAM_LR = 0.001
ADAM_B1 = 0.9
ADAM_B2 = 0.999
ADAM_EPS = 1e-08
ADAM_WD = 0.01
ADAM_STEP = 10
MESH = pl.DeviceIdType.MESH
ANY = pl.BlockSpec(memory_space=pl.ANY)
BIG = ("w_in", "w_pa", "w_pb", "w_out", "w_pe", "w_pg")
COL_SHARDED = {"w_in": True, "w_pa": False, "w_pb": False, "w_out": False, "w_pe": True, "w_pg": False}
SMALL = ("vn_g", "vn_b", "w_s", "b_s", "ln_g", "ln_b")
WEIGHTS = ("w_in", "vn_g", "vn_b", "w_s", "b_s", "w_pa", "w_pb", "w_out", "w_pe", "w_pg", "ln_g", "ln_b")


def _params(*sem):
    return pltpu.CompilerParams(dimension_semantics=sem)


def _dot(a, b):
    return jnp.dot(a, b, preferred_element_type=F32)


def _dot_nt(a, b):
    return lax.dot_general(a, b, (((1,), (1,)), ((), ())), preferred_element_type=F32)


def _dot_tn(a, b):
    return lax.dot_general(a, b, (((0,), (0,)), ((), ())), preferred_element_type=F32)


def _sigmoid(a):
    return 1.0 / (1.0 + jnp.exp(-a))


def _row_tile(rows, cols, cap_bytes):
    best = None
    for t in range(16, rows + 1, 16):
        if rows % t == 0 and t * cols * 4 <= cap_bytes:
            best = t
    return best or rows


def _col_tile(cols, cap):
    best = LANES
    for t in range(LANES, min(cols, cap) + 1, LANES):
        if cols % t == 0:
            best = t
    return best


def _ln_stats(h):
    mu = jnp.mean(h, axis=-1, keepdims=True)
    hc = h - mu
    var = jnp.mean(hc * hc, axis=-1, keepdims=True)
    rstd = lax.rsqrt(var + LN_EPS)
    return hc * rstd, rstd


def _ln_bwd(dxhat, xhat, rstd):
    m1 = jnp.mean(dxhat, axis=-1, keepdims=True)
    m2 = jnp.mean(dxhat * xhat, axis=-1, keepdims=True)
    return rstd * (dxhat - m1 - xhat * m2)


def _sum_rows8(a):
    t, d = a.shape
    return jnp.sum(a.reshape(t // SUBLANES, SUBLANES, d), axis=0)


def _chip(x_ref, y_ref):
    return 2 * x_ref[0] + y_ref[0]


def _cast_into_place(shards, l, pos, col_sharded):
    _, r, c = shards.shape
    tr = _row_tile(r, c, 2 << 20)
    nb = r // tr

    def body(c_ref, x_ref, y_ref, a_ref, o_ref):
        o_ref[...] = a_ref[...].astype(BF16)

    if col_sharded:
        out_spec = pl.BlockSpec((tr, c), lambda i, c_ref, x_ref, y_ref: (i, _chip(x_ref, y_ref)))
        full = (r, c * N_CHIPS)
    else:
        out_spec = pl.BlockSpec((tr, c), lambda i, c_ref, x_ref, y_ref: (_chip(x_ref, y_ref) * nb + i, 0))
        full = (r * N_CHIPS, c)
    grid_spec = pltpu.PrefetchScalarGridSpec(
        num_scalar_prefetch=3, grid=(nb,),
        in_specs=[pl.BlockSpec((None, tr, c), lambda i, c_ref, x_ref, y_ref: (l, i, 0))], out_specs=out_spec)
    return pl.pallas_call(body, name="cast_into_place", grid_spec=grid_spec,
                          out_shape=jax.ShapeDtypeStruct(full, BF16),
                          compiler_params=_params("parallel"))(*pos, shards)


def _add_own_half(own, recv, pos, col_sharded):
    r, c = recv.shape
    tr = _row_tile(r, c, 2 << 20)
    nb = r // tr

    def body(c_ref, x_ref, y_ref, own_ref, recv_ref, o_ref):
        o_ref[...] = (own_ref[...] + recv_ref[...]).astype(BF16)

    if col_sharded:
        own_spec = pl.BlockSpec((tr, c), lambda i, c_ref, x_ref, y_ref: (c_ref[0] * nb + i, 0))
    else:
        own_spec = pl.BlockSpec((tr, c), lambda i, c_ref, x_ref, y_ref: (i, c_ref[0]))
    spec = pl.BlockSpec((tr, c), lambda i, c_ref, x_ref, y_ref: (i, 0))
    grid_spec = pltpu.PrefetchScalarGridSpec(num_scalar_prefetch=3, grid=(nb,), in_specs=[own_spec, spec],
                                             out_specs=spec)
    return pl.pallas_call(body, name="add_own_half", grid_spec=grid_spec,
                          out_shape=jax.ShapeDtypeStruct(recv.shape, BF16),
                          compiler_params=_params("parallel"))(*pos, own, recv)


def _reduce_block(buf, own, recv, slots, l, nl, pos, col_sharded):
    _, r, c = slots.shape
    tr = _row_tile(r, c, 1 << 20)
    nb = r // tr

    def body(c_ref, x_ref, y_ref, own_ref, recv_ref, slots_ref, *rest):
        acc = own_ref[...] + recv_ref[...]
        for j in range(N_CHIPS - 1):
            acc = acc + slots_ref[j].astype(F32)
        rest[-1][...] = acc

    if col_sharded:
        own_spec = pl.BlockSpec((tr, c), lambda i, c_ref, x_ref, y_ref: (c_ref[0] * nb + i, _chip(x_ref, y_ref)))
        recv_spec = pl.BlockSpec((tr, c), lambda i, c_ref, x_ref, y_ref: (i, _chip(x_ref, y_ref)))
        out_spec = pl.BlockSpec((None, tr, c), lambda i, c_ref, x_ref, y_ref: (l, c_ref[0] * nb + i, 0))
        out_shape = (nl, 2 * r, c)
    else:
        own_spec = pl.BlockSpec((tr, c), lambda i, c_ref, x_ref, y_ref: (_chip(x_ref, y_ref) * nb + i, c_ref[0]))
        recv_spec = pl.BlockSpec((tr, c), lambda i, c_ref, x_ref, y_ref: (_chip(x_ref, y_ref) * nb + i, 0))
        out_spec = pl.BlockSpec((None, tr, c), lambda i, c_ref, x_ref, y_ref: (l, i, c_ref[0]))
        out_shape = (nl, r, 2 * c)
    in_specs = [own_spec, recv_spec,
                pl.BlockSpec((N_CHIPS - 1, tr, c), lambda i, c_ref, x_ref, y_ref: (0, i, 0))]
    args = [*pos, own, recv, slots]
    aliases = {}
    if buf is not None:
        in_specs.append(ANY)
        args.append(buf)
        aliases = {len(args) - 1: 0}
    grid_spec = pltpu.PrefetchScalarGridSpec(num_scalar_prefetch=3, grid=(nb,), in_specs=in_specs,
                                             out_specs=out_spec)
    return pl.pallas_call(body, name="reduce_block", grid_spec=grid_spec,
                          out_shape=jax.ShapeDtypeStruct(out_shape, F32), input_output_aliases=aliases,
                          compiler_params=_params("parallel"))(*args)


def _sum_slots(a):
    n, r, c = a.shape
    tr = _row_tile(r, c * n, 4 << 20)

    def body(a_ref, o_ref):
        acc = a_ref[0]
        for s in range(1, n):
            acc = acc + a_ref[s]
        o_ref[...] = acc

    return pl.pallas_call(body, name="sum_slots", grid=(r // tr,),
                          in_specs=[pl.BlockSpec((n, tr, c), lambda i: (0, i, 0))],
                          out_specs=pl.BlockSpec((tr, c), lambda i: (i, 0)),
                          out_shape=jax.ShapeDtypeStruct((r, c), F32),
                          compiler_params=_params("parallel"))(a)


def _adamw(w, g, m, v):
    r, c = w.shape
    tr = _row_tile(r, c, 1 << 20)

    def body(w_ref, g_ref, m_ref, v_ref, d_ref, nm_ref, nv_ref):
        gg = g_ref[...]
        nm = ADAM_B1 * m_ref[...] + (1.0 - ADAM_B1) * gg
        nv = ADAM_B2 * v_ref[...] + (1.0 - ADAM_B2) * (gg * gg)
        m_hat = nm / (1.0 - ADAM_B1 ** ADAM_STEP)
        v_hat = nv / (1.0 - ADAM_B2 ** ADAM_STEP)
        d_ref[...] = -ADAM_LR * (m_hat / (jnp.sqrt(v_hat) + ADAM_EPS) + ADAM_WD * w_ref[...])
        nm_ref[...] = nm
        nv_ref[...] = nv

    spec = pl.BlockSpec((tr, c), lambda i: (i, 0))
    sh = jax.ShapeDtypeStruct((r, c), F32)
    return pl.pallas_call(body, name="adamw", grid=(r // tr,), in_specs=[spec] * 4, out_specs=[spec] * 3,
                          out_shape=[sh, sh, sh], compiler_params=_params("parallel"))(w, g, m, v)


def _proj_fwd(x, w):
    s, d = x.shape
    n = w.shape[1]
    tm, tn = min(s, 1024), _col_tile(n, 1024)

    def body(x_ref, w_ref, o_ref):
        o_ref[...] = _dot(x_ref[...].astype(BF16), w_ref[...]).astype(BF16)

    return pl.pallas_call(body, name="proj_fwd", grid=(s // tm, n // tn),
                          in_specs=[pl.BlockSpec((tm, d), lambda i, j: (i, 0)),
                                    pl.BlockSpec((d, tn), lambda i, j: (0, j))],
                          out_specs=pl.BlockSpec((tm, tn), lambda i, j: (i, j)),
                          out_shape=jax.ShapeDtypeStruct((s, n), BF16),
                          compiler_params=_params("parallel", "parallel"))(x, w)


def _matmul_tn(a, b, rider=None):
    s, m = a.shape
    n = b.shape[1]
    tk, tn = min(s, 1024), _col_tile(n, 1024)
    nn, nk = n // tn, s // tk

    def body(*refs):
        j, k = pl.program_id(0), pl.program_id(1)
        first = jnp.logical_and(j == 0, k == 0)
        last = jnp.logical_and(j == nn - 1, k == nk - 1)
        (a_ref, b_ref), (o_ref,), _, start, finish = _ride(rider, refs, 2, 1, first, last)
        start()

        @pl.when(k == 0)
        def _():
            o_ref[...] = jnp.zeros_like(o_ref)

        o_ref[...] += _dot_tn(a_ref[...].astype(BF16), b_ref[...].astype(BF16))
        finish()

    r_in, r_out, r_shapes, r_sems, r_alias = _rider_call_args(rider, 2, 1)
    outs = pl.pallas_call(body, name="matmul_tn", grid=(nn, nk),
                          in_specs=[pl.BlockSpec((tk, m), lambda j, k: (k, 0)),
                                    pl.BlockSpec((tk, tn), lambda j, k: (k, j))] + r_in,
                          out_specs=[pl.BlockSpec((m, tn), lambda j, k: (0, j))] + r_out,
                          out_shape=[jax.ShapeDtypeStruct((m, n), F32)] + r_shapes, scratch_shapes=r_sems,
                          input_output_aliases=r_alias,
                          compiler_params=_params("arbitrary", "arbitrary"))(a, b, *(rider.operands if rider else []))
    return outs[0] if rider is None else (outs[0], outs[1:])


def _dx_matmul(dxr, dproj, w, rider=None):
    s, d = dxr.shape
    n = w.shape[1]
    tm, tk = min(s, 1024), _col_tile(n, 1024)
    nm, nk = s // tm, n // tk

    def body(*refs):
        i, k = pl.program_id(0), pl.program_id(1)
        first = jnp.logical_and(i == 0, k == 0)
        last = jnp.logical_and(i == nm - 1, k == nk - 1)
        (r_ref, g_ref, w_ref), (o_ref,), _, start, finish = _ride(rider, refs, 3, 1, first, last)
        start()

        @pl.when(k == 0)
        def _():
            o_ref[...] = r_ref[...]

        o_ref[...] += _dot_nt(g_ref[...], w_ref[...])
        finish()

    r_in, r_out, r_shapes, r_sems, r_alias = _rider_call_args(rider, 3, 1)
    outs = pl.pallas_call(body, name="dx_matmul", grid=(nm, nk),
                          in_specs=[pl.BlockSpec((tm, d), lambda i, k: (i, 0)),
                                    pl.BlockSpec((tm, tk), lambda i, k: (i, k)),
                                    pl.BlockSpec((d, tk), lambda i, k: (0, k))] + r_in,
                          out_specs=[pl.BlockSpec((tm, d), lambda i, k: (i, 0))] + r_out,
                          out_shape=[jax.ShapeDtypeStruct((s, d), F32)] + r_shapes, scratch_shapes=r_sems,
                          input_output_aliases=r_alias,
                          compiler_params=_params("arbitrary", "arbitrary"))(dxr, dproj, w,
                                                                             *(rider.operands if rider else []))
    return outs[0], outs[1:]


def _mix_chunks(ws_ref, src_ref, dst_ref, bias_ref, t, groups, chunk):
    for c in range(t // chunk):
        rows = slice(c * chunk, (c + 1) * chunk)
        for g in range(groups):
            cols = slice(g * LANES, (g + 1) * LANES)
            val = _dot(ws_ref[g], src_ref[rows, cols])
            if bias_ref is not None:
                val = val + bias_ref[:, g:g + 1]
            dst_ref[rows, cols] = val


def _gmlp_fwd(proj, vn_g, vn_b, ws_m, bs_t):
    s = proj.shape[0]
    d = proj.shape[1] // N_IN
    groups, chunk = ws_m.shape[0], ws_m.shape[1]
    t = min(s, 512)

    def body(u_ref, v_ref, ga_ref, g_ref, b_ref, ws_ref, bs_ref, o_ref, vn_ref, mix_ref):
        xhat, _ = _ln_stats(v_ref[...].astype(F32))
        vn_ref[...] = (xhat * g_ref[...] + b_ref[...]).astype(BF16)
        _mix_chunks(ws_ref, vn_ref, mix_ref, bs_ref, t, groups, chunk)
        ga = ga_ref[...].astype(F32)
        o_ref[...] = (u_ref[...].astype(F32) * mix_ref[...] * (ga * _sigmoid(ga))).astype(BF16)

    col = lambda j: pl.BlockSpec((t, d), lambda i: (i, j))
    full = lambda a: pl.BlockSpec(a.shape, lambda i: (0,) * a.ndim)
    return pl.pallas_call(body, name="gmlp_fwd", grid=(s // t,),
                          in_specs=[col(0), col(1), col(2), full(vn_g), full(vn_b), full(ws_m), full(bs_t)],
                          out_specs=pl.BlockSpec((t, d), lambda i: (i, 0)),
                          out_shape=jax.ShapeDtypeStruct((s, d), BF16),
                          scratch_shapes=[pltpu.VMEM((t, d), BF16), pltpu.VMEM((t, d), F32)],
                          compiler_params=_params("parallel"))(proj, proj, proj, vn_g, vn_b, ws_m, bs_t)


def _gmlp_bwd(dproj, proj, dya, dq, dk, dv, vn_g, vn_b, ws_m, ws_mt, bs_t):
    s = proj.shape[0]
    d = proj.shape[1] // N_IN
    groups, chunk = ws_m.shape[0], ws_m.shape[1]
    t = min(s, 256)
    nsteps = s // t

    def body(dproj_hbm, u_ref, v_ref, ga_ref, dya_ref, dq_ref, dk_ref, dv_ref, g_ref, b_ref, ws_ref, wst_ref, bs_ref,
             o_ref, dg_ref, db_ref, dws_ref, dbs_ref,
             vn_ref, mix_ref, dm_ref, dvn_ref, dbs_acc, dg_acc, db_acc):
        del dproj_hbm
        i = pl.program_id(0)

        @pl.when(i == 0)
        def _():
            dws_ref[...] = jnp.zeros_like(dws_ref)
            dbs_acc[...] = jnp.zeros_like(dbs_acc)
            dg_acc[...] = jnp.zeros_like(dg_acc)
            db_acc[...] = jnp.zeros_like(db_acc)

        xhat, rstd = _ln_stats(v_ref[...].astype(F32))
        vn_ref[...] = (xhat * g_ref[...] + b_ref[...]).astype(BF16)
        _mix_chunks(ws_ref, vn_ref, mix_ref, bs_ref, t, groups, chunk)
        ga = ga_ref[...].astype(F32)
        sg = _sigmoid(ga)
        silu = ga * sg
        dsilu = sg * (1.0 + ga * (1.0 - sg))
        u = u_ref[...].astype(F32)
        dya_f = dya_ref[...].astype(F32)
        mix = mix_ref[...]
        o_ref[:, 0:d] = (dya_f * mix * silu).astype(BF16)
        o_ref[:, 2 * d:3 * d] = (dya_f * u * mix * dsilu).astype(BF16)
        dmix = dya_f * u * silu
        dm_ref[...] = dmix.astype(BF16)
        for c in range(t // chunk):
            dbs_acc[...] += dmix[c * chunk:(c + 1) * chunk, :]
        _mix_chunks(wst_ref, dm_ref, dvn_ref, None, t, groups, chunk)
        for c in range(t // chunk):
            rows = slice(c * chunk, (c + 1) * chunk)
            for g in range(groups):
                cols = slice(g * LANES, (g + 1) * LANES)
                dws_ref[g] += _dot_nt(dm_ref[rows, cols], vn_ref[rows, cols])
        dvn = dvn_ref[...]
        dg_acc[...] += _sum_rows8(dvn * xhat)
        db_acc[...] += _sum_rows8(dvn)
        o_ref[:, d:2 * d] = _ln_bwd(dvn * g_ref[...], xhat, rstd).astype(BF16)
        o_ref[:, 3 * d:4 * d] = dq_ref[...]
        o_ref[:, 4 * d:5 * d] = dk_ref[...]
        o_ref[:, 5 * d:6 * d] = dv_ref[...]

        @pl.when(i == nsteps - 1)
        def _():
            row = lax.broadcasted_iota(jnp.int32, (chunk, chunk), 0)
            col = lax.broadcasted_iota(jnp.int32, (chunk, chunk), 1)
            for g in range(groups):
                dws_ref[g] = jnp.where(col <= row, dws_ref[g], 0.0)
            lane = lax.broadcasted_iota(jnp.int32, (chunk, LANES), 1)
            res = jnp.zeros((chunk, LANES), F32)
            for g in range(groups):
                tot = jnp.sum(dbs_acc[:, g * LANES:(g + 1) * LANES], axis=1, keepdims=True)
                res = jnp.where(lane == g, tot, res)
            dbs_ref[...] = res
            dg_ref[...] = jnp.sum(dg_acc[...], axis=0, keepdims=True)
            db_ref[...] = jnp.sum(db_acc[...], axis=0, keepdims=True)

    col = lambda j: pl.BlockSpec((t, d), lambda i: (i, j))
    tok = pl.BlockSpec((t, d), lambda i: (i, 0))
    full = lambda a: pl.BlockSpec(a.shape, lambda i: (0,) * a.ndim)
    vec = jax.ShapeDtypeStruct((1, d), F32)
    outs = pl.pallas_call(
        body, name="gmlp_bwd", grid=(nsteps,),
        in_specs=[ANY, col(0), col(1), col(2), tok, tok, tok, tok,
                  full(vn_g), full(vn_b), full(ws_m), full(ws_mt), full(bs_t)],
        out_specs=[pl.BlockSpec((t, 6 * d), lambda i: (i, 0)),
                   pl.BlockSpec((1, d), lambda i: (0, 0)), pl.BlockSpec((1, d), lambda i: (0, 0)),
                   pl.BlockSpec((groups, chunk, chunk), lambda i: (0, 0, 0)),
                   pl.BlockSpec((chunk, LANES), lambda i: (0, 0))],
        out_shape=[jax.ShapeDtypeStruct(dproj.shape, BF16), vec, vec,
                   jax.ShapeDtypeStruct((groups, chunk, chunk), F32),
                   jax.ShapeDtypeStruct((chunk, LANES), F32)],
        scratch_shapes=[pltpu.VMEM((t, d), BF16), pltpu.VMEM((t, d), F32), pltpu.VMEM((t, d), BF16),
                        pltpu.VMEM((t, d), F32), pltpu.VMEM((chunk, d), F32),
                        pltpu.VMEM((SUBLANES, d), F32), pltpu.VMEM((SUBLANES, d), F32)],
        input_output_aliases={0: 0},
        compiler_params=_params("arbitrary"))(dproj, proj, proj, proj, dya, dq, dk, dv,
                                              vn_g, vn_b, ws_m, ws_mt, bs_t)
    return outs


ATTN_TILE = 256
EXP_UNDERFLOW = -104.0


def _log_sigmoid(z):
    return jnp.minimum(z, 0.0) - jnp.log(1.0 + jnp.exp(-jnp.abs(z)))


def _suffix_rhs():
    row = lax.broadcasted_iota(jnp.int32, (LANES, LANES), 0)
    col = lax.broadcasted_iota(jnp.int32, (LANES, LANES), 1)
    rhs = jnp.concatenate([(row > col).astype(BF16), jnp.ones((LANES, LANES), BF16)], axis=1)
    return jnp.concatenate([rhs, rhs], axis=0)


def _suffix_sums(a, rhs_ref, t):
    hi = a.astype(BF16)
    lo = (a - hi.astype(F32)).astype(BF16)
    n = t // LANES
    inside, totals = [], []
    for c in range(n):
        cols = slice(c * LANES, (c + 1) * LANES)
        res = _dot(jnp.concatenate([hi[:, cols], lo[:, cols]], axis=1), rhs_ref[...])
        inside.append(res[:, :LANES])
        totals.append(res[:, LANES:])
    later = totals[n - 1]
    for c in reversed(range(n - 1)):
        inside[c] = inside[c] + later
        later = later + totals[c]
    return jnp.concatenate(inside, axis=1), later


def _lanes_to_tile(a, t):
    return jnp.concatenate([a] * (t // LANES), axis=1)


def _sweep_tiles(tiles, i, tq, keep_sum_ref):
    half = tq // 2

    def live(rows):
        return jnp.max(jnp.maximum(keep_sum_ref[0, rows], keep_sum_ref[1, rows])) >= EXP_UNDERFLOW

    @pl.when(i == 0)
    def _():
        tiles([(i, True, 0, tq)])

    @pl.when(i > 0)
    def _():
        tiles([(i, True, 0, tq), (i - 1, False, 0, half)])
        pl.when(live(slice(half, tq)))(lambda: tiles([(i - 1, False, half, half)]))

        def cond(carry):
            n, alive = carry
            return jnp.logical_and(n < i - 1, alive)

        def step(carry):
            n, _ = carry
            tiles([(i - 2 - n, False, 0, tq)])
            return n + 1, live(slice(0, tq))

        lax.while_loop(cond, step, (jnp.int32(0), live(slice(0, tq))))


def _attn_masks(t):
    lane = lax.broadcasted_iota(jnp.int32, (t, LANES), 1)
    row = lax.broadcasted_iota(jnp.int32, (t, t), 0)
    col = lax.broadcasted_iota(jnp.int32, (t, t), 1)
    return lane < HEAD_DIM, col < row


def _split_heads(a, head0):
    zero = jnp.zeros_like(a)
    return [jnp.where(head0, a, zero), jnp.where(head0, zero, a)]


def _attn_fwd(proj, rider=None):
    s = proj.shape[0]
    d = proj.shape[1] // N_IN
    hp = d // LANES
    tq = min(s, ATTN_TILE)
    nq = s // tq
    scale = HEAD_DIM ** -0.5
    assert math.log2(scale).is_integer()

    def body(*refs):
        h_id, i = pl.program_id(0), pl.program_id(1)
        first = jnp.logical_and(h_id == 0, i == 0)
        last = jnp.logical_and(h_id == hp - 1, i == nq - 1)
        (q_ref, k_ref, v_ref, rhs_ref), (o_ref,), (acc_ref, r_ref), start, finish = _ride(rider, refs, 4, 1, first, last)
        start()
        head0, strict = _attn_masks(tq)
        qm = _split_heads((q_ref[...].astype(F32) * scale).astype(BF16), head0)
        acc_ref[...] = jnp.zeros_like(acc_ref)
        r_ref[...] = jnp.zeros_like(r_ref)

        def tiles(specs):
            offs = [pl.multiple_of(spec[0] * tq, tq) for spec in specs]
            k2 = [k_ref[pl.ds(off, tq), :] for off in offs]
            v2 = [v_ref[pl.ds(off, tq), :] for off in offs]
            rows = [slice(r0, r0 + nr) for _, _, r0, nr in specs]
            chains = [(t, h) for t in range(len(specs)) for h in range(HEADS_PER_BLOCK)]
            masked = lambda c: specs[c[0]][1]
            z = {c: _dot_nt(qm[c[1]][rows[c[0]]], k2[c[0]]) for c in chains}
            lsz = {c: _log_sigmoid(z[c]) for c in chains}
            keep = {c: lsz[c] - z[c] for c in chains}
            keep = {c: jnp.where(strict[rows[c[0]]], keep[c], 0.0) if masked(c) else keep[c] for c in chains}
            sums = {c: _suffix_sums(keep[c], rhs_ref, tq) for c in chains}
            for h in range(HEADS_PER_BLOCK):
                for t in range(len(specs)):
                    w = jnp.exp(lsz[t, h] + sums[t, h][0] + _lanes_to_tile(r_ref[h, rows[t]], tq))
                    if masked((t, h)):
                        w = jnp.where(strict[rows[t]], w, 0.0)
                    acc_ref[h, rows[t]] += _dot(w.astype(BF16), v2[t])
                    r_ref[h, rows[t]] += sums[t, h][1]

        _sweep_tiles(tiles, i, tq, r_ref)

        o_ref[...] = jnp.where(head0, acc_ref[0], acc_ref[1])
        finish()

    rhs = _suffix_rhs()
    r_in, r_out, r_shapes, r_sems, r_alias = _rider_call_args(rider, 4, 1)
    outs = pl.pallas_call(
        body, name="attn_fwd", grid=(hp, nq),
        in_specs=[pl.BlockSpec((tq, LANES), lambda h, i: (i, 3 * hp + h)),
                  pl.BlockSpec((s, LANES), lambda h, i: (0, 4 * hp + h)),
                  pl.BlockSpec((s, LANES), lambda h, i: (0, 5 * hp + h)),
                  pl.BlockSpec(rhs.shape, lambda h, i: (0, 0))] + r_in,
        out_specs=[pl.BlockSpec((tq, LANES), lambda h, i: (i, h))] + r_out,
        out_shape=[jax.ShapeDtypeStruct((s, d), F32)] + r_shapes,
        scratch_shapes=[pltpu.VMEM((HEADS_PER_BLOCK, tq, LANES), F32),
                        pltpu.VMEM((HEADS_PER_BLOCK, tq, LANES), F32)] + r_sems,
        input_output_aliases=r_alias,
        compiler_params=_params("arbitrary", "arbitrary"))(proj, proj, proj, rhs, *(rider.operands if rider else []))
    return outs[0], outs[1:]


def _attn_bwd(proj, o, do, rider=None):
    s = proj.shape[0]
    d = proj.shape[1] // N_IN
    hp = d // LANES
    tq = min(s, ATTN_TILE)
    nq = s // tq
    scale = HEAD_DIM ** -0.5

    def body(*refs):
        h_id, i = pl.program_id(0), pl.program_id(1)
        first = jnp.logical_and(h_id == 0, i == 0)
        last = jnp.logical_and(h_id == hp - 1, i == nq - 1)
        ((q_ref, k_ref, v_ref, o_ref, do_ref, rhs_ref), (dq_ref, dk_ref, dv_ref),
         (dq_acc, dk_acc, dv_acc, rk_ref, rg_ref), start, finish) = _ride(rider, refs, 6, 3, first, last)
        start()
        head0, strict = _attn_masks(tq)
        qm = _split_heads((q_ref[...].astype(F32) * scale).astype(BF16), head0)
        dom = _split_heads(do_ref[...], head0)
        prod = do_ref[...].astype(F32) * o_ref[...]
        delta = [jnp.sum(jnp.where(head0, prod, 0.0), axis=1, keepdims=True),
                 jnp.sum(jnp.where(head0, 0.0, prod), axis=1, keepdims=True)]

        @pl.when(i == 0)
        def _():
            dk_acc[...] = jnp.zeros_like(dk_acc)
            dv_acc[...] = jnp.zeros_like(dv_acc)

        dq_acc[...] = jnp.zeros_like(dq_acc)
        rk_ref[...] = jnp.zeros_like(rk_ref)
        for h in range(HEADS_PER_BLOCK):
            rg_ref[h] = jnp.broadcast_to(delta[h], (tq, LANES))

        def tiles(specs):
            n_t = len(specs)
            offs = [pl.multiple_of(spec[0] * tq, tq) for spec in specs]
            k2 = [k_ref[pl.ds(off, tq), :] for off in offs]
            v2 = [v_ref[pl.ds(off, tq), :] for off in offs]
            rows = [slice(r0, r0 + nr) for _, _, r0, nr in specs]
            chains = [(t, h) for t in range(n_t) for h in range(HEADS_PER_BLOCK)]
            masked = lambda c: specs[c[0]][1]
            z = {c: _dot_nt(qm[c[1]][rows[c[0]]], k2[c[0]]) for c in chains}
            dw = {c: _dot_nt(dom[c[1]][rows[c[0]]], v2[c[0]]) for c in chains}
            lsz = {c: _log_sigmoid(z[c]) for c in chains}
            keep = {c: lsz[c] - z[c] for c in chains}
            keep = {c: jnp.where(strict[rows[c[0]]], keep[c], 0.0) if masked(c) else keep[c] for c in chains}
            ksum = {c: _suffix_sums(keep[c], rhs_ref, tq) for c in chains}
            wb, g = {}, {}
            for h in range(HEADS_PER_BLOCK):
                for t in range(n_t):
                    w = jnp.exp(lsz[t, h] + ksum[t, h][0] + _lanes_to_tile(rk_ref[h, rows[t]], tq))
                    if masked((t, h)):
                        w = jnp.where(strict[rows[t]], w, 0.0)
                    wb[t, h] = w.astype(BF16)
                    g[t, h] = dw[t, h] * wb[t, h].astype(F32)
                    rk_ref[h, rows[t]] += ksum[t, h][1]
            gsum = {c: _suffix_sums(g[c], rhs_ref, tq) for c in chains}
            dzb = {}
            for h in range(HEADS_PER_BLOCK):
                for t in range(n_t):
                    dz = g[t, h] - jnp.exp(lsz[t, h]) * (_lanes_to_tile(rg_ref[h, rows[t]], tq) - gsum[t, h][0])
                    if masked((t, h)):
                        dz = jnp.where(strict[rows[t]], dz, 0.0)
                    dzb[t, h] = dz.astype(BF16)
                    dq_acc[h, rows[t]] += _dot(dzb[t, h], k2[t])
                    rg_ref[h, rows[t]] -= gsum[t, h][1]
            for t in range(n_t):
                dk_acc[pl.ds(offs[t], tq), :] += (_dot_tn(dzb[t, 0], qm[0][rows[t]])
                                                  + _dot_tn(dzb[t, 1], qm[1][rows[t]]))
                dv_acc[pl.ds(offs[t], tq), :] += (_dot_tn(wb[t, 0], dom[0][rows[t]])
                                                  + _dot_tn(wb[t, 1], dom[1][rows[t]]))

        _sweep_tiles(tiles, i, tq, rk_ref)

        dq_ref[...] = (jnp.where(head0, dq_acc[0], dq_acc[1]) * scale).astype(BF16)

        @pl.when(i == nq - 1)
        def _():
            dk_ref[...] = dk_acc[...].astype(BF16)
            dv_ref[...] = dv_acc[...].astype(BF16)

        finish()

    blk = pl.BlockSpec((tq, LANES), lambda h, i: (i, h))
    seq = pl.BlockSpec((s, LANES), lambda h, i: (0, h))
    sh = jax.ShapeDtypeStruct((s, d), BF16)
    rhs = _suffix_rhs()
    r_in, r_out, r_shapes, r_sems, r_alias = _rider_call_args(rider, 6, 3)
    outs = pl.pallas_call(
        body, name="attn_bwd", grid=(hp, nq),
        in_specs=[pl.BlockSpec((tq, LANES), lambda h, i: (i, 3 * hp + h)),
                  pl.BlockSpec((s, LANES), lambda h, i: (0, 4 * hp + h)),
                  pl.BlockSpec((s, LANES), lambda h, i: (0, 5 * hp + h)),
                  blk, blk, pl.BlockSpec(rhs.shape, lambda h, i: (0, 0))] + r_in,
        out_specs=[blk, seq, seq] + r_out, out_shape=[sh, sh, sh] + r_shapes,
        scratch_shapes=[pltpu.VMEM((HEADS_PER_BLOCK, tq, LANES), F32),
                        pltpu.VMEM((s, LANES), F32), pltpu.VMEM((s, LANES), F32),
                        pltpu.VMEM((HEADS_PER_BLOCK, tq, LANES), F32),
                        pltpu.VMEM((HEADS_PER_BLOCK, tq, LANES), F32)] + r_sems,
        input_output_aliases=r_alias,
        compiler_params=_params("arbitrary", "arbitrary"))(proj, proj, proj, o, do, rhs,
                                                           *(rider.operands if rider else []))
    return outs[:3], outs[3:]


def _post_math(ya_ref, o_ref, gb_ref, ma_ref, mb_ref, x_ref, p_ref, wpa_ref, wpb_ref, wout_ref, wpe_ref, wpg_ref,
               alpha):
    gb = gb_ref[...].astype(F32)
    sgb = _sigmoid(gb)
    o = o_ref[...]
    yb = (o * (gb * sgb)).astype(BF16)
    pa = _dot(ya_ref[...], wpa_ref[...])
    pb = _dot(yb, wpb_ref[...])
    sa = _sigmoid(ma_ref[...].astype(F32))
    sb = _sigmoid(mb_ref[...].astype(F32))
    merged = (sa * pa + sb * pb).astype(BF16)
    h1 = alpha * x_ref[...] + _dot(merged, wout_ref[...])
    h1b = h1.astype(BF16)
    e = _dot(p_ref[...].astype(BF16), wpe_ref[...])
    sg = _sigmoid(_dot(h1b, wpg_ref[...]))
    h2 = h1 + e * sg
    return dict(gb=gb, sgb=sgb, o=o, yb=yb, pa=pa, pb=pb, sa=sa, sb=sb, merged=merged, h1b=h1b, e=e, sg=sg, h2=h2)


def _post_specs(tm, d, ple, weights):
    tok = pl.BlockSpec((tm, d), lambda i: (i, 0))
    col = lambda j: pl.BlockSpec((tm, d), lambda i: (i, j))
    full = lambda a: pl.BlockSpec(a.shape, lambda i: (0,) * a.ndim, pipeline_mode=pl.Buffered(1))
    return tok, [tok, tok, col(6), col(7), col(8), tok, pl.BlockSpec((tm, ple), lambda i: (i, 0))] + [
        full(w) for w in weights]


def _post_fwd(ya, o, proj, x, p, w_pa, w_pb, w_out, w_pe, w_pg, ln_g, ln_b, alpha, rider=None):
    s, d = x.shape
    ple = p.shape[1]
    tm = min(s, 256)
    nsteps = s // tm
    weights = (w_pa, w_pb, w_out, w_pe, w_pg, ln_g, ln_b)

    def body(*refs):
        i = pl.program_id(0)
        ins, (out_ref,), _, start, finish = _ride(rider, refs, 14, 1, i == 0, i == nsteps - 1)
        start()
        f = _post_math(*ins[:12], alpha)
        xhat, _ = _ln_stats(f["h2"])
        out_ref[...] = xhat * ins[12][...] + ins[13][...]
        finish()

    tok, in_specs = _post_specs(tm, d, ple, weights)
    r_in, r_out, r_shapes, r_sems, r_alias = _rider_call_args(rider, 14, 1)
    outs = pl.pallas_call(body, name="post_fwd", grid=(nsteps,), in_specs=in_specs + r_in, out_specs=[tok] + r_out,
                          out_shape=[jax.ShapeDtypeStruct((s, d), F32)] + r_shapes, scratch_shapes=r_sems,
                          input_output_aliases=r_alias,
                          compiler_params=_params("arbitrary"))(ya, o, proj, proj, proj, x, p, *weights,
                                                                *(rider.operands if rider else []))
    return outs[0], outs[1:]


def _post_bwd(dxo, ya, o, proj, x, p, w_pa, w_pb, w_out, w_pe, w_pg, ln_g, ln_b, alpha):
    s, d = x.shape
    ple = p.shape[1]
    tm = min(s, 256)
    nsteps = s // tm
    weights = (w_pa, w_pb, w_out, w_pe, w_pg, ln_g, ln_b)

    def body(dxo_ref, ya_ref, o_ref, gb_ref, ma_ref, mb_ref, x_ref, p_ref, wpa_ref, wpb_ref, wout_ref, wpe_ref,
             wpg_ref, g_ref, b_ref,
             dproj_ref, dxr_ref, dya_ref, do_ref, de_ref, h1_ref, dzg_ref, mrg_ref, dh1_ref, dpa_ref, yb_ref, dpb_ref,
             dg_ref, db_ref, dg_acc, db_acc):
        i = pl.program_id(0)

        @pl.when(i == 0)
        def _():
            dg_acc[...] = jnp.zeros_like(dg_acc)
            db_acc[...] = jnp.zeros_like(db_acc)

        f = _post_math(ya_ref, o_ref, gb_ref, ma_ref, mb_ref, x_ref, p_ref, wpa_ref, wpb_ref, wout_ref, wpe_ref,
                       wpg_ref, alpha)
        xhat, rstd = _ln_stats(f["h2"])
        dxo = dxo_ref[...]
        dg_acc[...] += _sum_rows8(dxo * xhat)
        db_acc[...] += _sum_rows8(dxo)
        dh2 = _ln_bwd(dxo * g_ref[...], xhat, rstd)
        sg, e = f["sg"], f["e"]
        de_ref[...] = (dh2 * sg).astype(BF16)
        dzg = (dh2 * e * sg * (1.0 - sg)).astype(BF16)
        dzg_ref[...] = dzg
        dh1 = dh2 + _dot_nt(dzg, wpg_ref[...])
        dh1b = dh1.astype(BF16)
        dxr_ref[...] = alpha * dh1
        dh1_ref[...] = dh1b
        h1_ref[...] = f["h1b"]
        mrg_ref[...] = f["merged"]
        yb_ref[...] = f["yb"]
        dmerged = _dot_nt(dh1b, wout_ref[...])
        sa, sb = f["sa"], f["sb"]
        dpa = (dmerged * sa).astype(BF16)
        dpb = (dmerged * sb).astype(BF16)
        dpa_ref[...] = dpa
        dpb_ref[...] = dpb
        dproj_ref[:, d:2 * d] = (dmerged * f["pa"] * sa * (1.0 - sa)).astype(BF16)
        dproj_ref[:, 2 * d:3 * d] = (dmerged * f["pb"] * sb * (1.0 - sb)).astype(BF16)
        dya_ref[...] = _dot_nt(dpa, wpa_ref[...]).astype(BF16)
        dyb = _dot_nt(dpb, wpb_ref[...])
        gb, sgb = f["gb"], f["sgb"]
        do_ref[...] = (dyb * (gb * sgb)).astype(BF16)
        dproj_ref[:, 0:d] = (dyb * f["o"] * (sgb * (1.0 + gb * (1.0 - sgb)))).astype(BF16)

        @pl.when(i == nsteps - 1)
        def _():
            dg_ref[...] = jnp.sum(dg_acc[...], axis=0, keepdims=True)
            db_ref[...] = jnp.sum(db_acc[...], axis=0, keepdims=True)

    tok, in_specs = _post_specs(tm, d, ple, weights)
    vec_spec = pl.BlockSpec((1, d), lambda i: (0, 0))
    vec = jax.ShapeDtypeStruct((1, d), F32)
    act = jax.ShapeDtypeStruct((s, d), BF16)
    return pl.pallas_call(
        body, name="post_bwd", grid=(nsteps,), in_specs=[tok] + in_specs,
        out_specs=[pl.BlockSpec((tm, 3 * d), lambda i: (i, 2)), tok] + [tok] * 10 + [vec_spec, vec_spec],
        out_shape=[jax.ShapeDtypeStruct((s, N_IN * d), BF16), jax.ShapeDtypeStruct((s, d), F32)] + [act] * 10 + [vec, vec],
        scratch_shapes=[pltpu.VMEM((SUBLANES, d), F32), pltpu.VMEM((SUBLANES, d), F32)],
        compiler_params=_params("arbitrary"))(dxo, ya, o, proj, proj, proj, x, p, *weights)


def _loss_head(y, target):
    s, d = y.shape
    tm = min(s, 512)

    def body(y_ref, t_ref, dy_ref, l_ref):
        @pl.when(pl.program_id(0) == 0)
        def _():
            l_ref[...] = jnp.zeros_like(l_ref)

        err = y_ref[...] - t_ref[...]
        dy_ref[...] = err / d
        row = jnp.sum(err * err, axis=1, keepdims=True) / d
        l_ref[...] += 0.5 * jnp.sum(row, axis=0, keepdims=True)

    tok = pl.BlockSpec((tm, d), lambda i: (i, 0))
    return pl.pallas_call(body, name="loss_head", grid=(s // tm,), in_specs=[tok, tok],
                          out_specs=[tok, pl.BlockSpec((SUBLANES, LANES), lambda i: (0, 0))],
                          out_shape=[jax.ShapeDtypeStruct((s, d), F32),
                                     jax.ShapeDtypeStruct((SUBLANES, LANES), F32)],
                          compiler_params=_params("arbitrary"))(y, target)


def _position():
    x, y, c = lax.axis_index("x"), lax.axis_index("y"), lax.axis_index("c")
    chips = [(1 - x, y), (x, 1 - y), (1 - x, 1 - y)]
    return x, y, c, chips


def _shard_of(ref, col_sharded, j, n):
    off = pl.multiple_of(j * n, n)
    return ref.at[:, pl.ds(off, n)] if col_sharded else ref.at[pl.ds(off, n), :]


def _half_of(ref, col_sharded, h, n):
    off = pl.multiple_of(h * n, n)
    return ref.at[pl.ds(off, n), :] if col_sharded else ref.at[:, pl.ds(off, n)]


def _piece_of(ref, col_sharded, chip, n_block, half, n_half):
    block = pl.ds(pl.multiple_of(chip * n_block, n_block), n_block)
    part = pl.ds(pl.multiple_of(half * n_half, n_half), n_half)
    return ref.at[part, block] if col_sharded else ref.at[block, part]


class _Rider(NamedTuple):
    operands: list
    out_shapes: list
    aliases: dict
    n_sems: int
    start: Callable
    finish: Callable


def _rider_call_args(rider, n_in, n_out):
    if rider is None:
        return [], [], [], [], {}
    sems = [pltpu.SemaphoreType.DMA((rider.n_sems,))] * 2
    aliases = {n_in + i: n_out + o for i, o in rider.aliases.items()}
    return [ANY] * len(rider.operands), [ANY] * len(rider.out_shapes), list(rider.out_shapes), sems, aliases


def _ride(rider, refs, n_in, n_out, first, last):
    if rider is None:
        return refs[:n_in], refs[n_in:n_in + n_out], refs[n_in + n_out:], lambda: None, lambda: None
    r_in, r_out = len(rider.operands), len(rider.out_shapes)
    ins, rins = refs[:n_in], refs[n_in:n_in + r_in]
    outs = refs[n_in + r_in:n_in + r_in + n_out]
    routs = refs[n_in + r_in + n_out:n_in + r_in + n_out + r_out]
    scratch = refs[n_in + r_in + n_out + r_out:-2]
    send_sems, recv_sems = refs[-2:]

    def start():
        pl.when(first)(lambda: rider.start(rins, routs, send_sems, recv_sems))

    def finish():
        pl.when(last)(lambda: rider.finish(rins, routs, send_sems, recv_sems))

    return ins, outs, scratch, start, finish


def _gather_copies(outs, ici_sems, d2d_sems):
    x, y, c, chips = _position()
    my_chip = 2 * x + y
    sends, arrivals, passes, passed = [], [], [], []
    for a, out in enumerate(outs):
        cs = COL_SHARDED[BIG[a % len(BIG)]]
        rows, cols = out.shape
        n_block = (cols if cs else rows) // N_CHIPS
        n_half = (rows if cs else cols) // 2
        piece = lambda chip, half: _piece_of(out, cs, chip, n_block, half, n_half)
        for j, chip in enumerate(chips):
            k = a * 3 + j
            their = 2 * chip[0] + chip[1]
            if ici_sems is not None:
                send_sems, recv_sems = ici_sems
                sends.append(pltpu.make_async_remote_copy(
                    src_ref=piece(my_chip, c), dst_ref=piece(my_chip, c), send_sem=send_sems.at[k],
                    recv_sem=recv_sems.at[k], device_id=(chip[0], chip[1], c), device_id_type=MESH))
                arrivals.append(pltpu.make_async_remote_copy(
                    src_ref=piece(their, c), dst_ref=piece(their, c), send_sem=send_sems.at[k],
                    recv_sem=recv_sems.at[k], device_id=(chip[0], chip[1], c), device_id_type=MESH))
            if d2d_sems is not None:
                send_sems, recv_sems = d2d_sems
                passes.append(pltpu.make_async_remote_copy(
                    src_ref=piece(their, c), dst_ref=piece(their, c), send_sem=send_sems.at[k],
                    recv_sem=recv_sems.at[k], device_id=(x, y, 1 - c), device_id_type=MESH))
                passed.append(pltpu.make_async_remote_copy(
                    src_ref=piece(their, 1 - c), dst_ref=piece(their, 1 - c), send_sem=send_sems.at[k],
                    recv_sem=recv_sems.at[k], device_id=(x, y, 1 - c), device_id_type=MESH))
    return sends, arrivals, passes, passed


def _gather_weights(bufs):
    flat = [bufs[name] for name in BIG]
    n_arr = len(flat)

    def body(*refs):
        outs = refs[n_arr:2 * n_arr]
        sems = refs[2 * n_arr:]
        sends, arrivals, passes, passed = _gather_copies(outs, sems[:2], sems[2:])
        for cp in sends:
            cp.start()
        for arrival, onward in zip(arrivals, passes):
            arrival.wait_recv()
            onward.start()
        for cp in passed:
            cp.wait_recv()
        for cp in sends + passes:
            cp.wait_send()

    outs = pl.pallas_call(
        body, name="gather_weights", in_specs=[ANY] * n_arr, out_specs=[ANY] * n_arr,
        out_shape=[jax.ShapeDtypeStruct(a.shape, BF16) for a in flat],
        input_output_aliases={a: a for a in range(n_arr)},
        scratch_shapes=[pltpu.SemaphoreType.DMA((n_arr * 3,))] * 4,
    )(*flat)
    return dict(zip(BIG, outs))


def _gather_rider(bufs, over_ici):
    flat = [bufs[name] for name in BIG]

    def copies(routs, send_sems, recv_sems):
        sems = (send_sems, recv_sems)
        sends, arrivals, passes, passed = _gather_copies(routs, sems if over_ici else None, None if over_ici else sems)
        return (sends, arrivals) if over_ici else (passes, passed)

    def start(rins, routs, send_sems, recv_sems):
        for cp in copies(routs, send_sems, recv_sems)[0]:
            cp.start()

    def finish(rins, routs, send_sems, recv_sems):
        out, due = copies(routs, send_sems, recv_sems)
        for cp in due:
            cp.wait_recv()
        for cp in out:
            cp.wait_send()

    return _Rider(flat, [jax.ShapeDtypeStruct(a.shape, a.dtype) for a in flat], {i: i for i in range(len(flat))},
                  len(flat) * 3, start, finish)


def _half_shape(shape, col_sharded):
    r, c = shape
    return (r // 2, c) if col_sharded else (r, c // 2)


def _exchange_rider(grads):
    flat = [grads[name] for name in BIG]

    def copies(rins, routs, send_sems, recv_sems):
        x, y, c, _ = _position()
        out = []
        for a, name in enumerate(BIG):
            cs = COL_SHARDED[name]
            n = routs[a].shape[0] if cs else routs[a].shape[1]
            out.append(pltpu.make_async_remote_copy(
                src_ref=_half_of(rins[a], cs, 1 - c, n), dst_ref=routs[a], send_sem=send_sems.at[a],
                recv_sem=recv_sems.at[a], device_id=(x, y, 1 - c), device_id_type=MESH))
        return out

    def start(rins, routs, send_sems, recv_sems):
        for cp in copies(rins, routs, send_sems, recv_sems):
            cp.start()

    def finish(rins, routs, send_sems, recv_sems):
        cps = copies(rins, routs, send_sems, recv_sems)
        for cp in cps:
            cp.wait_recv()
        for cp in cps:
            cp.wait_send()

    return _Rider(flat, [jax.ShapeDtypeStruct(_half_shape(a.shape, COL_SHARDED[name]), F32)
                         for a, name in zip(flat, BIG)], {}, len(flat), start, finish)


def _scatter_to_owners(halves):
    rider = _scatter_rider(halves)

    def body(*refs):
        n = len(rider.operands)
        args = (refs[:n], refs[n:2 * n], *refs[2 * n:])
        rider.start(*args)
        rider.finish(*args)

    outs = pl.pallas_call(
        body, name="scatter_to_owners", in_specs=[ANY] * len(rider.operands), out_specs=[ANY] * len(rider.out_shapes),
        out_shape=rider.out_shapes, scratch_shapes=[pltpu.SemaphoreType.DMA((rider.n_sems,))] * 2,
    )(*rider.operands)
    return dict(zip(BIG, outs))


def _scatter_rider(halves):
    flat = [halves[name] for name in BIG]

    def slots_shape(a, cs):
        r, c = a.shape
        return (N_CHIPS - 1,) + ((r, c // N_CHIPS) if cs else (r // N_CHIPS, c))

    def copies(rins, routs, send_sems, recv_sems):
        x, y, c, chips = _position()
        out = []
        for a, name in enumerate(BIG):
            cs = COL_SHARDED[name]
            n = routs[a].shape[2] if cs else routs[a].shape[1]
            for j, chip in enumerate(chips):
                k = a * 3 + j
                out.append(pltpu.make_async_remote_copy(
                    src_ref=_shard_of(rins[a], cs, 2 * chip[0] + chip[1], n), dst_ref=routs[a].at[j],
                    send_sem=send_sems.at[k], recv_sem=recv_sems.at[k], device_id=(chip[0], chip[1], c),
                    device_id_type=MESH))
        return out

    def start(rins, routs, send_sems, recv_sems):
        for cp in copies(rins, routs, send_sems, recv_sems):
            cp.start()

    def finish(rins, routs, send_sems, recv_sems):
        cps = copies(rins, routs, send_sems, recv_sems)
        for cp in cps:
            cp.wait_recv()
        for cp in cps:
            cp.wait_send()

    return _Rider(flat, [jax.ShapeDtypeStruct(slots_shape(a, COL_SHARDED[name]), a.dtype) for a, name in zip(flat, BIG)],
                  {}, len(flat) * 3, start, finish)


def _join_halves(halves):
    flat = [halves[name] for name in BIG]
    n_w = len(flat)

    def body(*refs):
        outs = refs[n_w:2 * n_w]
        send_sems, recv_sems = refs[2 * n_w:]
        x, y, c, _ = _position()
        sends, recvs = [], []
        for w, name in enumerate(BIG):
            cs = COL_SHARDED[name]
            n = (outs[w].shape[1] if cs else outs[w].shape[2]) // 2

            def half(h):
                part = pl.ds(pl.multiple_of(h * n, n), n)
                return outs[w].at[:, part, :] if cs else outs[w].at[:, :, part]

            sends.append(pltpu.make_async_remote_copy(
                src_ref=half(c), dst_ref=half(c), send_sem=send_sems.at[w], recv_sem=recv_sems.at[w],
                device_id=(x, y, 1 - c), device_id_type=MESH))
            recvs.append(pltpu.make_async_remote_copy(
                src_ref=half(1 - c), dst_ref=half(1 - c), send_sem=send_sems.at[w], recv_sem=recv_sems.at[w],
                device_id=(x, y, 1 - c), device_id_type=MESH))
        for cp in sends:
            cp.start()
        for cp in recvs:
            cp.wait_recv()
        for cp in sends:
            cp.wait_send()

    outs = pl.pallas_call(
        body, name="join_halves", in_specs=[ANY] * n_w, out_specs=[ANY] * n_w,
        out_shape=[jax.ShapeDtypeStruct(a.shape, F32) for a in flat],
        input_output_aliases={w: w for w in range(n_w)},
        scratch_shapes=[pltpu.SemaphoreType.DMA((n_w,)), pltpu.SemaphoreType.DMA((n_w,))],
    )(*flat)
    return dict(zip(BIG, outs))


def _small_rider(packed):
    r, lanes = packed.shape

    def copies(rins, routs, send_sems, recv_sems):
        x, y, c, _ = _position()
        me = 4 * x + 2 * y + c
        local = pltpu.make_async_copy(rins[0], routs[0].at[me], send_sems.at[N_DEV - 1])
        sends, recvs = [], []
        for k in range(1, N_DEV):
            px, py, pc = x ^ (k >> 2), y ^ ((k >> 1) & 1), c ^ (k & 1)
            sends.append(pltpu.make_async_remote_copy(
                src_ref=rins[0], dst_ref=routs[0].at[me], send_sem=send_sems.at[k - 1], recv_sem=recv_sems.at[k - 1],
                device_id=(px, py, pc), device_id_type=MESH))
            recvs.append(pltpu.make_async_remote_copy(
                src_ref=rins[0], dst_ref=routs[0].at[4 * px + 2 * py + pc], send_sem=send_sems.at[k - 1],
                recv_sem=recv_sems.at[k - 1], device_id=(px, py, pc), device_id_type=MESH))
        return local, sends, recvs

    def start(rins, routs, send_sems, recv_sems):
        local, sends, _ = copies(rins, routs, send_sems, recv_sems)
        local.start()
        for cp in sends:
            cp.start()

    def finish(rins, routs, send_sems, recv_sems):
        local, sends, recvs = copies(rins, routs, send_sems, recv_sems)
        for cp in recvs:
            cp.wait_recv()
        for cp in sends:
            cp.wait_send()
        local.wait()

    return _Rider([packed], [jax.ShapeDtypeStruct((N_DEV, r, lanes), F32)], {}, N_DEV, start, finish)


def _pack_small(t):
    return jnp.concatenate([t[name].reshape(-1, LANES) for name in SMALL], axis=0)


def _unpack_small(packed, like):
    out, row = {}, 0
    for name in SMALL:
        n = like[name].size // LANES
        out[name] = packed[row:row + n].reshape(like[name].shape)
        row += n
    return out


def kernel(x, p, w_in, vn_g, vn_b, w_s, b_s, w_pa, w_pb, w_out, w_pe, w_pg, ln_g, ln_b, loss_target, m_w_in, m_vn_g, m_vn_b, m_w_s, m_b_s, m_w_pa, m_w_pb, m_w_out, m_w_pe, m_w_pg, m_ln_g, m_ln_b, v_w_in, v_vn_g, v_vn_b, v_w_s, v_b_s, v_w_pa, v_w_pb, v_w_out, v_w_pe, v_w_pg, v_ln_g, v_ln_b):
    weights = dict(w_in=w_in, vn_g=vn_g, vn_b=vn_b, w_s=w_s, b_s=b_s, w_pa=w_pa, w_pb=w_pb, w_out=w_out, w_pe=w_pe,
                   w_pg=w_pg, ln_g=ln_g, ln_b=ln_b)
    mom1 = dict(w_in=m_w_in, vn_g=m_vn_g, vn_b=m_vn_b, w_s=m_w_s, b_s=m_b_s, w_pa=m_w_pa, w_pb=m_w_pb, w_out=m_w_out,
                w_pe=m_w_pe, w_pg=m_w_pg, ln_g=m_ln_g, ln_b=m_ln_b)
    mom2 = dict(w_in=v_w_in, vn_g=v_vn_g, vn_b=v_vn_b, w_s=v_w_s, b_s=v_b_s, w_pa=v_w_pa, w_pb=v_w_pb, w_out=v_w_out,
                w_pe=v_w_pe, w_pg=v_w_pg, ln_g=v_ln_g, ln_b=v_ln_b)
    nl, d = vn_g.shape
    chunk = w_s.shape[2]
    assert chunk == LANES and w_s.shape[3] == LANES and d % LANES == 0
    alpha = (2 * nl) ** 0.25
    pos = tuple(lax.axis_index(a).astype(jnp.int32).reshape(1) for a in ("c", "x", "y"))

    placed = [{name: _cast_into_place(weights[name], l, pos, COL_SHARDED[name]) for name in BIG} for l in range(nl)]
    full = [_gather_weights(placed[0])] + [None] * (nl - 1)
    causal = jnp.tril(jnp.ones((chunk, chunk), dtype=bool))
    ws_m = jnp.where(causal, w_s, 0.0).astype(BF16)
    ws_mt = jnp.swapaxes(ws_m, 2, 3)
    bs_t = jnp.swapaxes(b_s, 1, 2)

    xs, projs, yas, os_ = [x[0]], [], [], []
    for l in range(nl):
        proj = _proj_fwd(xs[l], full[l]["w_in"])
        ya = _gmlp_fwd(proj, vn_g[l:l + 1], vn_b[l:l + 1], ws_m[l], bs_t[l])
        more = l + 1 < nl
        o, arrived = _attn_fwd(proj, _gather_rider(placed[l + 1], over_ici=True) if more else None)
        x_next, handed = _post_fwd(ya, o, proj, xs[l], p[l, 0], full[l]["w_pa"], full[l]["w_pb"], full[l]["w_out"],
                                   full[l]["w_pe"], full[l]["w_pg"], ln_g[l:l + 1], ln_b[l:l + 1], alpha,
                                   _gather_rider(dict(zip(BIG, arrived)), over_ici=False) if more else None)
        if more:
            full[l + 1] = dict(zip(BIG, handed))
        xs.append(x_next)
        projs.append(proj)
        yas.append(ya)
        os_.append(o)

    dx, loss_tile = _loss_head(xs[nl], loss_target[0])
    loss = lax.psum(loss_tile[0, 0], ("x", "y", "c"))

    big_grads, received, slots = [None] * nl, [None] * nl, [None] * nl
    chip_sums = None
    small_grads = [None] * nl
    for l in reversed(range(nl)):
        w = full[l]
        (dproj, dxr, dya, do, de, h1b, dzg, merged, dh1, dpa, yb, dpb, dln_g, dln_b) = _post_bwd(
            dx, yas[l], os_[l], projs[l], xs[l], p[l, 0], w["w_pa"], w["w_pb"], w["w_out"], w["w_pe"], w["w_pg"],
            ln_g[l:l + 1], ln_b[l:l + 1], alpha)
        (dq, dk, dv), scattered = _attn_bwd(projs[l], os_[l], do,
                                            _scatter_rider(chip_sums) if chip_sums is not None else None)
        if chip_sums is not None:
            slots[l + 1] = dict(zip(BIG, scattered))
        dproj, dvn_g, dvn_b, dw_s, dbs_cols = _gmlp_bwd(dproj, projs[l], dya, dq, dk, dv, vn_g[l:l + 1],
                                                         vn_b[l:l + 1], ws_m[l], ws_mt[l], bs_t[l])
        partial = _pack_small(dict(vn_g=dvn_g[0], vn_b=dvn_b[0], w_s=dw_s, b_s=dbs_cols[:, :b_s.shape[1]].T,
                                   ln_g=dln_g[0], ln_b=dln_b[0]))
        dw_in, (partials,) = _matmul_tn(xs[l], dproj, _small_rider(partial))
        small_grads[l] = _unpack_small(_sum_slots(partials), {name: weights[name][0] for name in SMALL})
        big_grads[l] = dict(w_in=dw_in, w_pa=_matmul_tn(yas[l], dpa), w_pb=_matmul_tn(yb, dpb),
                            w_out=_matmul_tn(merged, dh1), w_pe=_matmul_tn(p[l, 0], de), w_pg=_matmul_tn(h1b, dzg))
        dx, from_sibling = _dx_matmul(dxr, dproj, w["w_in"], _exchange_rider(big_grads[l]))
        received[l] = dict(zip(BIG, from_sibling))
        chip_sums = {name: _add_own_half(big_grads[l][name], received[l][name], pos, COL_SHARDED[name])
                     for name in BIG}
    slots[0] = _scatter_to_owners(chip_sums)
    reduced = {}
    for name in BIG:
        buf = None
        for l in range(nl):
            buf = _reduce_block(buf, big_grads[l][name], received[l][name], slots[l][name], l, nl, pos,
                                COL_SHARDED[name])
        reduced[name] = buf
    grads = _join_halves(reduced)

    small_like = {name: weights[name] for name in SMALL}
    grads.update({name: jnp.stack([small_grads[l][name] for l in range(nl)]) for name in SMALL})

    delta, new_m, new_v = {}, {}, {}
    for name in BIG:
        sh = weights[name].shape
        flat = lambda a: a.reshape(sh[0] * sh[1], sh[2])
        dl, nm, nv = _adamw(flat(weights[name]), flat(grads[name]), flat(mom1[name]), flat(mom2[name]))
        delta[name], new_m[name], new_v[name] = dl.reshape(sh), nm.reshape(sh), nv.reshape(sh)
    dl, nm, nv = _adamw(_pack_small(small_like), _pack_small({n: grads[n] for n in SMALL}),
                        _pack_small({n: mom1[n] for n in SMALL}), _pack_small({n: mom2[n] for n in SMALL}))
    delta.update(_unpack_small(dl, small_like))
    new_m.update(_unpack_small(nm, small_like))
    new_v.update(_unpack_small(nv, small_like))

    return (loss, dx[None], *[grads[n] for n in WEIGHTS], *[delta[n] for n in WEIGHTS],
            *[new_m[n] for n in WEIGHTS], *[new_v[n] for n in WEIGHTS])
```

```python
import math
from typing import Callable, NamedTuple

import jax
import jax.numpy as jnp
from jax import lax
from jax.experimental import pallas as pl
from jax.experimental.pallas import tpu as pltpu

F32 = jnp.float32
BF16 = jnp.bfloat16
LANES = 128
SUBLANES = 8
HEAD_DIM = 64
HEADS_PER_BLOCK = LANES // HEAD_DIM
LN_EPS = 1e-5
N_IN = 9
N_CHIPS = 4
N_DEV = 8
ADAM_LR = 0.001
ADAM_B1 = 0.9
ADAM_B2 = 0.999
ADAM_EPS = 1e-08
ADAM_WD = 0.01
ADAM_STEP = 10
MESH = pl.DeviceIdType.MESH
ANY = pl.BlockSpec(memory_space=pl.ANY)
BIG = ("w_in", "w_pa", "w_pb", "w_out", "w_pe", "w_pg")
COL_SHARDED = {"w_in": True, "w_pa": False, "w_pb": False, "w_out": False, "w_pe": True, "w_pg": False}
SMALL = ("vn_g", "vn_b", "w_s", "b_s", "ln_g", "ln_b")
WEIGHTS = ("w_in", "vn_g", "vn_b", "w_s", "b_s", "w_pa", "w_pb", "w_out", "w_pe", "w_pg", "ln_g", "ln_b")


def _params(*sem):
    return pltpu.CompilerParams(dimension_semantics=sem)


def _dot(a, b):
    return jnp.dot(a, b, preferred_element_type=F32)


def _dot_nt(a, b):
    return lax.dot_general(a, b, (((1,), (1,)), ((), ())), preferred_element_type=F32)


def _dot_tn(a, b):
    return lax.dot_general(a, b, (((0,), (0,)), ((), ())), preferred_element_type=F32)


def _sigmoid(a):
    return 1.0 / (1.0 + jnp.exp(-a))


def _row_tile(rows, cols, cap_bytes):
    best = None
    for t in range(16, rows + 1, 16):
        if rows % t == 0 and t * cols * 4 <= cap_bytes:
            best = t
    return best or rows


def _col_tile(cols, cap):
    best = LANES
    for t in range(LANES, min(cols, cap) + 1, LANES):
        if cols % t == 0:
            best = t
    return best


def _ln_stats(h):
    mu = jnp.mean(h, axis=-1, keepdims=True)
    hc = h - mu
    var = jnp.mean(hc * hc, axis=-1, keepdims=True)
    rstd = lax.rsqrt(var + LN_EPS)
    return hc * rstd, rstd


def _ln_bwd(dxhat, xhat, rstd):
    m1 = jnp.mean(dxhat, axis=-1, keepdims=True)
    m2 = jnp.mean(dxhat * xhat, axis=-1, keepdims=True)
    return rstd * (dxhat - m1 - xhat * m2)


def _sum_rows8(a):
    t, d = a.shape
    return jnp.sum(a.reshape(t // SUBLANES, SUBLANES, d), axis=0)


def _chip(x_ref, y_ref):
    return 2 * x_ref[0] + y_ref[0]


def _cast_into_place(shards, l, pos, col_sharded):
    _, r, c = shards.shape
    tr = _row_tile(r, c, 2 << 20)
    nb = r // tr

    def body(c_ref, x_ref, y_ref, a_ref, o_ref):
        o_ref[...] = a_ref[...].astype(BF16)

    if col_sharded:
        out_spec = pl.BlockSpec((tr, c), lambda i, c_ref, x_ref, y_ref: (i, _chip(x_ref, y_ref)))
        full = (r, c * N_CHIPS)
    else:
        out_spec = pl.BlockSpec((tr, c), lambda i, c_ref, x_ref, y_ref: (_chip(x_ref, y_ref) * nb + i, 0))
        full = (r * N_CHIPS, c)
    grid_spec = pltpu.PrefetchScalarGridSpec(
        num_scalar_prefetch=3, grid=(nb,),
        in_specs=[pl.BlockSpec((None, tr, c), lambda i, c_ref, x_ref, y_ref: (l, i, 0))], out_specs=out_spec)
    return pl.pallas_call(body, name="cast_into_place", grid_spec=grid_spec,
                          out_shape=jax.ShapeDtypeStruct(full, BF16),
                          compiler_params=_params("parallel"))(*pos, shards)


def _add_own_half(own, recv, pos, col_sharded):
    r, c = recv.shape
    tr = _row_tile(r, c, 2 << 20)
    nb = r // tr

    def body(c_ref, x_ref, y_ref, own_ref, recv_ref, o_ref):
        o_ref[...] = (own_ref[...] + recv_ref[...]).astype(BF16)

    if col_sharded:
        own_spec = pl.BlockSpec((tr, c), lambda i, c_ref, x_ref, y_ref: (c_ref[0] * nb + i, 0))
    else:
        own_spec = pl.BlockSpec((tr, c), lambda i, c_ref, x_ref, y_ref: (i, c_ref[0]))
    spec = pl.BlockSpec((tr, c), lambda i, c_ref, x_ref, y_ref: (i, 0))
    grid_spec = pltpu.PrefetchScalarGridSpec(num_scalar_prefetch=3, grid=(nb,), in_specs=[own_spec, spec],
                                             out_specs=spec)
    return pl.pallas_call(body, name="add_own_half", grid_spec=grid_spec,
                          out_shape=jax.ShapeDtypeStruct(recv.shape, BF16),
                          compiler_params=_params("parallel"))(*pos, own, recv)


def _reduce_block(buf, own, recv, slots, l, nl, pos, col_sharded):
    _, r, c = slots.shape
    tr = _row_tile(r, c, 1 << 20)
    nb = r // tr

    def body(c_ref, x_ref, y_ref, own_ref, recv_ref, slots_ref, *rest):
        acc = own_ref[...] + recv_ref[...]
        for j in range(N_CHIPS - 1):
            acc = acc + slots_ref[j].astype(F32)
        rest[-1][...] = acc

    if col_sharded:
        own_spec = pl.BlockSpec((tr, c), lambda i, c_ref, x_ref, y_ref: (c_ref[0] * nb + i, _chip(x_ref, y_ref)))
        recv_spec = pl.BlockSpec((tr, c), lambda i, c_ref, x_ref, y_ref: (i, _chip(x_ref, y_ref)))
        out_spec = pl.BlockSpec((None, tr, c), lambda i, c_ref, x_ref, y_ref: (l, c_ref[0] * nb + i, 0))
        out_shape = (nl, 2 * r, c)
    else:
        own_spec = pl.BlockSpec((tr, c), lambda i, c_ref, x_ref, y_ref: (_chip(x_ref, y_ref) * nb + i, c_ref[0]))
        recv_spec = pl.BlockSpec((tr, c), lambda i, c_ref, x_ref, y_ref: (_chip(x_ref, y_ref) * nb + i, 0))
        out_spec = pl.BlockSpec((None, tr, c), lambda i, c_ref, x_ref, y_ref: (l, i, c_ref[0]))
        out_shape = (nl, r, 2 * c)
    in_specs = [own_spec, recv_spec,
                pl.BlockSpec((N_CHIPS - 1, tr, c), lambda i, c_ref, x_ref, y_ref: (0, i, 0))]
    args = [*pos, own, recv, slots]
    aliases = {}
    if buf is not None:
        in_specs.append(ANY)
        args.append(buf)
        aliases = {len(args) - 1: 0}
    grid_spec = pltpu.PrefetchScalarGridSpec(num_scalar_prefetch=3, grid=(nb,), in_specs=in_specs,
                                             out_specs=out_spec)
    return pl.pallas_call(body, name="reduce_block", grid_spec=grid_spec,
                          out_shape=jax.ShapeDtypeStruct(out_shape, F32), input_output_aliases=aliases,
                          compiler_params=_params("parallel"))(*args)


def _sum_slots(a):
    n, r, c = a.shape
    tr = _row_tile(r, c * n, 4 << 20)

    def body(a_ref, o_ref):
        acc = a_ref[0]
        for s in range(1, n):
            acc = acc + a_ref[s]
        o_ref[...] = acc

    return pl.pallas_call(body, name="sum_slots", grid=(r // tr,),
                          in_specs=[pl.BlockSpec((n, tr, c), lambda i: (0, i, 0))],
                          out_specs=pl.BlockSpec((tr, c), lambda i: (i, 0)),
                          out_shape=jax.ShapeDtypeStruct((r, c), F32),
                          compiler_params=_params("parallel"))(a)


def _adamw(w, g, m, v):
    r, c = w.shape
    tr = _row_tile(r, c, 1 << 20)

    def body(w_ref, g_ref, m_ref, v_ref, d_ref, nm_ref, nv_ref):
        gg = g_ref[...]
        nm = ADAM_B1 * m_ref[...] + (1.0 - ADAM_B1) * gg
        nv = ADAM_B2 * v_ref[...] + (1.0 - ADAM_B2) * (gg * gg)
        m_hat = nm / (1.0 - ADAM_B1 ** ADAM_STEP)
        v_hat = nv / (1.0 - ADAM_B2 ** ADAM_STEP)
        d_ref[...] = -ADAM_LR * (m_hat / (jnp.sqrt(v_hat) + ADAM_EPS) + ADAM_WD * w_ref[...])
        nm_ref[...] = nm
        nv_ref[...] = nv

    spec = pl.BlockSpec((tr, c), lambda i: (i, 0))
    sh = jax.ShapeDtypeStruct((r, c), F32)
    return pl.pallas_call(body, name="adamw", grid=(r // tr,), in_specs=[spec] * 4, out_specs=[spec] * 3,
                          out_shape=[sh, sh, sh], compiler_params=_params("parallel"))(w, g, m, v)


def _proj_fwd(x, w):
    s, d = x.shape
    n = w.shape[1]
    tm, tn = min(s, 1024), _col_tile(n, 1024)

    def body(x_ref, w_ref, o_ref):
        o_ref[...] = _dot(x_ref[...].astype(BF16), w_ref[...]).astype(BF16)

    return pl.pallas_call(body, name="proj_fwd", grid=(s // tm, n // tn),
                          in_specs=[pl.BlockSpec((tm, d), lambda i, j: (i, 0)),
                                    pl.BlockSpec((d, tn), lambda i, j: (0, j))],
                          out_specs=pl.BlockSpec((tm, tn), lambda i, j: (i, j)),
                          out_shape=jax.ShapeDtypeStruct((s, n), BF16),
                          compiler_params=_params("parallel", "parallel"))(x, w)


def _matmul_tn(a, b, rider=None):
    s, m = a.shape
    n = b.shape[1]
    tk, tn = min(s, 1024), _col_tile(n, 1024)
    nn, nk = n // tn, s // tk

    def body(*refs):
        j, k = pl.program_id(0), pl.program_id(1)
        first = jnp.logical_and(j == 0, k == 0)
        last = jnp.logical_and(j == nn - 1, k == nk - 1)
        (a_ref, b_ref), (o_ref,), _, start, finish = _ride(rider, refs, 2, 1, first, last)
        start()

        @pl.when(k == 0)
        def _():
            o_ref[...] = jnp.zeros_like(o_ref)

        o_ref[...] += _dot_tn(a_ref[...].astype(BF16), b_ref[...].astype(BF16))
        finish()

    r_in, r_out, r_shapes, r_sems, r_alias = _rider_call_args(rider, 2, 1)
    outs = pl.pallas_call(body, name="matmul_tn", grid=(nn, nk),
                          in_specs=[pl.BlockSpec((tk, m), lambda j, k: (k, 0)),
                                    pl.BlockSpec((tk, tn), lambda j, k: (k, j))] + r_in,
                          out_specs=[pl.BlockSpec((m, tn), lambda j, k: (0, j))] + r_out,
                          out_shape=[jax.ShapeDtypeStruct((m, n), F32)] + r_shapes, scratch_shapes=r_sems,
                          input_output_aliases=r_alias,
                          compiler_params=_params("arbitrary", "arbitrary"))(a, b, *(rider.operands if rider else []))
    return outs[0] if rider is None else (outs[0], outs[1:])


def _dx_matmul(dxr, dproj, w, rider=None):
    s, d = dxr.shape
    n = w.shape[1]
    tm, tk = min(s, 1024), _col_tile(n, 1024)
    nm, nk = s // tm, n // tk

    def body(*refs):
        i, k = pl.program_id(0), pl.program_id(1)
        first = jnp.logical_and(i == 0, k == 0)
        last = jnp.logical_and(i == nm - 1, k == nk - 1)
        (r_ref, g_ref, w_ref), (o_ref,), _, start, finish = _ride(rider, refs, 3, 1, first, last)
        start()

        @pl.when(k == 0)
        def _():
            o_ref[...] = r_ref[...]

        o_ref[...] += _dot_nt(g_ref[...], w_ref[...])
        finish()

    r_in, r_out, r_shapes, r_sems, r_alias = _rider_call_args(rider, 3, 1)
    outs = pl.pallas_call(body, name="dx_matmul", grid=(nm, nk),
                          in_specs=[pl.BlockSpec((tm, d), lambda i, k: (i, 0)),
                                    pl.BlockSpec((tm, tk), lambda i, k: (i, k)),
                                    pl.BlockSpec((d, tk), lambda i, k: (0, k))] + r_in,
                          out_specs=[pl.BlockSpec((tm, d), lambda i, k: (i, 0))] + r_out,
                          out_shape=[jax.ShapeDtypeStruct((s, d), F32)] + r_shapes, scratch_shapes=r_sems,
                          input_output_aliases=r_alias,
                          compiler_params=_params("arbitrary", "arbitrary"))(dxr, dproj, w,
                                                                             *(rider.operands if rider else []))
    return outs[0], outs[1:]


def _mix_chunks(ws_ref, src_ref, dst_ref, bias_ref, t, groups, chunk):
    for c in range(t // chunk):
        rows = slice(c * chunk, (c + 1) * chunk)
        for g in range(groups):
            cols = slice(g * LANES, (g + 1) * LANES)
            val = _dot(ws_ref[g], src_ref[rows, cols])
            if bias_ref is not None:
                val = val + bias_ref[:, g:g + 1]
            dst_ref[rows, cols] = val


def _gmlp_fwd(proj, vn_g, vn_b, ws_m, bs_t):
    s = proj.shape[0]
    d = proj.shape[1] // N_IN
    groups, chunk = ws_m.shape[0], ws_m.shape[1]
    t = min(s, 512)

    def body(u_ref, v_ref, ga_ref, g_ref, b_ref, ws_ref, bs_ref, o_ref, vn_ref, mix_ref):
        xhat, _ = _ln_stats(v_ref[...].astype(F32))
        vn_ref[...] = (xhat * g_ref[...] + b_ref[...]).astype(BF16)
        _mix_chunks(ws_ref, vn_ref, mix_ref, bs_ref, t, groups, chunk)
        ga = ga_ref[...].astype(F32)
        o_ref[...] = (u_ref[...].astype(F32) * mix_ref[...] * (ga * _sigmoid(ga))).astype(BF16)

    col = lambda j: pl.BlockSpec((t, d), lambda i: (i, j))
    full = lambda a: pl.BlockSpec(a.shape, lambda i: (0,) * a.ndim)
    return pl.pallas_call(body, name="gmlp_fwd", grid=(s // t,),
                          in_specs=[col(0), col(1), col(2), full(vn_g), full(vn_b), full(ws_m), full(bs_t)],
                          out_specs=pl.BlockSpec((t, d), lambda i: (i, 0)),
                          out_shape=jax.ShapeDtypeStruct((s, d), BF16),
                          scratch_shapes=[pltpu.VMEM((t, d), BF16), pltpu.VMEM((t, d), F32)],
                          compiler_params=_params("parallel"))(proj, proj, proj, vn_g, vn_b, ws_m, bs_t)


def _gmlp_bwd(dproj, proj, dya, dq, dk, dv, vn_g, vn_b, ws_m, ws_mt, bs_t):
    s = proj.shape[0]
    d = proj.shape[1] // N_IN
    groups, chunk = ws_m.shape[0], ws_m.shape[1]
    t = min(s, 256)
    nsteps = s // t

    def body(dproj_hbm, u_ref, v_ref, ga_ref, dya_ref, dq_ref, dk_ref, dv_ref, g_ref, b_ref, ws_ref, wst_ref, bs_ref,
             o_ref, dg_ref, db_ref, dws_ref, dbs_ref,
             vn_ref, mix_ref, dm_ref, dvn_ref, dbs_acc, dg_acc, db_acc):
        del dproj_hbm
        i = pl.program_id(0)

        @pl.when(i == 0)
        def _():
            dws_ref[...] = jnp.zeros_like(dws_ref)
            dbs_acc[...] = jnp.zeros_like(dbs_acc)
            dg_acc[...] = jnp.zeros_like(dg_acc)
            db_acc[...] = jnp.zeros_like(db_acc)

        xhat, rstd = _ln_stats(v_ref[...].astype(F32))
        vn_ref[...] = (xhat * g_ref[...] + b_ref[...]).astype(BF16)
        _mix_chunks(ws_ref, vn_ref, mix_ref, bs_ref, t, groups, chunk)
        ga = ga_ref[...].astype(F32)
        sg = _sigmoid(ga)
        silu = ga * sg
        dsilu = sg * (1.0 + ga * (1.0 - sg))
        u = u_ref[...].astype(F32)
        dya_f = dya_ref[...].astype(F32)
        mix = mix_ref[...]
        o_ref[:, 0:d] = (dya_f * mix * silu).astype(BF16)
        o_ref[:, 2 * d:3 * d] = (dya_f * u * mix * dsilu).astype(BF16)
        dmix = dya_f * u * silu
        dm_ref[...] = dmix.astype(BF16)
        for c in range(t // chunk):
            dbs_acc[...] += dmix[c * chunk:(c + 1) * chunk, :]
        _mix_chunks(wst_ref, dm_ref, dvn_ref, None, t, groups, chunk)
        for c in range(t // chunk):
            rows = slice(c * chunk, (c + 1) * chunk)
            for g in range(groups):
                cols = slice(g * LANES, (g + 1) * LANES)
                dws_ref[g] += _dot_nt(dm_ref[rows, cols], vn_ref[rows, cols])
        dvn = dvn_ref[...]
        dg_acc[...] += _sum_rows8(dvn * xhat)
        db_acc[...] += _sum_rows8(dvn)
        o_ref[:, d:2 * d] = _ln_bwd(dvn * g_ref[...], xhat, rstd).astype(BF16)
        o_ref[:, 3 * d:4 * d] = dq_ref[...]
        o_ref[:, 4 * d:5 * d] = dk_ref[...]
        o_ref[:, 5 * d:6 * d] = dv_ref[...]

        @pl.when(i == nsteps - 1)
        def _():
            row = lax.broadcasted_iota(jnp.int32, (chunk, chunk), 0)
            col = lax.broadcasted_iota(jnp.int32, (chunk, chunk), 1)
            for g in range(groups):
                dws_ref[g] = jnp.where(col <= row, dws_ref[g], 0.0)
            lane = lax.broadcasted_iota(jnp.int32, (chunk, LANES), 1)
            res = jnp.zeros((chunk, LANES), F32)
            for g in range(groups):
                tot = jnp.sum(dbs_acc[:, g * LANES:(g + 1) * LANES], axis=1, keepdims=True)
                res = jnp.where(lane == g, tot, res)
            dbs_ref[...] = res
            dg_ref[...] = jnp.sum(dg_acc[...], axis=0, keepdims=True)
            db_ref[...] = jnp.sum(db_acc[...], axis=0, keepdims=True)

    col = lambda j: pl.BlockSpec((t, d), lambda i: (i, j))
    tok = pl.BlockSpec((t, d), lambda i: (i, 0))
    full = lambda a: pl.BlockSpec(a.shape, lambda i: (0,) * a.ndim)
    vec = jax.ShapeDtypeStruct((1, d), F32)
    outs = pl.pallas_call(
        body, name="gmlp_bwd", grid=(nsteps,),
        in_specs=[ANY, col(0), col(1), col(2), tok, tok, tok, tok,
                  full(vn_g), full(vn_b), full(ws_m), full(ws_mt), full(bs_t)],
        out_specs=[pl.BlockSpec((t, 6 * d), lambda i: (i, 0)),
                   pl.BlockSpec((1, d), lambda i: (0, 0)), pl.BlockSpec((1, d), lambda i: (0, 0)),
                   pl.BlockSpec((groups, chunk, chunk), lambda i: (0, 0, 0)),
                   pl.BlockSpec((chunk, LANES), lambda i: (0, 0))],
        out_shape=[jax.ShapeDtypeStruct(dproj.shape, BF16), vec, vec,
                   jax.ShapeDtypeStruct((groups, chunk, chunk), F32),
                   jax.ShapeDtypeStruct((chunk, LANES), F32)],
        scratch_shapes=[pltpu.VMEM((t, d), BF16), pltpu.VMEM((t, d), F32), pltpu.VMEM((t, d), BF16),
                        pltpu.VMEM((t, d), F32), pltpu.VMEM((chunk, d), F32),
                        pltpu.VMEM((SUBLANES, d), F32), pltpu.VMEM((SUBLANES, d), F32)],
        input_output_aliases={0: 0},
        compiler_params=_params("arbitrary"))(dproj, proj, proj, proj, dya, dq, dk, dv,
                                              vn_g, vn_b, ws_m, ws_mt, bs_t)
    return outs


ATTN_TILE = 256
ATTN_BLOCKS = 2
EXP_UNDERFLOW = -104.0


def _block(b):
    return slice(b * LANES, (b + 1) * LANES)


def _log_sigmoid(z):
    return jnp.minimum(z, 0.0) - jnp.log(1.0 + jnp.exp(-jnp.abs(z)))


def _suffix_rhs():
    row = lax.broadcasted_iota(jnp.int32, (LANES, LANES), 0)
    col = lax.broadcasted_iota(jnp.int32, (LANES, LANES), 1)
    rhs = jnp.concatenate([(row > col).astype(BF16), jnp.ones((LANES, LANES), BF16)], axis=1)
    return jnp.concatenate([rhs, rhs], axis=0)


def _suffix_sums(a, rhs_ref, t):
    hi = a.astype(BF16)
    lo = (a - hi.astype(F32)).astype(BF16)
    n = t // LANES
    inside, totals = [], []
    for c in range(n):
        cols = slice(c * LANES, (c + 1) * LANES)
        res = _dot(jnp.concatenate([hi[:, cols], lo[:, cols]], axis=1), rhs_ref[...])
        inside.append(res[:, :LANES])
        totals.append(res[:, LANES:])
    later = totals[n - 1]
    for c in reversed(range(n - 1)):
        inside[c] = inside[c] + later
        later = later + totals[c]
    return jnp.concatenate(inside, axis=1), later


def _lanes_to_tile(a, t):
    return jnp.concatenate([a] * (t // LANES), axis=1)


def _sweep_tiles(tiles, i, tq, keep_sum_ref):
    def live():
        most = keep_sum_ref[0]
        for h in range(1, keep_sum_ref.shape[0]):
            most = jnp.maximum(most, keep_sum_ref[h])
        return jnp.max(most) >= EXP_UNDERFLOW

    @pl.when(i == 0)
    def _():
        tiles([(i, True, 0, tq)])

    @pl.when(i > 0)
    def _():
        tiles([(i, True, 0, tq), (i - 1, False, 0, tq)])

        def cond(carry):
            n, alive = carry
            return jnp.logical_and(n < i - 1, alive)

        def step(carry):
            n, _ = carry
            tiles([(i - 2 - n, False, 0, tq)])
            return n + 1, live()

        lax.while_loop(cond, step, (jnp.int32(0), live()))


def _attn_masks(t):
    lane = lax.broadcasted_iota(jnp.int32, (t, LANES), 1)
    row = lax.broadcasted_iota(jnp.int32, (t, t), 0)
    col = lax.broadcasted_iota(jnp.int32, (t, t), 1)
    return lane < HEAD_DIM, col < row


def _split_heads(a, head0):
    zero = jnp.zeros_like(a)
    return [jnp.where(head0, a, zero), jnp.where(head0, zero, a)]


def _attn_fwd(proj, rider=None):
    s = proj.shape[0]
    d = proj.shape[1] // N_IN
    hp = d // LANES
    tq = min(s, ATTN_TILE)
    nq = s // tq
    scale = HEAD_DIM ** -0.5
    assert math.log2(scale).is_integer()

    nb = ATTN_BLOCKS
    n_heads = nb * HEADS_PER_BLOCK
    assert hp % nb == 0
    wide = nb * LANES

    def body(*refs):
        h_id, i = pl.program_id(0), pl.program_id(1)
        first = jnp.logical_and(h_id == 0, i == 0)
        last = jnp.logical_and(h_id == hp // nb - 1, i == nq - 1)
        (q_ref, k_ref, v_ref, rhs_ref), (o_ref,), (acc_ref, r_ref), start, finish = _ride(rider, refs, 4, 1, first, last)
        start()
        head0, strict = _attn_masks(tq)
        qm = []
        for b in range(nb):
            qm += _split_heads((q_ref[:, _block(b)].astype(F32) * scale).astype(BF16), head0)
        acc_ref[...] = jnp.zeros_like(acc_ref)
        r_ref[...] = jnp.zeros_like(r_ref)

        def tiles(specs):
            offs = [pl.multiple_of(spec[0] * tq, tq) for spec in specs]
            k2 = {(t, b): k_ref[pl.ds(off, tq), _block(b)] for t, off in enumerate(offs) for b in range(nb)}
            v2 = {(t, b): v_ref[pl.ds(off, tq), _block(b)] for t, off in enumerate(offs) for b in range(nb)}
            rows = [slice(r0, r0 + nr) for _, _, r0, nr in specs]
            chains = [(t, h) for t in range(len(specs)) for h in range(n_heads)]
            masked = lambda c: specs[c[0]][1]
            z = {c: _dot_nt(qm[c[1]][rows[c[0]]], k2[c[0], c[1] // HEADS_PER_BLOCK]) for c in chains}
            lsz = {c: _log_sigmoid(z[c]) for c in chains}
            keep = {c: lsz[c] - z[c] for c in chains}
            keep = {c: jnp.where(strict[rows[c[0]]], keep[c], 0.0) if masked(c) else keep[c] for c in chains}
            sums = {c: _suffix_sums(keep[c], rhs_ref, tq) for c in chains}
            for h in range(n_heads):
                for t in range(len(specs)):
                    w = jnp.exp(lsz[t, h] + sums[t, h][0] + _lanes_to_tile(r_ref[h, rows[t]], tq))
                    if masked((t, h)):
                        w = jnp.where(strict[rows[t]], w, 0.0)
                    acc_ref[h, rows[t]] += _dot(w.astype(BF16), v2[t, h // HEADS_PER_BLOCK])
                    r_ref[h, rows[t]] += sums[t, h][1]

        _sweep_tiles(tiles, i, tq, r_ref)

        for b in range(nb):
            o_ref[:, _block(b)] = jnp.where(head0, acc_ref[2 * b], acc_ref[2 * b + 1])
        finish()

    rhs = _suffix_rhs()
    r_in, r_out, r_shapes, r_sems, r_alias = _rider_call_args(rider, 4, 1)
    outs = pl.pallas_call(
        body, name="attn_fwd", grid=(hp // nb, nq),
        in_specs=[pl.BlockSpec((tq, wide), lambda h, i: (i, 3 * hp // nb + h)),
                  pl.BlockSpec((s, wide), lambda h, i: (0, 4 * hp // nb + h)),
                  pl.BlockSpec((s, wide), lambda h, i: (0, 5 * hp // nb + h)),
                  pl.BlockSpec(rhs.shape, lambda h, i: (0, 0))] + r_in,
        out_specs=[pl.BlockSpec((tq, wide), lambda h, i: (i, h))] + r_out,
        out_shape=[jax.ShapeDtypeStruct((s, d), F32)] + r_shapes,
        scratch_shapes=[pltpu.VMEM((n_heads, tq, LANES), F32), pltpu.VMEM((n_heads, tq, LANES), F32)] + r_sems,
        input_output_aliases=r_alias,
        compiler_params=_params("arbitrary", "arbitrary"))(proj, proj, proj, rhs, *(rider.operands if rider else []))
    return outs[0], outs[1:]


def _attn_bwd(proj, o, do, rider=None):
    s = proj.shape[0]
    d = proj.shape[1] // N_IN
    hp = d // LANES
    tq = min(s, ATTN_TILE)
    nq = s // tq
    scale = HEAD_DIM ** -0.5
    nb = ATTN_BLOCKS
    n_heads = nb * HEADS_PER_BLOCK
    assert hp % nb == 0
    wide = nb * LANES

    def body(*refs):
        h_id, i = pl.program_id(0), pl.program_id(1)
        first = jnp.logical_and(h_id == 0, i == 0)
        last = jnp.logical_and(h_id == hp // nb - 1, i == nq - 1)
        ((q_ref, k_ref, v_ref, o_ref, do_ref, rhs_ref), (dq_ref, dk_ref, dv_ref),
         (dq_acc, dk_acc, dv_acc, rk_ref, rg_ref), start, finish) = _ride(rider, refs, 6, 3, first, last)
        start()
        head0, strict = _attn_masks(tq)
        qm, dom, delta = [], [], []
        for b in range(nb):
            qm += _split_heads((q_ref[:, _block(b)].astype(F32) * scale).astype(BF16), head0)
            dom += _split_heads(do_ref[:, _block(b)], head0)
            prod = do_ref[:, _block(b)].astype(F32) * o_ref[:, _block(b)]
            delta += [jnp.sum(jnp.where(head0, prod, 0.0), axis=1, keepdims=True),
                      jnp.sum(jnp.where(head0, 0.0, prod), axis=1, keepdims=True)]

        @pl.when(i == 0)
        def _():
            dk_acc[...] = jnp.zeros_like(dk_acc)
            dv_acc[...] = jnp.zeros_like(dv_acc)

        dq_acc[...] = jnp.zeros_like(dq_acc)
        rk_ref[...] = jnp.zeros_like(rk_ref)
        for h in range(n_heads):
            rg_ref[h] = jnp.broadcast_to(delta[h], (tq, LANES))

        def tiles(specs):
            n_t = len(specs)
            offs = [pl.multiple_of(spec[0] * tq, tq) for spec in specs]
            k2 = {(t, b): k_ref[pl.ds(off, tq), _block(b)] for t, off in enumerate(offs) for b in range(nb)}
            v2 = {(t, b): v_ref[pl.ds(off, tq), _block(b)] for t, off in enumerate(offs) for b in range(nb)}
            rows = [slice(r0, r0 + nr) for _, _, r0, nr in specs]
            chains = [(t, h) for t in range(n_t) for h in range(n_heads)]
            masked = lambda c: specs[c[0]][1]
            z = {c: _dot_nt(qm[c[1]][rows[c[0]]], k2[c[0], c[1] // HEADS_PER_BLOCK]) for c in chains}
            dw = {c: _dot_nt(dom[c[1]][rows[c[0]]], v2[c[0], c[1] // HEADS_PER_BLOCK]) for c in chains}
            lsz = {c: _log_sigmoid(z[c]) for c in chains}
            keep = {c: lsz[c] - z[c] for c in chains}
            keep = {c: jnp.where(strict[rows[c[0]]], keep[c], 0.0) if masked(c) else keep[c] for c in chains}
            ksum = {c: _suffix_sums(keep[c], rhs_ref, tq) for c in chains}
            wb, g = {}, {}
            for h in range(n_heads):
                for t in range(n_t):
                    w = jnp.exp(lsz[t, h] + ksum[t, h][0] + _lanes_to_tile(rk_ref[h, rows[t]], tq))
                    if masked((t, h)):
                        w = jnp.where(strict[rows[t]], w, 0.0)
                    wb[t, h] = w.astype(BF16)
                    g[t, h] = dw[t, h] * wb[t, h].astype(F32)
                    rk_ref[h, rows[t]] += ksum[t, h][1]
            gsum = {c: _suffix_sums(g[c], rhs_ref, tq) for c in chains}
            dzb = {}
            for h in range(n_heads):
                for t in range(n_t):
                    dz = g[t, h] - jnp.exp(lsz[t, h]) * (_lanes_to_tile(rg_ref[h, rows[t]], tq) - gsum[t, h][0])
                    if masked((t, h)):
                        dz = jnp.where(strict[rows[t]], dz, 0.0)
                    dzb[t, h] = dz.astype(BF16)
                    dq_acc[h, rows[t]] += _dot(dzb[t, h], k2[t, h // HEADS_PER_BLOCK])
                    rg_ref[h, rows[t]] -= gsum[t, h][1]
            for t in range(n_t):
                for b in range(nb):
                    h0, h1 = 2 * b, 2 * b + 1
                    dk_acc[pl.ds(offs[t], tq), _block(b)] += (_dot_tn(dzb[t, h0], qm[h0][rows[t]])
                                                              + _dot_tn(dzb[t, h1], qm[h1][rows[t]]))
                    dv_acc[pl.ds(offs[t], tq), _block(b)] += (_dot_tn(wb[t, h0], dom[h0][rows[t]])
                                                              + _dot_tn(wb[t, h1], dom[h1][rows[t]]))

        _sweep_tiles(tiles, i, tq, rk_ref)

        for b in range(nb):
            dq_ref[:, _block(b)] = (jnp.where(head0, dq_acc[2 * b], dq_acc[2 * b + 1]) * scale).astype(BF16)

        @pl.when(i == nq - 1)
        def _():
            dk_ref[...] = dk_acc[...].astype(BF16)
            dv_ref[...] = dv_acc[...].astype(BF16)

        finish()

    blk = pl.BlockSpec((tq, wide), lambda h, i: (i, h))
    seq = pl.BlockSpec((s, wide), lambda h, i: (0, h))
    sh = jax.ShapeDtypeStruct((s, d), BF16)
    rhs = _suffix_rhs()
    r_in, r_out, r_shapes, r_sems, r_alias = _rider_call_args(rider, 6, 3)
    outs = pl.pallas_call(
        body, name="attn_bwd", grid=(hp // nb, nq),
        in_specs=[pl.BlockSpec((tq, wide), lambda h, i: (i, 3 * hp // nb + h)),
                  pl.BlockSpec((s, wide), lambda h, i: (0, 4 * hp // nb + h)),
                  pl.BlockSpec((s, wide), lambda h, i: (0, 5 * hp // nb + h)),
                  blk, blk, pl.BlockSpec(rhs.shape, lambda h, i: (0, 0))] + r_in,
        out_specs=[blk, seq, seq] + r_out, out_shape=[sh, sh, sh] + r_shapes,
        scratch_shapes=[pltpu.VMEM((n_heads, tq, LANES), F32),
                        pltpu.VMEM((s, wide), F32), pltpu.VMEM((s, wide), F32),
                        pltpu.VMEM((n_heads, tq, LANES), F32),
                        pltpu.VMEM((n_heads, tq, LANES), F32)] + r_sems,
        input_output_aliases=r_alias,
        compiler_params=_params("arbitrary", "arbitrary"))(proj, proj, proj, o, do, rhs,
                                                           *(rider.operands if rider else []))
    return outs[:3], outs[3:]


def _post_math(ya_ref, o_ref, gb_ref, ma_ref, mb_ref, x_ref, p_ref, wpa_ref, wpb_ref, wout_ref, wpe_ref, wpg_ref,
               alpha):
    gb = gb_ref[...].astype(F32)
    sgb = _sigmoid(gb)
    o = o_ref[...]
    yb = (o * (gb * sgb)).astype(BF16)
    pa = _dot(ya_ref[...], wpa_ref[...])
    pb = _dot(yb, wpb_ref[...])
    sa = _sigmoid(ma_ref[...].astype(F32))
    sb = _sigmoid(mb_ref[...].astype(F32))
    merged = (sa * pa + sb * pb).astype(BF16)
    h1 = alpha * x_ref[...] + _dot(merged, wout_ref[...])
    h1b = h1.astype(BF16)
    e = _dot(p_ref[...].astype(BF16), wpe_ref[...])
    sg = _sigmoid(_dot(h1b, wpg_ref[...]))
    h2 = h1 + e * sg
    return dict(gb=gb, sgb=sgb, o=o, yb=yb, pa=pa, pb=pb, sa=sa, sb=sb, merged=merged, h1b=h1b, e=e, sg=sg, h2=h2)


def _post_specs(tm, d, ple, weights):
    tok = pl.BlockSpec((tm, d), lambda i: (i, 0))
    col = lambda j: pl.BlockSpec((tm, d), lambda i: (i, j))
    full = lambda a: pl.BlockSpec(a.shape, lambda i: (0,) * a.ndim, pipeline_mode=pl.Buffered(1))
    return tok, [tok, tok, col(6), col(7), col(8), tok, pl.BlockSpec((tm, ple), lambda i: (i, 0))] + [
        full(w) for w in weights]


def _post_fwd(ya, o, proj, x, p, w_pa, w_pb, w_out, w_pe, w_pg, ln_g, ln_b, alpha, rider=None):
    s, d = x.shape
    ple = p.shape[1]
    tm = min(s, 256)
    nsteps = s // tm
    weights = (w_pa, w_pb, w_out, w_pe, w_pg, ln_g, ln_b)

    def body(*refs):
        i = pl.program_id(0)
        ins, (out_ref,), _, start, finish = _ride(rider, refs, 14, 1, i == 0, i == nsteps - 1)
        start()
        f = _post_math(*ins[:12], alpha)
        xhat, _ = _ln_stats(f["h2"])
        out_ref[...] = xhat * ins[12][...] + ins[13][...]
        finish()

    tok, in_specs = _post_specs(tm, d, ple, weights)
    r_in, r_out, r_shapes, r_sems, r_alias = _rider_call_args(rider, 14, 1)
    outs = pl.pallas_call(body, name="post_fwd", grid=(nsteps,), in_specs=in_specs + r_in, out_specs=[tok] + r_out,
                          out_shape=[jax.ShapeDtypeStruct((s, d), F32)] + r_shapes, scratch_shapes=r_sems,
                          input_output_aliases=r_alias,
                          compiler_params=_params("arbitrary"))(ya, o, proj, proj, proj, x, p, *weights,
                                                                *(rider.operands if rider else []))
    return outs[0], outs[1:]


def _post_bwd(dxo, ya, o, proj, x, p, w_pa, w_pb, w_out, w_pe, w_pg, ln_g, ln_b, alpha):
    s, d = x.shape
    ple = p.shape[1]
    tm = min(s, 256)
    nsteps = s // tm
    weights = (w_pa, w_pb, w_out, w_pe, w_pg, ln_g, ln_b)

    def body(dxo_ref, ya_ref, o_ref, gb_ref, ma_ref, mb_ref, x_ref, p_ref, wpa_ref, wpb_ref, wout_ref, wpe_ref,
             wpg_ref, g_ref, b_ref,
             dproj_ref, dxr_ref, dya_ref, do_ref, de_ref, h1_ref, dzg_ref, mrg_ref, dh1_ref, dpa_ref, yb_ref, dpb_ref,
             dg_ref, db_ref, dg_acc, db_acc):
        i = pl.program_id(0)

        @pl.when(i == 0)
        def _():
            dg_acc[...] = jnp.zeros_like(dg_acc)
            db_acc[...] = jnp.zeros_like(db_acc)

        f = _post_math(ya_ref, o_ref, gb_ref, ma_ref, mb_ref, x_ref, p_ref, wpa_ref, wpb_ref, wout_ref, wpe_ref,
                       wpg_ref, alpha)
        xhat, rstd = _ln_stats(f["h2"])
        dxo = dxo_ref[...]
        dg_acc[...] += _sum_rows8(dxo * xhat)
        db_acc[...] += _sum_rows8(dxo)
        dh2 = _ln_bwd(dxo * g_ref[...], xhat, rstd)
        sg, e = f["sg"], f["e"]
        de_ref[...] = (dh2 * sg).astype(BF16)
        dzg = (dh2 * e * sg * (1.0 - sg)).astype(BF16)
        dzg_ref[...] = dzg
        dh1 = dh2 + _dot_nt(dzg, wpg_ref[...])
        dh1b = dh1.astype(BF16)
        dxr_ref[...] = alpha * dh1
        dh1_ref[...] = dh1b
        h1_ref[...] = f["h1b"]
        mrg_ref[...] = f["merged"]
        yb_ref[...] = f["yb"]
        dmerged = _dot_nt(dh1b, wout_ref[...])
        sa, sb = f["sa"], f["sb"]
        dpa = (dmerged * sa).astype(BF16)
        dpb = (dmerged * sb).astype(BF16)
        dpa_ref[...] = dpa
        dpb_ref[...] = dpb
        dproj_ref[:, d:2 * d] = (dmerged * f["pa"] * sa * (1.0 - sa)).astype(BF16)
        dproj_ref[:, 2 * d:3 * d] = (dmerged * f["pb"] * sb * (1.0 - sb)).astype(BF16)
        dya_ref[...] = _dot_nt(dpa, wpa_ref[...]).astype(BF16)
        dyb = _dot_nt(dpb, wpb_ref[...])
        gb, sgb = f["gb"], f["sgb"]
        do_ref[...] = (dyb * (gb * sgb)).astype(BF16)
        dproj_ref[:, 0:d] = (dyb * f["o"] * (sgb * (1.0 + gb * (1.0 - sgb)))).astype(BF16)

        @pl.when(i == nsteps - 1)
        def _():
            dg_ref[...] = jnp.sum(dg_acc[...], axis=0, keepdims=True)
            db_ref[...] = jnp.sum(db_acc[...], axis=0, keepdims=True)

    tok, in_specs = _post_specs(tm, d, ple, weights)
    vec_spec = pl.BlockSpec((1, d), lambda i: (0, 0))
    vec = jax.ShapeDtypeStruct((1, d), F32)
    act = jax.ShapeDtypeStruct((s, d), BF16)
    return pl.pallas_call(
        body, name="post_bwd", grid=(nsteps,), in_specs=[tok] + in_specs,
        out_specs=[pl.BlockSpec((tm, 3 * d), lambda i: (i, 2)), tok] + [tok] * 10 + [vec_spec, vec_spec],
        out_shape=[jax.ShapeDtypeStruct((s, N_IN * d), BF16), jax.ShapeDtypeStruct((s, d), F32)] + [act] * 10 + [vec, vec],
        scratch_shapes=[pltpu.VMEM((SUBLANES, d), F32), pltpu.VMEM((SUBLANES, d), F32)],
        compiler_params=_params("arbitrary"))(dxo, ya, o, proj, proj, proj, x, p, *weights)


def _loss_head(y, target):
    s, d = y.shape
    tm = min(s, 512)

    def body(y_ref, t_ref, dy_ref, l_ref):
        @pl.when(pl.program_id(0) == 0)
        def _():
            l_ref[...] = jnp.zeros_like(l_ref)

        err = y_ref[...] - t_ref[...]
        dy_ref[...] = err / d
        row = jnp.sum(err * err, axis=1, keepdims=True) / d
        l_ref[...] += 0.5 * jnp.sum(row, axis=0, keepdims=True)

    tok = pl.BlockSpec((tm, d), lambda i: (i, 0))
    return pl.pallas_call(body, name="loss_head", grid=(s // tm,), in_specs=[tok, tok],
                          out_specs=[tok, pl.BlockSpec((SUBLANES, LANES), lambda i: (0, 0))],
                          out_shape=[jax.ShapeDtypeStruct((s, d), F32),
                                     jax.ShapeDtypeStruct((SUBLANES, LANES), F32)],
                          compiler_params=_params("arbitrary"))(y, target)


def _position():
    x, y, c = lax.axis_index("x"), lax.axis_index("y"), lax.axis_index("c")
    chips = [(1 - x, y), (x, 1 - y), (1 - x, 1 - y)]
    return x, y, c, chips


def _shard_of(ref, col_sharded, j, n):
    off = pl.multiple_of(j * n, n)
    return ref.at[:, pl.ds(off, n)] if col_sharded else ref.at[pl.ds(off, n), :]


def _half_of(ref, col_sharded, h, n):
    off = pl.multiple_of(h * n, n)
    return ref.at[pl.ds(off, n), :] if col_sharded else ref.at[:, pl.ds(off, n)]


def _piece_of(ref, col_sharded, chip, n_block, half, n_half):
    block = pl.ds(pl.multiple_of(chip * n_block, n_block), n_block)
    part = pl.ds(pl.multiple_of(half * n_half, n_half), n_half)
    return ref.at[part, block] if col_sharded else ref.at[block, part]


class _Rider(NamedTuple):
    operands: list
    out_shapes: list
    aliases: dict
    n_sems: int
    start: Callable
    finish: Callable


def _rider_call_args(rider, n_in, n_out):
    if rider is None:
        return [], [], [], [], {}
    sems = [pltpu.SemaphoreType.DMA((rider.n_sems,))] * 2
    aliases = {n_in + i: n_out + o for i, o in rider.aliases.items()}
    return [ANY] * len(rider.operands), [ANY] * len(rider.out_shapes), list(rider.out_shapes), sems, aliases


def _ride(rider, refs, n_in, n_out, first, last):
    if rider is None:
        return refs[:n_in], refs[n_in:n_in + n_out], refs[n_in + n_out:], lambda: None, lambda: None
    r_in, r_out = len(rider.operands), len(rider.out_shapes)
    ins, rins = refs[:n_in], refs[n_in:n_in + r_in]
    outs = refs[n_in + r_in:n_in + r_in + n_out]
    routs = refs[n_in + r_in + n_out:n_in + r_in + n_out + r_out]
    scratch = refs[n_in + r_in + n_out + r_out:-2]
    send_sems, recv_sems = refs[-2:]

    def start():
        pl.when(first)(lambda: rider.start(rins, routs, send_sems, recv_sems))

    def finish():
        pl.when(last)(lambda: rider.finish(rins, routs, send_sems, recv_sems))

    return ins, outs, scratch, start, finish


def _gather_copies(outs, ici_sems, d2d_sems):
    x, y, c, chips = _position()
    my_chip = 2 * x + y
    sends, arrivals, passes, passed = [], [], [], []
    for a, out in enumerate(outs):
        cs = COL_SHARDED[BIG[a % len(BIG)]]
        rows, cols = out.shape
        n_block = (cols if cs else rows) // N_CHIPS
        n_half = (rows if cs else cols) // 2
        piece = lambda chip, half: _piece_of(out, cs, chip, n_block, half, n_half)
        for j, chip in enumerate(chips):
            k = a * 3 + j
            their = 2 * chip[0] + chip[1]
            if ici_sems is not None:
                send_sems, recv_sems = ici_sems
                sends.append(pltpu.make_async_remote_copy(
                    src_ref=piece(my_chip, c), dst_ref=piece(my_chip, c), send_sem=send_sems.at[k],
                    recv_sem=recv_sems.at[k], device_id=(chip[0], chip[1], c), device_id_type=MESH))
                arrivals.append(pltpu.make_async_remote_copy(
                    src_ref=piece(their, c), dst_ref=piece(their, c), send_sem=send_sems.at[k],
                    recv_sem=recv_sems.at[k], device_id=(chip[0], chip[1], c), device_id_type=MESH))
            if d2d_sems is not None:
                send_sems, recv_sems = d2d_sems
                passes.append(pltpu.make_async_remote_copy(
                    src_ref=piece(their, c), dst_ref=piece(their, c), send_sem=send_sems.at[k],
                    recv_sem=recv_sems.at[k], device_id=(x, y, 1 - c), device_id_type=MESH))
                passed.append(pltpu.make_async_remote_copy(
                    src_ref=piece(their, 1 - c), dst_ref=piece(their, 1 - c), send_sem=send_sems.at[k],
                    recv_sem=recv_sems.at[k], device_id=(x, y, 1 - c), device_id_type=MESH))
    return sends, arrivals, passes, passed


def _gather_weights(bufs):
    flat = [bufs[name] for name in BIG]
    n_arr = len(flat)

    def body(*refs):
        outs = refs[n_arr:2 * n_arr]
        sems = refs[2 * n_arr:]
        sends, arrivals, passes, passed = _gather_copies(outs, sems[:2], sems[2:])
        for cp in sends:
            cp.start()
        for arrival, onward in zip(arrivals, passes):
            arrival.wait_recv()
            onward.start()
        for cp in passed:
            cp.wait_recv()
        for cp in sends + passes:
            cp.wait_send()

    outs = pl.pallas_call(
        body, name="gather_weights", in_specs=[ANY] * n_arr, out_specs=[ANY] * n_arr,
        out_shape=[jax.ShapeDtypeStruct(a.shape, BF16) for a in flat],
        input_output_aliases={a: a for a in range(n_arr)},
        scratch_shapes=[pltpu.SemaphoreType.DMA((n_arr * 3,))] * 4,
    )(*flat)
    return dict(zip(BIG, outs))


def _gather_rider(bufs, over_ici):
    flat = [bufs[name] for name in BIG]

    def copies(routs, send_sems, recv_sems):
        sems = (send_sems, recv_sems)
        sends, arrivals, passes, passed = _gather_copies(routs, sems if over_ici else None, None if over_ici else sems)
        return (sends, arrivals) if over_ici else (passes, passed)

    def start(rins, routs, send_sems, recv_sems):
        for cp in copies(routs, send_sems, recv_sems)[0]:
            cp.start()

    def finish(rins, routs, send_sems, recv_sems):
        out, due = copies(routs, send_sems, recv_sems)
        for cp in due:
            cp.wait_recv()
        for cp in out:
            cp.wait_send()

    return _Rider(flat, [jax.ShapeDtypeStruct(a.shape, a.dtype) for a in flat], {i: i for i in range(len(flat))},
                  len(flat) * 3, start, finish)


def _half_shape(shape, col_sharded):
    r, c = shape
    return (r // 2, c) if col_sharded else (r, c // 2)


def _exchange_rider(grads):
    flat = [grads[name] for name in BIG]

    def copies(rins, routs, send_sems, recv_sems):
        x, y, c, _ = _position()
        out = []
        for a, name in enumerate(BIG):
            cs = COL_SHARDED[name]
            n = routs[a].shape[0] if cs else routs[a].shape[1]
            out.append(pltpu.make_async_remote_copy(
                src_ref=_half_of(rins[a], cs, 1 - c, n), dst_ref=routs[a], send_sem=send_sems.at[a],
                recv_sem=recv_sems.at[a], device_id=(x, y, 1 - c), device_id_type=MESH))
        return out

    def start(rins, routs, send_sems, recv_sems):
        for cp in copies(rins, routs, send_sems, recv_sems):
            cp.start()

    def finish(rins, routs, send_sems, recv_sems):
        cps = copies(rins, routs, send_sems, recv_sems)
        for cp in cps:
            cp.wait_recv()
        for cp in cps:
            cp.wait_send()

    return _Rider(flat, [jax.ShapeDtypeStruct(_half_shape(a.shape, COL_SHARDED[name]), F32)
                         for a, name in zip(flat, BIG)], {}, len(flat), start, finish)


def _scatter_to_owners(halves):
    rider = _scatter_rider(halves)

    def body(*refs):
        n = len(rider.operands)
        args = (refs[:n], refs[n:2 * n], *refs[2 * n:])
        rider.start(*args)
        rider.finish(*args)

    outs = pl.pallas_call(
        body, name="scatter_to_owners", in_specs=[ANY] * len(rider.operands), out_specs=[ANY] * len(rider.out_shapes),
        out_shape=rider.out_shapes, scratch_shapes=[pltpu.SemaphoreType.DMA((rider.n_sems,))] * 2,
    )(*rider.operands)
    return dict(zip(BIG, outs))


def _scatter_rider(halves):
    flat = [halves[name] for name in BIG]

    def slots_shape(a, cs):
        r, c = a.shape
        return (N_CHIPS - 1,) + ((r, c // N_CHIPS) if cs else (r // N_CHIPS, c))

    def copies(rins, routs, send_sems, recv_sems):
        x, y, c, chips = _position()
        out = []
        for a, name in enumerate(BIG):
            cs = COL_SHARDED[name]
            n = routs[a].shape[2] if cs else routs[a].shape[1]
            for j, chip in enumerate(chips):
                k = a * 3 + j
                out.append(pltpu.make_async_remote_copy(
                    src_ref=_shard_of(rins[a], cs, 2 * chip[0] + chip[1], n), dst_ref=routs[a].at[j],
                    send_sem=send_sems.at[k], recv_sem=recv_sems.at[k], device_id=(chip[0], chip[1], c),
                    device_id_type=MESH))
        return out

    def start(rins, routs, send_sems, recv_sems):
        for cp in copies(rins, routs, send_sems, recv_sems):
            cp.start()

    def finish(rins, routs, send_sems, recv_sems):
        cps = copies(rins, routs, send_sems, recv_sems)
        for cp in cps:
            cp.wait_recv()
        for cp in cps:
            cp.wait_send()

    return _Rider(flat, [jax.ShapeDtypeStruct(slots_shape(a, COL_SHARDED[name]), a.dtype) for a, name in zip(flat, BIG)],
                  {}, len(flat) * 3, start, finish)


def _join_halves(halves):
    flat = [halves[name] for name in BIG]
    n_w = len(flat)

    def body(*refs):
        outs = refs[n_w:2 * n_w]
        send_sems, recv_sems = refs[2 * n_w:]
        x, y, c, _ = _position()
        sends, recvs = [], []
        for w, name in enumerate(BIG):
            cs = COL_SHARDED[name]
            n = (outs[w].shape[1] if cs else outs[w].shape[2]) // 2

            def half(h):
                part = pl.ds(pl.multiple_of(h * n, n), n)
                return outs[w].at[:, part, :] if cs else outs[w].at[:, :, part]

            sends.append(pltpu.make_async_remote_copy(
                src_ref=half(c), dst_ref=half(c), send_sem=send_sems.at[w], recv_sem=recv_sems.at[w],
                device_id=(x, y, 1 - c), device_id_type=MESH))
            recvs.append(pltpu.make_async_remote_copy(
                src_ref=half(1 - c), dst_ref=half(1 - c), send_sem=send_sems.at[w], recv_sem=recv_sems.at[w],
                device_id=(x, y, 1 - c), device_id_type=MESH))
        for cp in sends:
            cp.start()
        for cp in recvs:
            cp.wait_recv()
        for cp in sends:
            cp.wait_send()

    outs = pl.pallas_call(
        body, name="join_halves", in_specs=[ANY] * n_w, out_specs=[ANY] * n_w,
        out_shape=[jax.ShapeDtypeStruct(a.shape, F32) for a in flat],
        input_output_aliases={w: w for w in range(n_w)},
        scratch_shapes=[pltpu.SemaphoreType.DMA((n_w,)), pltpu.SemaphoreType.DMA((n_w,))],
    )(*flat)
    return dict(zip(BIG, outs))


def _small_rider(packed):
    r, lanes = packed.shape

    def copies(rins, routs, send_sems, recv_sems):
        x, y, c, _ = _position()
        me = 4 * x + 2 * y + c
        local = pltpu.make_async_copy(rins[0], routs[0].at[me], send_sems.at[N_DEV - 1])
        sends, recvs = [], []
        for k in range(1, N_DEV):
            px, py, pc = x ^ (k >> 2), y ^ ((k >> 1) & 1), c ^ (k & 1)
            sends.append(pltpu.make_async_remote_copy(
                src_ref=rins[0], dst_ref=routs[0].at[me], send_sem=send_sems.at[k - 1], recv_sem=recv_sems.at[k - 1],
                device_id=(px, py, pc), device_id_type=MESH))
            recvs.append(pltpu.make_async_remote_copy(
                src_ref=rins[0], dst_ref=routs[0].at[4 * px + 2 * py + pc], send_sem=send_sems.at[k - 1],
                recv_sem=recv_sems.at[k - 1], device_id=(px, py, pc), device_id_type=MESH))
        return local, sends, recvs

    def start(rins, routs, send_sems, recv_sems):
        local, sends, _ = copies(rins, routs, send_sems, recv_sems)
        local.start()
        for cp in sends:
            cp.start()

    def finish(rins, routs, send_sems, recv_sems):
        local, sends, recvs = copies(rins, routs, send_sems, recv_sems)
        for cp in recvs:
            cp.wait_recv()
        for cp in sends:
            cp.wait_send()
        local.wait()

    return _Rider([packed], [jax.ShapeDtypeStruct((N_DEV, r, lanes), F32)], {}, N_DEV, start, finish)


def _pack_small(t):
    return jnp.concatenate([t[name].reshape(-1, LANES) for name in SMALL], axis=0)


def _unpack_small(packed, like):
    out, row = {}, 0
    for name in SMALL:
        n = like[name].size // LANES
        out[name] = packed[row:row + n].reshape(like[name].shape)
        row += n
    return out


def kernel(x, p, w_in, vn_g, vn_b, w_s, b_s, w_pa, w_pb, w_out, w_pe, w_pg, ln_g, ln_b, loss_target, m_w_in, m_vn_g, m_vn_b, m_w_s, m_b_s, m_w_pa, m_w_pb, m_w_out, m_w_pe, m_w_pg, m_ln_g, m_ln_b, v_w_in, v_vn_g, v_vn_b, v_w_s, v_b_s, v_w_pa, v_w_pb, v_w_out, v_w_pe, v_w_pg, v_ln_g, v_ln_b):
    weights = dict(w_in=w_in, vn_g=vn_g, vn_b=vn_b, w_s=w_s, b_s=b_s, w_pa=w_pa, w_pb=w_pb, w_out=w_out, w_pe=w_pe,
                   w_pg=w_pg, ln_g=ln_g, ln_b=ln_b)
    mom1 = dict(w_in=m_w_in, vn_g=m_vn_g, vn_b=m_vn_b, w_s=m_w_s, b_s=m_b_s, w_pa=m_w_pa, w_pb=m_w_pb, w_out=m_w_out,
                w_pe=m_w_pe, w_pg=m_w_pg, ln_g=m_ln_g, ln_b=m_ln_b)
    mom2 = dict(w_in=v_w_in, vn_g=v_vn_g, vn_b=v_vn_b, w_s=v_w_s, b_s=v_b_s, w_pa=v_w_pa, w_pb=v_w_pb, w_out=v_w_out,
                w_pe=v_w_pe, w_pg=v_w_pg, ln_g=v_ln_g, ln_b=v_ln_b)
    nl, d = vn_g.shape
    chunk = w_s.shape[2]
    assert chunk == LANES and w_s.shape[3] == LANES and d % LANES == 0
    alpha = (2 * nl) ** 0.25
    pos = tuple(lax.axis_index(a).astype(jnp.int32).reshape(1) for a in ("c", "x", "y"))

    placed = [{name: _cast_into_place(weights[name], l, pos, COL_SHARDED[name]) for name in BIG} for l in range(nl)]
    full = [_gather_weights(placed[0])] + [None] * (nl - 1)
    causal = jnp.tril(jnp.ones((chunk, chunk), dtype=bool))
    ws_m = jnp.where(causal, w_s, 0.0).astype(BF16)
    ws_mt = jnp.swapaxes(ws_m, 2, 3)
    bs_t = jnp.swapaxes(b_s, 1, 2)

    xs, projs, yas, os_ = [x[0]], [], [], []
    for l in range(nl):
        proj = _proj_fwd(xs[l], full[l]["w_in"])
        ya = _gmlp_fwd(proj, vn_g[l:l + 1], vn_b[l:l + 1], ws_m[l], bs_t[l])
        more = l + 1 < nl
        o, arrived = _attn_fwd(proj, _gather_rider(placed[l + 1], over_ici=True) if more else None)
        x_next, handed = _post_fwd(ya, o, proj, xs[l], p[l, 0], full[l]["w_pa"], full[l]["w_pb"], full[l]["w_out"],
                                   full[l]["w_pe"], full[l]["w_pg"], ln_g[l:l + 1], ln_b[l:l + 1], alpha,
                                   _gather_rider(dict(zip(BIG, arrived)), over_ici=False) if more else None)
        if more:
            full[l + 1] = dict(zip(BIG, handed))
        xs.append(x_next)
        projs.append(proj)
        yas.append(ya)
        os_.append(o)

    dx, loss_tile = _loss_head(xs[nl], loss_target[0])
    loss = lax.psum(loss_tile[0, 0], ("x", "y", "c"))

    big_grads, received, slots = [None] * nl, [None] * nl, [None] * nl
    chip_sums = None
    small_grads = [None] * nl
    for l in reversed(range(nl)):
        w = full[l]
        (dproj, dxr, dya, do, de, h1b, dzg, merged, dh1, dpa, yb, dpb, dln_g, dln_b) = _post_bwd(
            dx, yas[l], os_[l], projs[l], xs[l], p[l, 0], w["w_pa"], w["w_pb"], w["w_out"], w["w_pe"], w["w_pg"],
            ln_g[l:l + 1], ln_b[l:l + 1], alpha)
        (dq, dk, dv), scattered = _attn_bwd(projs[l], os_[l], do,
                                            _scatter_rider(chip_sums) if chip_sums is not None else None)
        if chip_sums is not None:
            slots[l + 1] = dict(zip(BIG, scattered))
        dproj, dvn_g, dvn_b, dw_s, dbs_cols = _gmlp_bwd(dproj, projs[l], dya, dq, dk, dv, vn_g[l:l + 1],
                                                         vn_b[l:l + 1], ws_m[l], ws_mt[l], bs_t[l])
        partial = _pack_small(dict(vn_g=dvn_g[0], vn_b=dvn_b[0], w_s=dw_s, b_s=dbs_cols[:, :b_s.shape[1]].T,
                                   ln_g=dln_g[0], ln_b=dln_b[0]))
        dw_in, (partials,) = _matmul_tn(xs[l], dproj, _small_rider(partial))
        small_grads[l] = _unpack_small(_sum_slots(partials), {name: weights[name][0] for name in SMALL})
        big_grads[l] = dict(w_in=dw_in, w_pa=_matmul_tn(yas[l], dpa), w_pb=_matmul_tn(yb, dpb),
                            w_out=_matmul_tn(merged, dh1), w_pe=_matmul_tn(p[l, 0], de), w_pg=_matmul_tn(h1b, dzg))
        dx, from_sibling = _dx_matmul(dxr, dproj, w["w_in"], _exchange_rider(big_grads[l]))
        received[l] = dict(zip(BIG, from_sibling))
        chip_sums = {name: _add_own_half(big_grads[l][name], received[l][name], pos, COL_SHARDED[name])
                     for name in BIG}
    slots[0] = _scatter_to_owners(chip_sums)
    reduced = {}
    for name in BIG:
        buf = None
        for l in range(nl):
            buf = _reduce_block(buf, big_grads[l][name], received[l][name], slots[l][name], l, nl, pos,
                                COL_SHARDED[name])
        reduced[name] = buf
    grads = _join_halves(reduced)

    small_like = {name: weights[name] for name in SMALL}
    grads.update({name: jnp.stack([small_grads[l][name] for l in range(nl)]) for name in SMALL})

    delta, new_m, new_v = {}, {}, {}
    for name in BIG:
        sh = weights[name].shape
        flat = lambda a: a.reshape(sh[0] * sh[1], sh[2])
        dl, nm, nv = _adamw(flat(weights[name]), flat(grads[name]), flat(mom1[name]), flat(mom2[name]))
        delta[name], new_m[name], new_v[name] = dl.reshape(sh), nm.reshape(sh), nv.reshape(sh)
    dl, nm, nv = _adamw(_pack_small(small_like), _pack_small({n: grads[n] for n in SMALL}),
                        _pack_small({n: mom1[n] for n in SMALL}), _pack_small({n: mom2[n] for n in SMALL}))
    delta.update(_unpack_small(dl, small_like))
    new_m.update(_unpack_small(nm, small_like))
    new_v.update(_unpack_small(nv, small_like))

    return (loss, dx[None], *[grads[n] for n in WEIGHTS], *[delta[n] for n in WEIGHTS],
            *[new_m[n] for n in WEIGHTS], *[new_v[n] for n in WEIGHTS])
```

```python
import math
from typing import Callable, NamedTuple

import jax
import jax.numpy as jnp
from jax import lax
from jax.experimental import pallas as pl
from jax.experimental.pallas import tpu as pltpu

F32 = jnp.float32
BF16 = jnp.bfloat16
LANES = 128
SUBLANES = 8
HEAD_DIM = 64
HEADS_PER_BLOCK = LANES // HEAD_DIM
LN_EPS = 1e-5
N_IN = 9
N_CHIPS = 4
N_DEV = 8
ADAM_LR = 0.001
ADAM_B1 = 0.9
ADAM_B2 = 0.999
ADAM_EPS = 1e-08
ADAM_WD = 0.01
ADAM_STEP = 10
MESH = pl.DeviceIdType.MESH
ANY = pl.BlockSpec(memory_space=pl.ANY)
BIG = ("w_in", "w_pa", "w_pb", "w_out", "w_pe", "w_pg")
COL_SHARDED = {"w_in": True, "w_pa": False, "w_pb": False, "w_out": False, "w_pe": True, "w_pg": False}
SMALL = ("vn_g", "vn_b", "w_s", "b_s", "ln_g", "ln_b")
WEIGHTS = ("w_in", "vn_g", "vn_b", "w_s", "b_s", "w_pa", "w_pb", "w_out", "w_pe", "w_pg", "ln_g", "ln_b")


def _params(*sem):
    return pltpu.CompilerParams(dimension_semantics=sem)


def _dot(a, b):
    return jnp.dot(a, b, preferred_element_type=F32)


def _dot_nt(a, b):
    return lax.dot_general(a, b, (((1,), (1,)), ((), ())), preferred_element_type=F32)


def _dot_tn(a, b):
    return lax.dot_general(a, b, (((0,), (0,)), ((), ())), preferred_element_type=F32)


def _sigmoid(a):
    return 1.0 / (1.0 + jnp.exp(-a))


def _row_tile(rows, cols, cap_bytes):
    best = None
    for t in range(16, rows + 1, 16):
        if rows % t == 0 and t * cols * 4 <= cap_bytes:
            best = t
    return best or rows


def _col_tile(cols, cap):
    best = LANES
    for t in range(LANES, min(cols, cap) + 1, LANES):
        if cols % t == 0:
            best = t
    return best


def _ln_stats(h):
    mu = jnp.mean(h, axis=-1, keepdims=True)
    hc = h - mu
    var = jnp.mean(hc * hc, axis=-1, keepdims=True)
    rstd = lax.rsqrt(var + LN_EPS)
    return hc * rstd, rstd


def _ln_bwd(dxhat, xhat, rstd):
    m1 = jnp.mean(dxhat, axis=-1, keepdims=True)
    m2 = jnp.mean(dxhat * xhat, axis=-1, keepdims=True)
    return rstd * (dxhat - m1 - xhat * m2)


def _sum_rows8(a):
    t, d = a.shape
    return jnp.sum(a.reshape(t // SUBLANES, SUBLANES, d), axis=0)


def _chip(x_ref, y_ref):
    return 2 * x_ref[0] + y_ref[0]


def _cast_into_place(shards, l, pos, col_sharded):
    _, r, c = shards.shape
    tr = _row_tile(r, c, 2 << 20)
    nb = r // tr

    def body(c_ref, x_ref, y_ref, a_ref, o_ref):
        o_ref[...] = a_ref[...].astype(BF16)

    if col_sharded:
        out_spec = pl.BlockSpec((tr, c), lambda i, c_ref, x_ref, y_ref: (i, _chip(x_ref, y_ref)))
        full = (r, c * N_CHIPS)
    else:
        out_spec = pl.BlockSpec((tr, c), lambda i, c_ref, x_ref, y_ref: (_chip(x_ref, y_ref) * nb + i, 0))
        full = (r * N_CHIPS, c)
    grid_spec = pltpu.PrefetchScalarGridSpec(
        num_scalar_prefetch=3, grid=(nb,),
        in_specs=[pl.BlockSpec((None, tr, c), lambda i, c_ref, x_ref, y_ref: (l, i, 0))], out_specs=out_spec)
    return pl.pallas_call(body, name="cast_into_place", grid_spec=grid_spec,
                          out_shape=jax.ShapeDtypeStruct(full, BF16),
                          compiler_params=_params("parallel"))(*pos, shards)


def _add_own_half(own, recv, pos, col_sharded):
    r, c = recv.shape
    tr = _row_tile(r, c, 4 << 20)
    nb = r // tr

    def body(c_ref, x_ref, y_ref, own_ref, recv_ref, o_ref):
        o_ref[...] = (own_ref[...] + recv_ref[...]).astype(BF16)

    if col_sharded:
        own_spec = pl.BlockSpec((tr, c), lambda i, c_ref, x_ref, y_ref: (c_ref[0] * nb + i, 0))
    else:
        own_spec = pl.BlockSpec((tr, c), lambda i, c_ref, x_ref, y_ref: (i, c_ref[0]))
    spec = pl.BlockSpec((tr, c), lambda i, c_ref, x_ref, y_ref: (i, 0))
    grid_spec = pltpu.PrefetchScalarGridSpec(num_scalar_prefetch=3, grid=(nb,), in_specs=[own_spec, spec],
                                             out_specs=spec)
    return pl.pallas_call(body, name="add_own_half", grid_spec=grid_spec,
                          out_shape=jax.ShapeDtypeStruct(recv.shape, BF16),
                          compiler_params=_params("parallel"))(*pos, own, recv)


def _reduce_block(buf, own, recv, slots, l, nl, pos, col_sharded):
    _, r, c = slots.shape
    tr = _row_tile(r, c, 2 << 20)
    nb = r // tr

    def body(c_ref, x_ref, y_ref, own_ref, recv_ref, slots_ref, *rest):
        acc = own_ref[...] + recv_ref[...]
        for j in range(N_CHIPS - 1):
            acc = acc + slots_ref[j].astype(F32)
        rest[-1][...] = acc

    if col_sharded:
        own_spec = pl.BlockSpec((tr, c), lambda i, c_ref, x_ref, y_ref: (c_ref[0] * nb + i, _chip(x_ref, y_ref)))
        recv_spec = pl.BlockSpec((tr, c), lambda i, c_ref, x_ref, y_ref: (i, _chip(x_ref, y_ref)))
        out_spec = pl.BlockSpec((None, tr, c), lambda i, c_ref, x_ref, y_ref: (l, c_ref[0] * nb + i, 0))
        out_shape = (nl, 2 * r, c)
    else:
        own_spec = pl.BlockSpec((tr, c), lambda i, c_ref, x_ref, y_ref: (_chip(x_ref, y_ref) * nb + i, c_ref[0]))
        recv_spec = pl.BlockSpec((tr, c), lambda i, c_ref, x_ref, y_ref: (_chip(x_ref, y_ref) * nb + i, 0))
        out_spec = pl.BlockSpec((None, tr, c), lambda i, c_ref, x_ref, y_ref: (l, i, c_ref[0]))
        out_shape = (nl, r, 2 * c)
    in_specs = [own_spec, recv_spec,
                pl.BlockSpec((N_CHIPS - 1, tr, c), lambda i, c_ref, x_ref, y_ref: (0, i, 0))]
    args = [*pos, own, recv, slots]
    aliases = {}
    if buf is not None:
        in_specs.append(ANY)
        args.append(buf)
        aliases = {len(args) - 1: 0}
    grid_spec = pltpu.PrefetchScalarGridSpec(num_scalar_prefetch=3, grid=(nb,), in_specs=in_specs,
                                             out_specs=out_spec)
    return pl.pallas_call(body, name="reduce_block", grid_spec=grid_spec,
                          out_shape=jax.ShapeDtypeStruct(out_shape, F32), input_output_aliases=aliases,
                          compiler_params=_params("parallel"))(*args)


def _sum_slots(a):
    n, r, c = a.shape
    tr = _row_tile(r, c * n, 4 << 20)

    def body(a_ref, o_ref):
        acc = a_ref[0]
        for s in range(1, n):
            acc = acc + a_ref[s]
        o_ref[...] = acc

    return pl.pallas_call(body, name="sum_slots", grid=(r // tr,),
                          in_specs=[pl.BlockSpec((n, tr, c), lambda i: (0, i, 0))],
                          out_specs=pl.BlockSpec((tr, c), lambda i: (i, 0)),
                          out_shape=jax.ShapeDtypeStruct((r, c), F32),
                          compiler_params=_params("parallel"))(a)


def _adamw(w, g, m, v, emit_grad=False):
    r, c = w.shape
    tr = _row_tile(r, c, 2 << 20)
    n_out = 4 if emit_grad else 3

    def body(w_ref, g_ref, m_ref, v_ref, d_ref, nm_ref, nv_ref, *g_out):
        gg = g_ref[...]
        nm = ADAM_B1 * m_ref[...] + (1.0 - ADAM_B1) * gg
        nv = ADAM_B2 * v_ref[...] + (1.0 - ADAM_B2) * (gg * gg)
        m_hat = nm / (1.0 - ADAM_B1 ** ADAM_STEP)
        v_hat = nv / (1.0 - ADAM_B2 ** ADAM_STEP)
        d_ref[...] = -ADAM_LR * (m_hat / (jnp.sqrt(v_hat) + ADAM_EPS) + ADAM_WD * w_ref[...])
        nm_ref[...] = nm
        nv_ref[...] = nv
        for ref in g_out:
            ref[...] = gg

    spec = pl.BlockSpec((tr, c), lambda i: (i, 0))
    sh = jax.ShapeDtypeStruct((r, c), F32)
    return pl.pallas_call(body, name="adamw", grid=(r // tr,), in_specs=[spec] * 4, out_specs=[spec] * n_out,
                          out_shape=[sh] * n_out, compiler_params=_params("parallel"))(w, g, m, v)


def _proj_fwd(x, w):
    s, d = x.shape
    n = w.shape[1]
    tm, tn = min(s, 1024), _col_tile(n, 1024)

    def body(x_ref, w_ref, o_ref):
        o_ref[...] = _dot(x_ref[...].astype(BF16), w_ref[...]).astype(BF16)

    return pl.pallas_call(body, name="proj_fwd", grid=(s // tm, n // tn),
                          in_specs=[pl.BlockSpec((tm, d), lambda i, j: (i, 0)),
                                    pl.BlockSpec((d, tn), lambda i, j: (0, j))],
                          out_specs=pl.BlockSpec((tm, tn), lambda i, j: (i, j)),
                          out_shape=jax.ShapeDtypeStruct((s, n), BF16),
                          compiler_params=_params("parallel", "parallel"))(x, w)


def _matmul_tn(a, b, rider=None):
    s, m = a.shape
    n = b.shape[1]
    tk, tn = min(s, 1024), _col_tile(n, 1024)
    nn, nk = n // tn, s // tk

    def body(*refs):
        j, k = pl.program_id(0), pl.program_id(1)
        first = jnp.logical_and(j == 0, k == 0)
        last = jnp.logical_and(j == nn - 1, k == nk - 1)
        (a_ref, b_ref), (o_ref,), _, start, finish = _ride(rider, refs, 2, 1, first, last)
        start()

        @pl.when(k == 0)
        def _():
            o_ref[...] = jnp.zeros_like(o_ref)

        o_ref[...] += _dot_tn(a_ref[...].astype(BF16), b_ref[...].astype(BF16))
        finish()

    r_in, r_out, r_shapes, r_sems, r_alias = _rider_call_args(rider, 2, 1)
    outs = pl.pallas_call(body, name="matmul_tn", grid=(nn, nk),
                          in_specs=[pl.BlockSpec((tk, m), lambda j, k: (k, 0)),
                                    pl.BlockSpec((tk, tn), lambda j, k: (k, j))] + r_in,
                          out_specs=[pl.BlockSpec((m, tn), lambda j, k: (0, j))] + r_out,
                          out_shape=[jax.ShapeDtypeStruct((m, n), F32)] + r_shapes, scratch_shapes=r_sems,
                          input_output_aliases=r_alias,
                          compiler_params=_params("arbitrary", "arbitrary"))(a, b, *(rider.operands if rider else []))
    return outs[0] if rider is None else (outs[0], outs[1:])


def _dx_matmul(dxr, dproj, w, rider=None):
    s, d = dxr.shape
    n = w.shape[1]
    tm, tk = min(s, 1024), _col_tile(n, 1024)
    nm, nk = s // tm, n // tk

    def body(*refs):
        i, k = pl.program_id(0), pl.program_id(1)
        first = jnp.logical_and(i == 0, k == 0)
        last = jnp.logical_and(i == nm - 1, k == nk - 1)
        (r_ref, g_ref, w_ref), (o_ref,), _, start, finish = _ride(rider, refs, 3, 1, first, last)
        start()

        @pl.when(k == 0)
        def _():
            o_ref[...] = r_ref[...]

        o_ref[...] += _dot_nt(g_ref[...], w_ref[...])
        finish()

    r_in, r_out, r_shapes, r_sems, r_alias = _rider_call_args(rider, 3, 1)
    outs = pl.pallas_call(body, name="dx_matmul", grid=(nm, nk),
                          in_specs=[pl.BlockSpec((tm, d), lambda i, k: (i, 0)),
                                    pl.BlockSpec((tm, tk), lambda i, k: (i, k)),
                                    pl.BlockSpec((d, tk), lambda i, k: (0, k))] + r_in,
                          out_specs=[pl.BlockSpec((tm, d), lambda i, k: (i, 0))] + r_out,
                          out_shape=[jax.ShapeDtypeStruct((s, d), F32)] + r_shapes, scratch_shapes=r_sems,
                          input_output_aliases=r_alias,
                          compiler_params=_params("arbitrary", "arbitrary"))(dxr, dproj, w,
                                                                             *(rider.operands if rider else []))
    return outs[0], outs[1:]


def _mix_chunks(ws_ref, src_ref, dst_ref, bias_ref, t, groups, chunk):
    for c in range(t // chunk):
        rows = slice(c * chunk, (c + 1) * chunk)
        for g in range(groups):
            cols = slice(g * LANES, (g + 1) * LANES)
            val = _dot(ws_ref[g], src_ref[rows, cols])
            if bias_ref is not None:
                val = val + bias_ref[:, g:g + 1]
            dst_ref[rows, cols] = val


def _gmlp_fwd(proj, vn_g, vn_b, ws_m, bs_t):
    s = proj.shape[0]
    d = proj.shape[1] // N_IN
    groups, chunk = ws_m.shape[0], ws_m.shape[1]
    t = min(s, 512)

    def body(u_ref, v_ref, ga_ref, g_ref, b_ref, ws_ref, bs_ref, o_ref, vn_ref, mix_ref):
        xhat, _ = _ln_stats(v_ref[...].astype(F32))
        vn_ref[...] = (xhat * g_ref[...] + b_ref[...]).astype(BF16)
        _mix_chunks(ws_ref, vn_ref, mix_ref, bs_ref, t, groups, chunk)
        ga = ga_ref[...].astype(F32)
        o_ref[...] = (u_ref[...].astype(F32) * mix_ref[...] * (ga * _sigmoid(ga))).astype(BF16)

    col = lambda j: pl.BlockSpec((t, d), lambda i: (i, j))
    full = lambda a: pl.BlockSpec(a.shape, lambda i: (0,) * a.ndim)
    return pl.pallas_call(body, name="gmlp_fwd", grid=(s // t,),
                          in_specs=[col(0), col(1), col(2), full(vn_g), full(vn_b), full(ws_m), full(bs_t)],
                          out_specs=pl.BlockSpec((t, d), lambda i: (i, 0)),
                          out_shape=jax.ShapeDtypeStruct((s, d), BF16),
                          scratch_shapes=[pltpu.VMEM((t, d), BF16), pltpu.VMEM((t, d), F32)],
                          compiler_params=_params("parallel"))(proj, proj, proj, vn_g, vn_b, ws_m, bs_t)


def _gmlp_bwd(dproj, proj, dya, dq, dk, dv, vn_g, vn_b, ws_m, ws_mt, bs_t):
    s = proj.shape[0]
    d = proj.shape[1] // N_IN
    groups, chunk = ws_m.shape[0], ws_m.shape[1]
    t = min(s, 256)
    nsteps = s // t

    def body(dproj_hbm, u_ref, v_ref, ga_ref, dya_ref, dq_ref, dk_ref, dv_ref, g_ref, b_ref, ws_ref, wst_ref, bs_ref,
             o_ref, dg_ref, db_ref, dws_ref, dbs_ref,
             vn_ref, mix_ref, dm_ref, dvn_ref, dbs_acc, dg_acc, db_acc):
        del dproj_hbm
        i = pl.program_id(0)

        @pl.when(i == 0)
        def _():
            dws_ref[...] = jnp.zeros_like(dws_ref)
            dbs_acc[...] = jnp.zeros_like(dbs_acc)
            dg_acc[...] = jnp.zeros_like(dg_acc)
            db_acc[...] = jnp.zeros_like(db_acc)

        xhat, rstd = _ln_stats(v_ref[...].astype(F32))
        vn_ref[...] = (xhat * g_ref[...] + b_ref[...]).astype(BF16)
        _mix_chunks(ws_ref, vn_ref, mix_ref, bs_ref, t, groups, chunk)
        ga = ga_ref[...].astype(F32)
        sg = _sigmoid(ga)
        silu = ga * sg
        dsilu = sg * (1.0 + ga * (1.0 - sg))
        u = u_ref[...].astype(F32)
        dya_f = dya_ref[...].astype(F32)
        mix = mix_ref[...]
        o_ref[:, 0:d] = (dya_f * mix * silu).astype(BF16)
        o_ref[:, 2 * d:3 * d] = (dya_f * u * mix * dsilu).astype(BF16)
        dmix = dya_f * u * silu
        dm_ref[...] = dmix.astype(BF16)
        for c in range(t // chunk):
            dbs_acc[...] += dmix[c * chunk:(c + 1) * chunk, :]
        _mix_chunks(wst_ref, dm_ref, dvn_ref, None, t, groups, chunk)
        for c in range(t // chunk):
            rows = slice(c * chunk, (c + 1) * chunk)
            for g in range(groups):
                cols = slice(g * LANES, (g + 1) * LANES)
                dws_ref[g] += _dot_nt(dm_ref[rows, cols], vn_ref[rows, cols])
        dvn = dvn_ref[...]
        dg_acc[...] += _sum_rows8(dvn * xhat)
        db_acc[...] += _sum_rows8(dvn)
        o_ref[:, d:2 * d] = _ln_bwd(dvn * g_ref[...], xhat, rstd).astype(BF16)
        o_ref[:, 3 * d:4 * d] = dq_ref[...]
        o_ref[:, 4 * d:5 * d] = dk_ref[...]
        o_ref[:, 5 * d:6 * d] = dv_ref[...]

        @pl.when(i == nsteps - 1)
        def _():
            row = lax.broadcasted_iota(jnp.int32, (chunk, chunk), 0)
            col = lax.broadcasted_iota(jnp.int32, (chunk, chunk), 1)
            for g in range(groups):
                dws_ref[g] = jnp.where(col <= row, dws_ref[g], 0.0)
            lane = lax.broadcasted_iota(jnp.int32, (chunk, LANES), 1)
            res = jnp.zeros((chunk, LANES), F32)
            for g in range(groups):
                tot = jnp.sum(dbs_acc[:, g * LANES:(g + 1) * LANES], axis=1, keepdims=True)
                res = jnp.where(lane == g, tot, res)
            dbs_ref[...] = res
            dg_ref[...] = jnp.sum(dg_acc[...], axis=0, keepdims=True)
            db_ref[...] = jnp.sum(db_acc[...], axis=0, keepdims=True)

    col = lambda j: pl.BlockSpec((t, d), lambda i: (i, j))
    tok = pl.BlockSpec((t, d), lambda i: (i, 0))
    full = lambda a: pl.BlockSpec(a.shape, lambda i: (0,) * a.ndim)
    vec = jax.ShapeDtypeStruct((1, d), F32)
    outs = pl.pallas_call(
        body, name="gmlp_bwd", grid=(nsteps,),
        in_specs=[ANY, col(0), col(1), col(2), tok, tok, tok, tok,
                  full(vn_g), full(vn_b), full(ws_m), full(ws_mt), full(bs_t)],
        out_specs=[pl.BlockSpec((t, 6 * d), lambda i: (i, 0)),
                   pl.BlockSpec((1, d), lambda i: (0, 0)), pl.BlockSpec((1, d), lambda i: (0, 0)),
                   pl.BlockSpec((groups, chunk, chunk), lambda i: (0, 0, 0)),
                   pl.BlockSpec((chunk, LANES), lambda i: (0, 0))],
        out_shape=[jax.ShapeDtypeStruct(dproj.shape, BF16), vec, vec,
                   jax.ShapeDtypeStruct((groups, chunk, chunk), F32),
                   jax.ShapeDtypeStruct((chunk, LANES), F32)],
        scratch_shapes=[pltpu.VMEM((t, d), BF16), pltpu.VMEM((t, d), F32), pltpu.VMEM((t, d), BF16),
                        pltpu.VMEM((t, d), F32), pltpu.VMEM((chunk, d), F32),
                        pltpu.VMEM((SUBLANES, d), F32), pltpu.VMEM((SUBLANES, d), F32)],
        input_output_aliases={0: 0},
        compiler_params=_params("arbitrary"))(dproj, proj, proj, proj, dya, dq, dk, dv,
                                              vn_g, vn_b, ws_m, ws_mt, bs_t)
    return outs


ATTN_TILE = 256
ATTN_BLOCKS = 2
EXP_UNDERFLOW = -104.0


def _block(b):
    return slice(b * LANES, (b + 1) * LANES)


def _log_sigmoid(z):
    return jnp.minimum(z, 0.0) - jnp.log(1.0 + jnp.exp(-jnp.abs(z)))


def _suffix_rhs():
    row = lax.broadcasted_iota(jnp.int32, (LANES, LANES), 0)
    col = lax.broadcasted_iota(jnp.int32, (LANES, LANES), 1)
    rhs = jnp.concatenate([(row > col).astype(BF16), jnp.ones((LANES, LANES), BF16)], axis=1)
    return jnp.concatenate([rhs, rhs], axis=0)


def _suffix_sums(a, rhs_ref, t):
    hi = a.astype(BF16)
    lo = (a - hi.astype(F32)).astype(BF16)
    n = t // LANES
    inside, totals = [], []
    for c in range(n):
        cols = slice(c * LANES, (c + 1) * LANES)
        res = _dot(jnp.concatenate([hi[:, cols], lo[:, cols]], axis=1), rhs_ref[...])
        inside.append(res[:, :LANES])
        totals.append(res[:, LANES:])
    later = totals[n - 1]
    for c in reversed(range(n - 1)):
        inside[c] = inside[c] + later
        later = later + totals[c]
    return jnp.concatenate(inside, axis=1), later


def _lanes_to_tile(a, t):
    return jnp.concatenate([a] * (t // LANES), axis=1)


def _sweep_tiles(tiles, i, tq, keep_sum_ref):
    def live():
        most = keep_sum_ref[0]
        for h in range(1, keep_sum_ref.shape[0]):
            most = jnp.maximum(most, keep_sum_ref[h])
        return jnp.max(most) >= EXP_UNDERFLOW

    @pl.when(i == 0)
    def _():
        tiles([(i, True, 0, tq)])

    @pl.when(i > 0)
    def _():
        tiles([(i, True, 0, tq), (i - 1, False, 0, tq)])

        def cond(carry):
            n, alive = carry
            return jnp.logical_and(n < i - 1, alive)

        def step(carry):
            n, _ = carry
            tiles([(i - 2 - n, False, 0, tq)])
            return n + 1, live()

        lax.while_loop(cond, step, (jnp.int32(0), live()))


def _attn_masks(t):
    lane = lax.broadcasted_iota(jnp.int32, (t, LANES), 1)
    row = lax.broadcasted_iota(jnp.int32, (t, t), 0)
    col = lax.broadcasted_iota(jnp.int32, (t, t), 1)
    return lane < HEAD_DIM, col < row


def _split_heads(a, head0):
    zero = jnp.zeros_like(a)
    return [jnp.where(head0, a, zero), jnp.where(head0, zero, a)]


def _attn_fwd(proj, rider=None):
    s = proj.shape[0]
    d = proj.shape[1] // N_IN
    hp = d // LANES
    tq = min(s, ATTN_TILE)
    nq = s // tq
    scale = HEAD_DIM ** -0.5
    assert math.log2(scale).is_integer()

    nb = ATTN_BLOCKS
    n_heads = nb * HEADS_PER_BLOCK
    assert hp % nb == 0
    wide = nb * LANES

    def body(*refs):
        h_id, i = pl.program_id(0), pl.program_id(1)
        first = jnp.logical_and(h_id == 0, i == 0)
        last = jnp.logical_and(h_id == hp // nb - 1, i == nq - 1)
        (q_ref, k_ref, v_ref, rhs_ref), (o_ref,), (acc_ref, r_ref), start, finish = _ride(rider, refs, 4, 1, first, last)
        start()
        head0, strict = _attn_masks(tq)
        qm = []
        for b in range(nb):
            qm += _split_heads((q_ref[:, _block(b)].astype(F32) * scale).astype(BF16), head0)
        acc_ref[...] = jnp.zeros_like(acc_ref)
        r_ref[...] = jnp.zeros_like(r_ref)

        def tiles(specs):
            offs = [pl.multiple_of(spec[0] * tq, tq) for spec in specs]
            k2 = {(t, b): k_ref[pl.ds(off, tq), _block(b)] for t, off in enumerate(offs) for b in range(nb)}
            v2 = {(t, b): v_ref[pl.ds(off, tq), _block(b)] for t, off in enumerate(offs) for b in range(nb)}
            rows = [slice(r0, r0 + nr) for _, _, r0, nr in specs]
            chains = [(t, h) for t in range(len(specs)) for h in range(n_heads)]
            masked = lambda c: specs[c[0]][1]
            z = {c: _dot_nt(qm[c[1]][rows[c[0]]], k2[c[0], c[1] // HEADS_PER_BLOCK]) for c in chains}
            lsz = {c: _log_sigmoid(z[c]) for c in chains}
            keep = {c: lsz[c] - z[c] for c in chains}
            keep = {c: jnp.where(strict[rows[c[0]]], keep[c], 0.0) if masked(c) else keep[c] for c in chains}
            sums = {c: _suffix_sums(keep[c], rhs_ref, tq) for c in chains}
            for h in range(n_heads):
                for t in range(len(specs)):
                    w = jnp.exp(lsz[t, h] + sums[t, h][0] + _lanes_to_tile(r_ref[h, rows[t]], tq))
                    if masked((t, h)):
                        w = jnp.where(strict[rows[t]], w, 0.0)
                    acc_ref[h, rows[t]] += _dot(w.astype(BF16), v2[t, h // HEADS_PER_BLOCK])
                    r_ref[h, rows[t]] += sums[t, h][1]

        _sweep_tiles(tiles, i, tq, r_ref)

        for b in range(nb):
            o_ref[:, _block(b)] = jnp.where(head0, acc_ref[2 * b], acc_ref[2 * b + 1])
        finish()

    rhs = _suffix_rhs()
    r_in, r_out, r_shapes, r_sems, r_alias = _rider_call_args(rider, 4, 1)
    outs = pl.pallas_call(
        body, name="attn_fwd", grid=(hp // nb, nq),
        in_specs=[pl.BlockSpec((tq, wide), lambda h, i: (i, 3 * hp // nb + h)),
                  pl.BlockSpec((s, wide), lambda h, i: (0, 4 * hp // nb + h)),
                  pl.BlockSpec((s, wide), lambda h, i: (0, 5 * hp // nb + h)),
                  pl.BlockSpec(rhs.shape, lambda h, i: (0, 0))] + r_in,
        out_specs=[pl.BlockSpec((tq, wide), lambda h, i: (i, h))] + r_out,
        out_shape=[jax.ShapeDtypeStruct((s, d), F32)] + r_shapes,
        scratch_shapes=[pltpu.VMEM((n_heads, tq, LANES), F32), pltpu.VMEM((n_heads, tq, LANES), F32)] + r_sems,
        input_output_aliases=r_alias,
        compiler_params=_params("arbitrary", "arbitrary"))(proj, proj, proj, rhs, *(rider.operands if rider else []))
    return outs[0], outs[1:]


def _attn_bwd(proj, o, do, rider=None):
    s = proj.shape[0]
    d = proj.shape[1] // N_IN
    hp = d // LANES
    tq = min(s, ATTN_TILE)
    nq = s // tq
    scale = HEAD_DIM ** -0.5
    nb = ATTN_BLOCKS
    n_heads = nb * HEADS_PER_BLOCK
    assert hp % nb == 0
    wide = nb * LANES

    def body(*refs):
        h_id, i = pl.program_id(0), pl.program_id(1)
        first = jnp.logical_and(h_id == 0, i == 0)
        last = jnp.logical_and(h_id == hp // nb - 1, i == nq - 1)
        ((q_ref, k_ref, v_ref, o_ref, do_ref, rhs_ref), (dq_ref, dk_ref, dv_ref),
         (dq_acc, dk_acc, dv_acc, rk_ref, rg_ref), start, finish) = _ride(rider, refs, 6, 3, first, last)
        start()
        head0, strict = _attn_masks(tq)
        qm, dom, delta = [], [], []
        for b in range(nb):
            qm += _split_heads((q_ref[:, _block(b)].astype(F32) * scale).astype(BF16), head0)
            dom += _split_heads(do_ref[:, _block(b)], head0)
            prod = do_ref[:, _block(b)].astype(F32) * o_ref[:, _block(b)]
            delta += [jnp.sum(jnp.where(head0, prod, 0.0), axis=1, keepdims=True),
                      jnp.sum(jnp.where(head0, 0.0, prod), axis=1, keepdims=True)]

        @pl.when(i == 0)
        def _():
            dk_acc[...] = jnp.zeros_like(dk_acc)
            dv_acc[...] = jnp.zeros_like(dv_acc)

        dq_acc[...] = jnp.zeros_like(dq_acc)
        rk_ref[...] = jnp.zeros_like(rk_ref)
        for h in range(n_heads):
            rg_ref[h] = jnp.broadcast_to(delta[h], (tq, LANES))

        def tiles(specs):
            n_t = len(specs)
            offs = [pl.multiple_of(spec[0] * tq, tq) for spec in specs]
            k2 = {(t, b): k_ref[pl.ds(off, tq), _block(b)] for t, off in enumerate(offs) for b in range(nb)}
            v2 = {(t, b): v_ref[pl.ds(off, tq), _block(b)] for t, off in enumerate(offs) for b in range(nb)}
            rows = [slice(r0, r0 + nr) for _, _, r0, nr in specs]
            chains = [(t, h) for t in range(n_t) for h in range(n_heads)]
            masked = lambda c: specs[c[0]][1]
            z = {c: _dot_nt(qm[c[1]][rows[c[0]]], k2[c[0], c[1] // HEADS_PER_BLOCK]) for c in chains}
            dw = {c: _dot_nt(dom[c[1]][rows[c[0]]], v2[c[0], c[1] // HEADS_PER_BLOCK]) for c in chains}
            lsz = {c: _log_sigmoid(z[c]) for c in chains}
            keep = {c: lsz[c] - z[c] for c in chains}
            keep = {c: jnp.where(strict[rows[c[0]]], keep[c], 0.0) if masked(c) else keep[c] for c in chains}
            ksum = {c: _suffix_sums(keep[c], rhs_ref, tq) for c in chains}
            wb, g = {}, {}
            for h in range(n_heads):
                for t in range(n_t):
                    w = jnp.exp(lsz[t, h] + ksum[t, h][0] + _lanes_to_tile(rk_ref[h, rows[t]], tq))
                    if masked((t, h)):
                        w = jnp.where(strict[rows[t]], w, 0.0)
                    wb[t, h] = w.astype(BF16)
                    g[t, h] = dw[t, h] * wb[t, h].astype(F32)
                    rk_ref[h, rows[t]] += ksum[t, h][1]
            gsum = {c: _suffix_sums(g[c], rhs_ref, tq) for c in chains}
            dzb = {}
            for h in range(n_heads):
                for t in range(n_t):
                    dz = g[t, h] - jnp.exp(lsz[t, h]) * (_lanes_to_tile(rg_ref[h, rows[t]], tq) - gsum[t, h][0])
                    if masked((t, h)):
                        dz = jnp.where(strict[rows[t]], dz, 0.0)
                    dzb[t, h] = dz.astype(BF16)
                    dq_acc[h, rows[t]] += _dot(dzb[t, h], k2[t, h // HEADS_PER_BLOCK])
                    rg_ref[h, rows[t]] -= gsum[t, h][1]
            for t in range(n_t):
                for b in range(nb):
                    h0, h1 = 2 * b, 2 * b + 1
                    dk_acc[pl.ds(offs[t], tq), _block(b)] += (_dot_tn(dzb[t, h0], qm[h0][rows[t]])
                                                              + _dot_tn(dzb[t, h1], qm[h1][rows[t]]))
                    dv_acc[pl.ds(offs[t], tq), _block(b)] += (_dot_tn(wb[t, h0], dom[h0][rows[t]])
                                                              + _dot_tn(wb[t, h1], dom[h1][rows[t]]))

        _sweep_tiles(tiles, i, tq, rk_ref)

        for b in range(nb):
            dq_ref[:, _block(b)] = (jnp.where(head0, dq_acc[2 * b], dq_acc[2 * b + 1]) * scale).astype(BF16)

        @pl.when(i == nq - 1)
        def _():
            dk_ref[...] = dk_acc[...].astype(BF16)
            dv_ref[...] = dv_acc[...].astype(BF16)

        finish()

    blk = pl.BlockSpec((tq, wide), lambda h, i: (i, h))
    seq = pl.BlockSpec((s, wide), lambda h, i: (0, h))
    sh = jax.ShapeDtypeStruct((s, d), BF16)
    rhs = _suffix_rhs()
    r_in, r_out, r_shapes, r_sems, r_alias = _rider_call_args(rider, 6, 3)
    outs = pl.pallas_call(
        body, name="attn_bwd", grid=(hp // nb, nq),
        in_specs=[pl.BlockSpec((tq, wide), lambda h, i: (i, 3 * hp // nb + h)),
                  pl.BlockSpec((s, wide), lambda h, i: (0, 4 * hp // nb + h)),
                  pl.BlockSpec((s, wide), lambda h, i: (0, 5 * hp // nb + h)),
                  blk, blk, pl.BlockSpec(rhs.shape, lambda h, i: (0, 0))] + r_in,
        out_specs=[blk, seq, seq] + r_out, out_shape=[sh, sh, sh] + r_shapes,
        scratch_shapes=[pltpu.VMEM((n_heads, tq, LANES), F32),
                        pltpu.VMEM((s, wide), F32), pltpu.VMEM((s, wide), F32),
                        pltpu.VMEM((n_heads, tq, LANES), F32),
                        pltpu.VMEM((n_heads, tq, LANES), F32)] + r_sems,
        input_output_aliases=r_alias,
        compiler_params=_params("arbitrary", "arbitrary"))(proj, proj, proj, o, do, rhs,
                                                           *(rider.operands if rider else []))
    return outs[:3], outs[3:]


def _post_math(ya_ref, o_ref, gb_ref, ma_ref, mb_ref, x_ref, p_ref, wpa_ref, wpb_ref, wout_ref, wpe_ref, wpg_ref,
               alpha):
    gb = gb_ref[...].astype(F32)
    sgb = _sigmoid(gb)
    o = o_ref[...]
    yb = (o * (gb * sgb)).astype(BF16)
    pa = _dot(ya_ref[...], wpa_ref[...])
    pb = _dot(yb, wpb_ref[...])
    sa = _sigmoid(ma_ref[...].astype(F32))
    sb = _sigmoid(mb_ref[...].astype(F32))
    merged = (sa * pa + sb * pb).astype(BF16)
    h1 = alpha * x_ref[...] + _dot(merged, wout_ref[...])
    h1b = h1.astype(BF16)
    e = _dot(p_ref[...].astype(BF16), wpe_ref[...])
    sg = _sigmoid(_dot(h1b, wpg_ref[...]))
    h2 = h1 + e * sg
    return dict(gb=gb, sgb=sgb, o=o, yb=yb, pa=pa, pb=pb, sa=sa, sb=sb, merged=merged, h1b=h1b, e=e, sg=sg, h2=h2)


def _post_specs(tm, d, ple, weights):
    tok = pl.BlockSpec((tm, d), lambda i: (i, 0))
    col = lambda j: pl.BlockSpec((tm, d), lambda i: (i, j))
    full = lambda a: pl.BlockSpec(a.shape, lambda i: (0,) * a.ndim, pipeline_mode=pl.Buffered(1))
    return tok, [tok, tok, col(6), col(7), col(8), tok, pl.BlockSpec((tm, ple), lambda i: (i, 0))] + [
        full(w) for w in weights]


def _post_fwd(ya, o, proj, x, p, w_pa, w_pb, w_out, w_pe, w_pg, ln_g, ln_b, alpha, rider=None):
    s, d = x.shape
    ple = p.shape[1]
    tm = min(s, 256)
    nsteps = s // tm
    weights = (w_pa, w_pb, w_out, w_pe, w_pg, ln_g, ln_b)

    def body(*refs):
        i = pl.program_id(0)
        ins, (out_ref,), _, start, finish = _ride(rider, refs, 14, 1, i == 0, i == nsteps - 1)
        start()
        f = _post_math(*ins[:12], alpha)
        xhat, _ = _ln_stats(f["h2"])
        out_ref[...] = xhat * ins[12][...] + ins[13][...]
        finish()

    tok, in_specs = _post_specs(tm, d, ple, weights)
    r_in, r_out, r_shapes, r_sems, r_alias = _rider_call_args(rider, 14, 1)
    outs = pl.pallas_call(body, name="post_fwd", grid=(nsteps,), in_specs=in_specs + r_in, out_specs=[tok] + r_out,
                          out_shape=[jax.ShapeDtypeStruct((s, d), F32)] + r_shapes, scratch_shapes=r_sems,
                          input_output_aliases=r_alias,
                          compiler_params=_params("arbitrary"))(ya, o, proj, proj, proj, x, p, *weights,
                                                                *(rider.operands if rider else []))
    return outs[0], outs[1:]


def _post_bwd(dxo, ya, o, proj, x, p, w_pa, w_pb, w_out, w_pe, w_pg, ln_g, ln_b, alpha):
    s, d = x.shape
    ple = p.shape[1]
    tm = min(s, 256)
    nsteps = s // tm
    weights = (w_pa, w_pb, w_out, w_pe, w_pg, ln_g, ln_b)

    def body(dxo_ref, ya_ref, o_ref, gb_ref, ma_ref, mb_ref, x_ref, p_ref, wpa_ref, wpb_ref, wout_ref, wpe_ref,
             wpg_ref, g_ref, b_ref,
             dproj_ref, dxr_ref, dya_ref, do_ref, de_ref, h1_ref, dzg_ref, mrg_ref, dh1_ref, dpa_ref, yb_ref, dpb_ref,
             dg_ref, db_ref, dg_acc, db_acc):
        i = pl.program_id(0)

        @pl.when(i == 0)
        def _():
            dg_acc[...] = jnp.zeros_like(dg_acc)
            db_acc[...] = jnp.zeros_like(db_acc)

        f = _post_math(ya_ref, o_ref, gb_ref, ma_ref, mb_ref, x_ref, p_ref, wpa_ref, wpb_ref, wout_ref, wpe_ref,
                       wpg_ref, alpha)
        xhat, rstd = _ln_stats(f["h2"])
        dxo = dxo_ref[...]
        dg_acc[...] += _sum_rows8(dxo * xhat)
        db_acc[...] += _sum_rows8(dxo)
        dh2 = _ln_bwd(dxo * g_ref[...], xhat, rstd)
        sg, e = f["sg"], f["e"]
        de_ref[...] = (dh2 * sg).astype(BF16)
        dzg = (dh2 * e * sg * (1.0 - sg)).astype(BF16)
        dzg_ref[...] = dzg
        dh1 = dh2 + _dot_nt(dzg, wpg_ref[...])
        dh1b = dh1.astype(BF16)
        dxr_ref[...] = alpha * dh1
        dh1_ref[...] = dh1b
        h1_ref[...] = f["h1b"]
        mrg_ref[...] = f["merged"]
        yb_ref[...] = f["yb"]
        dmerged = _dot_nt(dh1b, wout_ref[...])
        sa, sb = f["sa"], f["sb"]
        dpa = (dmerged * sa).astype(BF16)
        dpb = (dmerged * sb).astype(BF16)
        dpa_ref[...] = dpa
        dpb_ref[...] = dpb
        dproj_ref[:, d:2 * d] = (dmerged * f["pa"] * sa * (1.0 - sa)).astype(BF16)
        dproj_ref[:, 2 * d:3 * d] = (dmerged * f["pb"] * sb * (1.0 - sb)).astype(BF16)
        dya_ref[...] = _dot_nt(dpa, wpa_ref[...]).astype(BF16)
        dyb = _dot_nt(dpb, wpb_ref[...])
        gb, sgb = f["gb"], f["sgb"]
        do_ref[...] = (dyb * (gb * sgb)).astype(BF16)
        dproj_ref[:, 0:d] = (dyb * f["o"] * (sgb * (1.0 + gb * (1.0 - sgb)))).astype(BF16)

        @pl.when(i == nsteps - 1)
        def _():
            dg_ref[...] = jnp.sum(dg_acc[...], axis=0, keepdims=True)
            db_ref[...] = jnp.sum(db_acc[...], axis=0, keepdims=True)

    tok, in_specs = _post_specs(tm, d, ple, weights)
    vec_spec = pl.BlockSpec((1, d), lambda i: (0, 0))
    vec = jax.ShapeDtypeStruct((1, d), F32)
    act = jax.ShapeDtypeStruct((s, d), BF16)
    return pl.pallas_call(
        body, name="post_bwd", grid=(nsteps,), in_specs=[tok] + in_specs,
        out_specs=[pl.BlockSpec((tm, 3 * d), lambda i: (i, 2)), tok] + [tok] * 10 + [vec_spec, vec_spec],
        out_shape=[jax.ShapeDtypeStruct((s, N_IN * d), BF16), jax.ShapeDtypeStruct((s, d), F32)] + [act] * 10 + [vec, vec],
        scratch_shapes=[pltpu.VMEM((SUBLANES, d), F32), pltpu.VMEM((SUBLANES, d), F32)],
        compiler_params=_params("arbitrary"))(dxo, ya, o, proj, proj, proj, x, p, *weights)


def _loss_head(y, target):
    s, d = y.shape
    tm = min(s, 512)

    def body(y_ref, t_ref, dy_ref, l_ref):
        @pl.when(pl.program_id(0) == 0)
        def _():
            l_ref[...] = jnp.zeros_like(l_ref)

        err = y_ref[...] - t_ref[...]
        dy_ref[...] = err / d
        row = jnp.sum(err * err, axis=1, keepdims=True) / d
        l_ref[...] += 0.5 * jnp.sum(row, axis=0, keepdims=True)

    tok = pl.BlockSpec((tm, d), lambda i: (i, 0))
    return pl.pallas_call(body, name="loss_head", grid=(s // tm,), in_specs=[tok, tok],
                          out_specs=[tok, pl.BlockSpec((SUBLANES, LANES), lambda i: (0, 0))],
                          out_shape=[jax.ShapeDtypeStruct((s, d), F32),
                                     jax.ShapeDtypeStruct((SUBLANES, LANES), F32)],
                          compiler_params=_params("arbitrary"))(y, target)


def _position():
    x, y, c = lax.axis_index("x"), lax.axis_index("y"), lax.axis_index("c")
    chips = [(1 - x, y), (x, 1 - y), (1 - x, 1 - y)]
    return x, y, c, chips


def _shard_of(ref, col_sharded, j, n):
    off = pl.multiple_of(j * n, n)
    return ref.at[:, pl.ds(off, n)] if col_sharded else ref.at[pl.ds(off, n), :]


def _half_of(ref, col_sharded, h, n):
    off = pl.multiple_of(h * n, n)
    return ref.at[pl.ds(off, n), :] if col_sharded else ref.at[:, pl.ds(off, n)]


def _piece_of(ref, col_sharded, chip, n_block, half, n_half):
    block = pl.ds(pl.multiple_of(chip * n_block, n_block), n_block)
    part = pl.ds(pl.multiple_of(half * n_half, n_half), n_half)
    return ref.at[part, block] if col_sharded else ref.at[block, part]


class _Rider(NamedTuple):
    operands: list
    out_shapes: list
    aliases: dict
    n_sems: int
    start: Callable
    finish: Callable


def _rider_call_args(rider, n_in, n_out):
    if rider is None:
        return [], [], [], [], {}
    sems = [pltpu.SemaphoreType.DMA((rider.n_sems,))] * 2
    aliases = {n_in + i: n_out + o for i, o in rider.aliases.items()}
    return [ANY] * len(rider.operands), [ANY] * len(rider.out_shapes), list(rider.out_shapes), sems, aliases


def _ride(rider, refs, n_in, n_out, first, last):
    if rider is None:
        return refs[:n_in], refs[n_in:n_in + n_out], refs[n_in + n_out:], lambda: None, lambda: None
    r_in, r_out = len(rider.operands), len(rider.out_shapes)
    ins, rins = refs[:n_in], refs[n_in:n_in + r_in]
    outs = refs[n_in + r_in:n_in + r_in + n_out]
    routs = refs[n_in + r_in + n_out:n_in + r_in + n_out + r_out]
    scratch = refs[n_in + r_in + n_out + r_out:-2]
    send_sems, recv_sems = refs[-2:]

    def start():
        pl.when(first)(lambda: rider.start(rins, routs, send_sems, recv_sems))

    def finish():
        pl.when(last)(lambda: rider.finish(rins, routs, send_sems, recv_sems))

    return ins, outs, scratch, start, finish


def _gather_copies(outs, ici_sems, d2d_sems):
    x, y, c, chips = _position()
    my_chip = 2 * x + y
    sends, arrivals, passes, passed = [], [], [], []
    for a, out in enumerate(outs):
        cs = COL_SHARDED[BIG[a % len(BIG)]]
        rows, cols = out.shape
        n_block = (cols if cs else rows) // N_CHIPS
        n_half = (rows if cs else cols) // 2
        piece = lambda chip, half: _piece_of(out, cs, chip, n_block, half, n_half)
        for j, chip in enumerate(chips):
            k = a * 3 + j
            their = 2 * chip[0] + chip[1]
            if ici_sems is not None:
                send_sems, recv_sems = ici_sems
                sends.append(pltpu.make_async_remote_copy(
                    src_ref=piece(my_chip, c), dst_ref=piece(my_chip, c), send_sem=send_sems.at[k],
                    recv_sem=recv_sems.at[k], device_id=(chip[0], chip[1], c), device_id_type=MESH))
                arrivals.append(pltpu.make_async_remote_copy(
                    src_ref=piece(their, c), dst_ref=piece(their, c), send_sem=send_sems.at[k],
                    recv_sem=recv_sems.at[k], device_id=(chip[0], chip[1], c), device_id_type=MESH))
            if d2d_sems is not None:
                send_sems, recv_sems = d2d_sems
                passes.append(pltpu.make_async_remote_copy(
                    src_ref=piece(their, c), dst_ref=piece(their, c), send_sem=send_sems.at[k],
                    recv_sem=recv_sems.at[k], device_id=(x, y, 1 - c), device_id_type=MESH))
                passed.append(pltpu.make_async_remote_copy(
                    src_ref=piece(their, 1 - c), dst_ref=piece(their, 1 - c), send_sem=send_sems.at[k],
                    recv_sem=recv_sems.at[k], device_id=(x, y, 1 - c), device_id_type=MESH))
    return sends, arrivals, passes, passed


def _gather_weights(bufs):
    flat = [bufs[name] for name in BIG]
    n_arr = len(flat)

    def body(*refs):
        outs = refs[n_arr:2 * n_arr]
        sems = refs[2 * n_arr:]
        sends, arrivals, passes, passed = _gather_copies(outs, sems[:2], sems[2:])
        for cp in sends:
            cp.start()
        for arrival, onward in zip(arrivals, passes):
            arrival.wait_recv()
            onward.start()
        for cp in passed:
            cp.wait_recv()
        for cp in sends + passes:
            cp.wait_send()

    outs = pl.pallas_call(
        body, name="gather_weights", in_specs=[ANY] * n_arr, out_specs=[ANY] * n_arr,
        out_shape=[jax.ShapeDtypeStruct(a.shape, BF16) for a in flat],
        input_output_aliases={a: a for a in range(n_arr)},
        scratch_shapes=[pltpu.SemaphoreType.DMA((n_arr * 3,))] * 4,
    )(*flat)
    return dict(zip(BIG, outs))


def _gather_rider(bufs, over_ici):
    flat = [bufs[name] for name in BIG]

    def copies(routs, send_sems, recv_sems):
        sems = (send_sems, recv_sems)
        sends, arrivals, passes, passed = _gather_copies(routs, sems if over_ici else None, None if over_ici else sems)
        return (sends, arrivals) if over_ici else (passes, passed)

    def start(rins, routs, send_sems, recv_sems):
        for cp in copies(routs, send_sems, recv_sems)[0]:
            cp.start()

    def finish(rins, routs, send_sems, recv_sems):
        out, due = copies(routs, send_sems, recv_sems)
        for cp in due:
            cp.wait_recv()
        for cp in out:
            cp.wait_send()

    return _Rider(flat, [jax.ShapeDtypeStruct(a.shape, a.dtype) for a in flat], {i: i for i in range(len(flat))},
                  len(flat) * 3, start, finish)


def _half_shape(shape, col_sharded):
    r, c = shape
    return (r // 2, c) if col_sharded else (r, c // 2)


def _exchange_rider(grads):
    flat = [grads[name] for name in BIG]

    def copies(rins, routs, send_sems, recv_sems):
        x, y, c, _ = _position()
        out = []
        for a, name in enumerate(BIG):
            cs = COL_SHARDED[name]
            n = routs[a].shape[0] if cs else routs[a].shape[1]
            out.append(pltpu.make_async_remote_copy(
                src_ref=_half_of(rins[a], cs, 1 - c, n), dst_ref=routs[a], send_sem=send_sems.at[a],
                recv_sem=recv_sems.at[a], device_id=(x, y, 1 - c), device_id_type=MESH))
        return out

    def start(rins, routs, send_sems, recv_sems):
        for cp in copies(rins, routs, send_sems, recv_sems):
            cp.start()

    def finish(rins, routs, send_sems, recv_sems):
        cps = copies(rins, routs, send_sems, recv_sems)
        for cp in cps:
            cp.wait_recv()
        for cp in cps:
            cp.wait_send()

    return _Rider(flat, [jax.ShapeDtypeStruct(_half_shape(a.shape, COL_SHARDED[name]), F32)
                         for a, name in zip(flat, BIG)], {}, len(flat), start, finish)


def _scatter_to_owners(halves):
    rider = _scatter_rider(halves)

    def body(*refs):
        n = len(rider.operands)
        args = (refs[:n], refs[n:2 * n], *refs[2 * n:])
        rider.start(*args)
        rider.finish(*args)

    outs = pl.pallas_call(
        body, name="scatter_to_owners", in_specs=[ANY] * len(rider.operands), out_specs=[ANY] * len(rider.out_shapes),
        out_shape=rider.out_shapes, scratch_shapes=[pltpu.SemaphoreType.DMA((rider.n_sems,))] * 2,
    )(*rider.operands)
    return dict(zip(BIG, outs))


def _scatter_rider(halves):
    flat = [halves[name] for name in BIG]

    def slots_shape(a, cs):
        r, c = a.shape
        return (N_CHIPS - 1,) + ((r, c // N_CHIPS) if cs else (r // N_CHIPS, c))

    def copies(rins, routs, send_sems, recv_sems):
        x, y, c, chips = _position()
        out = []
        for a, name in enumerate(BIG):
            cs = COL_SHARDED[name]
            n = routs[a].shape[2] if cs else routs[a].shape[1]
            for j, chip in enumerate(chips):
                k = a * 3 + j
                out.append(pltpu.make_async_remote_copy(
                    src_ref=_shard_of(rins[a], cs, 2 * chip[0] + chip[1], n), dst_ref=routs[a].at[j],
                    send_sem=send_sems.at[k], recv_sem=recv_sems.at[k], device_id=(chip[0], chip[1], c),
                    device_id_type=MESH))
        return out

    def start(rins, routs, send_sems, recv_sems):
        for cp in copies(rins, routs, send_sems, recv_sems):
            cp.start()

    def finish(rins, routs, send_sems, recv_sems):
        cps = copies(rins, routs, send_sems, recv_sems)
        for cp in cps:
            cp.wait_recv()
        for cp in cps:
            cp.wait_send()

    return _Rider(flat, [jax.ShapeDtypeStruct(slots_shape(a, COL_SHARDED[name]), a.dtype) for a, name in zip(flat, BIG)],
                  {}, len(flat) * 3, start, finish)


def _join_halves(halves):
    flat = [halves[name] for name in BIG]
    n_w = len(flat)

    def body(*refs):
        outs = refs[n_w:2 * n_w]
        send_sems, recv_sems = refs[2 * n_w:]
        x, y, c, _ = _position()
        sends, recvs = [], []
        for w, name in enumerate(BIG):
            cs = COL_SHARDED[name]
            n = (outs[w].shape[1] if cs else outs[w].shape[2]) // 2

            def half(h):
                part = pl.ds(pl.multiple_of(h * n, n), n)
                return outs[w].at[:, part, :] if cs else outs[w].at[:, :, part]

            sends.append(pltpu.make_async_remote_copy(
                src_ref=half(c), dst_ref=half(c), send_sem=send_sems.at[w], recv_sem=recv_sems.at[w],
                device_id=(x, y, 1 - c), device_id_type=MESH))
            recvs.append(pltpu.make_async_remote_copy(
                src_ref=half(1 - c), dst_ref=half(1 - c), send_sem=send_sems.at[w], recv_sem=recv_sems.at[w],
                device_id=(x, y, 1 - c), device_id_type=MESH))
        for cp in sends:
            cp.start()
        for cp in recvs:
            cp.wait_recv()
        for cp in sends:
            cp.wait_send()

    outs = pl.pallas_call(
        body, name="join_halves", in_specs=[ANY] * n_w, out_specs=[ANY] * n_w,
        out_shape=[jax.ShapeDtypeStruct(a.shape, F32) for a in flat],
        input_output_aliases={w: w for w in range(n_w)},
        scratch_shapes=[pltpu.SemaphoreType.DMA((n_w,)), pltpu.SemaphoreType.DMA((n_w,))],
    )(*flat)
    return dict(zip(BIG, outs))


def _small_rider(packed):
    r, lanes = packed.shape

    def copies(rins, routs, send_sems, recv_sems):
        x, y, c, _ = _position()
        me = 4 * x + 2 * y + c
        local = pltpu.make_async_copy(rins[0], routs[0].at[me], send_sems.at[N_DEV - 1])
        sends, recvs = [], []
        for k in range(1, N_DEV):
            px, py, pc = x ^ (k >> 2), y ^ ((k >> 1) & 1), c ^ (k & 1)
            sends.append(pltpu.make_async_remote_copy(
                src_ref=rins[0], dst_ref=routs[0].at[me], send_sem=send_sems.at[k - 1], recv_sem=recv_sems.at[k - 1],
                device_id=(px, py, pc), device_id_type=MESH))
            recvs.append(pltpu.make_async_remote_copy(
                src_ref=rins[0], dst_ref=routs[0].at[4 * px + 2 * py + pc], send_sem=send_sems.at[k - 1],
                recv_sem=recv_sems.at[k - 1], device_id=(px, py, pc), device_id_type=MESH))
        return local, sends, recvs

    def start(rins, routs, send_sems, recv_sems):
        local, sends, _ = copies(rins, routs, send_sems, recv_sems)
        local.start()
        for cp in sends:
            cp.start()

    def finish(rins, routs, send_sems, recv_sems):
        local, sends, recvs = copies(rins, routs, send_sems, recv_sems)
        for cp in recvs:
            cp.wait_recv()
        for cp in sends:
            cp.wait_send()
        local.wait()

    return _Rider([packed], [jax.ShapeDtypeStruct((N_DEV, r, lanes), F32)], {}, N_DEV, start, finish)


def _pack_small(t):
    return jnp.concatenate([t[name].reshape(-1, LANES) for name in SMALL], axis=0)


def _unpack_small(packed, like):
    out, row = {}, 0
    for name in SMALL:
        n = like[name].size // LANES
        out[name] = packed[row:row + n].reshape(like[name].shape)
        row += n
    return out


def kernel(x, p, w_in, vn_g, vn_b, w_s, b_s, w_pa, w_pb, w_out, w_pe, w_pg, ln_g, ln_b, loss_target, m_w_in, m_vn_g, m_vn_b, m_w_s, m_b_s, m_w_pa, m_w_pb, m_w_out, m_w_pe, m_w_pg, m_ln_g, m_ln_b, v_w_in, v_vn_g, v_vn_b, v_w_s, v_b_s, v_w_pa, v_w_pb, v_w_out, v_w_pe, v_w_pg, v_ln_g, v_ln_b):
    weights = dict(w_in=w_in, vn_g=vn_g, vn_b=vn_b, w_s=w_s, b_s=b_s, w_pa=w_pa, w_pb=w_pb, w_out=w_out, w_pe=w_pe,
                   w_pg=w_pg, ln_g=ln_g, ln_b=ln_b)
    mom1 = dict(w_in=m_w_in, vn_g=m_vn_g, vn_b=m_vn_b, w_s=m_w_s, b_s=m_b_s, w_pa=m_w_pa, w_pb=m_w_pb, w_out=m_w_out,
                w_pe=m_w_pe, w_pg=m_w_pg, ln_g=m_ln_g, ln_b=m_ln_b)
    mom2 = dict(w_in=v_w_in, vn_g=v_vn_g, vn_b=v_vn_b, w_s=v_w_s, b_s=v_b_s, w_pa=v_w_pa, w_pb=v_w_pb, w_out=v_w_out,
                w_pe=v_w_pe, w_pg=v_w_pg, ln_g=v_ln_g, ln_b=v_ln_b)
    nl, d = vn_g.shape
    chunk = w_s.shape[2]
    assert chunk == LANES and w_s.shape[3] == LANES and d % LANES == 0
    alpha = (2 * nl) ** 0.25
    pos = tuple(lax.axis_index(a).astype(jnp.int32).reshape(1) for a in ("c", "x", "y"))

    placed = [{name: _cast_into_place(weights[name], l, pos, COL_SHARDED[name]) for name in BIG} for l in range(nl)]
    full = [_gather_weights(placed[0])] + [None] * (nl - 1)
    causal = jnp.tril(jnp.ones((chunk, chunk), dtype=bool))
    ws_m = jnp.where(causal, w_s, 0.0).astype(BF16)
    ws_mt = jnp.swapaxes(ws_m, 2, 3)
    bs_t = jnp.swapaxes(b_s, 1, 2)

    xs, projs, yas, os_ = [x[0]], [], [], []
    for l in range(nl):
        proj = _proj_fwd(xs[l], full[l]["w_in"])
        ya = _gmlp_fwd(proj, vn_g[l:l + 1], vn_b[l:l + 1], ws_m[l], bs_t[l])
        more = l + 1 < nl
        o, arrived = _attn_fwd(proj, _gather_rider(placed[l + 1], over_ici=True) if more else None)
        x_next, handed = _post_fwd(ya, o, proj, xs[l], p[l, 0], full[l]["w_pa"], full[l]["w_pb"], full[l]["w_out"],
                                   full[l]["w_pe"], full[l]["w_pg"], ln_g[l:l + 1], ln_b[l:l + 1], alpha,
                                   _gather_rider(dict(zip(BIG, arrived)), over_ici=False) if more else None)
        if more:
            full[l + 1] = dict(zip(BIG, handed))
        xs.append(x_next)
        projs.append(proj)
        yas.append(ya)
        os_.append(o)

    dx, loss_tile = _loss_head(xs[nl], loss_target[0])
    loss = lax.psum(loss_tile[0, 0], ("x", "y", "c"))

    big_grads, received, slots = [None] * nl, [None] * nl, [None] * nl
    chip_sums = None
    small_grads = [None] * nl
    for l in reversed(range(nl)):
        w = full[l]
        (dproj, dxr, dya, do, de, h1b, dzg, merged, dh1, dpa, yb, dpb, dln_g, dln_b) = _post_bwd(
            dx, yas[l], os_[l], projs[l], xs[l], p[l, 0], w["w_pa"], w["w_pb"], w["w_out"], w["w_pe"], w["w_pg"],
            ln_g[l:l + 1], ln_b[l:l + 1], alpha)
        (dq, dk, dv), scattered = _attn_bwd(projs[l], os_[l], do,
                                            _scatter_rider(chip_sums) if chip_sums is not None else None)
        if chip_sums is not None:
            slots[l + 1] = dict(zip(BIG, scattered))
        dproj, dvn_g, dvn_b, dw_s, dbs_cols = _gmlp_bwd(dproj, projs[l], dya, dq, dk, dv, vn_g[l:l + 1],
                                                         vn_b[l:l + 1], ws_m[l], ws_mt[l], bs_t[l])
        partial = _pack_small(dict(vn_g=dvn_g[0], vn_b=dvn_b[0], w_s=dw_s, b_s=dbs_cols[:, :b_s.shape[1]].T,
                                   ln_g=dln_g[0], ln_b=dln_b[0]))
        dw_in, (partials,) = _matmul_tn(xs[l], dproj, _small_rider(partial))
        small_grads[l] = _unpack_small(_sum_slots(partials), {name: weights[name][0] for name in SMALL})
        big_grads[l] = dict(w_in=dw_in, w_pa=_matmul_tn(yas[l], dpa), w_pb=_matmul_tn(yb, dpb),
                            w_out=_matmul_tn(merged, dh1), w_pe=_matmul_tn(p[l, 0], de), w_pg=_matmul_tn(h1b, dzg))
        dx, from_sibling = _dx_matmul(dxr, dproj, w["w_in"], _exchange_rider(big_grads[l]))
        received[l] = dict(zip(BIG, from_sibling))
        chip_sums = {name: _add_own_half(big_grads[l][name], received[l][name], pos, COL_SHARDED[name])
                     for name in BIG}
    slots[0] = _scatter_to_owners(chip_sums)
    reduced = {}
    for name in BIG:
        buf = None
        for l in range(nl):
            buf = _reduce_block(buf, big_grads[l][name], received[l][name], slots[l][name], l, nl, pos,
                                COL_SHARDED[name])
        reduced[name] = buf
    grads = _join_halves(reduced)

    small_like = {name: weights[name] for name in SMALL}
    grads.update({name: jnp.stack([small_grads[l][name] for l in range(nl)]) for name in SMALL})

    delta, new_m, new_v = {}, {}, {}
    for name in BIG:
        sh = weights[name].shape
        flat = lambda a: a.reshape(sh[0] * sh[1], sh[2])
        dl, nm, nv, g_out = _adamw(flat(weights[name]), flat(grads[name]), flat(mom1[name]), flat(mom2[name]),
                                   emit_grad=True)
        delta[name], new_m[name], new_v[name] = dl.reshape(sh), nm.reshape(sh), nv.reshape(sh)
        grads[name] = g_out.reshape(sh)
    dl, nm, nv = _adamw(_pack_small(small_like), _pack_small({n: grads[n] for n in SMALL}),
                        _pack_small({n: mom1[n] for n in SMALL}), _pack_small({n: mom2[n] for n in SMALL}))
    delta.update(_unpack_small(dl, small_like))
    new_m.update(_unpack_small(nm, small_like))
    new_v.update(_unpack_small(nv, small_like))

    return (loss, dx[None], *[grads[n] for n in WEIGHTS], *[delta[n] for n in WEIGHTS],
            *[new_m[n] for n in WEIGHTS], *[new_v[n] for n in WEIGHTS])
```

```python
import math
from typing import Callable, NamedTuple

import jax
import jax.numpy as jnp
from jax import lax
from jax.experimental import pallas as pl
from jax.experimental.pallas import tpu as pltpu

F32 = jnp.float32
BF16 = jnp.bfloat16
LANES = 128
SUBLANES = 8
HEAD_DIM = 64
HEADS_PER_BLOCK = LANES // HEAD_DIM
LN_EPS = 1e-5
N_IN = 9
N_CHIPS = 4
N_DEV = 8
ADAM_LR = 0.001
ADAM_B1 = 0.9
ADAM_B2 = 0.999
ADAM_EPS = 1e-08
ADAM_WD = 0.01
ADAM_STEP = 10
MESH = pl.DeviceIdType.MESH
ANY = pl.BlockSpec(memory_space=pl.ANY)
BIG = ("w_in", "w_pa", "w_pb", "w_out", "w_pe", "w_pg")
COL_SHARDED = {"w_in": True, "w_pa": False, "w_pb": False, "w_out": False, "w_pe": True, "w_pg": False}
SMALL = ("vn_g", "vn_b", "w_s", "b_s", "ln_g", "ln_b")
WEIGHTS = ("w_in", "vn_g", "vn_b", "w_s", "b_s", "w_pa", "w_pb", "w_out", "w_pe", "w_pg", "ln_g", "ln_b")


def _params(*sem):
    return pltpu.CompilerParams(dimension_semantics=sem)


def _dot(a, b):
    return jnp.dot(a, b, preferred_element_type=F32)


def _dot_nt(a, b):
    return lax.dot_general(a, b, (((1,), (1,)), ((), ())), preferred_element_type=F32)


def _dot_tn(a, b):
    return lax.dot_general(a, b, (((0,), (0,)), ((), ())), preferred_element_type=F32)


def _sigmoid(a):
    return 1.0 / (1.0 + jnp.exp(-a))


def _row_tile(rows, cols, cap_bytes):
    best = None
    for t in range(16, rows + 1, 16):
        if rows % t == 0 and t * cols * 4 <= cap_bytes:
            best = t
    return best or rows


def _col_tile(cols, cap):
    best = LANES
    for t in range(LANES, min(cols, cap) + 1, LANES):
        if cols % t == 0:
            best = t
    return best


def _ln_stats(h):
    mu = jnp.mean(h, axis=-1, keepdims=True)
    hc = h - mu
    var = jnp.mean(hc * hc, axis=-1, keepdims=True)
    rstd = lax.rsqrt(var + LN_EPS)
    return hc * rstd, rstd


def _ln_bwd(dxhat, xhat, rstd):
    m1 = jnp.mean(dxhat, axis=-1, keepdims=True)
    m2 = jnp.mean(dxhat * xhat, axis=-1, keepdims=True)
    return rstd * (dxhat - m1 - xhat * m2)


def _sum_rows8(a):
    t, d = a.shape
    return jnp.sum(a.reshape(t // SUBLANES, SUBLANES, d), axis=0)


def _chip(x_ref, y_ref):
    return 2 * x_ref[0] + y_ref[0]


def _cast_into_place(shards, l, pos, col_sharded):
    _, r, c = shards.shape
    tr = _row_tile(r, c, 2 << 20)
    nb = r // tr

    def body(c_ref, x_ref, y_ref, a_ref, o_ref):
        o_ref[...] = a_ref[...].astype(BF16)

    if col_sharded:
        out_spec = pl.BlockSpec((tr, c), lambda i, c_ref, x_ref, y_ref: (i, _chip(x_ref, y_ref)))
        full = (r, c * N_CHIPS)
    else:
        out_spec = pl.BlockSpec((tr, c), lambda i, c_ref, x_ref, y_ref: (_chip(x_ref, y_ref) * nb + i, 0))
        full = (r * N_CHIPS, c)
    grid_spec = pltpu.PrefetchScalarGridSpec(
        num_scalar_prefetch=3, grid=(nb,),
        in_specs=[pl.BlockSpec((None, tr, c), lambda i, c_ref, x_ref, y_ref: (l, i, 0))], out_specs=out_spec)
    return pl.pallas_call(body, name="cast_into_place", grid_spec=grid_spec,
                          out_shape=jax.ShapeDtypeStruct(full, BF16),
                          compiler_params=_params("parallel"))(*pos, shards)


def _add_own_half(own, recv, pos, col_sharded):
    r, c = recv.shape
    tr = _row_tile(r, c, 4 << 20)
    nb = r // tr

    def body(c_ref, x_ref, y_ref, own_ref, recv_ref, o_ref):
        o_ref[...] = (own_ref[...] + recv_ref[...]).astype(BF16)

    if col_sharded:
        own_spec = pl.BlockSpec((tr, c), lambda i, c_ref, x_ref, y_ref: (c_ref[0] * nb + i, 0))
    else:
        own_spec = pl.BlockSpec((tr, c), lambda i, c_ref, x_ref, y_ref: (i, c_ref[0]))
    spec = pl.BlockSpec((tr, c), lambda i, c_ref, x_ref, y_ref: (i, 0))
    grid_spec = pltpu.PrefetchScalarGridSpec(num_scalar_prefetch=3, grid=(nb,), in_specs=[own_spec, spec],
                                             out_specs=spec)
    return pl.pallas_call(body, name="add_own_half", grid_spec=grid_spec,
                          out_shape=jax.ShapeDtypeStruct(recv.shape, BF16),
                          compiler_params=_params("parallel"))(*pos, own, recv)


def _reduce_block(buf, own, recv, slots, l, nl, pos, col_sharded):
    _, r, c = slots.shape
    tr = _row_tile(r, c, 2 << 20)
    nb = r // tr

    def body(c_ref, x_ref, y_ref, own_ref, recv_ref, slots_ref, *rest):
        acc = own_ref[...] + recv_ref[...]
        for j in range(N_CHIPS - 1):
            acc = acc + slots_ref[j].astype(F32)
        rest[-1][...] = acc

    if col_sharded:
        own_spec = pl.BlockSpec((tr, c), lambda i, c_ref, x_ref, y_ref: (c_ref[0] * nb + i, _chip(x_ref, y_ref)))
        recv_spec = pl.BlockSpec((tr, c), lambda i, c_ref, x_ref, y_ref: (i, _chip(x_ref, y_ref)))
        out_spec = pl.BlockSpec((None, tr, c), lambda i, c_ref, x_ref, y_ref: (l, c_ref[0] * nb + i, 0))
        out_shape = (nl, 2 * r, c)
    else:
        own_spec = pl.BlockSpec((tr, c), lambda i, c_ref, x_ref, y_ref: (_chip(x_ref, y_ref) * nb + i, c_ref[0]))
        recv_spec = pl.BlockSpec((tr, c), lambda i, c_ref, x_ref, y_ref: (_chip(x_ref, y_ref) * nb + i, 0))
        out_spec = pl.BlockSpec((None, tr, c), lambda i, c_ref, x_ref, y_ref: (l, i, c_ref[0]))
        out_shape = (nl, r, 2 * c)
    in_specs = [own_spec, recv_spec,
                pl.BlockSpec((N_CHIPS - 1, tr, c), lambda i, c_ref, x_ref, y_ref: (0, i, 0))]
    args = [*pos, own, recv, slots]
    aliases = {}
    if buf is not None:
        in_specs.append(ANY)
        args.append(buf)
        aliases = {len(args) - 1: 0}
    grid_spec = pltpu.PrefetchScalarGridSpec(num_scalar_prefetch=3, grid=(nb,), in_specs=in_specs,
                                             out_specs=out_spec)
    return pl.pallas_call(body, name="reduce_block", grid_spec=grid_spec,
                          out_shape=jax.ShapeDtypeStruct(out_shape, F32), input_output_aliases=aliases,
                          compiler_params=_params("parallel"))(*args)


def _sum_slots(a):
    n, r, c = a.shape
    tr = _row_tile(r, c * n, 4 << 20)

    def body(a_ref, o_ref):
        acc = a_ref[0]
        for s in range(1, n):
            acc = acc + a_ref[s]
        o_ref[...] = acc

    return pl.pallas_call(body, name="sum_slots", grid=(r // tr,),
                          in_specs=[pl.BlockSpec((n, tr, c), lambda i: (0, i, 0))],
                          out_specs=pl.BlockSpec((tr, c), lambda i: (i, 0)),
                          out_shape=jax.ShapeDtypeStruct((r, c), F32),
                          compiler_params=_params("parallel"))(a)


def _adamw(w, g, m, v, emit_grad=False):
    r, c = w.shape
    tr = _row_tile(r, c, 2 << 20)
    n_out = 4 if emit_grad else 3

    def body(w_ref, g_ref, m_ref, v_ref, d_ref, nm_ref, nv_ref, *g_out):
        gg = g_ref[...]
        nm = ADAM_B1 * m_ref[...] + (1.0 - ADAM_B1) * gg
        nv = ADAM_B2 * v_ref[...] + (1.0 - ADAM_B2) * (gg * gg)
        m_hat = nm / (1.0 - ADAM_B1 ** ADAM_STEP)
        v_hat = nv / (1.0 - ADAM_B2 ** ADAM_STEP)
        d_ref[...] = -ADAM_LR * (m_hat / (jnp.sqrt(v_hat) + ADAM_EPS) + ADAM_WD * w_ref[...])
        nm_ref[...] = nm
        nv_ref[...] = nv
        for ref in g_out:
            ref[...] = gg

    spec = pl.BlockSpec((tr, c), lambda i: (i, 0))
    sh = jax.ShapeDtypeStruct((r, c), F32)
    return pl.pallas_call(body, name="adamw", grid=(r // tr,), in_specs=[spec] * 4, out_specs=[spec] * n_out,
                          out_shape=[sh] * n_out, compiler_params=_params("parallel"))(w, g, m, v)


def _proj_fwd(x, w, rider=None):
    s, d = x.shape
    n = w.shape[1]
    tm, tn = min(s, 1024), _col_tile(n, 1024)
    nm, nn = s // tm, n // tn

    def body(*refs):
        i, j = pl.program_id(0), pl.program_id(1)
        first = jnp.logical_and(i == 0, j == 0)
        last = jnp.logical_and(i == nm - 1, j == nn - 1)
        (x_ref, w_ref), (o_ref,), _, start, finish = _ride(rider, refs, 2, 1, first, last)
        start()
        o_ref[...] = _dot(x_ref[...].astype(BF16), w_ref[...]).astype(BF16)
        finish()

    r_in, r_out, r_shapes, r_sems, r_alias = _rider_call_args(rider, 2, 1)
    outs = pl.pallas_call(body, name="proj_fwd", grid=(nm, nn),
                          in_specs=[pl.BlockSpec((tm, d), lambda i, j: (i, 0)),
                                    pl.BlockSpec((d, tn), lambda i, j: (0, j))] + r_in,
                          out_specs=[pl.BlockSpec((tm, tn), lambda i, j: (i, j))] + r_out,
                          out_shape=[jax.ShapeDtypeStruct((s, n), BF16)] + r_shapes, scratch_shapes=r_sems,
                          input_output_aliases=r_alias,
                          compiler_params=_params("arbitrary", "arbitrary"))(x, w, *(rider.operands if rider else []))
    return outs[0], outs[1:]


def _matmul_tn(a, b, rider=None):
    s, m = a.shape
    n = b.shape[1]
    tk, tn = min(s, 1024), _col_tile(n, 1024)
    nn, nk = n // tn, s // tk

    def body(*refs):
        j, k = pl.program_id(0), pl.program_id(1)
        first = jnp.logical_and(j == 0, k == 0)
        last = jnp.logical_and(j == nn - 1, k == nk - 1)
        (a_ref, b_ref), (o_ref,), _, start, finish = _ride(rider, refs, 2, 1, first, last)
        start()

        @pl.when(k == 0)
        def _():
            o_ref[...] = jnp.zeros_like(o_ref)

        o_ref[...] += _dot_tn(a_ref[...].astype(BF16), b_ref[...].astype(BF16))
        finish()

    r_in, r_out, r_shapes, r_sems, r_alias = _rider_call_args(rider, 2, 1)
    outs = pl.pallas_call(body, name="matmul_tn", grid=(nn, nk),
                          in_specs=[pl.BlockSpec((tk, m), lambda j, k: (k, 0)),
                                    pl.BlockSpec((tk, tn), lambda j, k: (k, j))] + r_in,
                          out_specs=[pl.BlockSpec((m, tn), lambda j, k: (0, j))] + r_out,
                          out_shape=[jax.ShapeDtypeStruct((m, n), F32)] + r_shapes, scratch_shapes=r_sems,
                          input_output_aliases=r_alias,
                          compiler_params=_params("arbitrary", "arbitrary"))(a, b, *(rider.operands if rider else []))
    return outs[0] if rider is None else (outs[0], outs[1:])


def _dx_matmul(dxr, dproj, w, rider=None):
    s, d = dxr.shape
    n = w.shape[1]
    tm, tk = min(s, 1024), _col_tile(n, 1024)
    nm, nk = s // tm, n // tk

    def body(*refs):
        i, k = pl.program_id(0), pl.program_id(1)
        first = jnp.logical_and(i == 0, k == 0)
        last = jnp.logical_and(i == nm - 1, k == nk - 1)
        (r_ref, g_ref, w_ref), (o_ref,), _, start, finish = _ride(rider, refs, 3, 1, first, last)
        start()

        @pl.when(k == 0)
        def _():
            o_ref[...] = r_ref[...]

        o_ref[...] += _dot_nt(g_ref[...], w_ref[...])
        finish()

    r_in, r_out, r_shapes, r_sems, r_alias = _rider_call_args(rider, 3, 1)
    outs = pl.pallas_call(body, name="dx_matmul", grid=(nm, nk),
                          in_specs=[pl.BlockSpec((tm, d), lambda i, k: (i, 0)),
                                    pl.BlockSpec((tm, tk), lambda i, k: (i, k)),
                                    pl.BlockSpec((d, tk), lambda i, k: (0, k))] + r_in,
                          out_specs=[pl.BlockSpec((tm, d), lambda i, k: (i, 0))] + r_out,
                          out_shape=[jax.ShapeDtypeStruct((s, d), F32)] + r_shapes, scratch_shapes=r_sems,
                          input_output_aliases=r_alias,
                          compiler_params=_params("arbitrary", "arbitrary"))(dxr, dproj, w,
                                                                             *(rider.operands if rider else []))
    return outs[0], outs[1:]


def _mix_chunks(ws_ref, src_ref, dst_ref, bias_ref, t, groups, chunk):
    for c in range(t // chunk):
        rows = slice(c * chunk, (c + 1) * chunk)
        for g in range(groups):
            cols = slice(g * LANES, (g + 1) * LANES)
            val = _dot(ws_ref[g], src_ref[rows, cols])
            if bias_ref is not None:
                val = val + bias_ref[:, g:g + 1]
            dst_ref[rows, cols] = val


def _gmlp_fwd(proj, vn_g, vn_b, ws_m, bs_t):
    s = proj.shape[0]
    d = proj.shape[1] // N_IN
    groups, chunk = ws_m.shape[0], ws_m.shape[1]
    t = min(s, 512)

    def body(u_ref, v_ref, ga_ref, g_ref, b_ref, ws_ref, bs_ref, o_ref, vn_ref, mix_ref):
        xhat, _ = _ln_stats(v_ref[...].astype(F32))
        vn_ref[...] = (xhat * g_ref[...] + b_ref[...]).astype(BF16)
        _mix_chunks(ws_ref, vn_ref, mix_ref, bs_ref, t, groups, chunk)
        ga = ga_ref[...].astype(F32)
        o_ref[...] = (u_ref[...].astype(F32) * mix_ref[...] * (ga * _sigmoid(ga))).astype(BF16)

    col = lambda j: pl.BlockSpec((t, d), lambda i: (i, j))
    full = lambda a: pl.BlockSpec(a.shape, lambda i: (0,) * a.ndim)
    return pl.pallas_call(body, name="gmlp_fwd", grid=(s // t,),
                          in_specs=[col(0), col(1), col(2), full(vn_g), full(vn_b), full(ws_m), full(bs_t)],
                          out_specs=pl.BlockSpec((t, d), lambda i: (i, 0)),
                          out_shape=jax.ShapeDtypeStruct((s, d), BF16),
                          scratch_shapes=[pltpu.VMEM((t, d), BF16), pltpu.VMEM((t, d), F32)],
                          compiler_params=_params("parallel"))(proj, proj, proj, vn_g, vn_b, ws_m, bs_t)


def _gmlp_bwd(dproj, proj, dya, dq, dk, dv, vn_g, vn_b, ws_m, ws_mt, bs_t):
    s = proj.shape[0]
    d = proj.shape[1] // N_IN
    groups, chunk = ws_m.shape[0], ws_m.shape[1]
    t = min(s, 256)
    nsteps = s // t

    def body(dproj_hbm, u_ref, v_ref, ga_ref, dya_ref, dq_ref, dk_ref, dv_ref, g_ref, b_ref, ws_ref, wst_ref, bs_ref,
             o_ref, dg_ref, db_ref, dws_ref, dbs_ref,
             vn_ref, mix_ref, dm_ref, dvn_ref, dbs_acc, dg_acc, db_acc):
        del dproj_hbm
        i = pl.program_id(0)

        @pl.when(i == 0)
        def _():
            dws_ref[...] = jnp.zeros_like(dws_ref)
            dbs_acc[...] = jnp.zeros_like(dbs_acc)
            dg_acc[...] = jnp.zeros_like(dg_acc)
            db_acc[...] = jnp.zeros_like(db_acc)

        xhat, rstd = _ln_stats(v_ref[...].astype(F32))
        vn_ref[...] = (xhat * g_ref[...] + b_ref[...]).astype(BF16)
        _mix_chunks(ws_ref, vn_ref, mix_ref, bs_ref, t, groups, chunk)
        ga = ga_ref[...].astype(F32)
        sg = _sigmoid(ga)
        silu = ga * sg
        dsilu = sg * (1.0 + ga * (1.0 - sg))
        u = u_ref[...].astype(F32)
        dya_f = dya_ref[...].astype(F32)
        mix = mix_ref[...]
        o_ref[:, 0:d] = (dya_f * mix * silu).astype(BF16)
        o_ref[:, 2 * d:3 * d] = (dya_f * u * mix * dsilu).astype(BF16)
        dmix = dya_f * u * silu
        dm_ref[...] = dmix.astype(BF16)
        for c in range(t // chunk):
            dbs_acc[...] += dmix[c * chunk:(c + 1) * chunk, :]
        _mix_chunks(wst_ref, dm_ref, dvn_ref, None, t, groups, chunk)
        for c in range(t // chunk):
            rows = slice(c * chunk, (c + 1) * chunk)
            for g in range(groups):
                cols = slice(g * LANES, (g + 1) * LANES)
                dws_ref[g] += _dot_nt(dm_ref[rows, cols], vn_ref[rows, cols])
        dvn = dvn_ref[...]
        dg_acc[...] += _sum_rows8(dvn * xhat)
        db_acc[...] += _sum_rows8(dvn)
        o_ref[:, d:2 * d] = _ln_bwd(dvn * g_ref[...], xhat, rstd).astype(BF16)
        o_ref[:, 3 * d:4 * d] = dq_ref[...]
        o_ref[:, 4 * d:5 * d] = dk_ref[...]
        o_ref[:, 5 * d:6 * d] = dv_ref[...]

        @pl.when(i == nsteps - 1)
        def _():
            row = lax.broadcasted_iota(jnp.int32, (chunk, chunk), 0)
            col = lax.broadcasted_iota(jnp.int32, (chunk, chunk), 1)
            for g in range(groups):
                dws_ref[g] = jnp.where(col <= row, dws_ref[g], 0.0)
            lane = lax.broadcasted_iota(jnp.int32, (chunk, LANES), 1)
            res = jnp.zeros((chunk, LANES), F32)
            for g in range(groups):
                tot = jnp.sum(dbs_acc[:, g * LANES:(g + 1) * LANES], axis=1, keepdims=True)
                res = jnp.where(lane == g, tot, res)
            dbs_ref[...] = res
            dg_ref[...] = jnp.sum(dg_acc[...], axis=0, keepdims=True)
            db_ref[...] = jnp.sum(db_acc[...], axis=0, keepdims=True)

    col = lambda j: pl.BlockSpec((t, d), lambda i: (i, j))
    tok = pl.BlockSpec((t, d), lambda i: (i, 0))
    full = lambda a: pl.BlockSpec(a.shape, lambda i: (0,) * a.ndim)
    vec = jax.ShapeDtypeStruct((1, d), F32)
    outs = pl.pallas_call(
        body, name="gmlp_bwd", grid=(nsteps,),
        in_specs=[ANY, col(0), col(1), col(2), tok, tok, tok, tok,
                  full(vn_g), full(vn_b), full(ws_m), full(ws_mt), full(bs_t)],
        out_specs=[pl.BlockSpec((t, 6 * d), lambda i: (i, 0)),
                   pl.BlockSpec((1, d), lambda i: (0, 0)), pl.BlockSpec((1, d), lambda i: (0, 0)),
                   pl.BlockSpec((groups, chunk, chunk), lambda i: (0, 0, 0)),
                   pl.BlockSpec((chunk, LANES), lambda i: (0, 0))],
        out_shape=[jax.ShapeDtypeStruct(dproj.shape, BF16), vec, vec,
                   jax.ShapeDtypeStruct((groups, chunk, chunk), F32),
                   jax.ShapeDtypeStruct((chunk, LANES), F32)],
        scratch_shapes=[pltpu.VMEM((t, d), BF16), pltpu.VMEM((t, d), F32), pltpu.VMEM((t, d), BF16),
                        pltpu.VMEM((t, d), F32), pltpu.VMEM((chunk, d), F32),
                        pltpu.VMEM((SUBLANES, d), F32), pltpu.VMEM((SUBLANES, d), F32)],
        input_output_aliases={0: 0},
        compiler_params=_params("arbitrary"))(dproj, proj, proj, proj, dya, dq, dk, dv,
                                              vn_g, vn_b, ws_m, ws_mt, bs_t)
    return outs


ATTN_TILE = 256
ATTN_BLOCKS = 2
EXP_UNDERFLOW = -104.0


def _block(b):
    return slice(b * LANES, (b + 1) * LANES)


def _log_sigmoid(z):
    return jnp.minimum(z, 0.0) - jnp.log(1.0 + jnp.exp(-jnp.abs(z)))


def _suffix_rhs():
    row = lax.broadcasted_iota(jnp.int32, (LANES, LANES), 0)
    col = lax.broadcasted_iota(jnp.int32, (LANES, LANES), 1)
    rhs = jnp.concatenate([(row > col).astype(BF16), jnp.ones((LANES, LANES), BF16)], axis=1)
    return jnp.concatenate([rhs, rhs], axis=0)


def _suffix_sums(a, rhs_ref, t):
    hi = a.astype(BF16)
    lo = (a - hi.astype(F32)).astype(BF16)
    n = t // LANES
    inside, totals = [], []
    for c in range(n):
        cols = slice(c * LANES, (c + 1) * LANES)
        res = _dot(jnp.concatenate([hi[:, cols], lo[:, cols]], axis=1), rhs_ref[...])
        inside.append(res[:, :LANES])
        totals.append(res[:, LANES:])
    later = totals[n - 1]
    for c in reversed(range(n - 1)):
        inside[c] = inside[c] + later
        later = later + totals[c]
    return jnp.concatenate(inside, axis=1), later


def _lanes_to_tile(a, t):
    return jnp.concatenate([a] * (t // LANES), axis=1)


def _sweep_tiles(tiles, i, tq, keep_sum_ref):
    def live():
        most = keep_sum_ref[0]
        for h in range(1, keep_sum_ref.shape[0]):
            most = jnp.maximum(most, keep_sum_ref[h])
        return jnp.max(most) >= EXP_UNDERFLOW

    @pl.when(i == 0)
    def _():
        tiles([(i, True, 0, tq)])

    @pl.when(i > 0)
    def _():
        tiles([(i, True, 0, tq), (i - 1, False, 0, tq)])

        def cond(carry):
            n, alive = carry
            return jnp.logical_and(n < i - 1, alive)

        def step(carry):
            n, _ = carry
            tiles([(i - 2 - n, False, 0, tq)])
            return n + 1, live()

        lax.while_loop(cond, step, (jnp.int32(0), live()))


def _attn_masks(t):
    lane = lax.broadcasted_iota(jnp.int32, (t, LANES), 1)
    row = lax.broadcasted_iota(jnp.int32, (t, t), 0)
    col = lax.broadcasted_iota(jnp.int32, (t, t), 1)
    return lane < HEAD_DIM, col < row


def _split_heads(a, head0):
    zero = jnp.zeros_like(a)
    return [jnp.where(head0, a, zero), jnp.where(head0, zero, a)]


def _attn_fwd(proj, rider=None):
    s = proj.shape[0]
    d = proj.shape[1] // N_IN
    hp = d // LANES
    tq = min(s, ATTN_TILE)
    nq = s // tq
    scale = HEAD_DIM ** -0.5
    assert math.log2(scale).is_integer()

    nb = ATTN_BLOCKS
    n_heads = nb * HEADS_PER_BLOCK
    assert hp % nb == 0
    wide = nb * LANES

    def body(*refs):
        h_id, i = pl.program_id(0), pl.program_id(1)
        first = jnp.logical_and(h_id == 0, i == 0)
        last = jnp.logical_and(h_id == hp // nb - 1, i == nq - 1)
        (q_ref, k_ref, v_ref, rhs_ref), (o_ref,), (acc_ref, r_ref), start, finish = _ride(rider, refs, 4, 1, first, last)
        start()
        head0, strict = _attn_masks(tq)
        qm = []
        for b in range(nb):
            qm += _split_heads((q_ref[:, _block(b)].astype(F32) * scale).astype(BF16), head0)
        acc_ref[...] = jnp.zeros_like(acc_ref)
        r_ref[...] = jnp.zeros_like(r_ref)

        def tiles(specs):
            offs = [pl.multiple_of(spec[0] * tq, tq) for spec in specs]
            k2 = {(t, b): k_ref[pl.ds(off, tq), _block(b)] for t, off in enumerate(offs) for b in range(nb)}
            v2 = {(t, b): v_ref[pl.ds(off, tq), _block(b)] for t, off in enumerate(offs) for b in range(nb)}
            rows = [slice(r0, r0 + nr) for _, _, r0, nr in specs]
            chains = [(t, h) for t in range(len(specs)) for h in range(n_heads)]
            masked = lambda c: specs[c[0]][1]
            z = {c: _dot_nt(qm[c[1]][rows[c[0]]], k2[c[0], c[1] // HEADS_PER_BLOCK]) for c in chains}
            lsz = {c: _log_sigmoid(z[c]) for c in chains}
            keep = {c: lsz[c] - z[c] for c in chains}
            keep = {c: jnp.where(strict[rows[c[0]]], keep[c], 0.0) if masked(c) else keep[c] for c in chains}
            sums = {c: _suffix_sums(keep[c], rhs_ref, tq) for c in chains}
            for h in range(n_heads):
                for t in range(len(specs)):
                    w = jnp.exp(lsz[t, h] + sums[t, h][0] + _lanes_to_tile(r_ref[h, rows[t]], tq))
                    if masked((t, h)):
                        w = jnp.where(strict[rows[t]], w, 0.0)
                    acc_ref[h, rows[t]] += _dot(w.astype(BF16), v2[t, h // HEADS_PER_BLOCK])
                    r_ref[h, rows[t]] += sums[t, h][1]

        _sweep_tiles(tiles, i, tq, r_ref)

        for b in range(nb):
            o_ref[:, _block(b)] = jnp.where(head0, acc_ref[2 * b], acc_ref[2 * b + 1])
        finish()

    rhs = _suffix_rhs()
    r_in, r_out, r_shapes, r_sems, r_alias = _rider_call_args(rider, 4, 1)
    outs = pl.pallas_call(
        body, name="attn_fwd", grid=(hp // nb, nq),
        in_specs=[pl.BlockSpec((tq, wide), lambda h, i: (i, 3 * hp // nb + h)),
                  pl.BlockSpec((s, wide), lambda h, i: (0, 4 * hp // nb + h)),
                  pl.BlockSpec((s, wide), lambda h, i: (0, 5 * hp // nb + h)),
                  pl.BlockSpec(rhs.shape, lambda h, i: (0, 0))] + r_in,
        out_specs=[pl.BlockSpec((tq, wide), lambda h, i: (i, h))] + r_out,
        out_shape=[jax.ShapeDtypeStruct((s, d), F32)] + r_shapes,
        scratch_shapes=[pltpu.VMEM((n_heads, tq, LANES), F32), pltpu.VMEM((n_heads, tq, LANES), F32)] + r_sems,
        input_output_aliases=r_alias,
        compiler_params=_params("arbitrary", "arbitrary"))(proj, proj, proj, rhs, *(rider.operands if rider else []))
    return outs[0], outs[1:]


def _attn_bwd(proj, o, do, rider=None):
    s = proj.shape[0]
    d = proj.shape[1] // N_IN
    hp = d // LANES
    tq = min(s, ATTN_TILE)
    nq = s // tq
    scale = HEAD_DIM ** -0.5
    nb = ATTN_BLOCKS
    n_heads = nb * HEADS_PER_BLOCK
    assert hp % nb == 0
    wide = nb * LANES

    def body(*refs):
        h_id, i = pl.program_id(0), pl.program_id(1)
        first = jnp.logical_and(h_id == 0, i == 0)
        last = jnp.logical_and(h_id == hp // nb - 1, i == nq - 1)
        ((q_ref, k_ref, v_ref, o_ref, do_ref, rhs_ref), (dq_ref, dk_ref, dv_ref),
         (dq_acc, dk_acc, dv_acc, rk_ref, rg_ref), start, finish) = _ride(rider, refs, 6, 3, first, last)
        start()
        head0, strict = _attn_masks(tq)
        qm, dom, delta = [], [], []
        for b in range(nb):
            qm += _split_heads((q_ref[:, _block(b)].astype(F32) * scale).astype(BF16), head0)
            dom += _split_heads(do_ref[:, _block(b)], head0)
            prod = do_ref[:, _block(b)].astype(F32) * o_ref[:, _block(b)]
            delta += [jnp.sum(jnp.where(head0, prod, 0.0), axis=1, keepdims=True),
                      jnp.sum(jnp.where(head0, 0.0, prod), axis=1, keepdims=True)]

        @pl.when(i == 0)
        def _():
            dk_acc[...] = jnp.zeros_like(dk_acc)
            dv_acc[...] = jnp.zeros_like(dv_acc)

        dq_acc[...] = jnp.zeros_like(dq_acc)
        rk_ref[...] = jnp.zeros_like(rk_ref)
        for h in range(n_heads):
            rg_ref[h] = jnp.broadcast_to(delta[h], (tq, LANES))

        def tiles(specs):
            n_t = len(specs)
            offs = [pl.multiple_of(spec[0] * tq, tq) for spec in specs]
            k2 = {(t, b): k_ref[pl.ds(off, tq), _block(b)] for t, off in enumerate(offs) for b in range(nb)}
            v2 = {(t, b): v_ref[pl.ds(off, tq), _block(b)] for t, off in enumerate(offs) for b in range(nb)}
            rows = [slice(r0, r0 + nr) for _, _, r0, nr in specs]
            chains = [(t, h) for t in range(n_t) for h in range(n_heads)]
            masked = lambda c: specs[c[0]][1]
            z = {c: _dot_nt(qm[c[1]][rows[c[0]]], k2[c[0], c[1] // HEADS_PER_BLOCK]) for c in chains}
            dw = {c: _dot_nt(dom[c[1]][rows[c[0]]], v2[c[0], c[1] // HEADS_PER_BLOCK]) for c in chains}
            lsz = {c: _log_sigmoid(z[c]) for c in chains}
            keep = {c: lsz[c] - z[c] for c in chains}
            keep = {c: jnp.where(strict[rows[c[0]]], keep[c], 0.0) if masked(c) else keep[c] for c in chains}
            ksum = {c: _suffix_sums(keep[c], rhs_ref, tq) for c in chains}
            wb, g = {}, {}
            for h in range(n_heads):
                for t in range(n_t):
                    w = jnp.exp(lsz[t, h] + ksum[t, h][0] + _lanes_to_tile(rk_ref[h, rows[t]], tq))
                    if masked((t, h)):
                        w = jnp.where(strict[rows[t]], w, 0.0)
                    wb[t, h] = w.astype(BF16)
                    g[t, h] = dw[t, h] * wb[t, h].astype(F32)
                    rk_ref[h, rows[t]] += ksum[t, h][1]
            gsum = {c: _suffix_sums(g[c], rhs_ref, tq) for c in chains}
            dzb = {}
            for h in range(n_heads):
                for t in range(n_t):
                    dz = g[t, h] - jnp.exp(lsz[t, h]) * (_lanes_to_tile(rg_ref[h, rows[t]], tq) - gsum[t, h][0])
                    if masked((t, h)):
                        dz = jnp.where(strict[rows[t]], dz, 0.0)
                    dzb[t, h] = dz.astype(BF16)
                    dq_acc[h, rows[t]] += _dot(dzb[t, h], k2[t, h // HEADS_PER_BLOCK])
                    rg_ref[h, rows[t]] -= gsum[t, h][1]
            for t in range(n_t):
                for b in range(nb):
                    h0, h1 = 2 * b, 2 * b + 1
                    dk_acc[pl.ds(offs[t], tq), _block(b)] += (_dot_tn(dzb[t, h0], qm[h0][rows[t]])
                                                              + _dot_tn(dzb[t, h1], qm[h1][rows[t]]))
                    dv_acc[pl.ds(offs[t], tq), _block(b)] += (_dot_tn(wb[t, h0], dom[h0][rows[t]])
                                                              + _dot_tn(wb[t, h1], dom[h1][rows[t]]))

        _sweep_tiles(tiles, i, tq, rk_ref)

        for b in range(nb):
            dq_ref[:, _block(b)] = (jnp.where(head0, dq_acc[2 * b], dq_acc[2 * b + 1]) * scale).astype(BF16)

        @pl.when(i == nq - 1)
        def _():
            dk_ref[...] = dk_acc[...].astype(BF16)
            dv_ref[...] = dv_acc[...].astype(BF16)

        finish()

    blk = pl.BlockSpec((tq, wide), lambda h, i: (i, h))
    seq = pl.BlockSpec((s, wide), lambda h, i: (0, h))
    sh = jax.ShapeDtypeStruct((s, d), BF16)
    rhs = _suffix_rhs()
    r_in, r_out, r_shapes, r_sems, r_alias = _rider_call_args(rider, 6, 3)
    outs = pl.pallas_call(
        body, name="attn_bwd", grid=(hp // nb, nq),
        in_specs=[pl.BlockSpec((tq, wide), lambda h, i: (i, 3 * hp // nb + h)),
                  pl.BlockSpec((s, wide), lambda h, i: (0, 4 * hp // nb + h)),
                  pl.BlockSpec((s, wide), lambda h, i: (0, 5 * hp // nb + h)),
                  blk, blk, pl.BlockSpec(rhs.shape, lambda h, i: (0, 0))] + r_in,
        out_specs=[blk, seq, seq] + r_out, out_shape=[sh, sh, sh] + r_shapes,
        scratch_shapes=[pltpu.VMEM((n_heads, tq, LANES), F32),
                        pltpu.VMEM((s, wide), F32), pltpu.VMEM((s, wide), F32),
                        pltpu.VMEM((n_heads, tq, LANES), F32),
                        pltpu.VMEM((n_heads, tq, LANES), F32)] + r_sems,
        input_output_aliases=r_alias,
        compiler_params=_params("arbitrary", "arbitrary"))(proj, proj, proj, o, do, rhs,
                                                           *(rider.operands if rider else []))
    return outs[:3], outs[3:]


def _post_math(ya_ref, o_ref, gb_ref, ma_ref, mb_ref, x_ref, p_ref, wpa_ref, wpb_ref, wout_ref, wpe_ref, wpg_ref,
               alpha):
    gb = gb_ref[...].astype(F32)
    sgb = _sigmoid(gb)
    o = o_ref[...]
    yb = (o * (gb * sgb)).astype(BF16)
    pa = _dot(ya_ref[...], wpa_ref[...])
    pb = _dot(yb, wpb_ref[...])
    sa = _sigmoid(ma_ref[...].astype(F32))
    sb = _sigmoid(mb_ref[...].astype(F32))
    merged = (sa * pa + sb * pb).astype(BF16)
    h1 = alpha * x_ref[...] + _dot(merged, wout_ref[...])
    h1b = h1.astype(BF16)
    e = _dot(p_ref[...].astype(BF16), wpe_ref[...])
    sg = _sigmoid(_dot(h1b, wpg_ref[...]))
    h2 = h1 + e * sg
    return dict(gb=gb, sgb=sgb, o=o, yb=yb, pa=pa, pb=pb, sa=sa, sb=sb, merged=merged, h1b=h1b, e=e, sg=sg, h2=h2)


def _post_specs(tm, d, ple, weights):
    tok = pl.BlockSpec((tm, d), lambda i: (i, 0))
    col = lambda j: pl.BlockSpec((tm, d), lambda i: (i, j))
    full = lambda a: pl.BlockSpec(a.shape, lambda i: (0,) * a.ndim, pipeline_mode=pl.Buffered(1))
    return tok, [tok, tok, col(6), col(7), col(8), tok, pl.BlockSpec((tm, ple), lambda i: (i, 0))] + [
        full(w) for w in weights]


def _post_fwd(ya, o, proj, x, p, w_pa, w_pb, w_out, w_pe, w_pg, ln_g, ln_b, alpha, rider=None):
    s, d = x.shape
    ple = p.shape[1]
    tm = min(s, 256)
    nsteps = s // tm
    weights = (w_pa, w_pb, w_out, w_pe, w_pg, ln_g, ln_b)

    def body(*refs):
        i = pl.program_id(0)
        ins, (out_ref,), _, start, finish = _ride(rider, refs, 14, 1, i == 0, i == nsteps - 1)
        start()
        f = _post_math(*ins[:12], alpha)
        xhat, _ = _ln_stats(f["h2"])
        out_ref[...] = xhat * ins[12][...] + ins[13][...]
        finish()

    tok, in_specs = _post_specs(tm, d, ple, weights)
    r_in, r_out, r_shapes, r_sems, r_alias = _rider_call_args(rider, 14, 1)
    outs = pl.pallas_call(body, name="post_fwd", grid=(nsteps,), in_specs=in_specs + r_in, out_specs=[tok] + r_out,
                          out_shape=[jax.ShapeDtypeStruct((s, d), F32)] + r_shapes, scratch_shapes=r_sems,
                          input_output_aliases=r_alias,
                          compiler_params=_params("arbitrary"))(ya, o, proj, proj, proj, x, p, *weights,
                                                                *(rider.operands if rider else []))
    return outs[0], outs[1:]


def _post_bwd(dxo, ya, o, proj, x, p, w_pa, w_pb, w_out, w_pe, w_pg, ln_g, ln_b, alpha):
    s, d = x.shape
    ple = p.shape[1]
    tm = min(s, 256)
    nsteps = s // tm
    weights = (w_pa, w_pb, w_out, w_pe, w_pg, ln_g, ln_b)

    def body(dxo_ref, ya_ref, o_ref, gb_ref, ma_ref, mb_ref, x_ref, p_ref, wpa_ref, wpb_ref, wout_ref, wpe_ref,
             wpg_ref, g_ref, b_ref,
             dproj_ref, dxr_ref, dya_ref, do_ref, de_ref, h1_ref, dzg_ref, mrg_ref, dh1_ref, dpa_ref, yb_ref, dpb_ref,
             dg_ref, db_ref, dg_acc, db_acc):
        i = pl.program_id(0)

        @pl.when(i == 0)
        def _():
            dg_acc[...] = jnp.zeros_like(dg_acc)
            db_acc[...] = jnp.zeros_like(db_acc)

        f = _post_math(ya_ref, o_ref, gb_ref, ma_ref, mb_ref, x_ref, p_ref, wpa_ref, wpb_ref, wout_ref, wpe_ref,
                       wpg_ref, alpha)
        xhat, rstd = _ln_stats(f["h2"])
        dxo = dxo_ref[...]
        dg_acc[...] += _sum_rows8(dxo * xhat)
        db_acc[...] += _sum_rows8(dxo)
        dh2 = _ln_bwd(dxo * g_ref[...], xhat, rstd)
        sg, e = f["sg"], f["e"]
        de_ref[...] = (dh2 * sg).astype(BF16)
        dzg = (dh2 * e * sg * (1.0 - sg)).astype(BF16)
        dzg_ref[...] = dzg
        dh1 = dh2 + _dot_nt(dzg, wpg_ref[...])
        dh1b = dh1.astype(BF16)
        dxr_ref[...] = alpha * dh1
        dh1_ref[...] = dh1b
        h1_ref[...] = f["h1b"]
        mrg_ref[...] = f["merged"]
        yb_ref[...] = f["yb"]
        dmerged = _dot_nt(dh1b, wout_ref[...])
        sa, sb = f["sa"], f["sb"]
        dpa = (dmerged * sa).astype(BF16)
        dpb = (dmerged * sb).astype(BF16)
        dpa_ref[...] = dpa
        dpb_ref[...] = dpb
        dproj_ref[:, d:2 * d] = (dmerged * f["pa"] * sa * (1.0 - sa)).astype(BF16)
        dproj_ref[:, 2 * d:3 * d] = (dmerged * f["pb"] * sb * (1.0 - sb)).astype(BF16)
        dya_ref[...] = _dot_nt(dpa, wpa_ref[...]).astype(BF16)
        dyb = _dot_nt(dpb, wpb_ref[...])
        gb, sgb = f["gb"], f["sgb"]
        do_ref[...] = (dyb * (gb * sgb)).astype(BF16)
        dproj_ref[:, 0:d] = (dyb * f["o"] * (sgb * (1.0 + gb * (1.0 - sgb)))).astype(BF16)

        @pl.when(i == nsteps - 1)
        def _():
            dg_ref[...] = jnp.sum(dg_acc[...], axis=0, keepdims=True)
            db_ref[...] = jnp.sum(db_acc[...], axis=0, keepdims=True)

    tok, in_specs = _post_specs(tm, d, ple, weights)
    vec_spec = pl.BlockSpec((1, d), lambda i: (0, 0))
    vec = jax.ShapeDtypeStruct((1, d), F32)
    act = jax.ShapeDtypeStruct((s, d), BF16)
    return pl.pallas_call(
        body, name="post_bwd", grid=(nsteps,), in_specs=[tok] + in_specs,
        out_specs=[pl.BlockSpec((tm, 3 * d), lambda i: (i, 2)), tok] + [tok] * 10 + [vec_spec, vec_spec],
        out_shape=[jax.ShapeDtypeStruct((s, N_IN * d), BF16), jax.ShapeDtypeStruct((s, d), F32)] + [act] * 10 + [vec, vec],
        scratch_shapes=[pltpu.VMEM((SUBLANES, d), F32), pltpu.VMEM((SUBLANES, d), F32)],
        compiler_params=_params("arbitrary"))(dxo, ya, o, proj, proj, proj, x, p, *weights)


def _loss_head(y, target):
    s, d = y.shape
    tm = min(s, 512)

    def body(y_ref, t_ref, dy_ref, l_ref):
        @pl.when(pl.program_id(0) == 0)
        def _():
            l_ref[...] = jnp.zeros_like(l_ref)

        err = y_ref[...] - t_ref[...]
        dy_ref[...] = err / d
        row = jnp.sum(err * err, axis=1, keepdims=True) / d
        l_ref[...] += 0.5 * jnp.sum(row, axis=0, keepdims=True)

    tok = pl.BlockSpec((tm, d), lambda i: (i, 0))
    return pl.pallas_call(body, name="loss_head", grid=(s // tm,), in_specs=[tok, tok],
                          out_specs=[tok, pl.BlockSpec((SUBLANES, LANES), lambda i: (0, 0))],
                          out_shape=[jax.ShapeDtypeStruct((s, d), F32),
                                     jax.ShapeDtypeStruct((SUBLANES, LANES), F32)],
                          compiler_params=_params("arbitrary"))(y, target)


def _position():
    x, y, c = lax.axis_index("x"), lax.axis_index("y"), lax.axis_index("c")
    chips = [(1 - x, y), (x, 1 - y), (1 - x, 1 - y)]
    return x, y, c, chips


def _shard_of(ref, col_sharded, j, n):
    off = pl.multiple_of(j * n, n)
    return ref.at[:, pl.ds(off, n)] if col_sharded else ref.at[pl.ds(off, n), :]


def _half_of(ref, col_sharded, h, n):
    off = pl.multiple_of(h * n, n)
    return ref.at[pl.ds(off, n), :] if col_sharded else ref.at[:, pl.ds(off, n)]


def _piece_of(ref, col_sharded, chip, n_block, half, n_half):
    block = pl.ds(pl.multiple_of(chip * n_block, n_block), n_block)
    part = pl.ds(pl.multiple_of(half * n_half, n_half), n_half)
    return ref.at[part, block] if col_sharded else ref.at[block, part]


class _Rider(NamedTuple):
    operands: list
    out_shapes: list
    aliases: dict
    n_sems: int
    start: Callable
    finish: Callable


def _rider_call_args(rider, n_in, n_out):
    if rider is None:
        return [], [], [], [], {}
    sems = [pltpu.SemaphoreType.DMA((rider.n_sems,))] * 2
    aliases = {n_in + i: n_out + o for i, o in rider.aliases.items()}
    return [ANY] * len(rider.operands), [ANY] * len(rider.out_shapes), list(rider.out_shapes), sems, aliases


def _ride(rider, refs, n_in, n_out, first, last):
    if rider is None:
        return refs[:n_in], refs[n_in:n_in + n_out], refs[n_in + n_out:], lambda: None, lambda: None
    r_in, r_out = len(rider.operands), len(rider.out_shapes)
    ins, rins = refs[:n_in], refs[n_in:n_in + r_in]
    outs = refs[n_in + r_in:n_in + r_in + n_out]
    routs = refs[n_in + r_in + n_out:n_in + r_in + n_out + r_out]
    scratch = refs[n_in + r_in + n_out + r_out:-2]
    send_sems, recv_sems = refs[-2:]

    def start():
        pl.when(first)(lambda: rider.start(rins, routs, send_sems, recv_sems))

    def finish():
        pl.when(last)(lambda: rider.finish(rins, routs, send_sems, recv_sems))

    return ins, outs, scratch, start, finish


def _gather_copies(outs, ici_sems, d2d_sems, names=BIG, d2d_first=0):
    x, y, c, chips = _position()
    my_chip = 2 * x + y
    sends, arrivals, passes, passed = [], [], [], []
    for a, out in enumerate(outs):
        cs = COL_SHARDED[names[a]]
        rows, cols = out.shape
        n_block = (cols if cs else rows) // N_CHIPS
        n_half = (rows if cs else cols) // 2
        piece = lambda chip, half: _piece_of(out, cs, chip, n_block, half, n_half)
        for j, chip in enumerate(chips):
            k = a * 3 + j
            their = 2 * chip[0] + chip[1]
            if ici_sems is not None:
                send_sems, recv_sems = ici_sems
                sends.append(pltpu.make_async_remote_copy(
                    src_ref=piece(my_chip, c), dst_ref=piece(my_chip, c), send_sem=send_sems.at[k],
                    recv_sem=recv_sems.at[k], device_id=(chip[0], chip[1], c), device_id_type=MESH))
                arrivals.append(pltpu.make_async_remote_copy(
                    src_ref=piece(their, c), dst_ref=piece(their, c), send_sem=send_sems.at[k],
                    recv_sem=recv_sems.at[k], device_id=(chip[0], chip[1], c), device_id_type=MESH))
            if d2d_sems is not None:
                send_sems, recv_sems = d2d_sems
                passes.append(pltpu.make_async_remote_copy(
                    src_ref=piece(their, c), dst_ref=piece(their, c), send_sem=send_sems.at[d2d_first + k],
                    recv_sem=recv_sems.at[d2d_first + k], device_id=(x, y, 1 - c), device_id_type=MESH))
                passed.append(pltpu.make_async_remote_copy(
                    src_ref=piece(their, 1 - c), dst_ref=piece(their, 1 - c), send_sem=send_sems.at[d2d_first + k],
                    recv_sem=recv_sems.at[d2d_first + k], device_id=(x, y, 1 - c), device_id_type=MESH))
    return sends, arrivals, passes, passed


def _whole_gather_rider(bufs, names):
    flat = [bufs[name] for name in names]
    n = len(flat) * 3

    def copies(routs, send_sems, recv_sems):
        sems = (send_sems, recv_sems)
        return _gather_copies(routs, sems, sems, names, d2d_first=n)

    def start(rins, routs, send_sems, recv_sems):
        for cp in copies(routs, send_sems, recv_sems)[0]:
            cp.start()

    def finish(rins, routs, send_sems, recv_sems):
        sends, arrivals, passes, passed = copies(routs, send_sems, recv_sems)
        for arrival, onward in zip(arrivals, passes):
            arrival.wait_recv()
            onward.start()
        for cp in passed:
            cp.wait_recv()
        for cp in sends + passes:
            cp.wait_send()

    return _Rider(flat, [jax.ShapeDtypeStruct(a.shape, a.dtype) for a in flat], {i: i for i in range(len(flat))},
                  2 * n, start, finish)


def _gather_weights(bufs, names):
    rider = _whole_gather_rider(bufs, names)

    def body(*refs):
        n = len(rider.operands)
        args = (refs[:n], refs[n:2 * n], *refs[2 * n:])
        rider.start(*args)
        rider.finish(*args)

    outs = pl.pallas_call(
        body, name="gather_weights", in_specs=[ANY] * len(rider.operands), out_specs=[ANY] * len(rider.out_shapes),
        out_shape=rider.out_shapes, input_output_aliases=rider.aliases,
        scratch_shapes=[pltpu.SemaphoreType.DMA((rider.n_sems,))] * 2,
    )(*rider.operands)
    return dict(zip(names, outs))


def _gather_rider(bufs, over_ici):
    flat = [bufs[name] for name in BIG]

    def copies(routs, send_sems, recv_sems):
        sems = (send_sems, recv_sems)
        sends, arrivals, passes, passed = _gather_copies(routs, sems if over_ici else None, None if over_ici else sems)
        return (sends, arrivals) if over_ici else (passes, passed)

    def start(rins, routs, send_sems, recv_sems):
        for cp in copies(routs, send_sems, recv_sems)[0]:
            cp.start()

    def finish(rins, routs, send_sems, recv_sems):
        out, due = copies(routs, send_sems, recv_sems)
        for cp in due:
            cp.wait_recv()
        for cp in out:
            cp.wait_send()

    return _Rider(flat, [jax.ShapeDtypeStruct(a.shape, a.dtype) for a in flat], {i: i for i in range(len(flat))},
                  len(flat) * 3, start, finish)


def _half_shape(shape, col_sharded):
    r, c = shape
    return (r // 2, c) if col_sharded else (r, c // 2)


def _exchange_rider(grads):
    flat = [grads[name] for name in BIG]

    def copies(rins, routs, send_sems, recv_sems):
        x, y, c, _ = _position()
        out = []
        for a, name in enumerate(BIG):
            cs = COL_SHARDED[name]
            n = routs[a].shape[0] if cs else routs[a].shape[1]
            out.append(pltpu.make_async_remote_copy(
                src_ref=_half_of(rins[a], cs, 1 - c, n), dst_ref=routs[a], send_sem=send_sems.at[a],
                recv_sem=recv_sems.at[a], device_id=(x, y, 1 - c), device_id_type=MESH))
        return out

    def start(rins, routs, send_sems, recv_sems):
        for cp in copies(rins, routs, send_sems, recv_sems):
            cp.start()

    def finish(rins, routs, send_sems, recv_sems):
        cps = copies(rins, routs, send_sems, recv_sems)
        for cp in cps:
            cp.wait_recv()
        for cp in cps:
            cp.wait_send()

    return _Rider(flat, [jax.ShapeDtypeStruct(_half_shape(a.shape, COL_SHARDED[name]), F32)
                         for a, name in zip(flat, BIG)], {}, len(flat), start, finish)


def _scatter_to_owners(halves):
    rider = _scatter_rider(halves)

    def body(*refs):
        n = len(rider.operands)
        args = (refs[:n], refs[n:2 * n], *refs[2 * n:])
        rider.start(*args)
        rider.finish(*args)

    outs = pl.pallas_call(
        body, name="scatter_to_owners", in_specs=[ANY] * len(rider.operands), out_specs=[ANY] * len(rider.out_shapes),
        out_shape=rider.out_shapes, scratch_shapes=[pltpu.SemaphoreType.DMA((rider.n_sems,))] * 2,
    )(*rider.operands)
    return dict(zip(BIG, outs))


def _scatter_rider(halves):
    flat = [halves[name] for name in BIG]

    def slots_shape(a, cs):
        r, c = a.shape
        return (N_CHIPS - 1,) + ((r, c // N_CHIPS) if cs else (r // N_CHIPS, c))

    def copies(rins, routs, send_sems, recv_sems):
        x, y, c, chips = _position()
        out = []
        for a, name in enumerate(BIG):
            cs = COL_SHARDED[name]
            n = routs[a].shape[2] if cs else routs[a].shape[1]
            for j, chip in enumerate(chips):
                k = a * 3 + j
                out.append(pltpu.make_async_remote_copy(
                    src_ref=_shard_of(rins[a], cs, 2 * chip[0] + chip[1], n), dst_ref=routs[a].at[j],
                    send_sem=send_sems.at[k], recv_sem=recv_sems.at[k], device_id=(chip[0], chip[1], c),
                    device_id_type=MESH))
        return out

    def start(rins, routs, send_sems, recv_sems):
        for cp in copies(rins, routs, send_sems, recv_sems):
            cp.start()

    def finish(rins, routs, send_sems, recv_sems):
        cps = copies(rins, routs, send_sems, recv_sems)
        for cp in cps:
            cp.wait_recv()
        for cp in cps:
            cp.wait_send()

    return _Rider(flat, [jax.ShapeDtypeStruct(slots_shape(a, COL_SHARDED[name]), a.dtype) for a, name in zip(flat, BIG)],
                  {}, len(flat) * 3, start, finish)


def _join_halves(halves):
    flat = [halves[name] for name in BIG]
    n_w = len(flat)

    def body(*refs):
        outs = refs[n_w:2 * n_w]
        send_sems, recv_sems = refs[2 * n_w:]
        x, y, c, _ = _position()
        sends, recvs = [], []
        for w, name in enumerate(BIG):
            cs = COL_SHARDED[name]
            n = (outs[w].shape[1] if cs else outs[w].shape[2]) // 2

            def half(h):
                part = pl.ds(pl.multiple_of(h * n, n), n)
                return outs[w].at[:, part, :] if cs else outs[w].at[:, :, part]

            sends.append(pltpu.make_async_remote_copy(
                src_ref=half(c), dst_ref=half(c), send_sem=send_sems.at[w], recv_sem=recv_sems.at[w],
                device_id=(x, y, 1 - c), device_id_type=MESH))
            recvs.append(pltpu.make_async_remote_copy(
                src_ref=half(1 - c), dst_ref=half(1 - c), send_sem=send_sems.at[w], recv_sem=recv_sems.at[w],
                device_id=(x, y, 1 - c), device_id_type=MESH))
        for cp in sends:
            cp.start()
        for cp in recvs:
            cp.wait_recv()
        for cp in sends:
            cp.wait_send()

    outs = pl.pallas_call(
        body, name="join_halves", in_specs=[ANY] * n_w, out_specs=[ANY] * n_w,
        out_shape=[jax.ShapeDtypeStruct(a.shape, F32) for a in flat],
        input_output_aliases={w: w for w in range(n_w)},
        scratch_shapes=[pltpu.SemaphoreType.DMA((n_w,)), pltpu.SemaphoreType.DMA((n_w,))],
    )(*flat)
    return dict(zip(BIG, outs))


def _small_rider(packed):
    r, lanes = packed.shape

    def copies(rins, routs, send_sems, recv_sems):
        x, y, c, _ = _position()
        me = 4 * x + 2 * y + c
        local = pltpu.make_async_copy(rins[0], routs[0].at[me], send_sems.at[N_DEV - 1])
        sends, recvs = [], []
        for k in range(1, N_DEV):
            px, py, pc = x ^ (k >> 2), y ^ ((k >> 1) & 1), c ^ (k & 1)
            sends.append(pltpu.make_async_remote_copy(
                src_ref=rins[0], dst_ref=routs[0].at[me], send_sem=send_sems.at[k - 1], recv_sem=recv_sems.at[k - 1],
                device_id=(px, py, pc), device_id_type=MESH))
            recvs.append(pltpu.make_async_remote_copy(
                src_ref=rins[0], dst_ref=routs[0].at[4 * px + 2 * py + pc], send_sem=send_sems.at[k - 1],
                recv_sem=recv_sems.at[k - 1], device_id=(px, py, pc), device_id_type=MESH))
        return local, sends, recvs

    def start(rins, routs, send_sems, recv_sems):
        local, sends, _ = copies(rins, routs, send_sems, recv_sems)
        local.start()
        for cp in sends:
            cp.start()

    def finish(rins, routs, send_sems, recv_sems):
        local, sends, recvs = copies(rins, routs, send_sems, recv_sems)
        for cp in recvs:
            cp.wait_recv()
        for cp in sends:
            cp.wait_send()
        local.wait()

    return _Rider([packed], [jax.ShapeDtypeStruct((N_DEV, r, lanes), F32)], {}, N_DEV, start, finish)


def _pack_small(t):
    return jnp.concatenate([t[name].reshape(-1, LANES) for name in SMALL], axis=0)


def _unpack_small(packed, like):
    out, row = {}, 0
    for name in SMALL:
        n = like[name].size // LANES
        out[name] = packed[row:row + n].reshape(like[name].shape)
        row += n
    return out


def kernel(x, p, w_in, vn_g, vn_b, w_s, b_s, w_pa, w_pb, w_out, w_pe, w_pg, ln_g, ln_b, loss_target, m_w_in, m_vn_g, m_vn_b, m_w_s, m_b_s, m_w_pa, m_w_pb, m_w_out, m_w_pe, m_w_pg, m_ln_g, m_ln_b, v_w_in, v_vn_g, v_vn_b, v_w_s, v_b_s, v_w_pa, v_w_pb, v_w_out, v_w_pe, v_w_pg, v_ln_g, v_ln_b):
    weights = dict(w_in=w_in, vn_g=vn_g, vn_b=vn_b, w_s=w_s, b_s=b_s, w_pa=w_pa, w_pb=w_pb, w_out=w_out, w_pe=w_pe,
                   w_pg=w_pg, ln_g=ln_g, ln_b=ln_b)
    mom1 = dict(w_in=m_w_in, vn_g=m_vn_g, vn_b=m_vn_b, w_s=m_w_s, b_s=m_b_s, w_pa=m_w_pa, w_pb=m_w_pb, w_out=m_w_out,
                w_pe=m_w_pe, w_pg=m_w_pg, ln_g=m_ln_g, ln_b=m_ln_b)
    mom2 = dict(w_in=v_w_in, vn_g=v_vn_g, vn_b=v_vn_b, w_s=v_w_s, b_s=v_b_s, w_pa=v_w_pa, w_pb=v_w_pb, w_out=v_w_out,
                w_pe=v_w_pe, w_pg=v_w_pg, ln_g=v_ln_g, ln_b=v_ln_b)
    nl, d = vn_g.shape
    chunk = w_s.shape[2]
    assert chunk == LANES and w_s.shape[3] == LANES and d % LANES == 0
    alpha = (2 * nl) ** 0.25
    pos = tuple(lax.axis_index(a).astype(jnp.int32).reshape(1) for a in ("c", "x", "y"))

    placed = [{name: _cast_into_place(weights[name], l, pos, COL_SHARDED[name]) for name in BIG} for l in range(nl)]
    later = tuple(name for name in BIG if name != "w_in")
    full = [_gather_weights(placed[0], ("w_in",))] + [None] * (nl - 1)
    causal = jnp.tril(jnp.ones((chunk, chunk), dtype=bool))
    ws_m = jnp.where(causal, w_s, 0.0).astype(BF16)
    ws_mt = jnp.swapaxes(ws_m, 2, 3)
    bs_t = jnp.swapaxes(b_s, 1, 2)

    xs, projs, yas, os_ = [x[0]], [], [], []
    for l in range(nl):
        proj, rest = _proj_fwd(xs[l], full[l]["w_in"], _whole_gather_rider(placed[0], later) if l == 0 else None)
        if l == 0:
            full[0].update(zip(later, rest))
        ya = _gmlp_fwd(proj, vn_g[l:l + 1], vn_b[l:l + 1], ws_m[l], bs_t[l])
        more = l + 1 < nl
        o, arrived = _attn_fwd(proj, _gather_rider(placed[l + 1], over_ici=True) if more else None)
        x_next, handed = _post_fwd(ya, o, proj, xs[l], p[l, 0], full[l]["w_pa"], full[l]["w_pb"], full[l]["w_out"],
                                   full[l]["w_pe"], full[l]["w_pg"], ln_g[l:l + 1], ln_b[l:l + 1], alpha,
                                   _gather_rider(dict(zip(BIG, arrived)), over_ici=False) if more else None)
        if more:
            full[l + 1] = dict(zip(BIG, handed))
        xs.append(x_next)
        projs.append(proj)
        yas.append(ya)
        os_.append(o)

    dx, loss_tile = _loss_head(xs[nl], loss_target[0])
    loss = lax.psum(loss_tile[0, 0], ("x", "y", "c"))

    big_grads, received, slots = [None] * nl, [None] * nl, [None] * nl
    chip_sums = None
    small_grads = [None] * nl
    for l in reversed(range(nl)):
        w = full[l]
        (dproj, dxr, dya, do, de, h1b, dzg, merged, dh1, dpa, yb, dpb, dln_g, dln_b) = _post_bwd(
            dx, yas[l], os_[l], projs[l], xs[l], p[l, 0], w["w_pa"], w["w_pb"], w["w_out"], w["w_pe"], w["w_pg"],
            ln_g[l:l + 1], ln_b[l:l + 1], alpha)
        (dq, dk, dv), scattered = _attn_bwd(projs[l], os_[l], do,
                                            _scatter_rider(chip_sums) if chip_sums is not None else None)
        if chip_sums is not None:
            slots[l + 1] = dict(zip(BIG, scattered))
        dproj, dvn_g, dvn_b, dw_s, dbs_cols = _gmlp_bwd(dproj, projs[l], dya, dq, dk, dv, vn_g[l:l + 1],
                                                         vn_b[l:l + 1], ws_m[l], ws_mt[l], bs_t[l])
        partial = _pack_small(dict(vn_g=dvn_g[0], vn_b=dvn_b[0], w_s=dw_s, b_s=dbs_cols[:, :b_s.shape[1]].T,
                                   ln_g=dln_g[0], ln_b=dln_b[0]))
        dw_in, (partials,) = _matmul_tn(xs[l], dproj, _small_rider(partial))
        small_grads[l] = _unpack_small(_sum_slots(partials), {name: weights[name][0] for name in SMALL})
        big_grads[l] = dict(w_in=dw_in, w_pa=_matmul_tn(yas[l], dpa), w_pb=_matmul_tn(yb, dpb),
                            w_out=_matmul_tn(merged, dh1), w_pe=_matmul_tn(p[l, 0], de), w_pg=_matmul_tn(h1b, dzg))
        dx, from_sibling = _dx_matmul(dxr, dproj, w["w_in"], _exchange_rider(big_grads[l]))
        received[l] = dict(zip(BIG, from_sibling))
        chip_sums = {name: _add_own_half(big_grads[l][name], received[l][name], pos, COL_SHARDED[name])
                     for name in BIG}
    slots[0] = _scatter_to_owners(chip_sums)
    reduced = {}
    for name in BIG:
        buf = None
        for l in range(nl):
            buf = _reduce_block(buf, big_grads[l][name], received[l][name], slots[l][name], l, nl, pos,
                                COL_SHARDED[name])
        reduced[name] = buf
    grads = _join_halves(reduced)

    small_like = {name: weights[name] for name in SMALL}
    grads.update({name: jnp.stack([small_grads[l][name] for l in range(nl)]) for name in SMALL})

    delta, new_m, new_v = {}, {}, {}
    for name in BIG:
        sh = weights[name].shape
        flat = lambda a: a.reshape(sh[0] * sh[1], sh[2])
        dl, nm, nv, g_out = _adamw(flat(weights[name]), flat(grads[name]), flat(mom1[name]), flat(mom2[name]),
                                   emit_grad=True)
        delta[name], new_m[name], new_v[name] = dl.reshape(sh), nm.reshape(sh), nv.reshape(sh)
        grads[name] = g_out.reshape(sh)
    dl, nm, nv = _adamw(_pack_small(small_like), _pack_small({n: grads[n] for n in SMALL}),
                        _pack_small({n: mom1[n] for n in SMALL}), _pack_small({n: mom2[n] for n in SMALL}))
    delta.update(_unpack_small(dl, small_like))
    new_m.update(_unpack_small(nm, small_like))
    new_v.update(_unpack_small(nv, small_like))

    return (loss, dx[None], *[grads[n] for n in WEIGHTS], *[delta[n] for n in WEIGHTS],
            *[new_m[n] for n in WEIGHTS], *[new_v[n] for n in WEIGHTS])
```

```python
import math
from typing import Callable, NamedTuple

import jax
import jax.numpy as jnp
from jax import lax
from jax.experimental import pallas as pl
from jax.experimental.pallas import tpu as pltpu

F32 = jnp.float32
BF16 = jnp.bfloat16
LANES = 128
SUBLANES = 8
HEAD_DIM = 64
HEADS_PER_BLOCK = LANES // HEAD_DIM
LN_EPS = 1e-5
N_IN = 9
N_CHIPS = 4
N_DEV = 8
ADAM_LR = 0.001
ADAM_B1 = 0.9
ADAM_B2 = 0.999
ADAM_EPS = 1e-08
ADAM_WD = 0.01
ADAM_STEP = 10
MESH = pl.DeviceIdType.MESH
ANY = pl.BlockSpec(memory_space=pl.ANY)
BIG = ("w_in", "w_pa", "w_pb", "w_out", "w_pe", "w_pg")
COL_SHARDED = {"w_in": True, "w_pa": False, "w_pb": False, "w_out": False, "w_pe": True, "w_pg": False}
SMALL = ("vn_g", "vn_b", "w_s", "b_s", "ln_g", "ln_b")
WEIGHTS = ("w_in", "vn_g", "vn_b", "w_s", "b_s", "w_pa", "w_pb", "w_out", "w_pe", "w_pg", "ln_g", "ln_b")


def _params(*sem):
    return pltpu.CompilerParams(dimension_semantics=sem)


def _dot(a, b):
    return jnp.dot(a, b, preferred_element_type=F32)


def _dot_nt(a, b):
    return lax.dot_general(a, b, (((1,), (1,)), ((), ())), preferred_element_type=F32)


def _dot_tn(a, b):
    return lax.dot_general(a, b, (((0,), (0,)), ((), ())), preferred_element_type=F32)


def _sigmoid(a):
    return 1.0 / (1.0 + jnp.exp(-a))


def _row_tile(rows, cols, cap_bytes):
    best = None
    for t in range(16, rows + 1, 16):
        if rows % t == 0 and t * cols * 4 <= cap_bytes:
            best = t
    return best or rows


def _col_tile(cols, cap):
    best = LANES
    for t in range(LANES, min(cols, cap) + 1, LANES):
        if cols % t == 0:
            best = t
    return best


def _ln_stats(h):
    mu = jnp.mean(h, axis=-1, keepdims=True)
    hc = h - mu
    var = jnp.mean(hc * hc, axis=-1, keepdims=True)
    rstd = lax.rsqrt(var + LN_EPS)
    return hc * rstd, rstd


def _ln_bwd(dxhat, xhat, rstd):
    m1 = jnp.mean(dxhat, axis=-1, keepdims=True)
    m2 = jnp.mean(dxhat * xhat, axis=-1, keepdims=True)
    return rstd * (dxhat - m1 - xhat * m2)


def _sum_rows8(a):
    t, d = a.shape
    return jnp.sum(a.reshape(t // SUBLANES, SUBLANES, d), axis=0)


def _chip(x_ref, y_ref):
    return 2 * x_ref[0] + y_ref[0]


def _cast_into_place(shards, l, pos, col_sharded):
    _, r, c = shards.shape
    tr = _row_tile(r, c, 2 << 20)
    nb = r // tr

    def body(c_ref, x_ref, y_ref, a_ref, o_ref):
        o_ref[...] = a_ref[...].astype(BF16)

    if col_sharded:
        out_spec = pl.BlockSpec((tr, c), lambda i, c_ref, x_ref, y_ref: (i, _chip(x_ref, y_ref)))
        full = (r, c * N_CHIPS)
    else:
        out_spec = pl.BlockSpec((tr, c), lambda i, c_ref, x_ref, y_ref: (_chip(x_ref, y_ref) * nb + i, 0))
        full = (r * N_CHIPS, c)
    grid_spec = pltpu.PrefetchScalarGridSpec(
        num_scalar_prefetch=3, grid=(nb,),
        in_specs=[pl.BlockSpec((None, tr, c), lambda i, c_ref, x_ref, y_ref: (l, i, 0))], out_specs=out_spec)
    return pl.pallas_call(body, name="cast_into_place", grid_spec=grid_spec,
                          out_shape=jax.ShapeDtypeStruct(full, BF16),
                          compiler_params=_params("parallel"))(*pos, shards)


def _add_own_half(own, recv, pos, col_sharded):
    r, c = recv.shape
    tr = _row_tile(r, c, 4 << 20)
    nb = r // tr

    def body(c_ref, x_ref, y_ref, own_ref, recv_ref, o_ref):
        o_ref[...] = (own_ref[...] + recv_ref[...]).astype(BF16)

    if col_sharded:
        own_spec = pl.BlockSpec((tr, c), lambda i, c_ref, x_ref, y_ref: (c_ref[0] * nb + i, 0))
    else:
        own_spec = pl.BlockSpec((tr, c), lambda i, c_ref, x_ref, y_ref: (i, c_ref[0]))
    spec = pl.BlockSpec((tr, c), lambda i, c_ref, x_ref, y_ref: (i, 0))
    grid_spec = pltpu.PrefetchScalarGridSpec(num_scalar_prefetch=3, grid=(nb,), in_specs=[own_spec, spec],
                                             out_specs=spec)
    return pl.pallas_call(body, name="add_own_half", grid_spec=grid_spec,
                          out_shape=jax.ShapeDtypeStruct(recv.shape, BF16),
                          compiler_params=_params("parallel"))(*pos, own, recv)


def _reduce_block(buf, own, recv, slots, l, nl, pos, col_sharded):
    _, r, c = slots.shape
    tr = _row_tile(r, c, 2 << 20)
    nb = r // tr

    def body(c_ref, x_ref, y_ref, own_ref, recv_ref, slots_ref, *rest):
        acc = own_ref[...] + recv_ref[...]
        for j in range(N_CHIPS - 1):
            acc = acc + slots_ref[j].astype(F32)
        rest[-1][...] = acc

    if col_sharded:
        own_spec = pl.BlockSpec((tr, c), lambda i, c_ref, x_ref, y_ref: (c_ref[0] * nb + i, _chip(x_ref, y_ref)))
        recv_spec = pl.BlockSpec((tr, c), lambda i, c_ref, x_ref, y_ref: (i, _chip(x_ref, y_ref)))
        out_spec = pl.BlockSpec((None, tr, c), lambda i, c_ref, x_ref, y_ref: (l, c_ref[0] * nb + i, 0))
        out_shape = (nl, 2 * r, c)
    else:
        own_spec = pl.BlockSpec((tr, c), lambda i, c_ref, x_ref, y_ref: (_chip(x_ref, y_ref) * nb + i, c_ref[0]))
        recv_spec = pl.BlockSpec((tr, c), lambda i, c_ref, x_ref, y_ref: (_chip(x_ref, y_ref) * nb + i, 0))
        out_spec = pl.BlockSpec((None, tr, c), lambda i, c_ref, x_ref, y_ref: (l, i, c_ref[0]))
        out_shape = (nl, r, 2 * c)
    in_specs = [own_spec, recv_spec,
                pl.BlockSpec((N_CHIPS - 1, tr, c), lambda i, c_ref, x_ref, y_ref: (0, i, 0))]
    args = [*pos, own, recv, slots]
    aliases = {}
    if buf is not None:
        in_specs.append(ANY)
        args.append(buf)
        aliases = {len(args) - 1: 0}
    grid_spec = pltpu.PrefetchScalarGridSpec(num_scalar_prefetch=3, grid=(nb,), in_specs=in_specs,
                                             out_specs=out_spec)
    return pl.pallas_call(body, name="reduce_block", grid_spec=grid_spec,
                          out_shape=jax.ShapeDtypeStruct(out_shape, F32), input_output_aliases=aliases,
                          compiler_params=_params("parallel"))(*args)


def _sum_slots(a):
    n, r, c = a.shape
    tr = _row_tile(r, c * n, 4 << 20)

    def body(a_ref, o_ref):
        acc = a_ref[0]
        for s in range(1, n):
            acc = acc + a_ref[s]
        o_ref[...] = acc

    return pl.pallas_call(body, name="sum_slots", grid=(r // tr,),
                          in_specs=[pl.BlockSpec((n, tr, c), lambda i: (0, i, 0))],
                          out_specs=pl.BlockSpec((tr, c), lambda i: (i, 0)),
                          out_shape=jax.ShapeDtypeStruct((r, c), F32),
                          compiler_params=_params("parallel"))(a)


def _adamw(w, g, m, v, emit_grad=False):
    r, c = w.shape
    tr = _row_tile(r, c, 2 << 20)
    n_out = 4 if emit_grad else 3

    def body(w_ref, g_ref, m_ref, v_ref, d_ref, nm_ref, nv_ref, *g_out):
        gg = g_ref[...]
        nm = ADAM_B1 * m_ref[...] + (1.0 - ADAM_B1) * gg
        nv = ADAM_B2 * v_ref[...] + (1.0 - ADAM_B2) * (gg * gg)
        m_hat = nm / (1.0 - ADAM_B1 ** ADAM_STEP)
        v_hat = nv / (1.0 - ADAM_B2 ** ADAM_STEP)
        d_ref[...] = -ADAM_LR * (m_hat / (jnp.sqrt(v_hat) + ADAM_EPS) + ADAM_WD * w_ref[...])
        nm_ref[...] = nm
        nv_ref[...] = nv
        for ref in g_out:
            ref[...] = gg

    spec = pl.BlockSpec((tr, c), lambda i: (i, 0))
    sh = jax.ShapeDtypeStruct((r, c), F32)
    return pl.pallas_call(body, name="adamw", grid=(r // tr,), in_specs=[spec] * 4, out_specs=[spec] * n_out,
                          out_shape=[sh] * n_out, compiler_params=_params("parallel"))(w, g, m, v)


def _proj_fwd(x, w, rider=None):
    s, d = x.shape
    n = w.shape[1]
    tm, tn = min(s, 1024), _col_tile(n, 1024)
    nm, nn = s // tm, n // tn

    def body(*refs):
        i, j = pl.program_id(0), pl.program_id(1)
        first = jnp.logical_and(i == 0, j == 0)
        last = jnp.logical_and(i == nm - 1, j == nn - 1)
        (x_ref, w_ref), (o_ref,), _, start, finish = _ride(rider, refs, 2, 1, first, last)
        start()
        o_ref[...] = _dot(x_ref[...].astype(BF16), w_ref[...]).astype(BF16)
        finish()

    r_in, r_out, r_shapes, r_sems, r_alias = _rider_call_args(rider, 2, 1)
    outs = pl.pallas_call(body, name="proj_fwd", grid=(nm, nn),
                          in_specs=[pl.BlockSpec((tm, d), lambda i, j: (i, 0)),
                                    pl.BlockSpec((d, tn), lambda i, j: (0, j))] + r_in,
                          out_specs=[pl.BlockSpec((tm, tn), lambda i, j: (i, j))] + r_out,
                          out_shape=[jax.ShapeDtypeStruct((s, n), BF16)] + r_shapes, scratch_shapes=r_sems,
                          input_output_aliases=r_alias,
                          compiler_params=_params("arbitrary", "arbitrary"))(x, w, *(rider.operands if rider else []))
    return outs[0], outs[1:]


def _matmul_tn(a, b, rider=None):
    s, m = a.shape
    n = b.shape[1]
    tk, tn = min(s, 1024), _col_tile(n, 1024)
    nn, nk = n // tn, s // tk

    def body(*refs):
        j, k = pl.program_id(0), pl.program_id(1)
        first = jnp.logical_and(j == 0, k == 0)
        last = jnp.logical_and(j == nn - 1, k == nk - 1)
        (a_ref, b_ref), (o_ref,), _, start, finish = _ride(rider, refs, 2, 1, first, last)
        start()

        @pl.when(k == 0)
        def _():
            o_ref[...] = jnp.zeros_like(o_ref)

        o_ref[...] += _dot_tn(a_ref[...].astype(BF16), b_ref[...].astype(BF16))
        finish()

    r_in, r_out, r_shapes, r_sems, r_alias = _rider_call_args(rider, 2, 1)
    outs = pl.pallas_call(body, name="matmul_tn", grid=(nn, nk),
                          in_specs=[pl.BlockSpec((tk, m), lambda j, k: (k, 0)),
                                    pl.BlockSpec((tk, tn), lambda j, k: (k, j))] + r_in,
                          out_specs=[pl.BlockSpec((m, tn), lambda j, k: (0, j))] + r_out,
                          out_shape=[jax.ShapeDtypeStruct((m, n), F32)] + r_shapes, scratch_shapes=r_sems,
                          input_output_aliases=r_alias,
                          compiler_params=_params("arbitrary", "arbitrary"))(a, b, *(rider.operands if rider else []))
    return outs[0] if rider is None else (outs[0], outs[1:])


def _dx_matmul(dxr, dproj, w, rider=None):
    s, d = dxr.shape
    n = w.shape[1]
    tm, tk = min(s, 1024), _col_tile(n, 1024)
    nm, nk = s // tm, n // tk

    def body(*refs):
        i, k = pl.program_id(0), pl.program_id(1)
        first = jnp.logical_and(i == 0, k == 0)
        last = jnp.logical_and(i == nm - 1, k == nk - 1)
        (r_ref, g_ref, w_ref), (o_ref,), _, start, finish = _ride(rider, refs, 3, 1, first, last)
        start()

        @pl.when(k == 0)
        def _():
            o_ref[...] = r_ref[...]

        o_ref[...] += _dot_nt(g_ref[...], w_ref[...])
        finish()

    r_in, r_out, r_shapes, r_sems, r_alias = _rider_call_args(rider, 3, 1)
    outs = pl.pallas_call(body, name="dx_matmul", grid=(nm, nk),
                          in_specs=[pl.BlockSpec((tm, d), lambda i, k: (i, 0)),
                                    pl.BlockSpec((tm, tk), lambda i, k: (i, k)),
                                    pl.BlockSpec((d, tk), lambda i, k: (0, k))] + r_in,
                          out_specs=[pl.BlockSpec((tm, d), lambda i, k: (i, 0))] + r_out,
                          out_shape=[jax.ShapeDtypeStruct((s, d), F32)] + r_shapes, scratch_shapes=r_sems,
                          input_output_aliases=r_alias,
                          compiler_params=_params("arbitrary", "arbitrary"))(dxr, dproj, w,
                                                                             *(rider.operands if rider else []))
    return outs[0], outs[1:]


def _mix_chunks(ws_ref, src_ref, dst_ref, bias_ref, t, groups, chunk):
    for c in range(t // chunk):
        rows = slice(c * chunk, (c + 1) * chunk)
        for g in range(groups):
            cols = slice(g * LANES, (g + 1) * LANES)
            val = _dot(ws_ref[g], src_ref[rows, cols])
            if bias_ref is not None:
                val = val + bias_ref[:, g:g + 1]
            dst_ref[rows, cols] = val


def _gmlp_fwd(proj, vn_g, vn_b, ws_m, bs_t):
    s = proj.shape[0]
    d = proj.shape[1] // N_IN
    groups, chunk = ws_m.shape[0], ws_m.shape[1]
    t = min(s, 512)

    def body(u_ref, v_ref, ga_ref, g_ref, b_ref, ws_ref, bs_ref, o_ref, vn_ref, mix_ref):
        xhat, _ = _ln_stats(v_ref[...].astype(F32))
        vn_ref[...] = (xhat * g_ref[...] + b_ref[...]).astype(BF16)
        _mix_chunks(ws_ref, vn_ref, mix_ref, bs_ref, t, groups, chunk)
        ga = ga_ref[...].astype(F32)
        o_ref[...] = (u_ref[...].astype(F32) * mix_ref[...] * (ga * _sigmoid(ga))).astype(BF16)

    col = lambda j: pl.BlockSpec((t, d), lambda i: (i, j))
    full = lambda a: pl.BlockSpec(a.shape, lambda i: (0,) * a.ndim)
    return pl.pallas_call(body, name="gmlp_fwd", grid=(s // t,),
                          in_specs=[col(0), col(1), col(2), full(vn_g), full(vn_b), full(ws_m), full(bs_t)],
                          out_specs=pl.BlockSpec((t, d), lambda i: (i, 0)),
                          out_shape=jax.ShapeDtypeStruct((s, d), BF16),
                          scratch_shapes=[pltpu.VMEM((t, d), BF16), pltpu.VMEM((t, d), F32)],
                          compiler_params=_params("parallel"))(proj, proj, proj, vn_g, vn_b, ws_m, bs_t)


def _gmlp_bwd(dproj, proj, dya, dq, dk, dv, vn_g, vn_b, ws_m, ws_mt, bs_t):
    s = proj.shape[0]
    d = proj.shape[1] // N_IN
    groups, chunk = ws_m.shape[0], ws_m.shape[1]
    t = min(s, 256)
    nsteps = s // t

    def body(dproj_hbm, u_ref, v_ref, ga_ref, dya_ref, dq_ref, dk_ref, dv_ref, g_ref, b_ref, ws_ref, wst_ref, bs_ref,
             o_ref, dg_ref, db_ref, dws_ref, dbs_ref,
             vn_ref, mix_ref, dm_ref, dvn_ref, dbs_acc, dg_acc, db_acc):
        del dproj_hbm
        i = pl.program_id(0)

        @pl.when(i == 0)
        def _():
            dws_ref[...] = jnp.zeros_like(dws_ref)
            dbs_acc[...] = jnp.zeros_like(dbs_acc)
            dg_acc[...] = jnp.zeros_like(dg_acc)
            db_acc[...] = jnp.zeros_like(db_acc)

        xhat, rstd = _ln_stats(v_ref[...].astype(F32))
        vn_ref[...] = (xhat * g_ref[...] + b_ref[...]).astype(BF16)
        _mix_chunks(ws_ref, vn_ref, mix_ref, bs_ref, t, groups, chunk)
        ga = ga_ref[...].astype(F32)
        sg = _sigmoid(ga)
        silu = ga * sg
        dsilu = sg * (1.0 + ga * (1.0 - sg))
        u = u_ref[...].astype(F32)
        dya_f = dya_ref[...].astype(F32)
        mix = mix_ref[...]
        o_ref[:, 0:d] = (dya_f * mix * silu).astype(BF16)
        o_ref[:, 2 * d:3 * d] = (dya_f * u * mix * dsilu).astype(BF16)
        dmix = dya_f * u * silu
        dm_ref[...] = dmix.astype(BF16)
        for c in range(t // chunk):
            dbs_acc[...] += dmix[c * chunk:(c + 1) * chunk, :]
        _mix_chunks(wst_ref, dm_ref, dvn_ref, None, t, groups, chunk)
        for c in range(t // chunk):
            rows = slice(c * chunk, (c + 1) * chunk)
            for g in range(groups):
                cols = slice(g * LANES, (g + 1) * LANES)
                dws_ref[g] += _dot_nt(dm_ref[rows, cols], vn_ref[rows, cols])
        dvn = dvn_ref[...]
        dg_acc[...] += _sum_rows8(dvn * xhat)
        db_acc[...] += _sum_rows8(dvn)
        o_ref[:, d:2 * d] = _ln_bwd(dvn * g_ref[...], xhat, rstd).astype(BF16)
        o_ref[:, 3 * d:4 * d] = dq_ref[...]
        o_ref[:, 4 * d:5 * d] = dk_ref[...]
        o_ref[:, 5 * d:6 * d] = dv_ref[...]

        @pl.when(i == nsteps - 1)
        def _():
            row = lax.broadcasted_iota(jnp.int32, (chunk, chunk), 0)
            col = lax.broadcasted_iota(jnp.int32, (chunk, chunk), 1)
            for g in range(groups):
                dws_ref[g] = jnp.where(col <= row, dws_ref[g], 0.0)
            lane = lax.broadcasted_iota(jnp.int32, (chunk, LANES), 1)
            res = jnp.zeros((chunk, LANES), F32)
            for g in range(groups):
                tot = jnp.sum(dbs_acc[:, g * LANES:(g + 1) * LANES], axis=1, keepdims=True)
                res = jnp.where(lane == g, tot, res)
            dbs_ref[...] = res
            dg_ref[...] = jnp.sum(dg_acc[...], axis=0, keepdims=True)
            db_ref[...] = jnp.sum(db_acc[...], axis=0, keepdims=True)

    col = lambda j: pl.BlockSpec((t, d), lambda i: (i, j))
    tok = pl.BlockSpec((t, d), lambda i: (i, 0))
    full = lambda a: pl.BlockSpec(a.shape, lambda i: (0,) * a.ndim)
    vec = jax.ShapeDtypeStruct((1, d), F32)
    outs = pl.pallas_call(
        body, name="gmlp_bwd", grid=(nsteps,),
        in_specs=[ANY, col(0), col(1), col(2), tok, tok, tok, tok,
                  full(vn_g), full(vn_b), full(ws_m), full(ws_mt), full(bs_t)],
        out_specs=[pl.BlockSpec((t, 6 * d), lambda i: (i, 0)),
                   pl.BlockSpec((1, d), lambda i: (0, 0)), pl.BlockSpec((1, d), lambda i: (0, 0)),
                   pl.BlockSpec((groups, chunk, chunk), lambda i: (0, 0, 0)),
                   pl.BlockSpec((chunk, LANES), lambda i: (0, 0))],
        out_shape=[jax.ShapeDtypeStruct(dproj.shape, BF16), vec, vec,
                   jax.ShapeDtypeStruct((groups, chunk, chunk), F32),
                   jax.ShapeDtypeStruct((chunk, LANES), F32)],
        scratch_shapes=[pltpu.VMEM((t, d), BF16), pltpu.VMEM((t, d), F32), pltpu.VMEM((t, d), BF16),
                        pltpu.VMEM((t, d), F32), pltpu.VMEM((chunk, d), F32),
                        pltpu.VMEM((SUBLANES, d), F32), pltpu.VMEM((SUBLANES, d), F32)],
        input_output_aliases={0: 0},
        compiler_params=_params("arbitrary"))(dproj, proj, proj, proj, dya, dq, dk, dv,
                                              vn_g, vn_b, ws_m, ws_mt, bs_t)
    return outs


ATTN_TILE = 256
ATTN_BLOCKS = 2
EXP_UNDERFLOW = -104.0


def _block(b):
    return slice(b * LANES, (b + 1) * LANES)


def _log_sigmoid(z):
    return jnp.minimum(z, 0.0) - jnp.log(1.0 + jnp.exp(-jnp.abs(z)))


def _suffix_rhs():
    row = lax.broadcasted_iota(jnp.int32, (LANES, LANES), 0)
    col = lax.broadcasted_iota(jnp.int32, (LANES, LANES), 1)
    rhs = jnp.concatenate([(row > col).astype(BF16), jnp.ones((LANES, LANES), BF16)], axis=1)
    return jnp.concatenate([rhs, rhs], axis=0)


def _suffix_sums(a, rhs_ref, t):
    hi = a.astype(BF16)
    lo = (a - hi.astype(F32)).astype(BF16)
    n = t // LANES
    inside, totals = [], []
    for c in range(n):
        cols = slice(c * LANES, (c + 1) * LANES)
        res = _dot(jnp.concatenate([hi[:, cols], lo[:, cols]], axis=1), rhs_ref[...])
        inside.append(res[:, :LANES])
        totals.append(res[:, LANES:])
    later = totals[n - 1]
    for c in reversed(range(n - 1)):
        inside[c] = inside[c] + later
        later = later + totals[c]
    return jnp.concatenate(inside, axis=1), later


def _lanes_to_tile(a, t):
    return jnp.concatenate([a] * (t // LANES), axis=1)


def _sweep_tiles(tiles, i, tq, keep_sum_ref):
    def live():
        most = keep_sum_ref[0]
        for h in range(1, keep_sum_ref.shape[0]):
            most = jnp.maximum(most, keep_sum_ref[h])
        return jnp.max(most) >= EXP_UNDERFLOW

    @pl.when(i == 0)
    def _():
        tiles([(i, True, 0, tq)])

    @pl.when(i > 0)
    def _():
        tiles([(i, True, 0, tq), (i - 1, False, 0, tq)])

        def cond(carry):
            n, alive = carry
            return jnp.logical_and(n < i - 1, alive)

        def step(carry):
            n, _ = carry
            tiles([(i - 2 - n, False, 0, tq)])
            return n + 1, live()

        lax.while_loop(cond, step, (jnp.int32(0), live()))


def _attn_masks(t):
    lane = lax.broadcasted_iota(jnp.int32, (t, LANES), 1)
    row = lax.broadcasted_iota(jnp.int32, (t, t), 0)
    col = lax.broadcasted_iota(jnp.int32, (t, t), 1)
    return lane < HEAD_DIM, col < row


def _split_heads(a, head0):
    zero = jnp.zeros_like(a)
    return [jnp.where(head0, a, zero), jnp.where(head0, zero, a)]


def _attn_fwd(proj, rider=None):
    s = proj.shape[0]
    d = proj.shape[1] // N_IN
    hp = d // LANES
    tq = min(s, ATTN_TILE)
    nq = s // tq
    scale = HEAD_DIM ** -0.5
    assert math.log2(scale).is_integer()

    nb = ATTN_BLOCKS
    n_heads = nb * HEADS_PER_BLOCK
    assert hp % nb == 0
    wide = nb * LANES

    def body(*refs):
        h_id, i = pl.program_id(0), pl.program_id(1)
        first = jnp.logical_and(h_id == 0, i == 0)
        last = jnp.logical_and(h_id == hp // nb - 1, i == nq - 1)
        (q_ref, k_ref, v_ref, rhs_ref), (o_ref,), (acc_ref, r_ref), start, finish = _ride(rider, refs, 4, 1, first, last)
        start()
        head0, strict = _attn_masks(tq)
        qm = []
        for b in range(nb):
            qm += _split_heads((q_ref[:, _block(b)].astype(F32) * scale).astype(BF16), head0)
        acc_ref[...] = jnp.zeros_like(acc_ref)
        r_ref[...] = jnp.zeros_like(r_ref)

        def tiles(specs):
            offs = [pl.multiple_of(spec[0] * tq, tq) for spec in specs]
            k2 = {(t, b): k_ref[pl.ds(off, tq), _block(b)] for t, off in enumerate(offs) for b in range(nb)}
            v2 = {(t, b): v_ref[pl.ds(off, tq), _block(b)] for t, off in enumerate(offs) for b in range(nb)}
            rows = [slice(r0, r0 + nr) for _, _, r0, nr in specs]
            chains = [(t, h) for t in range(len(specs)) for h in range(n_heads)]
            masked = lambda c: specs[c[0]][1]
            z = {c: _dot_nt(qm[c[1]][rows[c[0]]], k2[c[0], c[1] // HEADS_PER_BLOCK]) for c in chains}
            lsz = {c: _log_sigmoid(z[c]) for c in chains}
            keep = {c: lsz[c] - z[c] for c in chains}
            keep = {c: jnp.where(strict[rows[c[0]]], keep[c], 0.0) if masked(c) else keep[c] for c in chains}
            sums = {c: _suffix_sums(keep[c], rhs_ref, tq) for c in chains}
            for h in range(n_heads):
                for t in range(len(specs)):
                    w = jnp.exp(lsz[t, h] + sums[t, h][0] + _lanes_to_tile(r_ref[h, rows[t]], tq))
                    if masked((t, h)):
                        w = jnp.where(strict[rows[t]], w, 0.0)
                    acc_ref[h, rows[t]] += _dot(w.astype(BF16), v2[t, h // HEADS_PER_BLOCK])
                    r_ref[h, rows[t]] += sums[t, h][1]

        _sweep_tiles(tiles, i, tq, r_ref)

        for b in range(nb):
            o_ref[:, _block(b)] = jnp.where(head0, acc_ref[2 * b], acc_ref[2 * b + 1])
        finish()

    rhs = _suffix_rhs()
    r_in, r_out, r_shapes, r_sems, r_alias = _rider_call_args(rider, 4, 1)
    outs = pl.pallas_call(
        body, name="attn_fwd", grid=(hp // nb, nq),
        in_specs=[pl.BlockSpec((tq, wide), lambda h, i: (i, 3 * hp // nb + h)),
                  pl.BlockSpec((s, wide), lambda h, i: (0, 4 * hp // nb + h)),
                  pl.BlockSpec((s, wide), lambda h, i: (0, 5 * hp // nb + h)),
                  pl.BlockSpec(rhs.shape, lambda h, i: (0, 0))] + r_in,
        out_specs=[pl.BlockSpec((tq, wide), lambda h, i: (i, h))] + r_out,
        out_shape=[jax.ShapeDtypeStruct((s, d), F32)] + r_shapes,
        scratch_shapes=[pltpu.VMEM((n_heads, tq, LANES), F32), pltpu.VMEM((n_heads, tq, LANES), F32)] + r_sems,
        input_output_aliases=r_alias,
        compiler_params=_params("arbitrary", "arbitrary"))(proj, proj, proj, rhs, *(rider.operands if rider else []))
    return outs[0], outs[1:]


def _attn_bwd(proj, o, do, rider=None):
    s = proj.shape[0]
    d = proj.shape[1] // N_IN
    hp = d // LANES
    tq = min(s, ATTN_TILE)
    nq = s // tq
    scale = HEAD_DIM ** -0.5
    nb = ATTN_BLOCKS
    n_heads = nb * HEADS_PER_BLOCK
    assert hp % nb == 0
    wide = nb * LANES

    def body(*refs):
        h_id, i = pl.program_id(0), pl.program_id(1)
        first = jnp.logical_and(h_id == 0, i == 0)
        last = jnp.logical_and(h_id == hp // nb - 1, i == nq - 1)
        ((q_ref, k_ref, v_ref, o_ref, do_ref, rhs_ref), (dq_ref, dk_ref, dv_ref),
         (dq_acc, dk_acc, dv_acc, rk_ref, rg_ref), start, finish) = _ride(rider, refs, 6, 3, first, last)
        start()
        head0, strict = _attn_masks(tq)
        qm, dom, delta = [], [], []
        for b in range(nb):
            qm += _split_heads((q_ref[:, _block(b)].astype(F32) * scale).astype(BF16), head0)
            dom += _split_heads(do_ref[:, _block(b)], head0)
            prod = do_ref[:, _block(b)].astype(F32) * o_ref[:, _block(b)]
            delta += [jnp.sum(jnp.where(head0, prod, 0.0), axis=1, keepdims=True),
                      jnp.sum(jnp.where(head0, 0.0, prod), axis=1, keepdims=True)]

        @pl.when(i == 0)
        def _():
            dk_acc[...] = jnp.zeros_like(dk_acc)
            dv_acc[...] = jnp.zeros_like(dv_acc)

        dq_acc[...] = jnp.zeros_like(dq_acc)
        rk_ref[...] = jnp.zeros_like(rk_ref)
        for h in range(n_heads):
            rg_ref[h] = jnp.broadcast_to(delta[h], (tq, LANES))

        def tiles(specs):
            n_t = len(specs)
            offs = [pl.multiple_of(spec[0] * tq, tq) for spec in specs]
            k2 = {(t, b): k_ref[pl.ds(off, tq), _block(b)] for t, off in enumerate(offs) for b in range(nb)}
            v2 = {(t, b): v_ref[pl.ds(off, tq), _block(b)] for t, off in enumerate(offs) for b in range(nb)}
            rows = [slice(r0, r0 + nr) for _, _, r0, nr in specs]
            chains = [(t, h) for t in range(n_t) for h in range(n_heads)]
            masked = lambda c: specs[c[0]][1]
            z = {c: _dot_nt(qm[c[1]][rows[c[0]]], k2[c[0], c[1] // HEADS_PER_BLOCK]) for c in chains}
            dw = {c: _dot_nt(dom[c[1]][rows[c[0]]], v2[c[0], c[1] // HEADS_PER_BLOCK]) for c in chains}
            lsz = {c: _log_sigmoid(z[c]) for c in chains}
            keep = {c: lsz[c] - z[c] for c in chains}
            keep = {c: jnp.where(strict[rows[c[0]]], keep[c], 0.0) if masked(c) else keep[c] for c in chains}
            ksum = {c: _suffix_sums(keep[c], rhs_ref, tq) for c in chains}
            wb, g = {}, {}
            for h in range(n_heads):
                for t in range(n_t):
                    w = jnp.exp(lsz[t, h] + ksum[t, h][0] + _lanes_to_tile(rk_ref[h, rows[t]], tq))
                    if masked((t, h)):
                        w = jnp.where(strict[rows[t]], w, 0.0)
                    wb[t, h] = w.astype(BF16)
                    g[t, h] = dw[t, h] * wb[t, h].astype(F32)
                    rk_ref[h, rows[t]] += ksum[t, h][1]
            gsum = {c: _suffix_sums(g[c], rhs_ref, tq) for c in chains}
            dzb = {}
            for h in range(n_heads):
                for t in range(n_t):
                    dz = g[t, h] - jnp.exp(lsz[t, h]) * (_lanes_to_tile(rg_ref[h, rows[t]], tq) - gsum[t, h][0])
                    if masked((t, h)):
                        dz = jnp.where(strict[rows[t]], dz, 0.0)
                    dzb[t, h] = dz.astype(BF16)
                    dq_acc[h, rows[t]] += _dot(dzb[t, h], k2[t, h // HEADS_PER_BLOCK])
                    rg_ref[h, rows[t]] -= gsum[t, h][1]
            for t in range(n_t):
                for b in range(nb):
                    h0, h1 = 2 * b, 2 * b + 1
                    dk_acc[pl.ds(offs[t], tq), _block(b)] += (_dot_tn(dzb[t, h0], qm[h0][rows[t]])
                                                              + _dot_tn(dzb[t, h1], qm[h1][rows[t]]))
                    dv_acc[pl.ds(offs[t], tq), _block(b)] += (_dot_tn(wb[t, h0], dom[h0][rows[t]])
                                                              + _dot_tn(wb[t, h1], dom[h1][rows[t]]))

        _sweep_tiles(tiles, i, tq, rk_ref)

        for b in range(nb):
            dq_ref[:, _block(b)] = (jnp.where(head0, dq_acc[2 * b], dq_acc[2 * b + 1]) * scale).astype(BF16)

        @pl.when(i == nq - 1)
        def _():
            dk_ref[...] = dk_acc[...].astype(BF16)
            dv_ref[...] = dv_acc[...].astype(BF16)

        finish()

    blk = pl.BlockSpec((tq, wide), lambda h, i: (i, h))
    seq = pl.BlockSpec((s, wide), lambda h, i: (0, h))
    sh = jax.ShapeDtypeStruct((s, d), BF16)
    rhs = _suffix_rhs()
    r_in, r_out, r_shapes, r_sems, r_alias = _rider_call_args(rider, 6, 3)
    outs = pl.pallas_call(
        body, name="attn_bwd", grid=(hp // nb, nq),
        in_specs=[pl.BlockSpec((tq, wide), lambda h, i: (i, 3 * hp // nb + h)),
                  pl.BlockSpec((s, wide), lambda h, i: (0, 4 * hp // nb + h)),
                  pl.BlockSpec((s, wide), lambda h, i: (0, 5 * hp // nb + h)),
                  blk, blk, pl.BlockSpec(rhs.shape, lambda h, i: (0, 0))] + r_in,
        out_specs=[blk, seq, seq] + r_out, out_shape=[sh, sh, sh] + r_shapes,
        scratch_shapes=[pltpu.VMEM((n_heads, tq, LANES), F32),
                        pltpu.VMEM((s, wide), F32), pltpu.VMEM((s, wide), F32),
                        pltpu.VMEM((n_heads, tq, LANES), F32),
                        pltpu.VMEM((n_heads, tq, LANES), F32)] + r_sems,
        input_output_aliases=r_alias,
        compiler_params=_params("arbitrary", "arbitrary"))(proj, proj, proj, o, do, rhs,
                                                           *(rider.operands if rider else []))
    return outs[:3], outs[3:]


def _post_math(ya_ref, o_ref, gb_ref, ma_ref, mb_ref, x_ref, p_ref, wpa_ref, wpb_ref, wout_ref, wpe_ref, wpg_ref,
               alpha):
    gb = gb_ref[...].astype(F32)
    sgb = _sigmoid(gb)
    o = o_ref[...]
    yb = (o * (gb * sgb)).astype(BF16)
    pa = _dot(ya_ref[...], wpa_ref[...])
    pb = _dot(yb, wpb_ref[...])
    sa = _sigmoid(ma_ref[...].astype(F32))
    sb = _sigmoid(mb_ref[...].astype(F32))
    merged = (sa * pa + sb * pb).astype(BF16)
    h1 = alpha * x_ref[...] + _dot(merged, wout_ref[...])
    h1b = h1.astype(BF16)
    e = _dot(p_ref[...].astype(BF16), wpe_ref[...])
    sg = _sigmoid(_dot(h1b, wpg_ref[...]))
    h2 = h1 + e * sg
    return dict(gb=gb, sgb=sgb, o=o, yb=yb, pa=pa, pb=pb, sa=sa, sb=sb, merged=merged, h1b=h1b, e=e, sg=sg, h2=h2)


class _RowsOf:
    def __init__(self, ref, rows):
        self.ref, self.rows = ref, rows

    def __getitem__(self, idx):
        return self.ref[self.rows]


def _post_specs(tm, d, ple, weights):
    tok = pl.BlockSpec((tm, d), lambda i: (i, 0))
    col = lambda j: pl.BlockSpec((tm, d), lambda i: (i, j))
    full = lambda a: pl.BlockSpec(a.shape, lambda i: (0,) * a.ndim, pipeline_mode=pl.Buffered(1))
    return tok, [tok, tok, col(6), col(7), col(8), tok, pl.BlockSpec((tm, ple), lambda i: (i, 0))] + [
        full(w) for w in weights]


def _post_fwd(ya, o, proj, x, p, w_pa, w_pb, w_out, w_pe, w_pg, ln_g, ln_b, alpha, rider=None):
    s, d = x.shape
    ple = p.shape[1]
    tm = min(s, 512)
    nsteps = s // tm
    weights = (w_pa, w_pb, w_out, w_pe, w_pg, ln_g, ln_b)

    def body(*refs):
        i = pl.program_id(0)
        ins, (out_ref,), _, start, finish = _ride(rider, refs, 14, 1, i == 0, i == nsteps - 1)
        start()
        for rows in (slice(0, tm // 2), slice(tm // 2, tm)):
            f = _post_math(*[_RowsOf(ref, rows) for ref in ins[:7]], *ins[7:12], alpha)
            xhat, _ = _ln_stats(f["h2"])
            out_ref[rows] = xhat * ins[12][...] + ins[13][...]
        finish()

    tok, in_specs = _post_specs(tm, d, ple, weights)
    r_in, r_out, r_shapes, r_sems, r_alias = _rider_call_args(rider, 14, 1)
    outs = pl.pallas_call(body, name="post_fwd", grid=(nsteps,), in_specs=in_specs + r_in, out_specs=[tok] + r_out,
                          out_shape=[jax.ShapeDtypeStruct((s, d), F32)] + r_shapes, scratch_shapes=r_sems,
                          input_output_aliases=r_alias,
                          compiler_params=_params("arbitrary"))(ya, o, proj, proj, proj, x, p, *weights,
                                                                *(rider.operands if rider else []))
    return outs[0], outs[1:]


def _post_bwd(dxo, ya, o, proj, x, p, w_pa, w_pb, w_out, w_pe, w_pg, ln_g, ln_b, alpha):
    s, d = x.shape
    ple = p.shape[1]
    tm = min(s, 256)
    nsteps = s // tm
    weights = (w_pa, w_pb, w_out, w_pe, w_pg, ln_g, ln_b)

    def body(dxo_ref, ya_ref, o_ref, gb_ref, ma_ref, mb_ref, x_ref, p_ref, wpa_ref, wpb_ref, wout_ref, wpe_ref,
             wpg_ref, g_ref, b_ref,
             dproj_ref, dxr_ref, dya_ref, do_ref, de_ref, h1_ref, dzg_ref, mrg_ref, dh1_ref, dpa_ref, yb_ref, dpb_ref,
             dg_ref, db_ref, dg_acc, db_acc):
        i = pl.program_id(0)

        @pl.when(i == 0)
        def _():
            dg_acc[...] = jnp.zeros_like(dg_acc)
            db_acc[...] = jnp.zeros_like(db_acc)

        f = _post_math(ya_ref, o_ref, gb_ref, ma_ref, mb_ref, x_ref, p_ref, wpa_ref, wpb_ref, wout_ref, wpe_ref,
                       wpg_ref, alpha)
        xhat, rstd = _ln_stats(f["h2"])
        dxo = dxo_ref[...]
        dg_acc[...] += _sum_rows8(dxo * xhat)
        db_acc[...] += _sum_rows8(dxo)
        dh2 = _ln_bwd(dxo * g_ref[...], xhat, rstd)
        sg, e = f["sg"], f["e"]
        de_ref[...] = (dh2 * sg).astype(BF16)
        dzg = (dh2 * e * sg * (1.0 - sg)).astype(BF16)
        dzg_ref[...] = dzg
        dh1 = dh2 + _dot_nt(dzg, wpg_ref[...])
        dh1b = dh1.astype(BF16)
        dxr_ref[...] = alpha * dh1
        dh1_ref[...] = dh1b
        h1_ref[...] = f["h1b"]
        mrg_ref[...] = f["merged"]
        yb_ref[...] = f["yb"]
        dmerged = _dot_nt(dh1b, wout_ref[...])
        sa, sb = f["sa"], f["sb"]
        dpa = (dmerged * sa).astype(BF16)
        dpb = (dmerged * sb).astype(BF16)
        dpa_ref[...] = dpa
        dpb_ref[...] = dpb
        dproj_ref[:, d:2 * d] = (dmerged * f["pa"] * sa * (1.0 - sa)).astype(BF16)
        dproj_ref[:, 2 * d:3 * d] = (dmerged * f["pb"] * sb * (1.0 - sb)).astype(BF16)
        dya_ref[...] = _dot_nt(dpa, wpa_ref[...]).astype(BF16)
        dyb = _dot_nt(dpb, wpb_ref[...])
        gb, sgb = f["gb"], f["sgb"]
        do_ref[...] = (dyb * (gb * sgb)).astype(BF16)
        dproj_ref[:, 0:d] = (dyb * f["o"] * (sgb * (1.0 + gb * (1.0 - sgb)))).astype(BF16)

        @pl.when(i == nsteps - 1)
        def _():
            dg_ref[...] = jnp.sum(dg_acc[...], axis=0, keepdims=True)
            db_ref[...] = jnp.sum(db_acc[...], axis=0, keepdims=True)

    tok, in_specs = _post_specs(tm, d, ple, weights)
    vec_spec = pl.BlockSpec((1, d), lambda i: (0, 0))
    vec = jax.ShapeDtypeStruct((1, d), F32)
    act = jax.ShapeDtypeStruct((s, d), BF16)
    return pl.pallas_call(
        body, name="post_bwd", grid=(nsteps,), in_specs=[tok] + in_specs,
        out_specs=[pl.BlockSpec((tm, 3 * d), lambda i: (i, 2)), tok] + [tok] * 10 + [vec_spec, vec_spec],
        out_shape=[jax.ShapeDtypeStruct((s, N_IN * d), BF16), jax.ShapeDtypeStruct((s, d), F32)] + [act] * 10 + [vec, vec],
        scratch_shapes=[pltpu.VMEM((SUBLANES, d), F32), pltpu.VMEM((SUBLANES, d), F32)],
        compiler_params=_params("arbitrary"))(dxo, ya, o, proj, proj, proj, x, p, *weights)


def _loss_head(y, target):
    s, d = y.shape
    tm = min(s, 512)

    def body(y_ref, t_ref, dy_ref, l_ref):
        @pl.when(pl.program_id(0) == 0)
        def _():
            l_ref[...] = jnp.zeros_like(l_ref)

        err = y_ref[...] - t_ref[...]
        dy_ref[...] = err / d
        row = jnp.sum(err * err, axis=1, keepdims=True) / d
        l_ref[...] += 0.5 * jnp.sum(row, axis=0, keepdims=True)

    tok = pl.BlockSpec((tm, d), lambda i: (i, 0))
    return pl.pallas_call(body, name="loss_head", grid=(s // tm,), in_specs=[tok, tok],
                          out_specs=[tok, pl.BlockSpec((SUBLANES, LANES), lambda i: (0, 0))],
                          out_shape=[jax.ShapeDtypeStruct((s, d), F32),
                                     jax.ShapeDtypeStruct((SUBLANES, LANES), F32)],
                          compiler_params=_params("arbitrary"))(y, target)


def _position():
    x, y, c = lax.axis_index("x"), lax.axis_index("y"), lax.axis_index("c")
    chips = [(1 - x, y), (x, 1 - y), (1 - x, 1 - y)]
    return x, y, c, chips


def _shard_of(ref, col_sharded, j, n):
    off = pl.multiple_of(j * n, n)
    return ref.at[:, pl.ds(off, n)] if col_sharded else ref.at[pl.ds(off, n), :]


def _half_of(ref, col_sharded, h, n):
    off = pl.multiple_of(h * n, n)
    return ref.at[pl.ds(off, n), :] if col_sharded else ref.at[:, pl.ds(off, n)]


def _piece_of(ref, col_sharded, chip, n_block, half, n_half):
    block = pl.ds(pl.multiple_of(chip * n_block, n_block), n_block)
    part = pl.ds(pl.multiple_of(half * n_half, n_half), n_half)
    return ref.at[part, block] if col_sharded else ref.at[block, part]


class _Rider(NamedTuple):
    operands: list
    out_shapes: list
    aliases: dict
    n_sems: int
    start: Callable
    finish: Callable


def _rider_call_args(rider, n_in, n_out):
    if rider is None:
        return [], [], [], [], {}
    sems = [pltpu.SemaphoreType.DMA((rider.n_sems,))] * 2
    aliases = {n_in + i: n_out + o for i, o in rider.aliases.items()}
    return [ANY] * len(rider.operands), [ANY] * len(rider.out_shapes), list(rider.out_shapes), sems, aliases


def _ride(rider, refs, n_in, n_out, first, last):
    if rider is None:
        return refs[:n_in], refs[n_in:n_in + n_out], refs[n_in + n_out:], lambda: None, lambda: None
    r_in, r_out = len(rider.operands), len(rider.out_shapes)
    ins, rins = refs[:n_in], refs[n_in:n_in + r_in]
    outs = refs[n_in + r_in:n_in + r_in + n_out]
    routs = refs[n_in + r_in + n_out:n_in + r_in + n_out + r_out]
    scratch = refs[n_in + r_in + n_out + r_out:-2]
    send_sems, recv_sems = refs[-2:]

    def start():
        pl.when(first)(lambda: rider.start(rins, routs, send_sems, recv_sems))

    def finish():
        pl.when(last)(lambda: rider.finish(rins, routs, send_sems, recv_sems))

    return ins, outs, scratch, start, finish


def _gather_copies(outs, ici_sems, d2d_sems, names=BIG, d2d_first=0):
    x, y, c, chips = _position()
    my_chip = 2 * x + y
    sends, arrivals, passes, passed = [], [], [], []
    for a, out in enumerate(outs):
        cs = COL_SHARDED[names[a]]
        rows, cols = out.shape
        n_block = (cols if cs else rows) // N_CHIPS
        n_half = (rows if cs else cols) // 2
        piece = lambda chip, half: _piece_of(out, cs, chip, n_block, half, n_half)
        for j, chip in enumerate(chips):
            k = a * 3 + j
            their = 2 * chip[0] + chip[1]
            if ici_sems is not None:
                send_sems, recv_sems = ici_sems
                sends.append(pltpu.make_async_remote_copy(
                    src_ref=piece(my_chip, c), dst_ref=piece(my_chip, c), send_sem=send_sems.at[k],
                    recv_sem=recv_sems.at[k], device_id=(chip[0], chip[1], c), device_id_type=MESH))
                arrivals.append(pltpu.make_async_remote_copy(
                    src_ref=piece(their, c), dst_ref=piece(their, c), send_sem=send_sems.at[k],
                    recv_sem=recv_sems.at[k], device_id=(chip[0], chip[1], c), device_id_type=MESH))
            if d2d_sems is not None:
                send_sems, recv_sems = d2d_sems
                passes.append(pltpu.make_async_remote_copy(
                    src_ref=piece(their, c), dst_ref=piece(their, c), send_sem=send_sems.at[d2d_first + k],
                    recv_sem=recv_sems.at[d2d_first + k], device_id=(x, y, 1 - c), device_id_type=MESH))
                passed.append(pltpu.make_async_remote_copy(
                    src_ref=piece(their, 1 - c), dst_ref=piece(their, 1 - c), send_sem=send_sems.at[d2d_first + k],
                    recv_sem=recv_sems.at[d2d_first + k], device_id=(x, y, 1 - c), device_id_type=MESH))
    return sends, arrivals, passes, passed


def _whole_gather_rider(bufs, names):
    flat = [bufs[name] for name in names]
    n = len(flat) * 3

    def copies(routs, send_sems, recv_sems):
        sems = (send_sems, recv_sems)
        return _gather_copies(routs, sems, sems, names, d2d_first=n)

    def start(rins, routs, send_sems, recv_sems):
        for cp in copies(routs, send_sems, recv_sems)[0]:
            cp.start()

    def finish(rins, routs, send_sems, recv_sems):
        sends, arrivals, passes, passed = copies(routs, send_sems, recv_sems)
        for arrival, onward in zip(arrivals, passes):
            arrival.wait_recv()
            onward.start()
        for cp in passed:
            cp.wait_recv()
        for cp in sends + passes:
            cp.wait_send()

    return _Rider(flat, [jax.ShapeDtypeStruct(a.shape, a.dtype) for a in flat], {i: i for i in range(len(flat))},
                  2 * n, start, finish)


def _gather_weights(bufs, names):
    rider = _whole_gather_rider(bufs, names)

    def body(*refs):
        n = len(rider.operands)
        args = (refs[:n], refs[n:2 * n], *refs[2 * n:])
        rider.start(*args)
        rider.finish(*args)

    outs = pl.pallas_call(
        body, name="gather_weights", in_specs=[ANY] * len(rider.operands), out_specs=[ANY] * len(rider.out_shapes),
        out_shape=rider.out_shapes, input_output_aliases=rider.aliases,
        scratch_shapes=[pltpu.SemaphoreType.DMA((rider.n_sems,))] * 2,
    )(*rider.operands)
    return dict(zip(names, outs))


def _gather_rider(bufs, over_ici):
    flat = [bufs[name] for name in BIG]

    def copies(routs, send_sems, recv_sems):
        sems = (send_sems, recv_sems)
        sends, arrivals, passes, passed = _gather_copies(routs, sems if over_ici else None, None if over_ici else sems)
        return (sends, arrivals) if over_ici else (passes, passed)

    def start(rins, routs, send_sems, recv_sems):
        for cp in copies(routs, send_sems, recv_sems)[0]:
            cp.start()

    def finish(rins, routs, send_sems, recv_sems):
        out, due = copies(routs, send_sems, recv_sems)
        for cp in due:
            cp.wait_recv()
        for cp in out:
            cp.wait_send()

    return _Rider(flat, [jax.ShapeDtypeStruct(a.shape, a.dtype) for a in flat], {i: i for i in range(len(flat))},
                  len(flat) * 3, start, finish)


def _half_shape(shape, col_sharded):
    r, c = shape
    return (r // 2, c) if col_sharded else (r, c // 2)


def _exchange_rider(grads):
    flat = [grads[name] for name in BIG]

    def copies(rins, routs, send_sems, recv_sems):
        x, y, c, _ = _position()
        out = []
        for a, name in enumerate(BIG):
            cs = COL_SHARDED[name]
            n = routs[a].shape[0] if cs else routs[a].shape[1]
            out.append(pltpu.make_async_remote_copy(
                src_ref=_half_of(rins[a], cs, 1 - c, n), dst_ref=routs[a], send_sem=send_sems.at[a],
                recv_sem=recv_sems.at[a], device_id=(x, y, 1 - c), device_id_type=MESH))
        return out

    def start(rins, routs, send_sems, recv_sems):
        for cp in copies(rins, routs, send_sems, recv_sems):
            cp.start()

    def finish(rins, routs, send_sems, recv_sems):
        cps = copies(rins, routs, send_sems, recv_sems)
        for cp in cps:
            cp.wait_recv()
        for cp in cps:
            cp.wait_send()

    return _Rider(flat, [jax.ShapeDtypeStruct(_half_shape(a.shape, COL_SHARDED[name]), F32)
                         for a, name in zip(flat, BIG)], {}, len(flat), start, finish)


def _scatter_to_owners(halves):
    rider = _scatter_rider(halves)

    def body(*refs):
        n = len(rider.operands)
        args = (refs[:n], refs[n:2 * n], *refs[2 * n:])
        rider.start(*args)
        rider.finish(*args)

    outs = pl.pallas_call(
        body, name="scatter_to_owners", in_specs=[ANY] * len(rider.operands), out_specs=[ANY] * len(rider.out_shapes),
        out_shape=rider.out_shapes, scratch_shapes=[pltpu.SemaphoreType.DMA((rider.n_sems,))] * 2,
    )(*rider.operands)
    return dict(zip(BIG, outs))


def _scatter_rider(halves):
    flat = [halves[name] for name in BIG]

    def slots_shape(a, cs):
        r, c = a.shape
        return (N_CHIPS - 1,) + ((r, c // N_CHIPS) if cs else (r // N_CHIPS, c))

    def copies(rins, routs, send_sems, recv_sems):
        x, y, c, chips = _position()
        out = []
        for a, name in enumerate(BIG):
            cs = COL_SHARDED[name]
            n = routs[a].shape[2] if cs else routs[a].shape[1]
            for j, chip in enumerate(chips):
                k = a * 3 + j
                out.append(pltpu.make_async_remote_copy(
                    src_ref=_shard_of(rins[a], cs, 2 * chip[0] + chip[1], n), dst_ref=routs[a].at[j],
                    send_sem=send_sems.at[k], recv_sem=recv_sems.at[k], device_id=(chip[0], chip[1], c),
                    device_id_type=MESH))
        return out

    def start(rins, routs, send_sems, recv_sems):
        for cp in copies(rins, routs, send_sems, recv_sems):
            cp.start()

    def finish(rins, routs, send_sems, recv_sems):
        cps = copies(rins, routs, send_sems, recv_sems)
        for cp in cps:
            cp.wait_recv()
        for cp in cps:
            cp.wait_send()

    return _Rider(flat, [jax.ShapeDtypeStruct(slots_shape(a, COL_SHARDED[name]), a.dtype) for a, name in zip(flat, BIG)],
                  {}, len(flat) * 3, start, finish)


def _join_halves(halves):
    flat = [halves[name] for name in BIG]
    n_w = len(flat)

    def body(*refs):
        outs = refs[n_w:2 * n_w]
        send_sems, recv_sems = refs[2 * n_w:]
        x, y, c, _ = _position()
        sends, recvs = [], []
        for w, name in enumerate(BIG):
            cs = COL_SHARDED[name]
            n = (outs[w].shape[1] if cs else outs[w].shape[2]) // 2

            def half(h):
                part = pl.ds(pl.multiple_of(h * n, n), n)
                return outs[w].at[:, part, :] if cs else outs[w].at[:, :, part]

            sends.append(pltpu.make_async_remote_copy(
                src_ref=half(c), dst_ref=half(c), send_sem=send_sems.at[w], recv_sem=recv_sems.at[w],
                device_id=(x, y, 1 - c), device_id_type=MESH))
            recvs.append(pltpu.make_async_remote_copy(
                src_ref=half(1 - c), dst_ref=half(1 - c), send_sem=send_sems.at[w], recv_sem=recv_sems.at[w],
                device_id=(x, y, 1 - c), device_id_type=MESH))
        for cp in sends:
            cp.start()
        for cp in recvs:
            cp.wait_recv()
        for cp in sends:
            cp.wait_send()

    outs = pl.pallas_call(
        body, name="join_halves", in_specs=[ANY] * n_w, out_specs=[ANY] * n_w,
        out_shape=[jax.ShapeDtypeStruct(a.shape, F32) for a in flat],
        input_output_aliases={w: w for w in range(n_w)},
        scratch_shapes=[pltpu.SemaphoreType.DMA((n_w,)), pltpu.SemaphoreType.DMA((n_w,))],
    )(*flat)
    return dict(zip(BIG, outs))


def _small_rider(packed):
    r, lanes = packed.shape

    def copies(rins, routs, send_sems, recv_sems):
        x, y, c, _ = _position()
        me = 4 * x + 2 * y + c
        local = pltpu.make_async_copy(rins[0], routs[0].at[me], send_sems.at[N_DEV - 1])
        sends, recvs = [], []
        for k in range(1, N_DEV):
            px, py, pc = x ^ (k >> 2), y ^ ((k >> 1) & 1), c ^ (k & 1)
            sends.append(pltpu.make_async_remote_copy(
                src_ref=rins[0], dst_ref=routs[0].at[me], send_sem=send_sems.at[k - 1], recv_sem=recv_sems.at[k - 1],
                device_id=(px, py, pc), device_id_type=MESH))
            recvs.append(pltpu.make_async_remote_copy(
                src_ref=rins[0], dst_ref=routs[0].at[4 * px + 2 * py + pc], send_sem=send_sems.at[k - 1],
                recv_sem=recv_sems.at[k - 1], device_id=(px, py, pc), device_id_type=MESH))
        return local, sends, recvs

    def start(rins, routs, send_sems, recv_sems):
        local, sends, _ = copies(rins, routs, send_sems, recv_sems)
        local.start()
        for cp in sends:
            cp.start()

    def finish(rins, routs, send_sems, recv_sems):
        local, sends, recvs = copies(rins, routs, send_sems, recv_sems)
        for cp in recvs:
            cp.wait_recv()
        for cp in sends:
            cp.wait_send()
        local.wait()

    return _Rider([packed], [jax.ShapeDtypeStruct((N_DEV, r, lanes), F32)], {}, N_DEV, start, finish)


def _pack_small(t):
    return jnp.concatenate([t[name].reshape(-1, LANES) for name in SMALL], axis=0)


def _unpack_small(packed, like):
    out, row = {}, 0
    for name in SMALL:
        n = like[name].size // LANES
        out[name] = packed[row:row + n].reshape(like[name].shape)
        row += n
    return out


def kernel(x, p, w_in, vn_g, vn_b, w_s, b_s, w_pa, w_pb, w_out, w_pe, w_pg, ln_g, ln_b, loss_target, m_w_in, m_vn_g, m_vn_b, m_w_s, m_b_s, m_w_pa, m_w_pb, m_w_out, m_w_pe, m_w_pg, m_ln_g, m_ln_b, v_w_in, v_vn_g, v_vn_b, v_w_s, v_b_s, v_w_pa, v_w_pb, v_w_out, v_w_pe, v_w_pg, v_ln_g, v_ln_b):
    weights = dict(w_in=w_in, vn_g=vn_g, vn_b=vn_b, w_s=w_s, b_s=b_s, w_pa=w_pa, w_pb=w_pb, w_out=w_out, w_pe=w_pe,
                   w_pg=w_pg, ln_g=ln_g, ln_b=ln_b)
    mom1 = dict(w_in=m_w_in, vn_g=m_vn_g, vn_b=m_vn_b, w_s=m_w_s, b_s=m_b_s, w_pa=m_w_pa, w_pb=m_w_pb, w_out=m_w_out,
                w_pe=m_w_pe, w_pg=m_w_pg, ln_g=m_ln_g, ln_b=m_ln_b)
    mom2 = dict(w_in=v_w_in, vn_g=v_vn_g, vn_b=v_vn_b, w_s=v_w_s, b_s=v_b_s, w_pa=v_w_pa, w_pb=v_w_pb, w_out=v_w_out,
                w_pe=v_w_pe, w_pg=v_w_pg, ln_g=v_ln_g, ln_b=v_ln_b)
    nl, d = vn_g.shape
    chunk = w_s.shape[2]
    assert chunk == LANES and w_s.shape[3] == LANES and d % LANES == 0
    alpha = (2 * nl) ** 0.25
    pos = tuple(lax.axis_index(a).astype(jnp.int32).reshape(1) for a in ("c", "x", "y"))

    placed = [{name: _cast_into_place(weights[name], l, pos, COL_SHARDED[name]) for name in BIG} for l in range(nl)]
    later = tuple(name for name in BIG if name != "w_in")
    full = [_gather_weights(placed[0], ("w_in",))] + [None] * (nl - 1)
    causal = jnp.tril(jnp.ones((chunk, chunk), dtype=bool))
    ws_m = jnp.where(causal, w_s, 0.0).astype(BF16)
    ws_mt = jnp.swapaxes(ws_m, 2, 3)
    bs_t = jnp.swapaxes(b_s, 1, 2)

    xs, projs, yas, os_ = [x[0]], [], [], []
    for l in range(nl):
        proj, rest = _proj_fwd(xs[l], full[l]["w_in"], _whole_gather_rider(placed[0], later) if l == 0 else None)
        if l == 0:
            full[0].update(zip(later, rest))
        ya = _gmlp_fwd(proj, vn_g[l:l + 1], vn_b[l:l + 1], ws_m[l], bs_t[l])
        more = l + 1 < nl
        o, arrived = _attn_fwd(proj, _gather_rider(placed[l + 1], over_ici=True) if more else None)
        x_next, handed = _post_fwd(ya, o, proj, xs[l], p[l, 0], full[l]["w_pa"], full[l]["w_pb"], full[l]["w_out"],
                                   full[l]["w_pe"], full[l]["w_pg"], ln_g[l:l + 1], ln_b[l:l + 1], alpha,
                                   _gather_rider(dict(zip(BIG, arrived)), over_ici=False) if more else None)
        if more:
            full[l + 1] = dict(zip(BIG, handed))
        xs.append(x_next)
        projs.append(proj)
        yas.append(ya)
        os_.append(o)

    dx, loss_tile = _loss_head(xs[nl], loss_target[0])
    loss = lax.psum(loss_tile[0, 0], ("x", "y", "c"))

    big_grads, received, slots = [None] * nl, [None] * nl, [None] * nl
    chip_sums = None
    small_grads = [None] * nl
    for l in reversed(range(nl)):
        w = full[l]
        (dproj, dxr, dya, do, de, h1b, dzg, merged, dh1, dpa, yb, dpb, dln_g, dln_b) = _post_bwd(
            dx, yas[l], os_[l], projs[l], xs[l], p[l, 0], w["w_pa"], w["w_pb"], w["w_out"], w["w_pe"], w["w_pg"],
            ln_g[l:l + 1], ln_b[l:l + 1], alpha)
        (dq, dk, dv), scattered = _attn_bwd(projs[l], os_[l], do,
                                            _scatter_rider(chip_sums) if chip_sums is not None else None)
        if chip_sums is not None:
            slots[l + 1] = dict(zip(BIG, scattered))
        dproj, dvn_g, dvn_b, dw_s, dbs_cols = _gmlp_bwd(dproj, projs[l], dya, dq, dk, dv, vn_g[l:l + 1],
                                                         vn_b[l:l + 1], ws_m[l], ws_mt[l], bs_t[l])
        partial = _pack_small(dict(vn_g=dvn_g[0], vn_b=dvn_b[0], w_s=dw_s, b_s=dbs_cols[:, :b_s.shape[1]].T,
                                   ln_g=dln_g[0], ln_b=dln_b[0]))
        dw_in, (partials,) = _matmul_tn(xs[l], dproj, _small_rider(partial))
        small_grads[l] = _unpack_small(_sum_slots(partials), {name: weights[name][0] for name in SMALL})
        big_grads[l] = dict(w_in=dw_in, w_pa=_matmul_tn(yas[l], dpa), w_pb=_matmul_tn(yb, dpb),
                            w_out=_matmul_tn(merged, dh1), w_pe=_matmul_tn(p[l, 0], de), w_pg=_matmul_tn(h1b, dzg))
        dx, from_sibling = _dx_matmul(dxr, dproj, w["w_in"], _exchange_rider(big_grads[l]))
        received[l] = dict(zip(BIG, from_sibling))
        chip_sums = {name: _add_own_half(big_grads[l][name], received[l][name], pos, COL_SHARDED[name])
                     for name in BIG}
    slots[0] = _scatter_to_owners(chip_sums)
    reduced = {}
    for name in BIG:
        buf = None
        for l in range(nl):
            buf = _reduce_block(buf, big_grads[l][name], received[l][name], slots[l][name], l, nl, pos,
                                COL_SHARDED[name])
        reduced[name] = buf
    grads = _join_halves(reduced)

    small_like = {name: weights[name] for name in SMALL}
    grads.update({name: jnp.stack([small_grads[l][name] for l in range(nl)]) for name in SMALL})

    delta, new_m, new_v = {}, {}, {}
    for name in BIG:
        sh = weights[name].shape
        flat = lambda a: a.reshape(sh[0] * sh[1], sh[2])
        dl, nm, nv, g_out = _adamw(flat(weights[name]), flat(grads[name]), flat(mom1[name]), flat(mom2[name]),
                                   emit_grad=True)
        delta[name], new_m[name], new_v[name] = dl.reshape(sh), nm.reshape(sh), nv.reshape(sh)
        grads[name] = g_out.reshape(sh)
    dl, nm, nv = _adamw(_pack_small(small_like), _pack_small({n: grads[n] for n in SMALL}),
                        _pack_small({n: mom1[n] for n in SMALL}), _pack_small({n: mom2[n] for n in SMALL}))
    delta.update(_unpack_small(dl, small_like))
    new_m.update(_unpack_small(nm, small_like))
    new_v.update(_unpack_small(nv, small_like))

    return (loss, dx[None], *[grads[n] for n in WEIGHTS], *[delta[n] for n in WEIGHTS],
            *[new_m[n] for n in WEIGHTS], *[new_v[n] for n in WEIGHTS])
```
